```python
import math
import jax, jax.numpy as jnp
from jax import lax
import numpy as np

D_MODEL = 1024
BATCH = 8
SEQ = 2048
DEPTH = 1
DEC_BATCH = 16
DEC_SEQ = 32
PAST_LEN = 2048

CHUNK = 64
Q_BLOCK = 128
DA_HEADS = 8
DA_HD = 64
DA_V = 2 * DA_HD
LRU_W = D_MODEL
LRU_BLOCKS = 8
LRU_BW = LRU_W // LRU_BLOCKS
CONV_W = 4
LRU_C = 8.0
XA_HEADS = 4
XA_HD = 256
N_MEM = 256
N_BUCKETS = 32
MAX_DISTANCE = 128
D_FF = ((8 * D_MODEL + 3 * 256 - 1) // (3 * 256)) * 256
N_BRANCH = 3
EPS = 1e-6
DA_QK_W = DA_HEADS * 2 * DA_HD
DA_V_W = DA_HEADS * DA_V
XA_W = XA_HEADS * XA_HD
IN_SPLITS = (DA_QK_W, 2 * DA_QK_W, 2 * DA_QK_W + DA_V_W, 2 * DA_QK_W + DA_V_W + LRU_W, 2 * DA_QK_W + DA_V_W + 2 * LRU_W)
IN_W = 2 * DA_QK_W + DA_V_W + 2 * LRU_W + XA_W

kernel_name = "hybrid_diffattn_rglru_streaming_step"


def rms_norm(x, g):
    x32 = x.astype(jnp.float32)
    y = x32 * lax.rsqrt(jnp.mean(x32 * x32, axis=-1, keepdims=True) + EPS)
    return (y * g.astype(jnp.float32)).astype(x.dtype)


def rel_bucket(rel):
    half = N_BUCKETS // 2
    max_exact = half // 2
    n = jnp.abs(rel)
    nf = jnp.maximum(n, 1).astype(jnp.float32)
    large = max_exact + (jnp.log(nf / max_exact) / math.log(MAX_DISTANCE / max_exact) * (half - max_exact)).astype(jnp.int32)
    large = jnp.minimum(large, half - 1)
    return jnp.where(rel > 0, half, 0) + jnp.where(n < max_exact, n, large)


def rel_bias(q_pos, k_pos, rel_table):
    b = rel_bucket(k_pos[None, :] - q_pos[:, None])
    return jnp.moveaxis(rel_table[b], -1, 0).astype(jnp.float32)


def diff_attention(q, k, v, q_pos, k_pos, rel_table, lam, subln_g, lambda_init):
    logits = jnp.einsum("bqhcd,bkhcd->bhcqk", q, k).astype(jnp.float32) * (DA_HD ** -0.5)
    logits = logits + rel_bias(q_pos, k_pos, rel_table)[None, :, None]
    visible = (k_pos[None, :] // CHUNK) <= (q_pos[:, None] // CHUNK)
    logits = jnp.where(visible, logits, -1e30)
    p = jax.nn.softmax(logits, axis=-1)
    w = p[:, :, 0] - lam * p[:, :, 1]
    out = jnp.einsum("bhqk,bkhe->bqhe", w.astype(v.dtype), v)
    return rms_norm(out, subln_g) * (1.0 - lambda_init)


def blocked_queries(attend, q, q_pos):
    B, S = q.shape[:2]
    nb = S // Q_BLOCK
    qb = jnp.moveaxis(q.reshape((B, nb, Q_BLOCK) + q.shape[2:]), 1, 0)
    pb = q_pos.reshape(nb, Q_BLOCK)
    out = lax.map(lambda a: attend(a[0], a[1]), (qb, pb))
    return jnp.moveaxis(out, 0, 1).reshape((B, S) + out.shape[3:])


def lru_combine(left, right):
    a_l, b_l = left
    a_r, b_r = right
    return a_l * a_r, a_r * b_l + b_r


def rglru_branch(xb, gb, conv_state, h0, conv_w, conv_b, w_rg_a, b_rg_a, w_rg_x, b_rg_x, rg_lambda):
    B, S, _ = xb.shape
    xpad = jnp.concatenate([conv_state.astype(xb.dtype), xb], axis=1)
    xc = conv_b + sum(xpad[:, j:j + S] * conv_w[j] for j in range(CONV_W))
    new_conv = xpad[:, -(CONV_W - 1):]
    xh = xc.reshape(B, S, LRU_BLOCKS, LRU_BW)
    r = jax.nn.sigmoid((jnp.einsum("bsni,nij->bsnj", xh, w_rg_a).reshape(B, S, LRU_W) + b_rg_a).astype(jnp.float32))
    i = jax.nn.sigmoid((jnp.einsum("bsni,nij->bsnj", xh, w_rg_x).reshape(B, S, LRU_W) + b_rg_x).astype(jnp.float32))
    log_a = -LRU_C * r * jax.nn.softplus(-rg_lambda.astype(jnp.float32))
    a = jnp.exp(log_a)
    mult = jnp.sqrt(-jnp.expm1(2.0 * log_a))
    u = mult * (i * xc.astype(jnp.float32))
    u = u.at[:, 0].add(a[:, 0] * h0.astype(jnp.float32))
    _, h = lax.associative_scan(lru_combine, (a, u), axis=1)
    out = h.astype(xb.dtype) * jax.nn.gelu(gb)
    return out, new_conv, h[:, -1]


def memory_kv(mem, g, w_mem_kv):
    B, M, _ = mem.shape
    kv = (rms_norm(mem, g) @ w_mem_kv).reshape(B, M, 2, XA_HEADS, XA_HD)
    return kv[:, :, 0], kv[:, :, 1]


def cross_attention(q, mem_k, mem_v):
    logits = jnp.einsum("bqhd,bmhd->bhqm", q, mem_k).astype(jnp.float32) * (XA_HD ** -0.5)
    p = jax.nn.softmax(logits, axis=-1)
    return jnp.einsum("bhqm,bmhd->bqhd", p.astype(mem_v.dtype), mem_v)


def layer_forward(x, pos, past_k, past_v, past_pos, conv_state, h0, mem_k, mem_v, rel_table, lp, lambda_init):
    B, S, _ = x.shape
    h = rms_norm(x, lp["norm_mix"])
    q, k, v, xb, gb, qc = jnp.split(h @ lp["w_in"], IN_SPLITS, axis=-1)
    q = q.reshape(B, S, DA_HEADS, 2, DA_HD)
    k = k.reshape(B, S, DA_HEADS, 2, DA_HD)
    v = v.reshape(B, S, DA_HEADS, DA_V)
    if past_k is None:
        k_all, v_all, k_pos = k, v, pos
    else:
        k_all = jnp.concatenate([past_k.astype(k.dtype), k], axis=1)
        v_all = jnp.concatenate([past_v.astype(v.dtype), v], axis=1)
        k_pos = jnp.concatenate([past_pos, pos])
    lam = (jnp.exp(jnp.sum(lp["lambda_q1"] * lp["lambda_k1"]).astype(jnp.float32))
           - jnp.exp(jnp.sum(lp["lambda_q2"] * lp["lambda_k2"]).astype(jnp.float32)) + lambda_init)

    def attend(qq, pp):
        return diff_attention(qq, k_all, v_all, pp, k_pos, rel_table, lam, lp["subln_g"], lambda_init)

    if S > Q_BLOCK:
        a_out = blocked_queries(attend, q, pos)
    else:
        a_out = attend(q, pos)
    if conv_state is None:
        conv_state = jnp.zeros((B, CONV_W - 1, LRU_W), xb.dtype)
        h0 = jnp.zeros((B, LRU_W), jnp.float32)
    b_out, new_conv, h_last = rglru_branch(xb, gb, conv_state, h0, lp["conv_w"], lp["conv_b"], lp["w_rg_a"],
                                           lp["b_rg_a"], lp["w_rg_x"], lp["b_rg_x"], lp["rg_lambda"])
    c_out = cross_attention(qc.reshape(B, S, XA_HEADS, XA_HD), mem_k.astype(x.dtype), mem_v.astype(x.dtype))
    ya = a_out.reshape(B, S, DA_V_W) @ lp["w_proj_a"]
    yb = b_out @ lp["w_proj_b"]
    yc = c_out.reshape(B, S, XA_W) @ lp["w_proj_c"]
    gates = jax.nn.sigmoid((h @ lp["w_gate"] + lp["b_gate"]).astype(jnp.float32)).astype(x.dtype)
    gates = gates.reshape(B, S, N_BRANCH, D_MODEL)
    merged = gates[:, :, 0] * ya + gates[:, :, 1] * yb + gates[:, :, 2] * yc
    x = x + merged @ lp["w_out"]
    h2 = rms_norm(x, lp["norm_ffn"])
    g_ff, u_ff = jnp.split(h2 @ lp["w_ffn_in"], 2, axis=-1)
    x = x + (jax.nn.silu(g_ff) * u_ff) @ lp["w_ffn_out"]
    return x, k, v, new_conv, h_last


def setup_inputs(seed: int = 0) -> dict:
    key = jax.random.key(seed)
    ks = jax.random.split(key, 40)
    f32 = jnp.float32
    nrm = lambda i, shape, s: jax.random.normal(ks[i], shape, f32) * s
    gain = lambda i, shape: 1.0 + 0.05 * jax.random.normal(ks[i], shape, f32)
    u = jax.random.uniform(ks[39], (DEPTH, LRU_W), f32, 0.9, 0.999)
    s = u ** (1.0 / LRU_C)
    return {
        "x_prompt": nrm(0, (BATCH, SEQ, D_MODEL), 1.0),
        "x_sample": nrm(1, (DEC_BATCH, DEC_SEQ, D_MODEL), 1.0),
        "mem_prompt": nrm(2, (BATCH, N_MEM, D_MODEL), 1.0),
        "cache_k": nrm(3, (DEPTH, DEC_BATCH, PAST_LEN, DA_HEADS, 2, DA_HD), 1.0),
        "cache_v": nrm(4, (DEPTH, DEC_BATCH, PAST_LEN, DA_HEADS, DA_V), 1.0),
        "state_conv": nrm(5, (DEPTH, DEC_BATCH, CONV_W - 1, LRU_W), 1.0),
        "state_lru": nrm(6, (DEPTH, DEC_BATCH, LRU_W), 0.5),
        "cache_mem_k": nrm(7, (DEPTH, DEC_BATCH, N_MEM, XA_HEADS, XA_HD), 1.0),
        "cache_mem_v": nrm(8, (DEPTH, DEC_BATCH, N_MEM, XA_HEADS, XA_HD), 1.0),
        "rel_table": nrm(9, (N_BUCKETS, DA_HEADS), 0.5),
        "norm_mix": gain(10, (DEPTH, D_MODEL)),
        "w_in": nrm(11, (DEPTH, D_MODEL, IN_W), D_MODEL ** -0.5),
        "lambda_q1": nrm(12, (DEPTH, DA_HD), 0.1),
        "lambda_k1": nrm(13, (DEPTH, DA_HD), 0.1),
        "lambda_q2": nrm(14, (DEPTH, DA_HD), 0.1),
        "lambda_k2": nrm(15, (DEPTH, DA_HD), 0.1),
        "subln_g": gain(16, (DEPTH, DA_V)),
        "conv_w": nrm(17, (DEPTH, CONV_W, LRU_W), CONV_W ** -0.5),
        "conv_b": nrm(18, (DEPTH, LRU_W), 0.01),
        "w_rg_a": nrm(19, (DEPTH, LRU_BLOCKS, LRU_BW, LRU_BW), LRU_BW ** -0.5),
        "b_rg_a": nrm(20, (DEPTH, LRU_W), 0.01),
        "w_rg_x": nrm(21, (DEPTH, LRU_BLOCKS, LRU_BW, LRU_BW), LRU_BW ** -0.5),
        "b_rg_x": nrm(22, (DEPTH, LRU_W), 0.01),
        "rg_lambda": jnp.log(s / (1.0 - s)),
        "norm_mem": gain(23, (DEPTH, D_MODEL)),
        "w_mem_kv": nrm(24, (DEPTH, D_MODEL, 2 * XA_W), D_MODEL ** -0.5),
        "w_proj_a": nrm(25, (DEPTH, DA_V_W, D_MODEL), DA_V_W ** -0.5),
        "w_proj_b": nrm(26, (DEPTH, LRU_W, D_MODEL), LRU_W ** -0.5),
        "w_proj_c": nrm(27, (DEPTH, XA_W, D_MODEL), XA_W ** -0.5),
        "w_gate": nrm(28, (DEPTH, D_MODEL, N_BRANCH * D_MODEL), D_MODEL ** -0.5),
        "b_gate": nrm(29, (DEPTH, N_BRANCH * D_MODEL), 0.01),
        "w_out": nrm(30, (DEPTH, D_MODEL, D_MODEL), D_MODEL ** -0.5),
        "norm_ffn": gain(31, (DEPTH, D_MODEL)),
        "w_ffn_in": nrm(32, (DEPTH, D_MODEL, 2 * D_FF), D_MODEL ** -0.5),
        "w_ffn_out": nrm(33, (DEPTH, D_FF, D_MODEL), D_FF ** -0.5),
        "norm_final": gain(34, (D_MODEL,)),
    }


def reference(x_prompt, x_sample, mem_prompt, cache_k, cache_v, state_conv, state_lru, cache_mem_k, cache_mem_v,
              rel_table, norm_mix, w_in, lambda_q1, lambda_k1, lambda_q2, lambda_k2, subln_g, conv_w, conv_b,
              w_rg_a, b_rg_a, w_rg_x, b_rg_x, rg_lambda, norm_mem, w_mem_kv, w_proj_a, w_proj_b, w_proj_c,
              w_gate, b_gate, w_out, norm_ffn, w_ffn_in, w_ffn_out, norm_final):
    past_len = cache_k.shape[2]
    pos_p = jnp.arange(x_prompt.shape[1], dtype=jnp.int32)
    past_pos = jnp.arange(past_len, dtype=jnp.int32)
    pos_s = past_len + jnp.arange(x_sample.shape[1], dtype=jnp.int32)
    xp, xs = x_prompt, x_sample
    kp_l, vp_l, cp_l, hp_l, mkp_l, mvp_l = [], [], [], [], [], []
    ks_l, vs_l, cs_l, hs_l = [], [], [], []
    for l in range(DEPTH):
        lambda_init = 0.8 - 0.6 * math.exp(-0.3 * l)
        lp = dict(norm_mix=norm_mix[l], w_in=w_in[l], lambda_q1=lambda_q1[l], lambda_k1=lambda_k1[l],
                  lambda_q2=lambda_q2[l], lambda_k2=lambda_k2[l], subln_g=subln_g[l], conv_w=conv_w[l],
                  conv_b=conv_b[l], w_rg_a=w_rg_a[l], b_rg_a=b_rg_a[l], w_rg_x=w_rg_x[l], b_rg_x=b_rg_x[l],
                  rg_lambda=rg_lambda[l], w_proj_a=w_proj_a[l], w_proj_b=w_proj_b[l], w_proj_c=w_proj_c[l],
                  w_gate=w_gate[l], b_gate=b_gate[l], w_out=w_out[l], norm_ffn=norm_ffn[l],
                  w_ffn_in=w_ffn_in[l], w_ffn_out=w_ffn_out[l])
        mk, mv = memory_kv(mem_prompt, norm_mem[l], w_mem_kv[l])
        xp, kp, vp, cp, hp = layer_forward(xp, pos_p, None, None, None, None, None, mk, mv, rel_table, lp, lambda_init)
        kp_l.append(kp); vp_l.append(vp); cp_l.append(cp); hp_l.append(hp); mkp_l.append(mk); mvp_l.append(mv)
        xs, ks_, vs_, cs_, hs_ = layer_forward(xs, pos_s, cache_k[l], cache_v[l], past_pos, state_conv[l],
                                               state_lru[l], cache_mem_k[l], cache_mem_v[l], rel_table, lp, lambda_init)
        ks_l.append(ks_); vs_l.append(vs_); cs_l.append(cs_); hs_l.append(hs_)
    y_prompt = rms_norm(xp, norm_final)
    y_sample = rms_norm(xs, norm_final)
    return (y_prompt, y_sample,
            jnp.stack(kp_l), jnp.stack(vp_l), jnp.stack(cp_l), jnp.stack(hp_l), jnp.stack(mkp_l), jnp.stack(mvp_l),
            jnp.stack(ks_l), jnp.stack(vs_l), jnp.stack(cs_l), jnp.stack(hs_l))
```

```python
import functools
import math

import jax
import jax.numpy as jnp
from jax import lax
from jax.experimental import pallas as pl
from jax.experimental.pallas import tpu as pltpu

F32 = jnp.float32
BF16 = jnp.bfloat16

D_MODEL = 1024
CHUNK = 64
CHUNK_SHIFT = 6
DA_HEADS = 8
DA_HD = 64
DA_V = 2 * DA_HD
LRU_W = D_MODEL
LRU_BLOCKS = 8
LRU_BW = LRU_W // LRU_BLOCKS
CONV_W = 4
LRU_C = 8.0
XA_HEADS = 4
XA_HD = 256
N_BUCKETS = 32
MAX_DISTANCE = 128
D_FF = ((8 * D_MODEL + 3 * 256 - 1) // (3 * 256)) * 256
N_BRANCH = 3
EPS = 1e-6
DA_QK_W = DA_HEADS * 2 * DA_HD
DA_V_W = DA_HEADS * DA_V
XA_W = XA_HEADS * XA_HD
IN_OFFS = (0, DA_QK_W, 2 * DA_QK_W, 2 * DA_QK_W + DA_V_W, 2 * DA_QK_W + DA_V_W + LRU_W,
           2 * DA_QK_W + DA_V_W + 2 * LRU_W)
IN_W = 2 * DA_QK_W + DA_V_W + 2 * LRU_W + XA_W
DA_SCALE = DA_HD ** -0.5
XA_SCALE = XA_HD ** -0.5
MASK_VALUE = -1e30
FAR_BUCKET = N_BUCKETS // 2 - 1
FFN_CHUNK = D_FF // 2

V7X_VMEM_BYTES = 64 * 1024 * 1024
VMEM_LIMIT = V7X_VMEM_BYTES - 8 * 1024 * 1024

NT_DIMS = (((1,), (1,)), ((), ()))


def _cparams(*sem):
    return pltpu.CompilerParams(dimension_semantics=sem, vmem_limit_bytes=VMEM_LIMIT)


def _resident(shape):
    nd = len(shape)
    return pl.BlockSpec(shape, lambda *_: (0,) * nd, pipeline_mode=pl.Buffered(1))


def _rms(x, g):
    return x * lax.rsqrt(jnp.mean(x * x, axis=-1, keepdims=True) + EPS) * g


def _dot(a, b):
    return jnp.dot(a, b, preferred_element_type=F32)


def _in_proj_kernel(x_ref, g_ref, w_ref, q_ref, k_ref, v_ref, kb_ref, vb_ref, xb_ref, gb_ref, qc_ref):
    h = _rms(x_ref[...], g_ref[...]).astype(BF16)

    def proj(n, width):
        return _dot(h, w_ref[:, IN_OFFS[n]:IN_OFFS[n] + width])

    q_ref[...] = (proj(0, DA_QK_W) * DA_SCALE).astype(BF16)
    k = proj(1, DA_QK_W)
    k_ref[...] = k
    kb_ref[...] = k.astype(BF16)
    v = proj(2, DA_V_W)
    v_ref[...] = v
    vb_ref[...] = v.astype(BF16)
    xb_ref[...] = proj(3, LRU_W)
    gb_ref[...] = proj(4, LRU_W).astype(BF16)
    qc_ref[...] = (proj(5, XA_W) * XA_SCALE).astype(BF16)


def _in_proj(x2d, g, w_bf16, tm):
    t = x2d.shape[0]
    row = lambda width: pl.BlockSpec((tm, width), lambda i: (i, 0))
    sds = lambda width, dt: jax.ShapeDtypeStruct((t, width), dt)
    return pl.pallas_call(
        _in_proj_kernel,
        grid=(t // tm,),
        in_specs=[row(D_MODEL), _resident((1, D_MODEL)), _resident((D_MODEL, IN_W))],
        out_specs=[row(DA_QK_W), row(DA_QK_W), row(DA_V_W), row(DA_QK_W), row(DA_V_W), row(LRU_W), row(LRU_W),
                   row(XA_W)],
        out_shape=[sds(DA_QK_W, BF16), sds(DA_QK_W, F32), sds(DA_V_W, F32), sds(DA_QK_W, BF16), sds(DA_V_W, BF16),
                   sds(LRU_W, F32), sds(LRU_W, BF16), sds(XA_W, BF16)],
        compiler_params=_cparams("parallel"),
        name="in_proj",
    )(x2d, g, w_bf16)


def _mem_kv_kernel(m_ref, g_ref, w_ref, mk_ref, mv_ref, mkb_ref, mvb_ref):
    h = _rms(m_ref[...], g_ref[...]).astype(BF16)
    mk = _dot(h, w_ref[:, :XA_W])
    mv = _dot(h, w_ref[:, XA_W:])
    mk_ref[...] = mk
    mv_ref[...] = mv
    mkb_ref[...] = mk.astype(BF16)
    mvb_ref[...] = mv.astype(BF16)


def _mem_kv(mem2d, g, w_bf16, tm):
    t = mem2d.shape[0]
    row = pl.BlockSpec((tm, XA_W), lambda i: (i, 0))
    return pl.pallas_call(
        _mem_kv_kernel,
        grid=(t // tm,),
        in_specs=[pl.BlockSpec((tm, D_MODEL), lambda i: (i, 0)), _resident((1, D_MODEL)),
                  _resident((D_MODEL, 2 * XA_W))],
        out_specs=[row, row, row, row],
        out_shape=[jax.ShapeDtypeStruct((t, XA_W), F32), jax.ShapeDtypeStruct((t, XA_W), F32),
                   jax.ShapeDtypeStruct((t, XA_W), BF16), jax.ShapeDtypeStruct((t, XA_W), BF16)],
        compiler_params=_cparams("parallel"),
        name="mem_kv",
    )(mem2d, g, w_bf16)


def _rel_bucket(rel):
    half = N_BUCKETS // 2
    max_exact = half // 2
    n = jnp.abs(rel)
    nf = jnp.maximum(n, 1).astype(F32)
    large = max_exact + (jnp.log(nf / max_exact) / math.log(MAX_DISTANCE / max_exact)
                         * (half - max_exact)).astype(jnp.int32)
    large = jnp.minimum(large, half - 1)
    return jnp.where(rel > 0, half, 0) + jnp.where(n < max_exact, n, large)


def _bias_from_buckets(bkt, tab_ref, head, vis):
    bias = jnp.full(bkt.shape, tab_ref[0, head], F32)
    for t in range(1, N_BUCKETS):
        bias = jnp.where(bkt == t, tab_ref[t, head], bias)
    return jnp.where(vis, bias, MASK_VALUE)


def _split_maps(qh):
    lane = lax.broadcasted_iota(jnp.int32, qh.shape, 1)
    zero = jnp.zeros_like(qh)
    return jnp.concatenate([jnp.where(lane < DA_HD, qh, zero), jnp.where(lane >= DA_HD, qh, zero)], axis=0)


def _diff_finish(o, lam, g, tq, lambda_init):
    d = o[:tq] - lam * o[tq:]
    return _rms(d, g) * (1.0 - lambda_init)


def _attn_self_kernel(tab_ref, lam_ref, q_ref, k_ref, v_ref, bkt_ref, g_ref, o_ref,
                      bias_sc, q2_sc, m_sc, l_sc, acc_sc, *, tq, lambda_init):
    b = pl.program_id(0)
    i = pl.program_id(1)

    @pl.when((b == 0) & (i == 0))
    def _build_bias():
        bkt = bkt_ref[...]
        r = lax.broadcasted_iota(jnp.int32, bkt.shape, 0)
        c = lax.broadcasted_iota(jnp.int32, bkt.shape, 1)
        vis = lax.shift_right_arithmetic(c - tq, CHUNK_SHIFT) <= lax.shift_right_arithmetic(r, CHUNK_SHIFT)
        for h in range(DA_HEADS):
            bias_sc[h] = _bias_from_buckets(bkt, tab_ref, h, vis)

    q = q_ref[...]
    for h in range(DA_HEADS):
        q2_sc[h] = _split_maps(q[:, h * DA_V:(h + 1) * DA_V])
    m_sc[...] = jnp.full(m_sc.shape, MASK_VALUE, F32)
    l_sc[...] = jnp.zeros(l_sc.shape, F32)
    acc_sc[...] = jnp.zeros(acc_sc.shape, F32)

    def key_tile(h, start, bias):
        cols = slice(h * DA_V, (h + 1) * DA_V)
        kj = k_ref[pl.ds(start, tq), cols]
        vj = v_ref[pl.ds(start, tq), cols]
        s = lax.dot_general(q2_sc[h], kj, NT_DIMS, preferred_element_type=F32)
        if bias is None:
            s = s + tab_ref[FAR_BUCKET, h]
        else:
            s = (s.reshape(2, tq, tq) + bias[None]).reshape(2 * tq, tq)
        m_prev = m_sc[h]
        m_new = jnp.maximum(m_prev, jnp.max(s, axis=1, keepdims=True))
        alpha = jnp.exp(m_prev - m_new)
        p = jnp.exp(s - jnp.concatenate([m_new] * (tq // 128), axis=1))
        l_sc[h] = alpha * l_sc[h] + jnp.sum(p, axis=1, keepdims=True)
        acc_sc[h] = alpha * acc_sc[h] + _dot(p.astype(BF16), vj)
        m_sc[h] = m_new

    def far_body(j, carry):
        start = pl.multiple_of(j * tq, tq)
        for h in range(DA_HEADS):
            key_tile(h, start, None)
        return carry

    lax.fori_loop(0, jnp.maximum(i - 1, 0), far_body, 0)

    @pl.when(i >= 1)
    def _prev_tile():
        start = pl.multiple_of((i - 1) * tq, tq)
        for h in range(DA_HEADS):
            key_tile(h, start, bias_sc[h, :, :tq])

    start = pl.multiple_of(i * tq, tq)
    lam = lam_ref[0]
    g = g_ref[...]
    for h in range(DA_HEADS):
        key_tile(h, start, bias_sc[h, :, tq:])
        o = acc_sc[h] / l_sc[h]
        o_ref[:, h * DA_V:(h + 1) * DA_V] = _diff_finish(o, lam, g, tq, lambda_init).astype(BF16)


def _attn_self(q, kb, vb, rel_table, lam, subln_g, batch, seq, lambda_init, tq):
    assert seq % tq == 0 and tq % 128 == 0 and tq >= MAX_DISTANCE and tq % CHUNK == 0
    nq = seq // tq
    rel = jnp.arange(2 * tq, dtype=jnp.int32)[None, :] - (tq + jnp.arange(tq, dtype=jnp.int32))[:, None]
    bkt = _rel_bucket(rel)
    smem = pl.BlockSpec(memory_space=pltpu.SMEM)
    kv_spec = pl.BlockSpec((seq, DA_QK_W), lambda b, i: (b, 0))
    return pl.pallas_call(
        functools.partial(_attn_self_kernel, tq=tq, lambda_init=lambda_init),
        grid=(batch, nq),
        in_specs=[smem, smem,
                  pl.BlockSpec((tq, DA_QK_W), lambda b, i: (b * nq + i, 0)),
                  kv_spec, kv_spec,
                  _resident((tq, 2 * tq)), _resident((1, DA_V))],
        out_specs=pl.BlockSpec((tq, DA_V_W), lambda b, i: (b * nq + i, 0)),
        out_shape=jax.ShapeDtypeStruct((batch * seq, DA_V_W), BF16),
        scratch_shapes=[pltpu.VMEM((DA_HEADS, tq, 2 * tq), F32),
                        pltpu.VMEM((DA_HEADS, 2 * tq, DA_V), BF16),
                        pltpu.VMEM((DA_HEADS, 2 * tq, 128), F32),
                        pltpu.VMEM((DA_HEADS, 2 * tq, 128), F32),
                        pltpu.VMEM((DA_HEADS, 2 * tq, DA_V), F32)],
        compiler_params=_cparams("arbitrary", "arbitrary"),
        name="attn_self",
    )(rel_table, lam, q, kb, vb, bkt, subln_g)


def _attn_cached_kernel(tab_ref, lam_ref, q_ref, ck_ref, cv_ref, nk_ref, nv_ref, bktc_ref, bktn_ref, g_ref, o_ref,
                        biasc_sc, biasn_sc, *, sq, past, lambda_init):
    h = pl.program_id(0)
    b = pl.program_id(1)

    @pl.when(b == 0)
    def _build_bias():
        for bkt_ref, sc, k0 in ((bktc_ref, biasc_sc, 0), (bktn_ref, biasn_sc, past)):
            bkt = bkt_ref[...]
            qpos = past + lax.broadcasted_iota(jnp.int32, bkt.shape, 0)
            kpos = k0 + lax.broadcasted_iota(jnp.int32, bkt.shape, 1)
            vis = lax.shift_right_arithmetic(kpos, CHUNK_SHIFT) <= lax.shift_right_arithmetic(qpos, CHUNK_SHIFT)
            sc[...] = _bias_from_buckets(bkt, tab_ref, h, vis)

    q2 = _split_maps(q_ref[...])
    sc_ = lax.dot_general(q2, ck_ref[...].astype(BF16), NT_DIMS, preferred_element_type=F32)
    sc_ = (sc_.reshape(2, sq, past) + biasc_sc[...][None]).reshape(2 * sq, past)
    sn_ = lax.dot_general(q2, nk_ref[...].astype(BF16), NT_DIMS, preferred_element_type=F32)
    sn_ = (sn_.reshape(2, sq, sq) + biasn_sc[...][None]).reshape(2 * sq, sq)
    m = jnp.maximum(jnp.max(sc_, axis=1, keepdims=True), jnp.max(sn_, axis=1, keepdims=True))
    pc = jnp.exp(sc_ - m)
    pn = jnp.exp(sn_ - m)
    l = jnp.sum(pc, axis=1, keepdims=True) + jnp.sum(pn, axis=1, keepdims=True)
    o = (_dot(pc.astype(BF16), cv_ref[...].astype(BF16)) + _dot(pn.astype(BF16), nv_ref[...].astype(BF16))) / l
    o_ref[...] = _diff_finish(o, lam_ref[0], g_ref[...], sq, lambda_init).astype(BF16)


def _attn_cached(q, cache_k2d, cache_v2d, nk, nv, rel_table, lam, subln_g, batch, sq, past, lambda_init):
    qpos = past + jnp.arange(sq, dtype=jnp.int32)
    bkt_c = _rel_bucket(jnp.arange(past, dtype=jnp.int32)[None, :] - qpos[:, None])
    bkt_n = _rel_bucket(qpos[None, :] - qpos[:, None])
    smem = pl.BlockSpec(memory_space=pltpu.SMEM)
    new_spec = pl.BlockSpec((sq, DA_V), lambda h, b: (b, h))
    cache_spec = pl.BlockSpec((past, DA_V), lambda h, b: (b, h))
    return pl.pallas_call(
        functools.partial(_attn_cached_kernel, sq=sq, past=past, lambda_init=lambda_init),
        grid=(DA_HEADS, batch),
        in_specs=[smem, smem, new_spec, cache_spec, cache_spec, new_spec, new_spec,
                  _resident((sq, past)), _resident((sq, sq)), _resident((1, DA_V))],
        out_specs=new_spec,
        out_shape=jax.ShapeDtypeStruct((batch * sq, DA_V_W), BF16),
        scratch_shapes=[pltpu.VMEM((sq, past), F32), pltpu.VMEM((sq, sq), F32)],
        compiler_params=_cparams("arbitrary", "arbitrary"),
        name="attn_cached",
    )(rel_table, lam, q, cache_k2d, cache_v2d, nk, nv, bkt_c, bkt_n, subln_g)


def _rglru_kernel(xb_ref, gb_ref, cs_ref, h0_ref, cw_ref, cb_ref, wa_ref, ba_ref, wx_ref, bx_ref, lam_ref,
                  out_ref, nc_ref, hl_ref, xpad_sc, a_sc, u_sc, hs_sc, hc_sc, *, ts):
    halo = CONV_W - 1
    base = 8 - halo

    @pl.when(pl.program_id(1) == 0)
    def _load_state():
        xpad_sc[base:8, :] = cs_ref[0]
        hc_sc[...] = h0_ref[0]

    xpad_sc[8:8 + ts, :] = xb_ref[...]
    cw = cw_ref[...]
    xc = xpad_sc[base:base + ts, :] * cw[0:1]
    for j in range(1, CONV_W):
        xc = xc + xpad_sc[base + j:base + j + ts, :] * cw[j:j + 1]
    xc = cb_ref[...] + xc
    tail = xpad_sc[base + ts:8 + ts, :]
    nc_ref[0] = tail
    xpad_sc[base:8, :] = tail

    xcb = xc.astype(BF16)
    sp = jax.nn.softplus(-lam_ref[...])
    for n in range(LRU_BLOCKS):
        cols = slice(n * LRU_BW, (n + 1) * LRU_BW)
        r = jax.nn.sigmoid(_dot(xcb[:, cols], wa_ref[n]) + ba_ref[:, cols])
        gate = jax.nn.sigmoid(_dot(xcb[:, cols], wx_ref[n]) + bx_ref[:, cols])
        log_a = -LRU_C * r * sp[:, cols]
        a = jnp.exp(log_a)
        a_sc[:, cols] = a
        u_sc[:, cols] = jnp.sqrt(-jnp.tanh(log_a) * (a * a + 1.0)) * (gate * xc[:, cols])

    def step(t, h):
        h = a_sc[pl.ds(t, 1), :] * h + u_sc[pl.ds(t, 1), :]
        hs_sc[pl.ds(t, 1), :] = h
        return h

    h = lax.fori_loop(0, ts, step, hc_sc[...], unroll=8)
    hc_sc[...] = h
    hl_ref[0] = h
    out_ref[...] = (hs_sc[...] * jax.nn.gelu(gb_ref[...].astype(F32))).astype(BF16)


def _rglru(xb, gb, conv_state, h0, conv_w, conv_b, wa, ba, wx, bx, rg_lambda, batch, seq, ts):
    assert seq % ts == 0 and ts >= CONV_W - 1
    nt = seq // ts
    row = pl.BlockSpec((ts, LRU_W), lambda b, i: (b * nt + i, 0))
    vec = _resident((1, LRU_W))
    gate_w = _resident((LRU_BLOCKS, LRU_BW, LRU_BW))
    return pl.pallas_call(
        functools.partial(_rglru_kernel, ts=ts),
        grid=(batch, nt),
        in_specs=[row, row,
                  pl.BlockSpec((1, CONV_W - 1, LRU_W), lambda b, i: (b, 0, 0)),
                  pl.BlockSpec((1, 1, LRU_W), lambda b, i: (b, 0, 0)),
                  _resident((CONV_W, LRU_W)), vec, gate_w, vec, gate_w, vec, vec],
        out_specs=[row,
                   pl.BlockSpec((1, CONV_W - 1, LRU_W), lambda b, i: (b, 0, 0)),
                   pl.BlockSpec((1, 1, LRU_W), lambda b, i: (b, 0, 0))],
        out_shape=[jax.ShapeDtypeStruct((batch * seq, LRU_W), BF16),
                   jax.ShapeDtypeStruct((batch, CONV_W - 1, LRU_W), F32),
                   jax.ShapeDtypeStruct((batch, 1, LRU_W), F32)],
        scratch_shapes=[pltpu.VMEM((8 + ts, LRU_W), F32), pltpu.VMEM((ts, LRU_W), F32),
                        pltpu.VMEM((ts, LRU_W), F32), pltpu.VMEM((ts, LRU_W), F32), pltpu.VMEM((1, LRU_W), F32)],
        compiler_params=_cparams("arbitrary", "arbitrary"),
        name="rglru",
    )(xb, gb, conv_state, h0, conv_w, conv_b, wa, ba, wx, bx, rg_lambda)


def _xattn_kernel(q_ref, mk_ref, mv_ref, o_ref):
    q = q_ref[...]
    for h in range(XA_HEADS):
        cols = slice(h * XA_HD, (h + 1) * XA_HD)
        s = lax.dot_general(q[:, cols], mk_ref[:, cols].astype(BF16), NT_DIMS, preferred_element_type=F32)
        p = jnp.exp(s - jnp.max(s, axis=1, keepdims=True))
        l = jnp.sum(p, axis=1, keepdims=True)
        o_ref[:, cols] = (_dot(p.astype(BF16), mv_ref[:, cols].astype(BF16)) / l).astype(BF16)


def _xattn(qc, mk2d, mv2d, batch, seq, tq):
    nq = seq // tq
    n_mem = mk2d.shape[0] // batch
    row = pl.BlockSpec((tq, XA_W), lambda b, i: (b * nq + i, 0))
    mem = pl.BlockSpec((n_mem, XA_W), lambda b, i: (b, 0))
    return pl.pallas_call(
        _xattn_kernel,
        grid=(batch, nq),
        in_specs=[row, mem, mem],
        out_specs=row,
        out_shape=jax.ShapeDtypeStruct((batch * seq, XA_W), BF16),
        compiler_params=_cparams("parallel", "parallel"),
        name="xattn",
    )(qc, mk2d, mv2d)


def _mix_ffn_kernel(x_ref, a_ref, b_ref, c_ref, gmix_ref, wg_ref, bg_ref, wpa_ref, wpb_ref, wpc_ref, wo_ref,
                    gffn_ref, wfi_ref, wfo_ref, gfin_ref, y_ref):
    x = x_ref[...]
    h = _rms(x, gmix_ref[...]).astype(BF16)
    merged = None
    for n, (br_ref, wp_ref) in enumerate(((a_ref, wpa_ref), (b_ref, wpb_ref), (c_ref, wpc_ref))):
        cols = slice(n * D_MODEL, (n + 1) * D_MODEL)
        gate = jax.nn.sigmoid(_dot(h, wg_ref[:, cols]) + bg_ref[:, cols])
        term = gate * _dot(br_ref[...], wp_ref[...])
        merged = term if merged is None else merged + term
    x1 = x + _dot(merged.astype(BF16), wo_ref[...])
    h2 = _rms(x1, gffn_ref[...]).astype(BF16)
    ff = None
    for c in range(D_FF // FFN_CHUNK):
        gcols = slice(c * FFN_CHUNK, (c + 1) * FFN_CHUNK)
        ucols = slice(D_FF + c * FFN_CHUNK, D_FF + (c + 1) * FFN_CHUNK)
        act = (jax.nn.silu(_dot(h2, wfi_ref[:, gcols])) * _dot(h2, wfi_ref[:, ucols])).astype(BF16)
        term = _dot(act, wfo_ref[gcols, :])
        ff = term if ff is None else ff + term
    y_ref[...] = _rms(x1 + ff, gfin_ref[...])


def _mix_ffn(x2d, a_out, b_out, c_out, p, tm):
    t = x2d.shape[0]
    row = pl.BlockSpec((tm, D_MODEL), lambda i: (i, 0))
    vec = _resident((1, D_MODEL))
    sq_w = _resident((D_MODEL, D_MODEL))
    return pl.pallas_call(
        _mix_ffn_kernel,
        grid=(t // tm,),
        in_specs=[row, row, row, row, vec, _resident((D_MODEL, N_BRANCH * D_MODEL)),
                  _resident((1, N_BRANCH * D_MODEL)), sq_w, sq_w, sq_w, sq_w, vec,
                  _resident((D_MODEL, 2 * D_FF)), _resident((D_FF, D_MODEL)), vec],
        out_specs=row,
        out_shape=jax.ShapeDtypeStruct((t, D_MODEL), F32),
        compiler_params=_cparams("parallel"),
        name="mix_ffn",
    )(x2d, a_out, b_out, c_out, p["norm_mix"], p["w_gate"], p["b_gate"], p["w_proj_a"], p["w_proj_b"],
      p["w_proj_c"], p["w_out"], p["norm_ffn"], p["w_ffn_in"], p["w_ffn_out"], p["norm_final"])


def _layer(x, mk2d, mv2d, cache_k2d, cache_v2d, conv_state, h0, rel_table, lam, p, lambda_init, tm, tq, ts):
    batch, seq, _ = x.shape
    x2d = x.reshape(batch * seq, D_MODEL)
    q, k, v, kb, vb, xb, gb, qc = _in_proj(x2d, p["norm_mix"], p["w_in"], tm)
    if cache_k2d is None:
        a_out = _attn_self(q, kb, vb, rel_table, lam, p["subln_g"], batch, seq, lambda_init, tq)
    else:
        past = cache_k2d.shape[0] // batch
        a_out = _attn_cached(q, cache_k2d, cache_v2d, k, v, rel_table, lam, p["subln_g"], batch, seq, past,
                             lambda_init)
    b_out, new_conv, h_last = _rglru(xb, gb, conv_state, h0, p["conv_w"], p["conv_b"], p["w_rg_a"], p["b_rg_a"],
                                     p["w_rg_x"], p["b_rg_x"], p["rg_lambda"], batch, seq, ts)
    c_out = _xattn(qc, mk2d, mv2d, batch, seq, min(seq, tm))
    y = _mix_ffn(x2d, a_out, b_out, c_out, p, tm)
    return y.reshape(batch, seq, D_MODEL), k, v, new_conv, h_last.reshape(batch, LRU_W)


def kernel(x_prompt, x_sample, mem_prompt, cache_k, cache_v, state_conv, state_lru, cache_mem_k, cache_mem_v,
           rel_table, norm_mix, w_in, lambda_q1, lambda_k1, lambda_q2, lambda_k2, subln_g, conv_w, conv_b,
           w_rg_a, b_rg_a, w_rg_x, b_rg_x, rg_lambda, norm_mem, w_mem_kv, w_proj_a, w_proj_b, w_proj_c,
           w_gate, b_gate, w_out, norm_ffn, w_ffn_in, w_ffn_out, norm_final):
    depth = w_in.shape[0]
    assert depth == 1, "the final norm is fused into the layer's last kernel"
    l = 0
    lambda_init = 0.8 - 0.6 * math.exp(-0.3 * l)
    bp, sp_, _ = x_prompt.shape
    bs, ss, _ = x_sample.shape
    past = cache_k.shape[2]
    n_mem = mem_prompt.shape[1]
    row = lambda a: a.reshape(1, -1).astype(F32)
    p = dict(norm_mix=row(norm_mix[l]), w_in=w_in[l].astype(BF16), subln_g=row(subln_g[l]),
             conv_w=conv_w[l], conv_b=row(conv_b[l]), w_rg_a=w_rg_a[l].astype(BF16), b_rg_a=row(b_rg_a[l]),
             w_rg_x=w_rg_x[l].astype(BF16), b_rg_x=row(b_rg_x[l]), rg_lambda=row(rg_lambda[l]),
             w_proj_a=w_proj_a[l].astype(BF16), w_proj_b=w_proj_b[l].astype(BF16),
             w_proj_c=w_proj_c[l].astype(BF16), w_gate=w_gate[l].astype(BF16), b_gate=row(b_gate[l]),
             w_out=w_out[l].astype(BF16), norm_ffn=row(norm_ffn[l]), w_ffn_in=w_ffn_in[l].astype(BF16),
             w_ffn_out=w_ffn_out[l].astype(BF16), norm_final=row(norm_final))
    lam = (jnp.exp(jnp.sum(lambda_q1[l] * lambda_k1[l]).astype(F32))
           - jnp.exp(jnp.sum(lambda_q2[l] * lambda_k2[l]).astype(F32)) + lambda_init).reshape(1)

    mk, mv, mkb, mvb = _mem_kv(mem_prompt.reshape(bp * n_mem, D_MODEL), row(norm_mem[l]),
                               w_mem_kv[l].astype(BF16), 256)
    zeros_conv = jnp.zeros((bp, CONV_W - 1, LRU_W), F32)
    zeros_h = jnp.zeros((bp, 1, LRU_W), F32)
    yp, kp, vp, cp, hp = _layer(x_prompt, mkb, mvb, None, None, zeros_conv, zeros_h, rel_table, lam, p,
                                lambda_init, tm=512, tq=256, ts=256)
    ys, ks, vs, cs, hs = _layer(x_sample, cache_mem_k[l].reshape(bs * n_mem, XA_W),
                                cache_mem_v[l].reshape(bs * n_mem, XA_W),
                                cache_k[l].reshape(bs * past, DA_QK_W), cache_v[l].reshape(bs * past, DA_V_W),
                                state_conv[l], state_lru[l].reshape(bs, 1, LRU_W), rel_table, lam, p,
                                lambda_init, tm=256, tq=ss, ts=ss)
    return (yp, ys,
            kp.reshape(1, bp, sp_, DA_HEADS, 2, DA_HD), vp.reshape(1, bp, sp_, DA_HEADS, DA_V),
            cp[None], hp[None],
            mk.reshape(1, bp, n_mem, XA_HEADS, XA_HD), mv.reshape(1, bp, n_mem, XA_HEADS, XA_HD),
            ks.reshape(1, bs, ss, DA_HEADS, 2, DA_HD), vs.reshape(1, bs, ss, DA_HEADS, DA_V),
            cs[None], hs[None])
```

```python
import functools
import math

import jax
import jax.numpy as jnp
from jax import lax
from jax.experimental import pallas as pl
from jax.experimental.pallas import tpu as pltpu

F32 = jnp.float32
BF16 = jnp.bfloat16

D_MODEL = 1024
CHUNK = 64
CHUNK_SHIFT = 6
DA_HEADS = 8
DA_HD = 64
DA_V = 2 * DA_HD
LRU_W = D_MODEL
LRU_BLOCKS = 8
LRU_BW = LRU_W // LRU_BLOCKS
CONV_W = 4
LRU_C = 8.0
XA_HEADS = 4
XA_HD = 256
N_BUCKETS = 32
MAX_DISTANCE = 128
D_FF = ((8 * D_MODEL + 3 * 256 - 1) // (3 * 256)) * 256
N_BRANCH = 3
EPS = 1e-6
DA_QK_W = DA_HEADS * 2 * DA_HD
DA_V_W = DA_HEADS * DA_V
XA_W = XA_HEADS * XA_HD
IN_OFFS = (0, DA_QK_W, 2 * DA_QK_W, 2 * DA_QK_W + DA_V_W, 2 * DA_QK_W + DA_V_W + LRU_W,
           2 * DA_QK_W + DA_V_W + 2 * LRU_W)
IN_W = 2 * DA_QK_W + DA_V_W + 2 * LRU_W + XA_W
DA_SCALE = DA_HD ** -0.5
XA_SCALE = XA_HD ** -0.5
MASK_VALUE = -1e30
FAR_BUCKET = N_BUCKETS // 2 - 1
FFN_CHUNK = D_FF // 2

V7X_VMEM_BYTES = 64 * 1024 * 1024
VMEM_LIMIT = V7X_VMEM_BYTES - 8 * 1024 * 1024

NT_DIMS = (((1,), (1,)), ((), ()))


def _cparams(*sem):
    return pltpu.CompilerParams(dimension_semantics=sem, vmem_limit_bytes=VMEM_LIMIT)


def _resident(shape):
    nd = len(shape)
    return pl.BlockSpec(shape, lambda *_: (0,) * nd, pipeline_mode=pl.Buffered(1))


def _rms(x, g):
    return x * lax.rsqrt(jnp.mean(x * x, axis=-1, keepdims=True) + EPS) * g


def _dot(a, b):
    return jnp.dot(a, b, preferred_element_type=F32)


def _in_proj_kernel(x_ref, g_ref, w_ref, wkt_ref, q_ref, k_ref, v_ref, kb_ref, vb_ref, xb_ref, gb_ref, qc_ref, *,
                    k_feature_major):
    h = _rms(x_ref[...], g_ref[...]).astype(BF16)

    def proj(n, width):
        return _dot(h, w_ref[:, IN_OFFS[n]:IN_OFFS[n] + width])

    q_ref[...] = (proj(0, DA_QK_W) * DA_SCALE).astype(BF16)
    if k_feature_major:
        k = lax.dot_general(wkt_ref[...], h, NT_DIMS, preferred_element_type=F32)
    else:
        k = proj(1, DA_QK_W)
    k_ref[...] = k
    kb_ref[...] = k.astype(BF16)
    v = proj(2, DA_V_W)
    v_ref[...] = v
    vb_ref[...] = v.astype(BF16)
    xb_ref[...] = proj(3, LRU_W)
    gb_ref[...] = proj(4, LRU_W).astype(BF16)
    qc_ref[...] = (proj(5, XA_W) * XA_SCALE).astype(BF16)


def _in_proj(x2d, g, w_bf16, wkt_bf16, tm, seq, k_feature_major):
    t = x2d.shape[0]
    nt = seq // tm
    row = lambda width: pl.BlockSpec((tm, width), lambda i: (i, 0))
    sds = lambda width, dt: jax.ShapeDtypeStruct((t, width), dt)
    if k_feature_major:
        assert seq % tm == 0
        k_spec = pl.BlockSpec((DA_QK_W, tm), lambda i: (i // nt, i % nt))
        k_sds = lambda dt: jax.ShapeDtypeStruct((t // seq * DA_QK_W, seq), dt)
    else:
        k_spec = row(DA_QK_W)
        k_sds = lambda dt: sds(DA_QK_W, dt)
    return pl.pallas_call(
        functools.partial(_in_proj_kernel, k_feature_major=k_feature_major),
        grid=(t // tm,),
        in_specs=[row(D_MODEL), _resident((1, D_MODEL)), _resident((D_MODEL, IN_W)),
                  _resident((DA_QK_W, D_MODEL))],
        out_specs=[row(DA_QK_W), k_spec, row(DA_V_W), k_spec, row(DA_V_W), row(LRU_W), row(LRU_W), row(XA_W)],
        out_shape=[sds(DA_QK_W, BF16), k_sds(F32), sds(DA_V_W, F32), k_sds(BF16), sds(DA_V_W, BF16),
                   sds(LRU_W, F32), sds(LRU_W, BF16), sds(XA_W, BF16)],
        compiler_params=_cparams("parallel"),
        name="in_proj",
    )(x2d, g, w_bf16, wkt_bf16)


def _mem_kv_kernel(m_ref, g_ref, w_ref, mk_ref, mv_ref, mkb_ref, mvb_ref):
    h = _rms(m_ref[...], g_ref[...]).astype(BF16)
    mk = _dot(h, w_ref[:, :XA_W])
    mv = _dot(h, w_ref[:, XA_W:])
    mk_ref[...] = mk
    mv_ref[...] = mv
    mkb_ref[...] = mk.astype(BF16)
    mvb_ref[...] = mv.astype(BF16)


def _mem_kv(mem2d, g, w_bf16, tm):
    t = mem2d.shape[0]
    row = pl.BlockSpec((tm, XA_W), lambda i: (i, 0))
    return pl.pallas_call(
        _mem_kv_kernel,
        grid=(t // tm,),
        in_specs=[pl.BlockSpec((tm, D_MODEL), lambda i: (i, 0)), _resident((1, D_MODEL)),
                  _resident((D_MODEL, 2 * XA_W))],
        out_specs=[row, row, row, row],
        out_shape=[jax.ShapeDtypeStruct((t, XA_W), F32), jax.ShapeDtypeStruct((t, XA_W), F32),
                   jax.ShapeDtypeStruct((t, XA_W), BF16), jax.ShapeDtypeStruct((t, XA_W), BF16)],
        compiler_params=_cparams("parallel"),
        name="mem_kv",
    )(mem2d, g, w_bf16)


def _rel_bucket(rel):
    half = N_BUCKETS // 2
    max_exact = half // 2
    n = jnp.abs(rel)
    nf = jnp.maximum(n, 1).astype(F32)
    large = max_exact + (jnp.log(nf / max_exact) / math.log(MAX_DISTANCE / max_exact)
                         * (half - max_exact)).astype(jnp.int32)
    large = jnp.minimum(large, half - 1)
    return jnp.where(rel > 0, half, 0) + jnp.where(n < max_exact, n, large)


def _bias_from_buckets(bkt, tab_ref, head, vis):
    bias = jnp.full(bkt.shape, tab_ref[0, head], F32)
    for t in range(1, N_BUCKETS):
        bias = jnp.where(bkt == t, tab_ref[t, head], bias)
    return jnp.where(vis, bias, MASK_VALUE)


def _split_maps(qh):
    lane = lax.broadcasted_iota(jnp.int32, qh.shape, 1)
    zero = jnp.zeros_like(qh)
    return jnp.concatenate([jnp.where(lane < DA_HD, qh, zero), jnp.where(lane >= DA_HD, qh, zero)], axis=0)


def _diff_finish(o, lam, g, tq, lambda_init):
    d = o[:tq] - lam * o[tq:]
    return _rms(d, g) * (1.0 - lambda_init)


def _attn_self_kernel(tab_ref, lam_ref, q_ref, kt_ref, v_ref, bkt_ref, g_ref, o_ref,
                      bias_sc, q2_sc, m_sc, l_sc, acc_sc, *, tq, lambda_init):
    b = pl.program_id(0)
    i = pl.program_id(1)

    @pl.when((b == 0) & (i == 0))
    def _build_bias():
        bkt = bkt_ref[...]
        r = lax.broadcasted_iota(jnp.int32, bkt.shape, 0)
        c = lax.broadcasted_iota(jnp.int32, bkt.shape, 1)
        vis = lax.shift_right_arithmetic(c - tq, CHUNK_SHIFT) <= lax.shift_right_arithmetic(r, CHUNK_SHIFT)
        for h in range(DA_HEADS):
            bias_sc[h] = _bias_from_buckets(bkt, tab_ref, h, vis)

    q = q_ref[...]
    for h in range(DA_HEADS):
        q2_sc[h] = _split_maps(q[:, h * DA_V:(h + 1) * DA_V])
    m_sc[...] = jnp.full(m_sc.shape, MASK_VALUE, F32)
    l_sc[...] = jnp.zeros(l_sc.shape, F32)
    acc_sc[...] = jnp.zeros(acc_sc.shape, F32)

    def key_tile(h, start, bias):
        cols = slice(h * DA_V, (h + 1) * DA_V)
        kj = kt_ref[cols, pl.ds(start, tq)]
        vj = v_ref[pl.ds(start, tq), cols]
        s = _dot(q2_sc[h], kj)
        if bias is None:
            s = s + tab_ref[FAR_BUCKET, h]
        else:
            s = (s.reshape(2, tq, tq) + bias[None]).reshape(2 * tq, tq)
        m_prev = m_sc[h]
        m_new = jnp.maximum(m_prev, jnp.max(s, axis=1, keepdims=True))
        alpha = jnp.exp(m_prev - m_new)
        p = jnp.exp(s - jnp.concatenate([m_new] * (tq // 128), axis=1))
        l_sc[h] = alpha * l_sc[h] + jnp.sum(p, axis=1, keepdims=True)
        acc_sc[h] = alpha * acc_sc[h] + _dot(p.astype(BF16), vj)
        m_sc[h] = m_new

    def far_body(j, carry):
        start = pl.multiple_of(j * tq, tq)
        for h in range(DA_HEADS):
            key_tile(h, start, None)
        return carry

    lax.fori_loop(0, jnp.maximum(i - 1, 0), far_body, 0)

    @pl.when(i >= 1)
    def _prev_tile():
        start = pl.multiple_of((i - 1) * tq, tq)
        for h in range(DA_HEADS):
            key_tile(h, start, bias_sc[h, :, :tq])

    start = pl.multiple_of(i * tq, tq)
    lam = lam_ref[0]
    g = g_ref[...]
    for h in range(DA_HEADS):
        key_tile(h, start, bias_sc[h, :, tq:])
        o = acc_sc[h] / l_sc[h]
        o_ref[:, h * DA_V:(h + 1) * DA_V] = _diff_finish(o, lam, g, tq, lambda_init).astype(BF16)


def _attn_self(q, ktb, vb, rel_table, lam, subln_g, batch, seq, lambda_init, tq):
    assert seq % tq == 0 and tq % 128 == 0 and tq >= MAX_DISTANCE and tq % CHUNK == 0
    nq = seq // tq
    rel = jnp.arange(2 * tq, dtype=jnp.int32)[None, :] - (tq + jnp.arange(tq, dtype=jnp.int32))[:, None]
    bkt = _rel_bucket(rel)
    smem = pl.BlockSpec(memory_space=pltpu.SMEM)
    return pl.pallas_call(
        functools.partial(_attn_self_kernel, tq=tq, lambda_init=lambda_init),
        grid=(batch, nq),
        in_specs=[smem, smem,
                  pl.BlockSpec((tq, DA_QK_W), lambda b, i: (b * nq + i, 0)),
                  pl.BlockSpec((DA_QK_W, seq), lambda b, i: (b, 0)),
                  pl.BlockSpec((seq, DA_V_W), lambda b, i: (b, 0)),
                  _resident((tq, 2 * tq)), _resident((1, DA_V))],
        out_specs=pl.BlockSpec((tq, DA_V_W), lambda b, i: (b * nq + i, 0)),
        out_shape=jax.ShapeDtypeStruct((batch * seq, DA_V_W), BF16),
        scratch_shapes=[pltpu.VMEM((DA_HEADS, tq, 2 * tq), F32),
                        pltpu.VMEM((DA_HEADS, 2 * tq, DA_V), BF16),
                        pltpu.VMEM((DA_HEADS, 2 * tq, 128), F32),
                        pltpu.VMEM((DA_HEADS, 2 * tq, 128), F32),
                        pltpu.VMEM((DA_HEADS, 2 * tq, DA_V), F32)],
        compiler_params=_cparams("arbitrary", "arbitrary"),
        name="attn_self",
    )(rel_table, lam, q, ktb, vb, bkt, subln_g)


def _attn_cached_kernel(tab_ref, lam_ref, q_ref, ckt_ref, cv_ref, nk_ref, nv_ref, bktc_ref, bktn_ref, g_ref, o_ref,
                        biasc_sc, biasn_sc, *, sq, past, lambda_init):
    @pl.when(pl.program_id(0) == 0)
    def _build_bias():
        for bkt_ref, sc, k0 in ((bktc_ref, biasc_sc, 0), (bktn_ref, biasn_sc, past)):
            bkt = bkt_ref[...]
            qpos = past + lax.broadcasted_iota(jnp.int32, bkt.shape, 0)
            kpos = k0 + lax.broadcasted_iota(jnp.int32, bkt.shape, 1)
            vis = lax.shift_right_arithmetic(kpos, CHUNK_SHIFT) <= lax.shift_right_arithmetic(qpos, CHUNK_SHIFT)
            for h in range(DA_HEADS):
                sc[h] = _bias_from_buckets(bkt, tab_ref, h, vis)

    q = q_ref[...]
    lam = lam_ref[0]
    g = g_ref[...]
    for h in range(DA_HEADS):
        cols = slice(h * DA_V, (h + 1) * DA_V)
        q2 = _split_maps(q[:, cols])
        s_c = _dot(q2, ckt_ref[cols, :].astype(BF16))
        s_c = (s_c.reshape(2, sq, past) + biasc_sc[h][None]).reshape(2 * sq, past)
        s_n = lax.dot_general(q2, nk_ref[:, cols].astype(BF16), NT_DIMS, preferred_element_type=F32)
        s_n = (s_n.reshape(2, sq, sq) + biasn_sc[h][None]).reshape(2 * sq, sq)
        m = jnp.maximum(jnp.max(s_c, axis=1, keepdims=True), jnp.max(s_n, axis=1, keepdims=True))
        p_c = jnp.exp(s_c - m)
        p_n = jnp.exp(s_n - m)
        l = jnp.sum(p_c, axis=1, keepdims=True) + jnp.sum(p_n, axis=1, keepdims=True)
        o = (_dot(p_c.astype(BF16), cv_ref[:, h, :].astype(BF16))
             + _dot(p_n.astype(BF16), nv_ref[:, cols].astype(BF16))) / l
        o_ref[:, cols] = _diff_finish(o, lam, g, sq, lambda_init).astype(BF16)


def _attn_cached(q, cache_kt2d, cache_v3d, nk, nv, rel_table, lam, subln_g, batch, sq, past, lambda_init):
    qpos = past + jnp.arange(sq, dtype=jnp.int32)
    bkt_c = _rel_bucket(jnp.arange(past, dtype=jnp.int32)[None, :] - qpos[:, None])
    bkt_n = _rel_bucket(qpos[None, :] - qpos[:, None])
    smem = pl.BlockSpec(memory_space=pltpu.SMEM)
    new_spec = pl.BlockSpec((sq, DA_QK_W), lambda b: (b, 0))
    return pl.pallas_call(
        functools.partial(_attn_cached_kernel, sq=sq, past=past, lambda_init=lambda_init),
        grid=(batch,),
        in_specs=[smem, smem, new_spec,
                  pl.BlockSpec((DA_QK_W, past), lambda b: (b, 0)),
                  pl.BlockSpec((past, DA_HEADS, DA_V), lambda b: (b, 0, 0)),
                  new_spec, new_spec,
                  _resident((sq, past)), _resident((sq, sq)), _resident((1, DA_V))],
        out_specs=new_spec,
        out_shape=jax.ShapeDtypeStruct((batch * sq, DA_V_W), BF16),
        scratch_shapes=[pltpu.VMEM((DA_HEADS, sq, past), F32), pltpu.VMEM((DA_HEADS, sq, sq), F32)],
        compiler_params=_cparams("arbitrary"),
        name="attn_cached",
    )(rel_table, lam, q, cache_kt2d, cache_v3d, nk, nv, bkt_c, bkt_n, subln_g)


def _rglru_kernel(xb_ref, gb_ref, cs_ref, h0_ref, cw_ref, cb_ref, wa_ref, ba_ref, wx_ref, bx_ref, lam_ref,
                  out_ref, nc_ref, hl_ref, xpad_sc, a_sc, u_sc, hs_sc, hc_sc, *, ts):
    halo = CONV_W - 1
    base = 8 - halo

    @pl.when(pl.program_id(1) == 0)
    def _load_state():
        xpad_sc[base:8, :] = cs_ref[0]
        hc_sc[...] = h0_ref[0]

    xpad_sc[8:8 + ts, :] = xb_ref[...]
    cw = cw_ref[...]
    xc = xpad_sc[base:base + ts, :] * cw[0:1]
    for j in range(1, CONV_W):
        xc = xc + xpad_sc[base + j:base + j + ts, :] * cw[j:j + 1]
    xc = cb_ref[...] + xc
    tail = xpad_sc[base + ts:8 + ts, :]
    nc_ref[0] = tail
    xpad_sc[base:8, :] = tail

    xcb = xc.astype(BF16)
    sp = jax.nn.softplus(-lam_ref[...])
    for n in range(LRU_BLOCKS):
        cols = slice(n * LRU_BW, (n + 1) * LRU_BW)
        r = jax.nn.sigmoid(_dot(xcb[:, cols], wa_ref[n]) + ba_ref[:, cols])
        gate = jax.nn.sigmoid(_dot(xcb[:, cols], wx_ref[n]) + bx_ref[:, cols])
        log_a = -LRU_C * r * sp[:, cols]
        a = jnp.exp(log_a)
        a_sc[:, cols] = a
        u_sc[:, cols] = jnp.sqrt(-jnp.tanh(log_a) * (a * a + 1.0)) * (gate * xc[:, cols])

    def step(t, h):
        h = a_sc[pl.ds(t, 1), :] * h + u_sc[pl.ds(t, 1), :]
        hs_sc[pl.ds(t, 1), :] = h
        return h

    h = lax.fori_loop(0, ts, step, hc_sc[...], unroll=8)
    hc_sc[...] = h
    hl_ref[0] = h
    out_ref[...] = (hs_sc[...] * jax.nn.gelu(gb_ref[...].astype(F32))).astype(BF16)


def _rglru(xb, gb, conv_state, h0, conv_w, conv_b, wa, ba, wx, bx, rg_lambda, batch, seq, ts):
    assert seq % ts == 0 and ts >= CONV_W - 1
    nt = seq // ts
    row = pl.BlockSpec((ts, LRU_W), lambda b, i: (b * nt + i, 0))
    vec = _resident((1, LRU_W))
    gate_w = _resident((LRU_BLOCKS, LRU_BW, LRU_BW))
    return pl.pallas_call(
        functools.partial(_rglru_kernel, ts=ts),
        grid=(batch, nt),
        in_specs=[row, row,
                  pl.BlockSpec((1, CONV_W - 1, LRU_W), lambda b, i: (b, 0, 0)),
                  pl.BlockSpec((1, 1, LRU_W), lambda b, i: (b, 0, 0)),
                  _resident((CONV_W, LRU_W)), vec, gate_w, vec, gate_w, vec, vec],
        out_specs=[row,
                   pl.BlockSpec((1, CONV_W - 1, LRU_W), lambda b, i: (b, 0, 0)),
                   pl.BlockSpec((1, 1, LRU_W), lambda b, i: (b, 0, 0))],
        out_shape=[jax.ShapeDtypeStruct((batch * seq, LRU_W), BF16),
                   jax.ShapeDtypeStruct((batch, CONV_W - 1, LRU_W), F32),
                   jax.ShapeDtypeStruct((batch, 1, LRU_W), F32)],
        scratch_shapes=[pltpu.VMEM((8 + ts, LRU_W), F32), pltpu.VMEM((ts, LRU_W), F32),
                        pltpu.VMEM((ts, LRU_W), F32), pltpu.VMEM((ts, LRU_W), F32), pltpu.VMEM((1, LRU_W), F32)],
        compiler_params=_cparams("arbitrary", "arbitrary"),
        name="rglru",
    )(xb, gb, conv_state, h0, conv_w, conv_b, wa, ba, wx, bx, rg_lambda)


def _xattn_kernel(q_ref, mk_ref, mv_ref, o_ref):
    q = q_ref[...]
    for h in range(XA_HEADS):
        cols = slice(h * XA_HD, (h + 1) * XA_HD)
        s = lax.dot_general(q[:, cols], mk_ref[:, cols].astype(BF16), NT_DIMS, preferred_element_type=F32)
        p = jnp.exp(s - jnp.max(s, axis=1, keepdims=True))
        l = jnp.sum(p, axis=1, keepdims=True)
        o_ref[:, cols] = (_dot(p.astype(BF16), mv_ref[:, cols].astype(BF16)) / l).astype(BF16)


def _xattn(qc, mk2d, mv2d, batch, seq, tq):
    nq = seq // tq
    n_mem = mk2d.shape[0] // batch
    row = pl.BlockSpec((tq, XA_W), lambda b, i: (b * nq + i, 0))
    mem = pl.BlockSpec((n_mem, XA_W), lambda b, i: (b, 0))
    return pl.pallas_call(
        _xattn_kernel,
        grid=(batch, nq),
        in_specs=[row, mem, mem],
        out_specs=row,
        out_shape=jax.ShapeDtypeStruct((batch * seq, XA_W), BF16),
        compiler_params=_cparams("parallel", "parallel"),
        name="xattn",
    )(qc, mk2d, mv2d)


def _mix_ffn_kernel(x_ref, a_ref, b_ref, c_ref, gmix_ref, wg_ref, bg_ref, wpa_ref, wpb_ref, wpc_ref, wo_ref,
                    gffn_ref, wfi_ref, wfo_ref, gfin_ref, y_ref):
    x = x_ref[...]
    h = _rms(x, gmix_ref[...]).astype(BF16)
    merged = None
    for n, (br_ref, wp_ref) in enumerate(((a_ref, wpa_ref), (b_ref, wpb_ref), (c_ref, wpc_ref))):
        cols = slice(n * D_MODEL, (n + 1) * D_MODEL)
        gate = jax.nn.sigmoid(_dot(h, wg_ref[:, cols]) + bg_ref[:, cols])
        term = gate * _dot(br_ref[...], wp_ref[...])
        merged = term if merged is None else merged + term
    x1 = x + _dot(merged.astype(BF16), wo_ref[...])
    h2 = _rms(x1, gffn_ref[...]).astype(BF16)
    ff = None
    for c in range(D_FF // FFN_CHUNK):
        gcols = slice(c * FFN_CHUNK, (c + 1) * FFN_CHUNK)
        ucols = slice(D_FF + c * FFN_CHUNK, D_FF + (c + 1) * FFN_CHUNK)
        act = (jax.nn.silu(_dot(h2, wfi_ref[:, gcols])) * _dot(h2, wfi_ref[:, ucols])).astype(BF16)
        term = _dot(act, wfo_ref[gcols, :])
        ff = term if ff is None else ff + term
    y_ref[...] = _rms(x1 + ff, gfin_ref[...])


def _mix_ffn(x2d, a_out, b_out, c_out, p, tm):
    t = x2d.shape[0]
    row = pl.BlockSpec((tm, D_MODEL), lambda i: (i, 0))
    vec = _resident((1, D_MODEL))
    sq_w = _resident((D_MODEL, D_MODEL))
    return pl.pallas_call(
        _mix_ffn_kernel,
        grid=(t // tm,),
        in_specs=[row, row, row, row, vec, _resident((D_MODEL, N_BRANCH * D_MODEL)),
                  _resident((1, N_BRANCH * D_MODEL)), sq_w, sq_w, sq_w, sq_w, vec,
                  _resident((D_MODEL, 2 * D_FF)), _resident((D_FF, D_MODEL)), vec],
        out_specs=row,
        out_shape=jax.ShapeDtypeStruct((t, D_MODEL), F32),
        compiler_params=_cparams("parallel"),
        name="mix_ffn",
    )(x2d, a_out, b_out, c_out, p["norm_mix"], p["w_gate"], p["b_gate"], p["w_proj_a"], p["w_proj_b"],
      p["w_proj_c"], p["w_out"], p["norm_ffn"], p["w_ffn_in"], p["w_ffn_out"], p["norm_final"])


def _layer(x, mk2d, mv2d, cache_kt2d, cache_v3d, conv_state, h0, rel_table, lam, p, lambda_init, tm, tq, ts):
    batch, seq, _ = x.shape
    x2d = x.reshape(batch * seq, D_MODEL)
    no_history = cache_kt2d is None
    q, k, v, kb, vb, xb, gb, qc = _in_proj(x2d, p["norm_mix"], p["w_in"], p["w_k_t"], tm, seq, no_history)
    if no_history:
        a_out = _attn_self(q, kb, vb, rel_table, lam, p["subln_g"], batch, seq, lambda_init, tq)
    else:
        past = cache_kt2d.shape[1]
        a_out = _attn_cached(q, cache_kt2d, cache_v3d, k, v, rel_table, lam, p["subln_g"], batch, seq, past,
                             lambda_init)
    b_out, new_conv, h_last = _rglru(xb, gb, conv_state, h0, p["conv_w"], p["conv_b"], p["w_rg_a"], p["b_rg_a"],
                                     p["w_rg_x"], p["b_rg_x"], p["rg_lambda"], batch, seq, ts)
    c_out = _xattn(qc, mk2d, mv2d, batch, seq, min(seq, tm))
    y = _mix_ffn(x2d, a_out, b_out, c_out, p, tm)
    return y.reshape(batch, seq, D_MODEL), k, v, new_conv, h_last.reshape(batch, LRU_W)


def kernel(x_prompt, x_sample, mem_prompt, cache_k, cache_v, state_conv, state_lru, cache_mem_k, cache_mem_v,
           rel_table, norm_mix, w_in, lambda_q1, lambda_k1, lambda_q2, lambda_k2, subln_g, conv_w, conv_b,
           w_rg_a, b_rg_a, w_rg_x, b_rg_x, rg_lambda, norm_mem, w_mem_kv, w_proj_a, w_proj_b, w_proj_c,
           w_gate, b_gate, w_out, norm_ffn, w_ffn_in, w_ffn_out, norm_final):
    depth = w_in.shape[0]
    assert depth == 1, "the final norm is fused into the layer's last kernel"
    l = 0
    lambda_init = 0.8 - 0.6 * math.exp(-0.3 * l)
    bp, sp_, _ = x_prompt.shape
    bs, ss, _ = x_sample.shape
    past = cache_k.shape[2]
    n_mem = mem_prompt.shape[1]
    row = lambda a: a.reshape(1, -1).astype(F32)
    p = dict(norm_mix=row(norm_mix[l]), w_in=w_in[l].astype(BF16),
             w_k_t=w_in[l][:, IN_OFFS[1]:IN_OFFS[2]].T.astype(BF16), subln_g=row(subln_g[l]),
             conv_w=conv_w[l], conv_b=row(conv_b[l]), w_rg_a=w_rg_a[l].astype(BF16), b_rg_a=row(b_rg_a[l]),
             w_rg_x=w_rg_x[l].astype(BF16), b_rg_x=row(b_rg_x[l]), rg_lambda=row(rg_lambda[l]),
             w_proj_a=w_proj_a[l].astype(BF16), w_proj_b=w_proj_b[l].astype(BF16),
             w_proj_c=w_proj_c[l].astype(BF16), w_gate=w_gate[l].astype(BF16), b_gate=row(b_gate[l]),
             w_out=w_out[l].astype(BF16), norm_ffn=row(norm_ffn[l]), w_ffn_in=w_ffn_in[l].astype(BF16),
             w_ffn_out=w_ffn_out[l].astype(BF16), norm_final=row(norm_final))
    lam = (jnp.exp(jnp.sum(lambda_q1[l] * lambda_k1[l]).astype(F32))
           - jnp.exp(jnp.sum(lambda_q2[l] * lambda_k2[l]).astype(F32)) + lambda_init).reshape(1)

    mk, mv, mkb, mvb = _mem_kv(mem_prompt.reshape(bp * n_mem, D_MODEL), row(norm_mem[l]),
                               w_mem_kv[l].astype(BF16), 256)
    zeros_conv = jnp.zeros((bp, CONV_W - 1, LRU_W), F32)
    zeros_h = jnp.zeros((bp, 1, LRU_W), F32)
    yp, kp, vp, cp, hp = _layer(x_prompt, mkb, mvb, None, None, zeros_conv, zeros_h, rel_table, lam, p,
                                lambda_init, tm=512, tq=256, ts=256)
    cache_kt = jnp.transpose(cache_k[l], (0, 2, 3, 4, 1)).reshape(bs * DA_QK_W, past)
    ys, ks, vs, cs, hs = _layer(x_sample, cache_mem_k[l].reshape(bs * n_mem, XA_W),
                                cache_mem_v[l].reshape(bs * n_mem, XA_W),
                                cache_kt, cache_v[l].reshape(bs * past, DA_HEADS, DA_V),
                                state_conv[l], state_lru[l].reshape(bs, 1, LRU_W), rel_table, lam, p,
                                lambda_init, tm=256, tq=ss, ts=ss)
    kp = jnp.transpose(kp.reshape(bp, DA_HEADS, 2, DA_HD, sp_), (0, 4, 1, 2, 3))
    return (yp, ys,
            kp[None], vp.reshape(1, bp, sp_, DA_HEADS, DA_V),
            cp[None], hp[None],
            mk.reshape(1, bp, n_mem, XA_HEADS, XA_HD), mv.reshape(1, bp, n_mem, XA_HEADS, XA_HD),
            ks.reshape(1, bs, ss, DA_HEADS, 2, DA_HD), vs.reshape(1, bs, ss, DA_HEADS, DA_V),
            cs[None], hs[None])
```

```python
import functools
import math

import jax
import jax.numpy as jnp
from jax import lax
from jax.experimental import pallas as pl
from jax.experimental.pallas import tpu as pltpu

F32 = jnp.float32
BF16 = jnp.bfloat16

D_MODEL = 1024
CHUNK = 64
CHUNK_SHIFT = 6
DA_HEADS = 8
DA_HD = 64
DA_V = 2 * DA_HD
LRU_W = D_MODEL
LRU_BLOCKS = 8
LRU_BW = LRU_W // LRU_BLOCKS
CONV_W = 4
LRU_C = 8.0
XA_HEADS = 4
XA_HD = 256
N_BUCKETS = 32
MAX_DISTANCE = 128
D_FF = ((8 * D_MODEL + 3 * 256 - 1) // (3 * 256)) * 256
N_BRANCH = 3
EPS = 1e-6
DA_QK_W = DA_HEADS * 2 * DA_HD
DA_V_W = DA_HEADS * DA_V
XA_W = XA_HEADS * XA_HD
IN_OFFS = (0, DA_QK_W, 2 * DA_QK_W, 2 * DA_QK_W + DA_V_W, 2 * DA_QK_W + DA_V_W + LRU_W,
           2 * DA_QK_W + DA_V_W + 2 * LRU_W)
IN_W = 2 * DA_QK_W + DA_V_W + 2 * LRU_W + XA_W
LOG2E = math.log2(math.e)
DA_Q_SCALE = DA_HD ** -0.5 * LOG2E
XA_SCALE = XA_HD ** -0.5
MASK_VALUE = -1e30
FAR_BUCKET = N_BUCKETS // 2 - 1
FFN_CHUNK = D_FF // 2

V7X_VMEM_BYTES = 64 * 1024 * 1024
VMEM_LIMIT = V7X_VMEM_BYTES - 8 * 1024 * 1024

NT_DIMS = (((1,), (1,)), ((), ()))


def _cparams(*sem):
    return pltpu.CompilerParams(dimension_semantics=sem, vmem_limit_bytes=VMEM_LIMIT)


def _resident(shape):
    nd = len(shape)
    return pl.BlockSpec(shape, lambda *_: (0,) * nd, pipeline_mode=pl.Buffered(1))


def _rms(x, g):
    return x * lax.rsqrt(jnp.mean(x * x, axis=-1, keepdims=True) + EPS) * g


def _dot(a, b):
    return jnp.dot(a, b, preferred_element_type=F32)


def _in_proj_kernel(x_ref, g_ref, w_ref, wkt_ref, q_ref, k_ref, v_ref, kb_ref, vb_ref, xb_ref, gb_ref, qc_ref, *,
                    k_feature_major):
    h = _rms(x_ref[...], g_ref[...]).astype(BF16)

    def proj(n, width):
        return _dot(h, w_ref[:, IN_OFFS[n]:IN_OFFS[n] + width])

    q_ref[...] = (proj(0, DA_QK_W) * DA_Q_SCALE).astype(BF16)
    if k_feature_major:
        k = lax.dot_general(wkt_ref[...], h, NT_DIMS, preferred_element_type=F32)
    else:
        k = proj(1, DA_QK_W)
    k_ref[...] = k
    kb_ref[...] = k.astype(BF16)
    v = proj(2, DA_V_W)
    v_ref[...] = v
    vb_ref[...] = v.astype(BF16)
    xb_ref[...] = proj(3, LRU_W)
    gb_ref[...] = proj(4, LRU_W).astype(BF16)
    qc_ref[...] = (proj(5, XA_W) * XA_SCALE).astype(BF16)


def _in_proj(x2d, g, w_bf16, wkt_bf16, tm, seq, k_feature_major):
    t = x2d.shape[0]
    nt = seq // tm
    row = lambda width: pl.BlockSpec((tm, width), lambda i: (i, 0))
    sds = lambda width, dt: jax.ShapeDtypeStruct((t, width), dt)
    if k_feature_major:
        assert seq % tm == 0
        k_spec = pl.BlockSpec((DA_QK_W, tm), lambda i: (i // nt, i % nt))
        k_sds = lambda dt: jax.ShapeDtypeStruct((t // seq * DA_QK_W, seq), dt)
    else:
        k_spec = row(DA_QK_W)
        k_sds = lambda dt: sds(DA_QK_W, dt)
    return pl.pallas_call(
        functools.partial(_in_proj_kernel, k_feature_major=k_feature_major),
        grid=(t // tm,),
        in_specs=[row(D_MODEL), _resident((1, D_MODEL)), _resident((D_MODEL, IN_W)),
                  _resident((DA_QK_W, D_MODEL))],
        out_specs=[row(DA_QK_W), k_spec, row(DA_V_W), k_spec, row(DA_V_W), row(LRU_W), row(LRU_W), row(XA_W)],
        out_shape=[sds(DA_QK_W, BF16), k_sds(F32), sds(DA_V_W, F32), k_sds(BF16), sds(DA_V_W, BF16),
                   sds(LRU_W, F32), sds(LRU_W, BF16), sds(XA_W, BF16)],
        compiler_params=_cparams("parallel"),
        name="in_proj",
    )(x2d, g, w_bf16, wkt_bf16)


def _mem_kv_kernel(m_ref, g_ref, w_ref, mk_ref, mv_ref, mkb_ref, mvb_ref):
    h = _rms(m_ref[...], g_ref[...]).astype(BF16)
    mk = _dot(h, w_ref[:, :XA_W])
    mv = _dot(h, w_ref[:, XA_W:])
    mk_ref[...] = mk
    mv_ref[...] = mv
    mkb_ref[...] = mk.astype(BF16)
    mvb_ref[...] = mv.astype(BF16)


def _mem_kv(mem2d, g, w_bf16, tm):
    t = mem2d.shape[0]
    row = pl.BlockSpec((tm, XA_W), lambda i: (i, 0))
    return pl.pallas_call(
        _mem_kv_kernel,
        grid=(t // tm,),
        in_specs=[pl.BlockSpec((tm, D_MODEL), lambda i: (i, 0)), _resident((1, D_MODEL)),
                  _resident((D_MODEL, 2 * XA_W))],
        out_specs=[row, row, row, row],
        out_shape=[jax.ShapeDtypeStruct((t, XA_W), F32), jax.ShapeDtypeStruct((t, XA_W), F32),
                   jax.ShapeDtypeStruct((t, XA_W), BF16), jax.ShapeDtypeStruct((t, XA_W), BF16)],
        compiler_params=_cparams("parallel"),
        name="mem_kv",
    )(mem2d, g, w_bf16)


def _rel_bucket(rel):
    half = N_BUCKETS // 2
    max_exact = half // 2
    n = jnp.abs(rel)
    nf = jnp.maximum(n, 1).astype(F32)
    large = max_exact + (jnp.log(nf / max_exact) / math.log(MAX_DISTANCE / max_exact)
                         * (half - max_exact)).astype(jnp.int32)
    large = jnp.minimum(large, half - 1)
    return jnp.where(rel > 0, half, 0) + jnp.where(n < max_exact, n, large)


def _bias_from_buckets(bkt, tab_ref, head, vis, shift):
    bias = jnp.full(bkt.shape, tab_ref[0, head], F32)
    for t in range(1, N_BUCKETS):
        bias = jnp.where(bkt == t, tab_ref[t, head], bias)
    return jnp.where(vis, (bias - shift) * LOG2E, MASK_VALUE)


def _split_maps(qh):
    lane = lax.broadcasted_iota(jnp.int32, qh.shape, 1)
    zero = jnp.zeros_like(qh)
    return jnp.concatenate([jnp.where(lane < DA_HD, qh, zero), jnp.where(lane >= DA_HD, qh, zero)], axis=0)


def _sum_column(v):
    one = jnp.where(lax.broadcasted_iota(jnp.int32, v.shape, 1) == 0, 1.0, 0.0).astype(BF16)
    return jnp.concatenate([v, one], axis=1)


def _diff_finish(o, lam, g, tq, lambda_init):
    d = o[:tq] - lam * o[tq:]
    return _rms(d, g) * (1.0 - lambda_init)


def _attn_self_kernel(tab_ref, lam_ref, q_ref, kt_ref, v_ref, bkt_ref, g_ref, o_ref,
                      bias_sc, q2_sc, m_sc, acc_sc, *, tq, lambda_init):
    b = pl.program_id(0)
    i = pl.program_id(1)

    @pl.when((b == 0) & (i == 0))
    def _build_bias():
        bkt = bkt_ref[...]
        r = lax.broadcasted_iota(jnp.int32, bkt.shape, 0)
        c = lax.broadcasted_iota(jnp.int32, bkt.shape, 1)
        vis = lax.shift_right_arithmetic(c - tq, CHUNK_SHIFT) <= lax.shift_right_arithmetic(r, CHUNK_SHIFT)
        for h in range(DA_HEADS):
            bias_sc[h] = _bias_from_buckets(bkt, tab_ref, h, vis, tab_ref[FAR_BUCKET, h])

    q = q_ref[...]
    for h in range(DA_HEADS):
        q2_sc[h] = _split_maps(q[:, h * DA_V:(h + 1) * DA_V])
    m_sc[...] = jnp.full(m_sc.shape, MASK_VALUE, F32)
    acc_sc[...] = jnp.zeros(acc_sc.shape, F32)

    def key_tile(h, start, bias):
        cols = slice(h * DA_V, (h + 1) * DA_V)
        kj = kt_ref[cols, pl.ds(start, tq)]
        vj = _sum_column(v_ref[pl.ds(start, tq), cols])
        s = _dot(q2_sc[h], kj)
        if bias is not None:
            s = (s.reshape(2, tq, tq) + bias[None]).reshape(2 * tq, tq)
        m_prev = m_sc[h]
        m_new = jnp.maximum(m_prev, jnp.max(s, axis=1, keepdims=True))
        alpha = jnp.exp2(m_prev - m_new)
        p = jnp.exp2(s - jnp.concatenate([m_new] * (tq // 128), axis=1))
        acc_sc[h] = jnp.concatenate([alpha, alpha], axis=1) * acc_sc[h] + _dot(p.astype(BF16), vj)
        m_sc[h] = m_new

    def far_body(j, carry):
        start = pl.multiple_of(j * tq, tq)
        for h in range(DA_HEADS):
            key_tile(h, start, None)
        return carry

    lax.fori_loop(0, jnp.maximum(i - 1, 0), far_body, 0)

    @pl.when(i >= 1)
    def _prev_tile():
        start = pl.multiple_of((i - 1) * tq, tq)
        for h in range(DA_HEADS):
            key_tile(h, start, bias_sc[h, :, :tq])

    start = pl.multiple_of(i * tq, tq)
    lam = lam_ref[0]
    g = g_ref[...]
    for h in range(DA_HEADS):
        key_tile(h, start, bias_sc[h, :, tq:])
        acc = acc_sc[h]
        o = acc[:, :DA_V] / acc[:, DA_V:DA_V + 1]
        o_ref[:, h * DA_V:(h + 1) * DA_V] = _diff_finish(o, lam, g, tq, lambda_init).astype(BF16)


def _attn_self(q, ktb, vb, rel_table, lam, subln_g, batch, seq, lambda_init, tq):
    assert seq % tq == 0 and tq % 128 == 0 and tq >= MAX_DISTANCE and tq % CHUNK == 0
    nq = seq // tq
    rel = jnp.arange(2 * tq, dtype=jnp.int32)[None, :] - (tq + jnp.arange(tq, dtype=jnp.int32))[:, None]
    bkt = _rel_bucket(rel)
    smem = pl.BlockSpec(memory_space=pltpu.SMEM)
    return pl.pallas_call(
        functools.partial(_attn_self_kernel, tq=tq, lambda_init=lambda_init),
        grid=(batch, nq),
        in_specs=[smem, smem,
                  pl.BlockSpec((tq, DA_QK_W), lambda b, i: (b * nq + i, 0)),
                  pl.BlockSpec((DA_QK_W, seq), lambda b, i: (b, 0)),
                  pl.BlockSpec((seq, DA_V_W), lambda b, i: (b, 0)),
                  _resident((tq, 2 * tq)), _resident((1, DA_V))],
        out_specs=pl.BlockSpec((tq, DA_V_W), lambda b, i: (b * nq + i, 0)),
        out_shape=jax.ShapeDtypeStruct((batch * seq, DA_V_W), BF16),
        scratch_shapes=[pltpu.VMEM((DA_HEADS, tq, 2 * tq), F32),
                        pltpu.VMEM((DA_HEADS, 2 * tq, DA_V), BF16),
                        pltpu.VMEM((DA_HEADS, 2 * tq, 128), F32),
                        pltpu.VMEM((DA_HEADS, 2 * tq, 2 * DA_V), F32)],
        compiler_params=_cparams("arbitrary", "arbitrary"),
        name="attn_self",
    )(rel_table, lam, q, ktb, vb, bkt, subln_g)


def _attn_cached_kernel(tab_ref, lam_ref, q_ref, ckt_ref, cv_ref, nk_ref, nv_ref, bktc_ref, bktn_ref, g_ref, o_ref,
                        biasc_sc, biasn_sc, *, sq, past, lambda_init):
    @pl.when(pl.program_id(0) == 0)
    def _build_bias():
        for bkt_ref, sc, k0 in ((bktc_ref, biasc_sc, 0), (bktn_ref, biasn_sc, past)):
            bkt = bkt_ref[...]
            qpos = past + lax.broadcasted_iota(jnp.int32, bkt.shape, 0)
            kpos = k0 + lax.broadcasted_iota(jnp.int32, bkt.shape, 1)
            vis = lax.shift_right_arithmetic(kpos, CHUNK_SHIFT) <= lax.shift_right_arithmetic(qpos, CHUNK_SHIFT)
            for h in range(DA_HEADS):
                sc[h] = _bias_from_buckets(bkt, tab_ref, h, vis, 0.0)

    q = q_ref[...]
    lam = lam_ref[0]
    g = g_ref[...]
    for h in range(DA_HEADS):
        cols = slice(h * DA_V, (h + 1) * DA_V)
        q2 = _split_maps(q[:, cols])
        s_c = _dot(q2, ckt_ref[cols, :].astype(BF16))
        s_c = (s_c.reshape(2, sq, past) + biasc_sc[h][None]).reshape(2 * sq, past)
        s_n = lax.dot_general(q2, nk_ref[:, cols].astype(BF16), NT_DIMS, preferred_element_type=F32)
        s_n = (s_n.reshape(2, sq, sq) + biasn_sc[h][None]).reshape(2 * sq, sq)
        m = jnp.maximum(jnp.max(s_c, axis=1, keepdims=True), jnp.max(s_n, axis=1, keepdims=True))
        p_c = jnp.exp2(s_c - m)
        p_n = jnp.exp2(s_n - m)
        l = jnp.sum(p_c, axis=1, keepdims=True) + jnp.sum(p_n, axis=1, keepdims=True)
        v_c = cv_ref[pl.ds(h, past, stride=DA_HEADS), :]
        o = (_dot(p_c.astype(BF16), v_c.astype(BF16))
             + _dot(p_n.astype(BF16), nv_ref[:, cols].astype(BF16))) / l
        o_ref[:, cols] = _diff_finish(o, lam, g, sq, lambda_init).astype(BF16)


def _attn_cached(q, cache_kt2d, cache_v2d, nk, nv, rel_table, lam, subln_g, batch, sq, past, lambda_init):
    qpos = past + jnp.arange(sq, dtype=jnp.int32)
    bkt_c = _rel_bucket(jnp.arange(past, dtype=jnp.int32)[None, :] - qpos[:, None])
    bkt_n = _rel_bucket(qpos[None, :] - qpos[:, None])
    smem = pl.BlockSpec(memory_space=pltpu.SMEM)
    new_spec = pl.BlockSpec((sq, DA_QK_W), lambda b: (b, 0))
    return pl.pallas_call(
        functools.partial(_attn_cached_kernel, sq=sq, past=past, lambda_init=lambda_init),
        grid=(batch,),
        in_specs=[smem, smem, new_spec,
                  pl.BlockSpec((DA_QK_W, past), lambda b: (b, 0)),
                  pl.BlockSpec((past * DA_HEADS, DA_V), lambda b: (b, 0)),
                  new_spec, new_spec,
                  _resident((sq, past)), _resident((sq, sq)), _resident((1, DA_V))],
        out_specs=new_spec,
        out_shape=jax.ShapeDtypeStruct((batch * sq, DA_V_W), BF16),
        scratch_shapes=[pltpu.VMEM((DA_HEADS, sq, past), F32), pltpu.VMEM((DA_HEADS, sq, sq), F32)],
        compiler_params=_cparams("arbitrary"),
        name="attn_cached",
    )(rel_table, lam, q, cache_kt2d, cache_v2d, nk, nv, bkt_c, bkt_n, subln_g)


def _rglru_kernel(xb_ref, gb_ref, cs_ref, h0_ref, cw_ref, cb_ref, wa_ref, ba_ref, wx_ref, bx_ref, lam_ref,
                  out_ref, nc_ref, hl_ref, xpad_sc, a_sc, u_sc, hs_sc, hc_sc, *, ts):
    halo = CONV_W - 1
    base = 8 - halo

    @pl.when(pl.program_id(1) == 0)
    def _load_state():
        xpad_sc[base:8, :] = cs_ref[0]
        hc_sc[...] = h0_ref[0]

    xpad_sc[8:8 + ts, :] = xb_ref[...]
    cw = cw_ref[...]
    xc = xpad_sc[base:base + ts, :] * cw[0:1]
    for j in range(1, CONV_W):
        xc = xc + xpad_sc[base + j:base + j + ts, :] * cw[j:j + 1]
    xc = cb_ref[...] + xc
    tail = xpad_sc[base + ts:8 + ts, :]
    nc_ref[0] = tail
    xpad_sc[base:8, :] = tail

    xcb = xc.astype(BF16)
    sp = jax.nn.softplus(-lam_ref[...])
    for n in range(LRU_BLOCKS):
        cols = slice(n * LRU_BW, (n + 1) * LRU_BW)
        r = jax.nn.sigmoid(_dot(xcb[:, cols], wa_ref[n]) + ba_ref[:, cols])
        gate = jax.nn.sigmoid(_dot(xcb[:, cols], wx_ref[n]) + bx_ref[:, cols])
        log_a = -LRU_C * r * sp[:, cols]
        a = jnp.exp(log_a)
        a_sc[:, cols] = a
        u_sc[:, cols] = jnp.sqrt(-jnp.tanh(log_a) * (a * a + 1.0)) * (gate * xc[:, cols])

    def step(t, h):
        h = a_sc[pl.ds(t, 1), :] * h + u_sc[pl.ds(t, 1), :]
        hs_sc[pl.ds(t, 1), :] = h
        return h

    h = lax.fori_loop(0, ts, step, hc_sc[...], unroll=8)
    hc_sc[...] = h
    hl_ref[0] = h
    out_ref[...] = (hs_sc[...] * jax.nn.gelu(gb_ref[...].astype(F32))).astype(BF16)


def _rglru(xb, gb, conv_state, h0, conv_w, conv_b, wa, ba, wx, bx, rg_lambda, batch, seq, ts):
    assert seq % ts == 0 and ts >= CONV_W - 1
    nt = seq // ts
    row = pl.BlockSpec((ts, LRU_W), lambda b, i: (b * nt + i, 0))
    vec = _resident((1, LRU_W))
    gate_w = _resident((LRU_BLOCKS, LRU_BW, LRU_BW))
    return pl.pallas_call(
        functools.partial(_rglru_kernel, ts=ts),
        grid=(batch, nt),
        in_specs=[row, row,
                  pl.BlockSpec((1, CONV_W - 1, LRU_W), lambda b, i: (b, 0, 0)),
                  pl.BlockSpec((1, 1, LRU_W), lambda b, i: (b, 0, 0)),
                  _resident((CONV_W, LRU_W)), vec, gate_w, vec, gate_w, vec, vec],
        out_specs=[row,
                   pl.BlockSpec((1, CONV_W - 1, LRU_W), lambda b, i: (b, 0, 0)),
                   pl.BlockSpec((1, 1, LRU_W), lambda b, i: (b, 0, 0))],
        out_shape=[jax.ShapeDtypeStruct((batch * seq, LRU_W), BF16),
                   jax.ShapeDtypeStruct((batch, CONV_W - 1, LRU_W), F32),
                   jax.ShapeDtypeStruct((batch, 1, LRU_W), F32)],
        scratch_shapes=[pltpu.VMEM((8 + ts, LRU_W), F32), pltpu.VMEM((ts, LRU_W), F32),
                        pltpu.VMEM((ts, LRU_W), F32), pltpu.VMEM((ts, LRU_W), F32), pltpu.VMEM((1, LRU_W), F32)],
        compiler_params=_cparams("arbitrary", "arbitrary"),
        name="rglru",
    )(xb, gb, conv_state, h0, conv_w, conv_b, wa, ba, wx, bx, rg_lambda)


def _xattn_kernel(q_ref, mk_ref, mv_ref, o_ref):
    q = q_ref[...]
    for h in range(XA_HEADS):
        cols = slice(h * XA_HD, (h + 1) * XA_HD)
        s = lax.dot_general(q[:, cols], mk_ref[:, cols].astype(BF16), NT_DIMS, preferred_element_type=F32)
        p = jnp.exp(s - jnp.max(s, axis=1, keepdims=True))
        l = jnp.sum(p, axis=1, keepdims=True)
        o_ref[:, cols] = (_dot(p.astype(BF16), mv_ref[:, cols].astype(BF16)) / l).astype(BF16)


def _xattn(qc, mk2d, mv2d, batch, seq, tq):
    nq = seq // tq
    n_mem = mk2d.shape[0] // batch
    row = pl.BlockSpec((tq, XA_W), lambda b, i: (b * nq + i, 0))
    mem = pl.BlockSpec((n_mem, XA_W), lambda b, i: (b, 0))
    return pl.pallas_call(
        _xattn_kernel,
        grid=(batch, nq),
        in_specs=[row, mem, mem],
        out_specs=row,
        out_shape=jax.ShapeDtypeStruct((batch * seq, XA_W), BF16),
        compiler_params=_cparams("parallel", "parallel"),
        name="xattn",
    )(qc, mk2d, mv2d)


def _mix_ffn_kernel(x_ref, a_ref, b_ref, c_ref, gmix_ref, wg_ref, bg_ref, wpa_ref, wpb_ref, wpc_ref, wo_ref,
                    gffn_ref, wfi_ref, wfo_ref, gfin_ref, y_ref):
    x = x_ref[...]
    h = _rms(x, gmix_ref[...]).astype(BF16)
    merged = None
    for n, (br_ref, wp_ref) in enumerate(((a_ref, wpa_ref), (b_ref, wpb_ref), (c_ref, wpc_ref))):
        cols = slice(n * D_MODEL, (n + 1) * D_MODEL)
        gate = jax.nn.sigmoid(_dot(h, wg_ref[:, cols]) + bg_ref[:, cols])
        term = gate * _dot(br_ref[...], wp_ref[...])
        merged = term if merged is None else merged + term
    x1 = x + _dot(merged.astype(BF16), wo_ref[...])
    h2 = _rms(x1, gffn_ref[...]).astype(BF16)
    ff = None
    for c in range(D_FF // FFN_CHUNK):
        gcols = slice(c * FFN_CHUNK, (c + 1) * FFN_CHUNK)
        ucols = slice(D_FF + c * FFN_CHUNK, D_FF + (c + 1) * FFN_CHUNK)
        act = (jax.nn.silu(_dot(h2, wfi_ref[:, gcols])) * _dot(h2, wfi_ref[:, ucols])).astype(BF16)
        term = _dot(act, wfo_ref[gcols, :])
        ff = term if ff is None else ff + term
    y_ref[...] = _rms(x1 + ff, gfin_ref[...])


def _mix_ffn(x2d, a_out, b_out, c_out, p, tm):
    t = x2d.shape[0]
    row = pl.BlockSpec((tm, D_MODEL), lambda i: (i, 0))
    vec = _resident((1, D_MODEL))
    sq_w = _resident((D_MODEL, D_MODEL))
    return pl.pallas_call(
        _mix_ffn_kernel,
        grid=(t // tm,),
        in_specs=[row, row, row, row, vec, _resident((D_MODEL, N_BRANCH * D_MODEL)),
                  _resident((1, N_BRANCH * D_MODEL)), sq_w, sq_w, sq_w, sq_w, vec,
                  _resident((D_MODEL, 2 * D_FF)), _resident((D_FF, D_MODEL)), vec],
        out_specs=row,
        out_shape=jax.ShapeDtypeStruct((t, D_MODEL), F32),
        compiler_params=_cparams("parallel"),
        name="mix_ffn",
    )(x2d, a_out, b_out, c_out, p["norm_mix"], p["w_gate"], p["b_gate"], p["w_proj_a"], p["w_proj_b"],
      p["w_proj_c"], p["w_out"], p["norm_ffn"], p["w_ffn_in"], p["w_ffn_out"], p["norm_final"])


def _layer(x, mk2d, mv2d, cache_kt2d, cache_v2d, conv_state, h0, rel_table, lam, p, lambda_init, tm, tq, ts):
    batch, seq, _ = x.shape
    x2d = x.reshape(batch * seq, D_MODEL)
    no_history = cache_kt2d is None
    q, k, v, kb, vb, xb, gb, qc = _in_proj(x2d, p["norm_mix"], p["w_in"], p["w_k_t"], tm, seq, no_history)
    if no_history:
        a_out = _attn_self(q, kb, vb, rel_table, lam, p["subln_g"], batch, seq, lambda_init, tq)
    else:
        past = cache_kt2d.shape[1]
        a_out = _attn_cached(q, cache_kt2d, cache_v2d, k, v, rel_table, lam, p["subln_g"], batch, seq, past,
                             lambda_init)
    b_out, new_conv, h_last = _rglru(xb, gb, conv_state, h0, p["conv_w"], p["conv_b"], p["w_rg_a"], p["b_rg_a"],
                                     p["w_rg_x"], p["b_rg_x"], p["rg_lambda"], batch, seq, ts)
    c_out = _xattn(qc, mk2d, mv2d, batch, seq, min(seq, tm))
    y = _mix_ffn(x2d, a_out, b_out, c_out, p, tm)
    return y.reshape(batch, seq, D_MODEL), k, v, new_conv, h_last.reshape(batch, LRU_W)


def kernel(x_prompt, x_sample, mem_prompt, cache_k, cache_v, state_conv, state_lru, cache_mem_k, cache_mem_v,
           rel_table, norm_mix, w_in, lambda_q1, lambda_k1, lambda_q2, lambda_k2, subln_g, conv_w, conv_b,
           w_rg_a, b_rg_a, w_rg_x, b_rg_x, rg_lambda, norm_mem, w_mem_kv, w_proj_a, w_proj_b, w_proj_c,
           w_gate, b_gate, w_out, norm_ffn, w_ffn_in, w_ffn_out, norm_final):
    depth = w_in.shape[0]
    assert depth == 1, "the final norm is fused into the layer's last kernel"
    l = 0
    lambda_init = 0.8 - 0.6 * math.exp(-0.3 * l)
    bp, sp_, _ = x_prompt.shape
    bs, ss, _ = x_sample.shape
    past = cache_k.shape[2]
    n_mem = mem_prompt.shape[1]
    row = lambda a: a.reshape(1, -1).astype(F32)
    p = dict(norm_mix=row(norm_mix[l]), w_in=w_in[l].astype(BF16),
             w_k_t=w_in[l][:, IN_OFFS[1]:IN_OFFS[2]].T.astype(BF16), subln_g=row(subln_g[l]),
             conv_w=conv_w[l], conv_b=row(conv_b[l]), w_rg_a=w_rg_a[l].astype(BF16), b_rg_a=row(b_rg_a[l]),
             w_rg_x=w_rg_x[l].astype(BF16), b_rg_x=row(b_rg_x[l]), rg_lambda=row(rg_lambda[l]),
             w_proj_a=w_proj_a[l].astype(BF16), w_proj_b=w_proj_b[l].astype(BF16),
             w_proj_c=w_proj_c[l].astype(BF16), w_gate=w_gate[l].astype(BF16), b_gate=row(b_gate[l]),
             w_out=w_out[l].astype(BF16), norm_ffn=row(norm_ffn[l]), w_ffn_in=w_ffn_in[l].astype(BF16),
             w_ffn_out=w_ffn_out[l].astype(BF16), norm_final=row(norm_final))
    lam = (jnp.exp(jnp.sum(lambda_q1[l] * lambda_k1[l]).astype(F32))
           - jnp.exp(jnp.sum(lambda_q2[l] * lambda_k2[l]).astype(F32)) + lambda_init).reshape(1)

    mk, mv, mkb, mvb = _mem_kv(mem_prompt.reshape(bp * n_mem, D_MODEL), row(norm_mem[l]),
                               w_mem_kv[l].astype(BF16), 256)
    zeros_conv = jnp.zeros((bp, CONV_W - 1, LRU_W), F32)
    zeros_h = jnp.zeros((bp, 1, LRU_W), F32)
    yp, kp, vp, cp, hp = _layer(x_prompt, mkb, mvb, None, None, zeros_conv, zeros_h, rel_table, lam, p,
                                lambda_init, tm=512, tq=256, ts=256)
    cache_kt = jnp.transpose(cache_k[l], (0, 2, 3, 4, 1)).reshape(bs * DA_QK_W, past)
    ys, ks, vs, cs, hs = _layer(x_sample, cache_mem_k[l].reshape(bs * n_mem, XA_W),
                                cache_mem_v[l].reshape(bs * n_mem, XA_W),
                                cache_kt, cache_v[l].reshape(bs * past * DA_HEADS, DA_V),
                                state_conv[l], state_lru[l].reshape(bs, 1, LRU_W), rel_table, lam, p,
                                lambda_init, tm=256, tq=ss, ts=ss)
    kp = jnp.transpose(kp.reshape(bp, DA_HEADS, 2, DA_HD, sp_), (0, 4, 1, 2, 3))
    return (yp, ys,
            kp[None], vp.reshape(1, bp, sp_, DA_HEADS, DA_V),
            cp[None], hp[None],
            mk.reshape(1, bp, n_mem, XA_HEADS, XA_HD), mv.reshape(1, bp, n_mem, XA_HEADS, XA_HD),
            ks.reshape(1, bs, ss, DA_HEADS, 2, DA_HD), vs.reshape(1, bs, ss, DA_HEADS, DA_V),
            cs[None], hs[None])
```

```python
import functools
import math

import jax
import jax.numpy as jnp
from jax import lax
from jax.experimental import pallas as pl
from jax.experimental.pallas import tpu as pltpu

F32 = jnp.float32
BF16 = jnp.bfloat16

D_MODEL = 1024
CHUNK = 64
CHUNK_SHIFT = 6
DA_HEADS = 8
DA_HD = 64
DA_V = 2 * DA_HD
LRU_W = D_MODEL
LRU_BLOCKS = 8
LRU_BW = LRU_W // LRU_BLOCKS
CONV_W = 4
LRU_C = 8.0
XA_HEADS = 4
XA_HD = 256
N_BUCKETS = 32
MAX_DISTANCE = 128
D_FF = ((8 * D_MODEL + 3 * 256 - 1) // (3 * 256)) * 256
N_BRANCH = 3
EPS = 1e-6
DA_QK_W = DA_HEADS * 2 * DA_HD
DA_V_W = DA_HEADS * DA_V
XA_W = XA_HEADS * XA_HD
IN_OFFS = (0, DA_QK_W, 2 * DA_QK_W, 2 * DA_QK_W + DA_V_W, 2 * DA_QK_W + DA_V_W + LRU_W,
           2 * DA_QK_W + DA_V_W + 2 * LRU_W)
IN_W = 2 * DA_QK_W + DA_V_W + 2 * LRU_W + XA_W
LOG2E = math.log2(math.e)
DA_Q_SCALE = DA_HD ** -0.5 * LOG2E
XA_SCALE = XA_HD ** -0.5
MASK_VALUE = -1e30
FAR_BUCKET = N_BUCKETS // 2 - 1
FFN_CHUNK = D_FF // 2

V7X_VMEM_BYTES = 64 * 1024 * 1024
VMEM_LIMIT = V7X_VMEM_BYTES - 8 * 1024 * 1024

NT_DIMS = (((1,), (1,)), ((), ()))


def _cparams(*sem):
    return pltpu.CompilerParams(dimension_semantics=sem, vmem_limit_bytes=VMEM_LIMIT)


def _resident(shape):
    nd = len(shape)
    return pl.BlockSpec(shape, lambda *_: (0,) * nd, pipeline_mode=pl.Buffered(1))


def _rms(x, g):
    return x * lax.rsqrt(jnp.mean(x * x, axis=-1, keepdims=True) + EPS) * g


def _dot(a, b):
    return jnp.dot(a, b, preferred_element_type=F32)


def _in_proj_kernel(x_ref, g_ref, w_ref, wkt_ref, q_ref, k_ref, v_ref, kb_ref, vb_ref, xb_ref, gb_ref, qc_ref, *,
                    k_feature_major):
    h = _rms(x_ref[...], g_ref[...]).astype(BF16)

    def proj(n, width):
        return _dot(h, w_ref[:, IN_OFFS[n]:IN_OFFS[n] + width])

    q_ref[...] = (proj(0, DA_QK_W) * DA_Q_SCALE).astype(BF16)
    if k_feature_major:
        k = lax.dot_general(wkt_ref[...], h, NT_DIMS, preferred_element_type=F32)
    else:
        k = proj(1, DA_QK_W)
    k_ref[...] = k
    kb_ref[...] = k.astype(BF16)
    v = proj(2, DA_V_W)
    v_ref[...] = v
    vb_ref[...] = v.astype(BF16)
    xb_ref[...] = proj(3, LRU_W)
    gb_ref[...] = proj(4, LRU_W).astype(BF16)
    qc_ref[...] = (proj(5, XA_W) * XA_SCALE).astype(BF16)


def _in_proj(x2d, g, w_bf16, wkt_bf16, tm, seq, k_feature_major):
    t = x2d.shape[0]
    nt = seq // tm
    row = lambda width: pl.BlockSpec((tm, width), lambda i: (i, 0))
    sds = lambda width, dt: jax.ShapeDtypeStruct((t, width), dt)
    if k_feature_major:
        assert seq % tm == 0
        k_spec = pl.BlockSpec((DA_QK_W, tm), lambda i: (i // nt, i % nt))
        k_sds = lambda dt: jax.ShapeDtypeStruct((t // seq * DA_QK_W, seq), dt)
    else:
        k_spec = row(DA_QK_W)
        k_sds = lambda dt: sds(DA_QK_W, dt)
    return pl.pallas_call(
        functools.partial(_in_proj_kernel, k_feature_major=k_feature_major),
        grid=(t // tm,),
        in_specs=[row(D_MODEL), _resident((1, D_MODEL)), _resident((D_MODEL, IN_W)),
                  _resident((DA_QK_W, D_MODEL))],
        out_specs=[row(DA_QK_W), k_spec, row(DA_V_W), k_spec, row(DA_V_W), row(LRU_W), row(LRU_W), row(XA_W)],
        out_shape=[sds(DA_QK_W, BF16), k_sds(F32), sds(DA_V_W, F32), k_sds(BF16), sds(DA_V_W, BF16),
                   sds(LRU_W, F32), sds(LRU_W, BF16), sds(XA_W, BF16)],
        compiler_params=_cparams("parallel"),
        name="in_proj",
    )(x2d, g, w_bf16, wkt_bf16)


def _mem_kv_kernel(m_ref, g_ref, w_ref, mk_ref, mv_ref, mkb_ref, mvb_ref):
    h = _rms(m_ref[...], g_ref[...]).astype(BF16)
    mk = _dot(h, w_ref[:, :XA_W])
    mv = _dot(h, w_ref[:, XA_W:])
    mk_ref[...] = mk
    mv_ref[...] = mv
    mkb_ref[...] = mk.astype(BF16)
    mvb_ref[...] = mv.astype(BF16)


def _mem_kv(mem2d, g, w_bf16, tm):
    t = mem2d.shape[0]
    row = pl.BlockSpec((tm, XA_W), lambda i: (i, 0))
    return pl.pallas_call(
        _mem_kv_kernel,
        grid=(t // tm,),
        in_specs=[pl.BlockSpec((tm, D_MODEL), lambda i: (i, 0)), _resident((1, D_MODEL)),
                  _resident((D_MODEL, 2 * XA_W))],
        out_specs=[row, row, row, row],
        out_shape=[jax.ShapeDtypeStruct((t, XA_W), F32), jax.ShapeDtypeStruct((t, XA_W), F32),
                   jax.ShapeDtypeStruct((t, XA_W), BF16), jax.ShapeDtypeStruct((t, XA_W), BF16)],
        compiler_params=_cparams("parallel"),
        name="mem_kv",
    )(mem2d, g, w_bf16)


def _rel_bucket(rel):
    half = N_BUCKETS // 2
    max_exact = half // 2
    n = jnp.abs(rel)
    nf = jnp.maximum(n, 1).astype(F32)
    large = max_exact + (jnp.log(nf / max_exact) / math.log(MAX_DISTANCE / max_exact)
                         * (half - max_exact)).astype(jnp.int32)
    large = jnp.minimum(large, half - 1)
    return jnp.where(rel > 0, half, 0) + jnp.where(n < max_exact, n, large)


def _bias_from_buckets(bkt, tab_ref, head, vis, shift):
    bias = jnp.full(bkt.shape, tab_ref[0, head], F32)
    for t in range(1, N_BUCKETS):
        bias = jnp.where(bkt == t, tab_ref[t, head], bias)
    return jnp.where(vis, (bias - shift) * LOG2E, MASK_VALUE)


def _split_maps(qh):
    lane = lax.broadcasted_iota(jnp.int32, qh.shape, 1)
    zero = jnp.zeros_like(qh)
    return jnp.concatenate([jnp.where(lane < DA_HD, qh, zero), jnp.where(lane >= DA_HD, qh, zero)], axis=0)


def _ones_column(n):
    return jnp.where(lax.broadcasted_iota(jnp.int32, (n, DA_V), 1) == 0, 1.0, 0.0).astype(BF16)


def _diff_finish(o, lam, g, tq, lambda_init):
    d = o[:tq] - lam * o[tq:]
    return _rms(d, g) * (1.0 - lambda_init)


def _attn_self_kernel(tab_ref, lam_ref, q_ref, kt_ref, v_ref, bkt_ref, g_ref, o_ref,
                      bias_sc, q2_sc, m_sc, acc_sc, *, tq, lambda_init):
    b = pl.program_id(0)
    i = pl.program_id(1)

    @pl.when((b == 0) & (i == 0))
    def _build_bias():
        bkt = bkt_ref[...]
        r = lax.broadcasted_iota(jnp.int32, bkt.shape, 0)
        c = lax.broadcasted_iota(jnp.int32, bkt.shape, 1)
        vis = lax.shift_right_arithmetic(c - tq, CHUNK_SHIFT) <= lax.shift_right_arithmetic(r, CHUNK_SHIFT)
        for h in range(DA_HEADS):
            bias_sc[h] = _bias_from_buckets(bkt, tab_ref, h, vis, tab_ref[FAR_BUCKET, h])

    q = q_ref[...]
    for h in range(DA_HEADS):
        q2_sc[h] = _split_maps(q[:, h * DA_V:(h + 1) * DA_V])
    m_sc[...] = jnp.full(m_sc.shape, MASK_VALUE, F32)
    acc_sc[...] = jnp.zeros(acc_sc.shape, F32)
    ones_col = _ones_column(tq)

    def key_tile(h, start, bias):
        cols = slice(h * DA_V, (h + 1) * DA_V)
        kj = kt_ref[cols, pl.ds(start, tq)]
        vj = jnp.concatenate([v_ref[pl.ds(start, tq), cols], ones_col], axis=1)
        s = _dot(q2_sc[h], kj)
        if bias is not None:
            s = (s.reshape(2, tq, tq) + bias[None]).reshape(2 * tq, tq)
        m_prev = m_sc[h]
        m_new = jnp.maximum(m_prev, jnp.max(s, axis=1, keepdims=True))
        alpha = jnp.exp2(m_prev - m_new)
        p = jnp.exp2(s - jnp.concatenate([m_new] * (tq // 128), axis=1))
        acc_sc[h] = jnp.concatenate([alpha, alpha], axis=1) * acc_sc[h] + _dot(p.astype(BF16), vj)
        m_sc[h] = m_new

    def far_body(j, carry):
        start = pl.multiple_of(j * tq, tq)
        for h in range(DA_HEADS):
            key_tile(h, start, None)
        return carry

    lax.fori_loop(0, jnp.maximum(i - 1, 0), far_body, 0)

    @pl.when(i >= 1)
    def _prev_tile():
        start = pl.multiple_of((i - 1) * tq, tq)
        for h in range(DA_HEADS):
            key_tile(h, start, bias_sc[h, :, :tq])

    start = pl.multiple_of(i * tq, tq)
    lam = lam_ref[0]
    g = g_ref[...]
    for h in range(DA_HEADS):
        key_tile(h, start, bias_sc[h, :, tq:])
        acc = acc_sc[h]
        o = acc[:, :DA_V] / jnp.sum(acc[:, DA_V:], axis=1, keepdims=True)
        o_ref[:, h * DA_V:(h + 1) * DA_V] = _diff_finish(o, lam, g, tq, lambda_init).astype(BF16)


def _attn_self(q, ktb, vb, rel_table, lam, subln_g, batch, seq, lambda_init, tq):
    assert seq % tq == 0 and tq % 128 == 0 and tq >= MAX_DISTANCE and tq % CHUNK == 0
    nq = seq // tq
    rel = jnp.arange(2 * tq, dtype=jnp.int32)[None, :] - (tq + jnp.arange(tq, dtype=jnp.int32))[:, None]
    bkt = _rel_bucket(rel)
    smem = pl.BlockSpec(memory_space=pltpu.SMEM)
    return pl.pallas_call(
        functools.partial(_attn_self_kernel, tq=tq, lambda_init=lambda_init),
        grid=(batch, nq),
        in_specs=[smem, smem,
                  pl.BlockSpec((tq, DA_QK_W), lambda b, i: (b * nq + i, 0)),
                  pl.BlockSpec((DA_QK_W, seq), lambda b, i: (b, 0)),
                  pl.BlockSpec((seq, DA_V_W), lambda b, i: (b, 0)),
                  _resident((tq, 2 * tq)), _resident((1, DA_V))],
        out_specs=pl.BlockSpec((tq, DA_V_W), lambda b, i: (b * nq + i, 0)),
        out_shape=jax.ShapeDtypeStruct((batch * seq, DA_V_W), BF16),
        scratch_shapes=[pltpu.VMEM((DA_HEADS, tq, 2 * tq), F32),
                        pltpu.VMEM((DA_HEADS, 2 * tq, DA_V), BF16),
                        pltpu.VMEM((DA_HEADS, 2 * tq, 128), F32),
                        pltpu.VMEM((DA_HEADS, 2 * tq, 2 * DA_V), F32)],
        compiler_params=_cparams("arbitrary", "arbitrary"),
        name="attn_self",
    )(rel_table, lam, q, ktb, vb, bkt, subln_g)


def _attn_cached_kernel(tab_ref, lam_ref, q_ref, ckt_ref, cv_ref, nk_ref, nv_ref, bktc_ref, bktn_ref, g_ref, o_ref,
                        biasc_sc, biasn_sc, *, sq, past, lambda_init):
    @pl.when(pl.program_id(0) == 0)
    def _build_bias():
        for bkt_ref, sc, k0 in ((bktc_ref, biasc_sc, 0), (bktn_ref, biasn_sc, past)):
            bkt = bkt_ref[...]
            qpos = past + lax.broadcasted_iota(jnp.int32, bkt.shape, 0)
            kpos = k0 + lax.broadcasted_iota(jnp.int32, bkt.shape, 1)
            vis = lax.shift_right_arithmetic(kpos, CHUNK_SHIFT) <= lax.shift_right_arithmetic(qpos, CHUNK_SHIFT)
            for h in range(DA_HEADS):
                sc[h] = _bias_from_buckets(bkt, tab_ref, h, vis, 0.0)

    q = q_ref[...]
    lam = lam_ref[0]
    g = g_ref[...]
    for h in range(DA_HEADS):
        cols = slice(h * DA_V, (h + 1) * DA_V)
        q2 = _split_maps(q[:, cols])
        s_c = _dot(q2, ckt_ref[cols, :].astype(BF16))
        s_c = (s_c.reshape(2, sq, past) + biasc_sc[h][None]).reshape(2 * sq, past)
        s_n = lax.dot_general(q2, nk_ref[:, cols].astype(BF16), NT_DIMS, preferred_element_type=F32)
        s_n = (s_n.reshape(2, sq, sq) + biasn_sc[h][None]).reshape(2 * sq, sq)
        m = jnp.maximum(jnp.max(s_c, axis=1, keepdims=True), jnp.max(s_n, axis=1, keepdims=True))
        p_c = jnp.exp2(s_c - m)
        p_n = jnp.exp2(s_n - m)
        l = jnp.sum(p_c, axis=1, keepdims=True) + jnp.sum(p_n, axis=1, keepdims=True)
        v_c = cv_ref[pl.ds(h, past, stride=DA_HEADS), :]
        o = (_dot(p_c.astype(BF16), v_c.astype(BF16))
             + _dot(p_n.astype(BF16), nv_ref[:, cols].astype(BF16))) / l
        o_ref[:, cols] = _diff_finish(o, lam, g, sq, lambda_init).astype(BF16)


def _attn_cached(q, cache_kt2d, cache_v2d, nk, nv, rel_table, lam, subln_g, batch, sq, past, lambda_init):
    qpos = past + jnp.arange(sq, dtype=jnp.int32)
    bkt_c = _rel_bucket(jnp.arange(past, dtype=jnp.int32)[None, :] - qpos[:, None])
    bkt_n = _rel_bucket(qpos[None, :] - qpos[:, None])
    smem = pl.BlockSpec(memory_space=pltpu.SMEM)
    new_spec = pl.BlockSpec((sq, DA_QK_W), lambda b: (b, 0))
    return pl.pallas_call(
        functools.partial(_attn_cached_kernel, sq=sq, past=past, lambda_init=lambda_init),
        grid=(batch,),
        in_specs=[smem, smem, new_spec,
                  pl.BlockSpec((DA_QK_W, past), lambda b: (b, 0)),
                  pl.BlockSpec((past * DA_HEADS, DA_V), lambda b: (b, 0)),
                  new_spec, new_spec,
                  _resident((sq, past)), _resident((sq, sq)), _resident((1, DA_V))],
        out_specs=new_spec,
        out_shape=jax.ShapeDtypeStruct((batch * sq, DA_V_W), BF16),
        scratch_shapes=[pltpu.VMEM((DA_HEADS, sq, past), F32), pltpu.VMEM((DA_HEADS, sq, sq), F32)],
        compiler_params=_cparams("arbitrary"),
        name="attn_cached",
    )(rel_table, lam, q, cache_kt2d, cache_v2d, nk, nv, bkt_c, bkt_n, subln_g)


def _rglru_kernel(xb_ref, gb_ref, cs_ref, h0_ref, cw_ref, cb_ref, wa_ref, ba_ref, wx_ref, bx_ref, lam_ref,
                  out_ref, nc_ref, hl_ref, xpad_sc, a_sc, u_sc, hs_sc, hc_sc, *, ts):
    halo = CONV_W - 1
    base = 8 - halo

    @pl.when(pl.program_id(1) == 0)
    def _load_state():
        xpad_sc[base:8, :] = cs_ref[0]
        hc_sc[...] = h0_ref[0]

    xpad_sc[8:8 + ts, :] = xb_ref[...]
    cw = cw_ref[...]
    xc = xpad_sc[base:base + ts, :] * cw[0:1]
    for j in range(1, CONV_W):
        xc = xc + xpad_sc[base + j:base + j + ts, :] * cw[j:j + 1]
    xc = cb_ref[...] + xc
    tail = xpad_sc[base + ts:8 + ts, :]
    nc_ref[0] = tail
    xpad_sc[base:8, :] = tail

    xcb = xc.astype(BF16)
    sp = jax.nn.softplus(-lam_ref[...])
    for n in range(LRU_BLOCKS):
        cols = slice(n * LRU_BW, (n + 1) * LRU_BW)
        r = jax.nn.sigmoid(_dot(xcb[:, cols], wa_ref[n]) + ba_ref[:, cols])
        gate = jax.nn.sigmoid(_dot(xcb[:, cols], wx_ref[n]) + bx_ref[:, cols])
        log_a = -LRU_C * r * sp[:, cols]
        a = jnp.exp(log_a)
        a_sc[:, cols] = a
        u_sc[:, cols] = jnp.sqrt(-jnp.tanh(log_a) * (a * a + 1.0)) * (gate * xc[:, cols])

    def step(t, h):
        h = a_sc[pl.ds(t, 1), :] * h + u_sc[pl.ds(t, 1), :]
        hs_sc[pl.ds(t, 1), :] = h
        return h

    h = lax.fori_loop(0, ts, step, hc_sc[...], unroll=8)
    hc_sc[...] = h
    hl_ref[0] = h
    out_ref[...] = (hs_sc[...] * jax.nn.gelu(gb_ref[...].astype(F32))).astype(BF16)


def _rglru(xb, gb, conv_state, h0, conv_w, conv_b, wa, ba, wx, bx, rg_lambda, batch, seq, ts):
    assert seq % ts == 0 and ts >= CONV_W - 1
    nt = seq // ts
    row = pl.BlockSpec((ts, LRU_W), lambda b, i: (b * nt + i, 0))
    vec = _resident((1, LRU_W))
    gate_w = _resident((LRU_BLOCKS, LRU_BW, LRU_BW))
    return pl.pallas_call(
        functools.partial(_rglru_kernel, ts=ts),
        grid=(batch, nt),
        in_specs=[row, row,
                  pl.BlockSpec((1, CONV_W - 1, LRU_W), lambda b, i: (b, 0, 0)),
                  pl.BlockSpec((1, 1, LRU_W), lambda b, i: (b, 0, 0)),
                  _resident((CONV_W, LRU_W)), vec, gate_w, vec, gate_w, vec, vec],
        out_specs=[row,
                   pl.BlockSpec((1, CONV_W - 1, LRU_W), lambda b, i: (b, 0, 0)),
                   pl.BlockSpec((1, 1, LRU_W), lambda b, i: (b, 0, 0))],
        out_shape=[jax.ShapeDtypeStruct((batch * seq, LRU_W), BF16),
                   jax.ShapeDtypeStruct((batch, CONV_W - 1, LRU_W), F32),
                   jax.ShapeDtypeStruct((batch, 1, LRU_W), F32)],
        scratch_shapes=[pltpu.VMEM((8 + ts, LRU_W), F32), pltpu.VMEM((ts, LRU_W), F32),
                        pltpu.VMEM((ts, LRU_W), F32), pltpu.VMEM((ts, LRU_W), F32), pltpu.VMEM((1, LRU_W), F32)],
        compiler_params=_cparams("arbitrary", "arbitrary"),
        name="rglru",
    )(xb, gb, conv_state, h0, conv_w, conv_b, wa, ba, wx, bx, rg_lambda)


def _xattn_kernel(q_ref, mk_ref, mv_ref, o_ref):
    q = q_ref[...]
    for h in range(XA_HEADS):
        cols = slice(h * XA_HD, (h + 1) * XA_HD)
        s = lax.dot_general(q[:, cols], mk_ref[:, cols].astype(BF16), NT_DIMS, preferred_element_type=F32)
        p = jnp.exp(s - jnp.max(s, axis=1, keepdims=True))
        l = jnp.sum(p, axis=1, keepdims=True)
        o_ref[:, cols] = (_dot(p.astype(BF16), mv_ref[:, cols].astype(BF16)) / l).astype(BF16)


def _xattn(qc, mk2d, mv2d, batch, seq, tq):
    nq = seq // tq
    n_mem = mk2d.shape[0] // batch
    row = pl.BlockSpec((tq, XA_W), lambda b, i: (b * nq + i, 0))
    mem = pl.BlockSpec((n_mem, XA_W), lambda b, i: (b, 0))
    return pl.pallas_call(
        _xattn_kernel,
        grid=(batch, nq),
        in_specs=[row, mem, mem],
        out_specs=row,
        out_shape=jax.ShapeDtypeStruct((batch * seq, XA_W), BF16),
        compiler_params=_cparams("parallel", "parallel"),
        name="xattn",
    )(qc, mk2d, mv2d)


def _mix_ffn_kernel(x_ref, a_ref, b_ref, c_ref, gmix_ref, wg_ref, bg_ref, wpa_ref, wpb_ref, wpc_ref, wo_ref,
                    gffn_ref, wfi_ref, wfo_ref, gfin_ref, y_ref):
    x = x_ref[...]
    h = _rms(x, gmix_ref[...]).astype(BF16)
    merged = None
    for n, (br_ref, wp_ref) in enumerate(((a_ref, wpa_ref), (b_ref, wpb_ref), (c_ref, wpc_ref))):
        cols = slice(n * D_MODEL, (n + 1) * D_MODEL)
        gate = jax.nn.sigmoid(_dot(h, wg_ref[:, cols]) + bg_ref[:, cols])
        term = gate * _dot(br_ref[...], wp_ref[...])
        merged = term if merged is None else merged + term
    x1 = x + _dot(merged.astype(BF16), wo_ref[...])
    h2 = _rms(x1, gffn_ref[...]).astype(BF16)
    ff = None
    for c in range(D_FF // FFN_CHUNK):
        gcols = slice(c * FFN_CHUNK, (c + 1) * FFN_CHUNK)
        ucols = slice(D_FF + c * FFN_CHUNK, D_FF + (c + 1) * FFN_CHUNK)
        act = (jax.nn.silu(_dot(h2, wfi_ref[:, gcols])) * _dot(h2, wfi_ref[:, ucols])).astype(BF16)
        term = _dot(act, wfo_ref[gcols, :])
        ff = term if ff is None else ff + term
    y_ref[...] = _rms(x1 + ff, gfin_ref[...])


def _mix_ffn(x2d, a_out, b_out, c_out, p, tm):
    t = x2d.shape[0]
    row = pl.BlockSpec((tm, D_MODEL), lambda i: (i, 0))
    vec = _resident((1, D_MODEL))
    sq_w = _resident((D_MODEL, D_MODEL))
    return pl.pallas_call(
        _mix_ffn_kernel,
        grid=(t // tm,),
        in_specs=[row, row, row, row, vec, _resident((D_MODEL, N_BRANCH * D_MODEL)),
                  _resident((1, N_BRANCH * D_MODEL)), sq_w, sq_w, sq_w, sq_w, vec,
                  _resident((D_MODEL, 2 * D_FF)), _resident((D_FF, D_MODEL)), vec],
        out_specs=row,
        out_shape=jax.ShapeDtypeStruct((t, D_MODEL), F32),
        compiler_params=_cparams("parallel"),
        name="mix_ffn",
    )(x2d, a_out, b_out, c_out, p["norm_mix"], p["w_gate"], p["b_gate"], p["w_proj_a"], p["w_proj_b"],
      p["w_proj_c"], p["w_out"], p["norm_ffn"], p["w_ffn_in"], p["w_ffn_out"], p["norm_final"])


def _layer(x, mk2d, mv2d, cache_kt2d, cache_v2d, conv_state, h0, rel_table, lam, p, lambda_init, tm, tq, ts):
    batch, seq, _ = x.shape
    x2d = x.reshape(batch * seq, D_MODEL)
    no_history = cache_kt2d is None
    q, k, v, kb, vb, xb, gb, qc = _in_proj(x2d, p["norm_mix"], p["w_in"], p["w_k_t"], tm, seq, no_history)
    if no_history:
        a_out = _attn_self(q, kb, vb, rel_table, lam, p["subln_g"], batch, seq, lambda_init, tq)
    else:
        past = cache_kt2d.shape[1]
        a_out = _attn_cached(q, cache_kt2d, cache_v2d, k, v, rel_table, lam, p["subln_g"], batch, seq, past,
                             lambda_init)
    b_out, new_conv, h_last = _rglru(xb, gb, conv_state, h0, p["conv_w"], p["conv_b"], p["w_rg_a"], p["b_rg_a"],
                                     p["w_rg_x"], p["b_rg_x"], p["rg_lambda"], batch, seq, ts)
    c_out = _xattn(qc, mk2d, mv2d, batch, seq, min(seq, tm))
    y = _mix_ffn(x2d, a_out, b_out, c_out, p, tm)
    return y.reshape(batch, seq, D_MODEL), k, v, new_conv, h_last.reshape(batch, LRU_W)


def kernel(x_prompt, x_sample, mem_prompt, cache_k, cache_v, state_conv, state_lru, cache_mem_k, cache_mem_v,
           rel_table, norm_mix, w_in, lambda_q1, lambda_k1, lambda_q2, lambda_k2, subln_g, conv_w, conv_b,
           w_rg_a, b_rg_a, w_rg_x, b_rg_x, rg_lambda, norm_mem, w_mem_kv, w_proj_a, w_proj_b, w_proj_c,
           w_gate, b_gate, w_out, norm_ffn, w_ffn_in, w_ffn_out, norm_final):
    depth = w_in.shape[0]
    assert depth == 1, "the final norm is fused into the layer's last kernel"
    l = 0
    lambda_init = 0.8 - 0.6 * math.exp(-0.3 * l)
    bp, sp_, _ = x_prompt.shape
    bs, ss, _ = x_sample.shape
    past = cache_k.shape[2]
    n_mem = mem_prompt.shape[1]
    row = lambda a: a.reshape(1, -1).astype(F32)
    w_k = lax.optimization_barrier(w_in[l][:, IN_OFFS[1]:IN_OFFS[2]])
    p = dict(norm_mix=row(norm_mix[l]), w_in=w_in[l].astype(BF16), w_k_t=w_k.T.astype(BF16),
             subln_g=row(subln_g[l]),
             conv_w=conv_w[l], conv_b=row(conv_b[l]), w_rg_a=w_rg_a[l].astype(BF16), b_rg_a=row(b_rg_a[l]),
             w_rg_x=w_rg_x[l].astype(BF16), b_rg_x=row(b_rg_x[l]), rg_lambda=row(rg_lambda[l]),
             w_proj_a=w_proj_a[l].astype(BF16), w_proj_b=w_proj_b[l].astype(BF16),
             w_proj_c=w_proj_c[l].astype(BF16), w_gate=w_gate[l].astype(BF16), b_gate=row(b_gate[l]),
             w_out=w_out[l].astype(BF16), norm_ffn=row(norm_ffn[l]), w_ffn_in=w_ffn_in[l].astype(BF16),
             w_ffn_out=w_ffn_out[l].astype(BF16), norm_final=row(norm_final))
    lam = (jnp.exp(jnp.sum(lambda_q1[l] * lambda_k1[l]).astype(F32))
           - jnp.exp(jnp.sum(lambda_q2[l] * lambda_k2[l]).astype(F32)) + lambda_init).reshape(1)

    mk, mv, mkb, mvb = _mem_kv(mem_prompt.reshape(bp * n_mem, D_MODEL), row(norm_mem[l]),
                               w_mem_kv[l].astype(BF16), 256)
    zeros_conv = jnp.zeros((bp, CONV_W - 1, LRU_W), F32)
    zeros_h = jnp.zeros((bp, 1, LRU_W), F32)
    yp, kp, vp, cp, hp = _layer(x_prompt, mkb, mvb, None, None, zeros_conv, zeros_h, rel_table, lam, p,
                                lambda_init, tm=512, tq=256, ts=256)
    cache_kt = jnp.transpose(cache_k[l], (0, 2, 3, 4, 1)).reshape(bs * DA_QK_W, past)
    ys, ks, vs, cs, hs = _layer(x_sample, cache_mem_k[l].reshape(bs * n_mem, XA_W),
                                cache_mem_v[l].reshape(bs * n_mem, XA_W),
                                cache_kt, cache_v[l].reshape(bs * past * DA_HEADS, DA_V),
                                state_conv[l], state_lru[l].reshape(bs, 1, LRU_W), rel_table, lam, p,
                                lambda_init, tm=256, tq=ss, ts=ss)
    kp = jnp.transpose(kp.reshape(bp, DA_HEADS, 2, DA_HD, sp_), (0, 4, 1, 2, 3))
    return (yp, ys,
            kp[None], vp.reshape(1, bp, sp_, DA_HEADS, DA_V),
            cp[None], hp[None],
            mk.reshape(1, bp, n_mem, XA_HEADS, XA_HD), mv.reshape(1, bp, n_mem, XA_HEADS, XA_HD),
            ks.reshape(1, bs, ss, DA_HEADS, 2, DA_HD), vs.reshape(1, bs, ss, DA_HEADS, DA_V),
            cs[None], hs[None])
```

```python
import functools
import math

import jax
import jax.numpy as jnp
from jax import lax
from jax.experimental import pallas as pl
from jax.experimental.pallas import tpu as pltpu

F32 = jnp.float32
BF16 = jnp.bfloat16

D_MODEL = 1024
CHUNK = 64
CHUNK_SHIFT = 6
DA_HEADS = 8
DA_HD = 64
DA_V = 2 * DA_HD
LRU_W = D_MODEL
LRU_BLOCKS = 8
LRU_BW = LRU_W // LRU_BLOCKS
CONV_W = 4
LRU_C = 8.0
XA_HEADS = 4
XA_HD = 256
N_BUCKETS = 32
MAX_DISTANCE = 128
D_FF = ((8 * D_MODEL + 3 * 256 - 1) // (3 * 256)) * 256
N_BRANCH = 3
EPS = 1e-6
DA_QK_W = DA_HEADS * 2 * DA_HD
DA_V_W = DA_HEADS * DA_V
XA_W = XA_HEADS * XA_HD
IN_OFFS = (0, DA_QK_W, 2 * DA_QK_W, 2 * DA_QK_W + DA_V_W, 2 * DA_QK_W + DA_V_W + LRU_W,
           2 * DA_QK_W + DA_V_W + 2 * LRU_W)
IN_W = 2 * DA_QK_W + DA_V_W + 2 * LRU_W + XA_W
LOG2E = math.log2(math.e)
DA_Q_SCALE = DA_HD ** -0.5 * LOG2E
XA_SCALE = XA_HD ** -0.5
MASK_VALUE = -1e30
FAR_BUCKET = N_BUCKETS // 2 - 1
FFN_CHUNK = D_FF // 2
PROJ_CHUNK = 256

V7X_VMEM_BYTES = 64 * 1024 * 1024
VMEM_LIMIT = V7X_VMEM_BYTES - 8 * 1024 * 1024

NT_DIMS = (((1,), (1,)), ((), ()))


def _cparams(*sem):
    return pltpu.CompilerParams(dimension_semantics=sem, vmem_limit_bytes=VMEM_LIMIT)


def _resident(shape):
    nd = len(shape)
    return pl.BlockSpec(shape, lambda *_: (0,) * nd, pipeline_mode=pl.Buffered(1))


def _rms(x, g):
    return x * lax.rsqrt(jnp.mean(x * x, axis=-1, keepdims=True) + EPS) * g


def _dot(a, b):
    return jnp.dot(a, b, preferred_element_type=F32)


def _in_proj_kernel(x_ref, g_ref, w_ref, cs_ref, h0_ref, cw_ref, cb_ref, wa_ref, ba_ref, wx_ref, bx_ref, lam_ref,
                    q_ref, k_ref, v_ref, kb_ref, vb_ref, qc_ref, bo_ref, nc_ref, hl_ref,
                    xpad_sc, a_sc, u_sc, hs_sc, gg_sc, hc_sc, *, nt, seg, k_feature_major):
    tm = x_ref.shape[0]
    nseg = tm // seg
    halo = CONV_W - 1
    base = 8 - halo

    @pl.when(pl.program_id(0) % nt == 0)
    def _load_state():
        for s in range(nseg):
            xpad_sc[s, base:8, :] = cs_ref[s]
        hc_sc[...] = h0_ref[...]

    h = _rms(x_ref[...], g_ref[...]).astype(BF16)

    def proj(n, c0, width):
        return _dot(h, w_ref[:, IN_OFFS[n] + c0:IN_OFFS[n] + c0 + width])

    xb = proj(3, 0, LRU_W)
    for s in range(nseg):
        xpad_sc[s, 8:8 + seg, :] = xb[s * seg:(s + 1) * seg]

    def proj_chunk(kind, c0):
        cols = slice(c0, c0 + PROJ_CHUNK)
        if kind == "g":
            gg_sc[:, cols] = jax.nn.gelu(proj(4, c0, PROJ_CHUNK))
        elif kind == "q":
            q_ref[:, cols] = (proj(0, c0, PROJ_CHUNK) * DA_Q_SCALE).astype(BF16)
        elif kind == "k":
            kc = proj(1, c0, PROJ_CHUNK)
            if k_feature_major:
                k_ref[cols, :] = kc.T
                kb_ref[cols, :] = kc.T.astype(BF16)
            else:
                k_ref[:, cols] = kc
                kb_ref[:, cols] = kc.astype(BF16)
        elif kind == "v":
            vc = proj(2, c0, PROJ_CHUNK)
            v_ref[:, cols] = vc
            vb_ref[:, cols] = vc.astype(BF16)
        else:
            qc_ref[:, cols] = (proj(5, c0, PROJ_CHUNK) * XA_SCALE).astype(BF16)

    chunks = [(kind, c0) for kind, width in (("g", LRU_W), ("q", DA_QK_W), ("k", DA_QK_W), ("v", DA_V_W), ("c", XA_W))
              for c0 in range(0, width, PROJ_CHUNK)]

    cw = cw_ref[...]
    sp = jax.nn.softplus(-lam_ref[...])
    for n in range(LRU_BLOCKS):
        for kind, c0 in chunks[n * len(chunks) // LRU_BLOCKS:(n + 1) * len(chunks) // LRU_BLOCKS]:
            proj_chunk(kind, c0)
        cols = slice(n * LRU_BW, (n + 1) * LRU_BW)
        xc_segs = []
        for s in range(nseg):
            xc = xpad_sc[s, base:base + seg, cols] * cw[0:1, cols]
            for j in range(1, CONV_W):
                xc = xc + xpad_sc[s, base + j:base + j + seg, cols] * cw[j:j + 1, cols]
            xc_segs.append(cb_ref[:, cols] + xc)
        xc = xc_segs[0] if nseg == 1 else jnp.concatenate(xc_segs, axis=0)
        xcb = xc.astype(BF16)
        r = jax.nn.sigmoid(_dot(xcb, wa_ref[n]) + ba_ref[:, cols])
        gate = jax.nn.sigmoid(_dot(xcb, wx_ref[n]) + bx_ref[:, cols])
        log_a = -LRU_C * r * sp[:, cols]
        a = jnp.exp(log_a)
        a_sc[:, cols] = a
        u_sc[:, cols] = jnp.sqrt(-jnp.tanh(log_a) * (a * a + 1.0)) * (gate * xc)
    for s in range(nseg):
        tail = xpad_sc[s, base + seg:8 + seg, :]
        nc_ref[s] = tail
        xpad_sc[s, base:8, :] = tail

    for s in range(nseg):
        def step(t, hprev, row0=s * seg):
            hnew = a_sc[pl.ds(row0 + t, 1), :] * hprev + u_sc[pl.ds(row0 + t, 1), :]
            hs_sc[pl.ds(row0 + t, 1), :] = hnew
            return hnew

        hlast = lax.fori_loop(0, seg, step, hc_sc[s], unroll=8)
        hc_sc[s] = hlast
        hl_ref[s] = hlast
    bo_ref[...] = (hs_sc[...] * gg_sc[...]).astype(BF16)


def _in_proj(x2d, conv_state, h0, p, tm, seq, k_feature_major):
    t = x2d.shape[0]
    seg = min(tm, seq)
    nseg = tm // seg
    nt = seq // seg
    assert tm % seg == 0 and seq % seg == 0 and seg % 8 == 0 and seg >= CONV_W - 1
    batch_block = lambda i: (i // nt, 0, 0)
    row = lambda width: pl.BlockSpec((tm, width), lambda i: (i, 0))
    sds = lambda width, dt: jax.ShapeDtypeStruct((t, width), dt)
    if k_feature_major:
        assert nseg == 1
        k_spec = pl.BlockSpec((DA_QK_W, tm), lambda i: (i // nt, i % nt))
        k_sds = lambda dt: jax.ShapeDtypeStruct((t // seq * DA_QK_W, seq), dt)
    else:
        k_spec = row(DA_QK_W)
        k_sds = lambda dt: sds(DA_QK_W, dt)
    vec = _resident((1, LRU_W))
    gate_w = _resident((LRU_BLOCKS, LRU_BW, LRU_BW))
    conv_spec = pl.BlockSpec((nseg, CONV_W - 1, LRU_W), batch_block)
    state_spec = pl.BlockSpec((nseg, 1, LRU_W), batch_block)
    return pl.pallas_call(
        functools.partial(_in_proj_kernel, nt=nt, seg=seg, k_feature_major=k_feature_major),
        grid=(t // tm,),
        in_specs=[row(D_MODEL), _resident((1, D_MODEL)), _resident((D_MODEL, IN_W)), conv_spec, state_spec,
                  _resident((CONV_W, LRU_W)), vec, gate_w, vec, gate_w, vec, vec],
        out_specs=[row(DA_QK_W), k_spec, row(DA_V_W), k_spec, row(DA_V_W), row(XA_W), row(LRU_W),
                   conv_spec, state_spec],
        out_shape=[sds(DA_QK_W, BF16), k_sds(F32), sds(DA_V_W, F32), k_sds(BF16), sds(DA_V_W, BF16),
                   sds(XA_W, BF16), sds(LRU_W, BF16),
                   jax.ShapeDtypeStruct(conv_state.shape, F32), jax.ShapeDtypeStruct(h0.shape, F32)],
        scratch_shapes=[pltpu.VMEM((nseg, 8 + seg, LRU_W), F32), pltpu.VMEM((tm, LRU_W), F32),
                        pltpu.VMEM((tm, LRU_W), F32), pltpu.VMEM((tm, LRU_W), F32), pltpu.VMEM((tm, LRU_W), F32),
                        pltpu.VMEM((nseg, 1, LRU_W), F32)],
        compiler_params=_cparams("arbitrary"),
        name="in_proj",
    )(x2d, p["norm_mix"], p["w_in"], conv_state, h0, p["conv_w"], p["conv_b"], p["w_rg_a"], p["b_rg_a"],
      p["w_rg_x"], p["b_rg_x"], p["rg_lambda"])


def _mem_kv_kernel(m_ref, g_ref, w_ref, mk_ref, mv_ref, mkb_ref, mvb_ref):
    h = _rms(m_ref[...], g_ref[...]).astype(BF16)
    mk = _dot(h, w_ref[:, :XA_W])
    mv = _dot(h, w_ref[:, XA_W:])
    mk_ref[...] = mk
    mv_ref[...] = mv
    mkb_ref[...] = mk.astype(BF16)
    mvb_ref[...] = mv.astype(BF16)


def _mem_kv(mem2d, g, w_bf16, tm):
    t = mem2d.shape[0]
    row = pl.BlockSpec((tm, XA_W), lambda i: (i, 0))
    return pl.pallas_call(
        _mem_kv_kernel,
        grid=(t // tm,),
        in_specs=[pl.BlockSpec((tm, D_MODEL), lambda i: (i, 0)), _resident((1, D_MODEL)),
                  _resident((D_MODEL, 2 * XA_W))],
        out_specs=[row, row, row, row],
        out_shape=[jax.ShapeDtypeStruct((t, XA_W), F32), jax.ShapeDtypeStruct((t, XA_W), F32),
                   jax.ShapeDtypeStruct((t, XA_W), BF16), jax.ShapeDtypeStruct((t, XA_W), BF16)],
        compiler_params=_cparams("parallel"),
        name="mem_kv",
    )(mem2d, g, w_bf16)


def _rel_bucket(rel):
    half = N_BUCKETS // 2
    max_exact = half // 2
    n = jnp.abs(rel)
    nf = jnp.maximum(n, 1).astype(F32)
    large = max_exact + (jnp.log(nf / max_exact) / math.log(MAX_DISTANCE / max_exact)
                         * (half - max_exact)).astype(jnp.int32)
    large = jnp.minimum(large, half - 1)
    return jnp.where(rel > 0, half, 0) + jnp.where(n < max_exact, n, large)


def _bias_from_buckets(bkt, tab_ref, head, vis, shift):
    bias = jnp.full(bkt.shape, tab_ref[0, head], F32)
    for t in range(1, N_BUCKETS):
        bias = jnp.where(bkt == t, tab_ref[t, head], bias)
    return jnp.where(vis, (bias - shift) * LOG2E, MASK_VALUE)


def _split_maps(qh):
    lane = lax.broadcasted_iota(jnp.int32, qh.shape, 1)
    zero = jnp.zeros_like(qh)
    return jnp.concatenate([jnp.where(lane < DA_HD, qh, zero), jnp.where(lane >= DA_HD, qh, zero)], axis=0)


def _ones_column(n):
    return jnp.where(lax.broadcasted_iota(jnp.int32, (n, DA_V), 1) == 0, 1.0, 0.0).astype(BF16)


def _diff_finish(o, lam, g, tq, lambda_init):
    d = o[:tq] - lam * o[tq:]
    return _rms(d, g) * (1.0 - lambda_init)


def _attn_self_kernel(tab_ref, lam_ref, q_ref, kt_ref, v_ref, bkt_ref, g_ref, o_ref,
                      bias_sc, q2_sc, m_sc, acc_sc, *, tq, lambda_init):
    b = pl.program_id(0)
    i = pl.program_id(1)

    @pl.when((b == 0) & (i == 0))
    def _build_bias():
        bkt = bkt_ref[...]
        r = lax.broadcasted_iota(jnp.int32, bkt.shape, 0)
        c = lax.broadcasted_iota(jnp.int32, bkt.shape, 1)
        vis = lax.shift_right_arithmetic(c - tq, CHUNK_SHIFT) <= lax.shift_right_arithmetic(r, CHUNK_SHIFT)
        for h in range(DA_HEADS):
            bias_sc[h] = _bias_from_buckets(bkt, tab_ref, h, vis, tab_ref[FAR_BUCKET, h])

    q = q_ref[...]
    for h in range(DA_HEADS):
        q2_sc[h] = _split_maps(q[:, h * DA_V:(h + 1) * DA_V])
    m_sc[...] = jnp.full(m_sc.shape, MASK_VALUE, F32)
    acc_sc[...] = jnp.zeros(acc_sc.shape, F32)
    ones_col = _ones_column(tq)

    def key_tile(h, start, bias):
        cols = slice(h * DA_V, (h + 1) * DA_V)
        kj = kt_ref[cols, pl.ds(start, tq)]
        vj = jnp.concatenate([v_ref[pl.ds(start, tq), cols], ones_col], axis=1)
        s = _dot(q2_sc[h], kj)
        if bias is not None:
            s = (s.reshape(2, tq, tq) + bias[None]).reshape(2 * tq, tq)
        m_prev = m_sc[h]
        m_new = jnp.maximum(m_prev, jnp.max(s, axis=1, keepdims=True))
        alpha = jnp.exp2(m_prev - m_new)
        p = jnp.exp2(s - jnp.concatenate([m_new] * (tq // 128), axis=1))
        acc_sc[h] = jnp.concatenate([alpha, alpha], axis=1) * acc_sc[h] + _dot(p.astype(BF16), vj)
        m_sc[h] = m_new

    def far_body(j, carry):
        start = pl.multiple_of(j * tq, tq)
        for h in range(DA_HEADS):
            key_tile(h, start, None)
        return carry

    lax.fori_loop(0, jnp.maximum(i - 1, 0), far_body, 0)

    @pl.when(i >= 1)
    def _prev_tile():
        start = pl.multiple_of((i - 1) * tq, tq)
        for h in range(DA_HEADS):
            key_tile(h, start, bias_sc[h, :, :tq])

    start = pl.multiple_of(i * tq, tq)
    lam = lam_ref[0]
    g = g_ref[...]
    for h in range(DA_HEADS):
        key_tile(h, start, bias_sc[h, :, tq:])
        acc = acc_sc[h]
        o = acc[:, :DA_V] / jnp.sum(acc[:, DA_V:], axis=1, keepdims=True)
        o_ref[:, h * DA_V:(h + 1) * DA_V] = _diff_finish(o, lam, g, tq, lambda_init).astype(BF16)


def _attn_self(q, ktb, vb, rel_table, lam, subln_g, batch, seq, lambda_init, tq):
    assert seq % tq == 0 and tq % 128 == 0 and tq >= MAX_DISTANCE and tq % CHUNK == 0
    nq = seq // tq
    rel = jnp.arange(2 * tq, dtype=jnp.int32)[None, :] - (tq + jnp.arange(tq, dtype=jnp.int32))[:, None]
    bkt = _rel_bucket(rel)
    smem = pl.BlockSpec(memory_space=pltpu.SMEM)
    return pl.pallas_call(
        functools.partial(_attn_self_kernel, tq=tq, lambda_init=lambda_init),
        grid=(batch, nq),
        in_specs=[smem, smem,
                  pl.BlockSpec((tq, DA_QK_W), lambda b, i: (b * nq + i, 0)),
                  pl.BlockSpec((DA_QK_W, seq), lambda b, i: (b, 0)),
                  pl.BlockSpec((seq, DA_V_W), lambda b, i: (b, 0)),
                  _resident((tq, 2 * tq)), _resident((1, DA_V))],
        out_specs=pl.BlockSpec((tq, DA_V_W), lambda b, i: (b * nq + i, 0)),
        out_shape=jax.ShapeDtypeStruct((batch * seq, DA_V_W), BF16),
        scratch_shapes=[pltpu.VMEM((DA_HEADS, tq, 2 * tq), F32),
                        pltpu.VMEM((DA_HEADS, 2 * tq, DA_V), BF16),
                        pltpu.VMEM((DA_HEADS, 2 * tq, 128), F32),
                        pltpu.VMEM((DA_HEADS, 2 * tq, 2 * DA_V), F32)],
        compiler_params=_cparams("arbitrary", "arbitrary"),
        name="attn_self",
    )(rel_table, lam, q, ktb, vb, bkt, subln_g)


def _attn_cached_kernel(tab_ref, lam_ref, q_ref, ckt_ref, cv_ref, nk_ref, nv_ref, bktc_ref, bktn_ref, g_ref, o_ref,
                        biasc_sc, biasn_sc, *, sq, past, lambda_init):
    @pl.when(pl.program_id(0) == 0)
    def _build_bias():
        for bkt_ref, sc, k0 in ((bktc_ref, biasc_sc, 0), (bktn_ref, biasn_sc, past)):
            bkt = bkt_ref[...]
            qpos = past + lax.broadcasted_iota(jnp.int32, bkt.shape, 0)
            kpos = k0 + lax.broadcasted_iota(jnp.int32, bkt.shape, 1)
            vis = lax.shift_right_arithmetic(kpos, CHUNK_SHIFT) <= lax.shift_right_arithmetic(qpos, CHUNK_SHIFT)
            for h in range(DA_HEADS):
                sc[h] = _bias_from_buckets(bkt, tab_ref, h, vis, 0.0)

    q = q_ref[...]
    lam = lam_ref[0]
    g = g_ref[...]
    for h in range(DA_HEADS):
        cols = slice(h * DA_V, (h + 1) * DA_V)
        q2 = _split_maps(q[:, cols])
        s_c = _dot(q2, ckt_ref[cols, :].astype(BF16))
        s_c = (s_c.reshape(2, sq, past) + biasc_sc[h][None]).reshape(2 * sq, past)
        s_n = lax.dot_general(q2, nk_ref[:, cols].astype(BF16), NT_DIMS, preferred_element_type=F32)
        s_n = (s_n.reshape(2, sq, sq) + biasn_sc[h][None]).reshape(2 * sq, sq)
        m = jnp.maximum(jnp.max(s_c, axis=1, keepdims=True), jnp.max(s_n, axis=1, keepdims=True))
        p_c = jnp.exp2(s_c - m)
        p_n = jnp.exp2(s_n - m)
        l = jnp.sum(p_c, axis=1, keepdims=True) + jnp.sum(p_n, axis=1, keepdims=True)
        v_c = cv_ref[pl.ds(h, past, stride=DA_HEADS), :]
        o = (_dot(p_c.astype(BF16), v_c.astype(BF16))
             + _dot(p_n.astype(BF16), nv_ref[:, cols].astype(BF16))) / l
        o_ref[:, cols] = _diff_finish(o, lam, g, sq, lambda_init).astype(BF16)


def _attn_cached(q, cache_kt2d, cache_v2d, nk, nv, rel_table, lam, subln_g, batch, sq, past, lambda_init):
    qpos = past + jnp.arange(sq, dtype=jnp.int32)
    bkt_c = _rel_bucket(jnp.arange(past, dtype=jnp.int32)[None, :] - qpos[:, None])
    bkt_n = _rel_bucket(qpos[None, :] - qpos[:, None])
    smem = pl.BlockSpec(memory_space=pltpu.SMEM)
    new_spec = pl.BlockSpec((sq, DA_QK_W), lambda b: (b, 0))
    return pl.pallas_call(
        functools.partial(_attn_cached_kernel, sq=sq, past=past, lambda_init=lambda_init),
        grid=(batch,),
        in_specs=[smem, smem, new_spec,
                  pl.BlockSpec((DA_QK_W, past), lambda b: (b, 0)),
                  pl.BlockSpec((past * DA_HEADS, DA_V), lambda b: (b, 0)),
                  new_spec, new_spec,
                  _resident((sq, past)), _resident((sq, sq)), _resident((1, DA_V))],
        out_specs=new_spec,
        out_shape=jax.ShapeDtypeStruct((batch * sq, DA_V_W), BF16),
        scratch_shapes=[pltpu.VMEM((DA_HEADS, sq, past), F32), pltpu.VMEM((DA_HEADS, sq, sq), F32)],
        compiler_params=_cparams("arbitrary"),
        name="attn_cached",
    )(rel_table, lam, q, cache_kt2d, cache_v2d, nk, nv, bkt_c, bkt_n, subln_g)


def _xattn_kernel(q_ref, mk_ref, mv_ref, o_ref):
    q = q_ref[...]
    for h in range(XA_HEADS):
        cols = slice(h * XA_HD, (h + 1) * XA_HD)
        s = lax.dot_general(q[:, cols], mk_ref[:, cols].astype(BF16), NT_DIMS, preferred_element_type=F32)
        p = jnp.exp(s - jnp.max(s, axis=1, keepdims=True))
        l = jnp.sum(p, axis=1, keepdims=True)
        o_ref[:, cols] = (_dot(p.astype(BF16), mv_ref[:, cols].astype(BF16)) / l).astype(BF16)


def _xattn(qc, mk2d, mv2d, batch, seq, tq):
    nq = seq // tq
    n_mem = mk2d.shape[0] // batch
    row = pl.BlockSpec((tq, XA_W), lambda b, i: (b * nq + i, 0))
    mem = pl.BlockSpec((n_mem, XA_W), lambda b, i: (b, 0))
    return pl.pallas_call(
        _xattn_kernel,
        grid=(batch, nq),
        in_specs=[row, mem, mem],
        out_specs=row,
        out_shape=jax.ShapeDtypeStruct((batch * seq, XA_W), BF16),
        compiler_params=_cparams("parallel", "parallel"),
        name="xattn",
    )(qc, mk2d, mv2d)


def _mix_ffn_kernel(x_ref, a_ref, b_ref, c_ref, gmix_ref, wg_ref, bg_ref, wpa_ref, wpb_ref, wpc_ref, wo_ref,
                    gffn_ref, wfi_ref, wfo_ref, gfin_ref, y_ref):
    x = x_ref[...]
    h = _rms(x, gmix_ref[...]).astype(BF16)
    merged = None
    for n, (br_ref, wp_ref) in enumerate(((a_ref, wpa_ref), (b_ref, wpb_ref), (c_ref, wpc_ref))):
        cols = slice(n * D_MODEL, (n + 1) * D_MODEL)
        gate = jax.nn.sigmoid(_dot(h, wg_ref[:, cols]) + bg_ref[:, cols])
        term = gate * _dot(br_ref[...], wp_ref[...])
        merged = term if merged is None else merged + term
    x1 = x + _dot(merged.astype(BF16), wo_ref[...])
    h2 = _rms(x1, gffn_ref[...]).astype(BF16)
    ff = None
    for c in range(D_FF // FFN_CHUNK):
        gcols = slice(c * FFN_CHUNK, (c + 1) * FFN_CHUNK)
        ucols = slice(D_FF + c * FFN_CHUNK, D_FF + (c + 1) * FFN_CHUNK)
        act = (jax.nn.silu(_dot(h2, wfi_ref[:, gcols])) * _dot(h2, wfi_ref[:, ucols])).astype(BF16)
        term = _dot(act, wfo_ref[gcols, :])
        ff = term if ff is None else ff + term
    y_ref[...] = _rms(x1 + ff, gfin_ref[...])


def _mix_ffn(x2d, a_out, b_out, c_out, p, tm):
    t = x2d.shape[0]
    row = pl.BlockSpec((tm, D_MODEL), lambda i: (i, 0))
    vec = _resident((1, D_MODEL))
    sq_w = _resident((D_MODEL, D_MODEL))
    return pl.pallas_call(
        _mix_ffn_kernel,
        grid=(t // tm,),
        in_specs=[row, row, row, row, vec, _resident((D_MODEL, N_BRANCH * D_MODEL)),
                  _resident((1, N_BRANCH * D_MODEL)), sq_w, sq_w, sq_w, sq_w, vec,
                  _resident((D_MODEL, 2 * D_FF)), _resident((D_FF, D_MODEL)), vec],
        out_specs=row,
        out_shape=jax.ShapeDtypeStruct((t, D_MODEL), F32),
        compiler_params=_cparams("parallel"),
        name="mix_ffn",
    )(x2d, a_out, b_out, c_out, p["norm_mix"], p["w_gate"], p["b_gate"], p["w_proj_a"], p["w_proj_b"],
      p["w_proj_c"], p["w_out"], p["norm_ffn"], p["w_ffn_in"], p["w_ffn_out"], p["norm_final"])


def _layer(x, mk2d, mv2d, cache_kt2d, cache_v2d, conv_state, h0, rel_table, lam, p, lambda_init, tm, tq):
    batch, seq, _ = x.shape
    x2d = x.reshape(batch * seq, D_MODEL)
    no_history = cache_kt2d is None
    q, k, v, kb, vb, qc, b_out, new_conv, h_last = _in_proj(x2d, conv_state, h0, p, tm, seq, no_history)
    if no_history:
        a_out = _attn_self(q, kb, vb, rel_table, lam, p["subln_g"], batch, seq, lambda_init, tq)
    else:
        past = cache_kt2d.shape[1]
        a_out = _attn_cached(q, cache_kt2d, cache_v2d, k, v, rel_table, lam, p["subln_g"], batch, seq, past,
                             lambda_init)
    c_out = _xattn(qc, mk2d, mv2d, batch, seq, min(seq, tm))
    y = _mix_ffn(x2d, a_out, b_out, c_out, p, tm)
    return y.reshape(batch, seq, D_MODEL), k, v, new_conv, h_last.reshape(batch, LRU_W)


def kernel(x_prompt, x_sample, mem_prompt, cache_k, cache_v, state_conv, state_lru, cache_mem_k, cache_mem_v,
           rel_table, norm_mix, w_in, lambda_q1, lambda_k1, lambda_q2, lambda_k2, subln_g, conv_w, conv_b,
           w_rg_a, b_rg_a, w_rg_x, b_rg_x, rg_lambda, norm_mem, w_mem_kv, w_proj_a, w_proj_b, w_proj_c,
           w_gate, b_gate, w_out, norm_ffn, w_ffn_in, w_ffn_out, norm_final):
    depth = w_in.shape[0]
    assert depth == 1, "the final norm is fused into the layer's last kernel"
    l = 0
    lambda_init = 0.8 - 0.6 * math.exp(-0.3 * l)
    bp, sp_, _ = x_prompt.shape
    bs, ss, _ = x_sample.shape
    past = cache_k.shape[2]
    n_mem = mem_prompt.shape[1]
    row = lambda a: a.reshape(1, -1).astype(F32)
    p = dict(norm_mix=row(norm_mix[l]), w_in=w_in[l].astype(BF16), subln_g=row(subln_g[l]),
             conv_w=conv_w[l], conv_b=row(conv_b[l]), w_rg_a=w_rg_a[l].astype(BF16), b_rg_a=row(b_rg_a[l]),
             w_rg_x=w_rg_x[l].astype(BF16), b_rg_x=row(b_rg_x[l]), rg_lambda=row(rg_lambda[l]),
             w_proj_a=w_proj_a[l].astype(BF16), w_proj_b=w_proj_b[l].astype(BF16),
             w_proj_c=w_proj_c[l].astype(BF16), w_gate=w_gate[l].astype(BF16), b_gate=row(b_gate[l]),
             w_out=w_out[l].astype(BF16), norm_ffn=row(norm_ffn[l]), w_ffn_in=w_ffn_in[l].astype(BF16),
             w_ffn_out=w_ffn_out[l].astype(BF16), norm_final=row(norm_final))
    lam = (jnp.exp(jnp.sum(lambda_q1[l] * lambda_k1[l]).astype(F32))
           - jnp.exp(jnp.sum(lambda_q2[l] * lambda_k2[l]).astype(F32)) + lambda_init).reshape(1)

    mk, mv, mkb, mvb = _mem_kv(mem_prompt.reshape(bp * n_mem, D_MODEL), row(norm_mem[l]),
                               w_mem_kv[l].astype(BF16), 256)
    zeros_conv = jnp.zeros((bp, CONV_W - 1, LRU_W), F32)
    zeros_h = jnp.zeros((bp, 1, LRU_W), F32)
    yp, kp, vp, cp, hp = _layer(x_prompt, mkb, mvb, None, None, zeros_conv, zeros_h, rel_table, lam, p,
                                lambda_init, tm=512, tq=256)
    cache_kt = jnp.transpose(cache_k[l], (0, 2, 3, 4, 1)).reshape(bs * DA_QK_W, past)
    ys, ks, vs, cs, hs = _layer(x_sample, cache_mem_k[l].reshape(bs * n_mem, XA_W),
                                cache_mem_v[l].reshape(bs * n_mem, XA_W),
                                cache_kt, cache_v[l].reshape(bs * past * DA_HEADS, DA_V),
                                state_conv[l], state_lru[l].reshape(bs, 1, LRU_W), rel_table, lam, p,
                                lambda_init, tm=256, tq=ss)
    kp = jnp.transpose(kp.reshape(bp, DA_HEADS, 2, DA_HD, sp_), (0, 4, 1, 2, 3))
    return (yp, ys,
            kp[None], vp.reshape(1, bp, sp_, DA_HEADS, DA_V),
            cp[None], hp[None],
            mk.reshape(1, bp, n_mem, XA_HEADS, XA_HD), mv.reshape(1, bp, n_mem, XA_HEADS, XA_HD),
            ks.reshape(1, bs, ss, DA_HEADS, 2, DA_HD), vs.reshape(1, bs, ss, DA_HEADS, DA_V),
            cs[None], hs[None])
```

```python
import functools
import math

import jax
import jax.numpy as jnp
from jax import lax
from jax.experimental import pallas as pl
from jax.experimental.pallas import tpu as pltpu

F32 = jnp.float32
BF16 = jnp.bfloat16

D_MODEL = 1024
CHUNK = 64
CHUNK_SHIFT = 6
DA_HEADS = 8
DA_HD = 64
DA_V = 2 * DA_HD
LRU_W = D_MODEL
LRU_BLOCKS = 8
LRU_BW = LRU_W // LRU_BLOCKS
CONV_W = 4
LRU_C = 8.0
XA_HEADS = 4
XA_HD = 256
N_BUCKETS = 32
MAX_DISTANCE = 128
D_FF = ((8 * D_MODEL + 3 * 256 - 1) // (3 * 256)) * 256
N_BRANCH = 3
EPS = 1e-6
DA_QK_W = DA_HEADS * 2 * DA_HD
DA_V_W = DA_HEADS * DA_V
XA_W = XA_HEADS * XA_HD
IN_OFFS = (0, DA_QK_W, 2 * DA_QK_W, 2 * DA_QK_W + DA_V_W, 2 * DA_QK_W + DA_V_W + LRU_W,
           2 * DA_QK_W + DA_V_W + 2 * LRU_W)
IN_W = 2 * DA_QK_W + DA_V_W + 2 * LRU_W + XA_W
LOG2E = math.log2(math.e)
DA_Q_SCALE = DA_HD ** -0.5 * LOG2E
XA_SCALE = XA_HD ** -0.5
MASK_VALUE = -1e30
FAR_BUCKET = N_BUCKETS // 2 - 1
FFN_CHUNK = D_FF // 2
PROJ_CHUNK = 256
ATTN_LOOKAHEAD = 3

V7X_VMEM_BYTES = 64 * 1024 * 1024
VMEM_LIMIT = V7X_VMEM_BYTES - 8 * 1024 * 1024

NT_DIMS = (((1,), (1,)), ((), ()))


def _cparams(*sem):
    return pltpu.CompilerParams(dimension_semantics=sem, vmem_limit_bytes=VMEM_LIMIT)


def _resident(shape):
    nd = len(shape)
    return pl.BlockSpec(shape, lambda *_: (0,) * nd, pipeline_mode=pl.Buffered(1))


def _rms(x, g):
    return x * lax.rsqrt(jnp.mean(x * x, axis=-1, keepdims=True) + EPS) * g


def _dot(a, b):
    return jnp.dot(a, b, preferred_element_type=F32)


def _in_proj_kernel(x_ref, g_ref, w_ref, cs_ref, h0_ref, cw_ref, cb_ref, wa_ref, ba_ref, wx_ref, bx_ref, lam_ref,
                    q_ref, k_ref, v_ref, kb_ref, vb_ref, qc_ref, bo_ref, nc_ref, hl_ref,
                    xpad_sc, a_sc, u_sc, hs_sc, gg_sc, hc_sc, *, nt, seg, feature_major):
    tm = x_ref.shape[0]
    nseg = tm // seg
    halo = CONV_W - 1
    base = 8 - halo

    @pl.when(pl.program_id(0) % nt == 0)
    def _load_state():
        for s in range(nseg):
            xpad_sc[s, base:8, :] = cs_ref[s]
        hc_sc[...] = h0_ref[...]

    h = _rms(x_ref[...], g_ref[...]).astype(BF16)

    def proj(n, c0, width):
        return _dot(h, w_ref[:, IN_OFFS[n] + c0:IN_OFFS[n] + c0 + width])

    xb = proj(3, 0, LRU_W)
    for s in range(nseg):
        xpad_sc[s, 8:8 + seg, :] = xb[s * seg:(s + 1) * seg]

    def proj_chunk(kind, c0):
        cols = slice(c0, c0 + PROJ_CHUNK)
        if kind == "g":
            gg_sc[:, cols] = jax.nn.gelu(proj(4, c0, PROJ_CHUNK))
        elif kind == "q":
            qs = proj(0, c0, PROJ_CHUNK) * DA_Q_SCALE
            if feature_major:
                q_ref[cols, :] = qs.T.astype(BF16)
            else:
                q_ref[:, cols] = qs.astype(BF16)
        elif kind == "k":
            kc = proj(1, c0, PROJ_CHUNK)
            kb_ref[:, cols] = kc.astype(BF16)
            if feature_major:
                k_ref[cols, :] = kc.T
            else:
                k_ref[:, cols] = kc
        elif kind == "v":
            vc = proj(2, c0, PROJ_CHUNK)
            v_ref[:, cols] = vc
            if feature_major:
                vb_ref[cols, :] = vc.T.astype(BF16)
            else:
                vb_ref[:, cols] = vc.astype(BF16)
        else:
            qc_ref[:, cols] = (proj(5, c0, PROJ_CHUNK) * XA_SCALE).astype(BF16)

    chunks = [(kind, c0) for kind, width in (("g", LRU_W), ("q", DA_QK_W), ("k", DA_QK_W), ("v", DA_V_W), ("c", XA_W))
              for c0 in range(0, width, PROJ_CHUNK)]

    cw = cw_ref[...]
    sp = jax.nn.softplus(-lam_ref[...])
    for n in range(LRU_BLOCKS):
        for kind, c0 in chunks[n * len(chunks) // LRU_BLOCKS:(n + 1) * len(chunks) // LRU_BLOCKS]:
            proj_chunk(kind, c0)
        cols = slice(n * LRU_BW, (n + 1) * LRU_BW)
        xc_segs = []
        for s in range(nseg):
            xc = xpad_sc[s, base:base + seg, cols] * cw[0:1, cols]
            for j in range(1, CONV_W):
                xc = xc + xpad_sc[s, base + j:base + j + seg, cols] * cw[j:j + 1, cols]
            xc_segs.append(cb_ref[:, cols] + xc)
        xc = xc_segs[0] if nseg == 1 else jnp.concatenate(xc_segs, axis=0)
        xcb = xc.astype(BF16)
        r = jax.nn.sigmoid(_dot(xcb, wa_ref[n]) + ba_ref[:, cols])
        gate = jax.nn.sigmoid(_dot(xcb, wx_ref[n]) + bx_ref[:, cols])
        log_a = -LRU_C * r * sp[:, cols]
        a = jnp.exp(log_a)
        a_sc[:, cols] = a
        u_sc[:, cols] = jnp.sqrt(-jnp.tanh(log_a) * (a * a + 1.0)) * (gate * xc)
    for s in range(nseg):
        tail = xpad_sc[s, base + seg:8 + seg, :]
        nc_ref[s] = tail
        xpad_sc[s, base:8, :] = tail

    for s in range(nseg):
        def step(t, hprev, row0=s * seg):
            hnew = a_sc[pl.ds(row0 + t, 1), :] * hprev + u_sc[pl.ds(row0 + t, 1), :]
            hs_sc[pl.ds(row0 + t, 1), :] = hnew
            return hnew

        hlast = lax.fori_loop(0, seg, step, hc_sc[s], unroll=8)
        hc_sc[s] = hlast
        hl_ref[s] = hlast
    bo_ref[...] = (hs_sc[...] * gg_sc[...]).astype(BF16)


def _in_proj(x2d, conv_state, h0, p, tm, seq, feature_major):
    t = x2d.shape[0]
    seg = min(tm, seq)
    nseg = tm // seg
    nt = seq // seg
    assert tm % seg == 0 and seq % seg == 0 and seg % 8 == 0 and seg >= CONV_W - 1
    batch_block = lambda i: (i // nt, 0, 0)
    row = lambda width: pl.BlockSpec((tm, width), lambda i: (i, 0))
    sds = lambda width, dt: jax.ShapeDtypeStruct((t, width), dt)
    if feature_major:
        assert nseg == 1
        fm_spec = lambda width: pl.BlockSpec((width, tm), lambda i: (i // nt, i % nt))
        fm_sds = lambda width, dt: jax.ShapeDtypeStruct((t // seq * width, seq), dt)
    else:
        fm_spec, fm_sds = row, sds
    vec = _resident((1, LRU_W))
    gate_w = _resident((LRU_BLOCKS, LRU_BW, LRU_BW))
    conv_spec = pl.BlockSpec((nseg, CONV_W - 1, LRU_W), batch_block)
    state_spec = pl.BlockSpec((nseg, 1, LRU_W), batch_block)
    return pl.pallas_call(
        functools.partial(_in_proj_kernel, nt=nt, seg=seg, feature_major=feature_major),
        grid=(t // tm,),
        in_specs=[row(D_MODEL), _resident((1, D_MODEL)), _resident((D_MODEL, IN_W)), conv_spec, state_spec,
                  _resident((CONV_W, LRU_W)), vec, gate_w, vec, gate_w, vec, vec],
        out_specs=[fm_spec(DA_QK_W), fm_spec(DA_QK_W), row(DA_V_W), row(DA_QK_W), fm_spec(DA_V_W), row(XA_W),
                   row(LRU_W), conv_spec, state_spec],
        out_shape=[fm_sds(DA_QK_W, BF16), fm_sds(DA_QK_W, F32), sds(DA_V_W, F32), sds(DA_QK_W, BF16),
                   fm_sds(DA_V_W, BF16), sds(XA_W, BF16), sds(LRU_W, BF16),
                   jax.ShapeDtypeStruct(conv_state.shape, F32), jax.ShapeDtypeStruct(h0.shape, F32)],
        scratch_shapes=[pltpu.VMEM((nseg, 8 + seg, LRU_W), F32), pltpu.VMEM((tm, LRU_W), F32),
                        pltpu.VMEM((tm, LRU_W), F32), pltpu.VMEM((tm, LRU_W), F32), pltpu.VMEM((tm, LRU_W), F32),
                        pltpu.VMEM((nseg, 1, LRU_W), F32)],
        compiler_params=_cparams("arbitrary"),
        name="in_proj",
    )(x2d, p["norm_mix"], p["w_in"], conv_state, h0, p["conv_w"], p["conv_b"], p["w_rg_a"], p["b_rg_a"],
      p["w_rg_x"], p["b_rg_x"], p["rg_lambda"])


def _mem_kv_kernel(m_ref, g_ref, w_ref, mk_ref, mv_ref, mkb_ref, mvb_ref):
    h = _rms(m_ref[...], g_ref[...]).astype(BF16)
    mk = _dot(h, w_ref[:, :XA_W])
    mv = _dot(h, w_ref[:, XA_W:])
    mk_ref[...] = mk
    mv_ref[...] = mv
    mkb_ref[...] = mk.astype(BF16)
    mvb_ref[...] = mv.astype(BF16)


def _mem_kv(mem2d, g, w_bf16, tm):
    t = mem2d.shape[0]
    row = pl.BlockSpec((tm, XA_W), lambda i: (i, 0))
    return pl.pallas_call(
        _mem_kv_kernel,
        grid=(t // tm,),
        in_specs=[pl.BlockSpec((tm, D_MODEL), lambda i: (i, 0)), _resident((1, D_MODEL)),
                  _resident((D_MODEL, 2 * XA_W))],
        out_specs=[row, row, row, row],
        out_shape=[jax.ShapeDtypeStruct((t, XA_W), F32), jax.ShapeDtypeStruct((t, XA_W), F32),
                   jax.ShapeDtypeStruct((t, XA_W), BF16), jax.ShapeDtypeStruct((t, XA_W), BF16)],
        compiler_params=_cparams("parallel"),
        name="mem_kv",
    )(mem2d, g, w_bf16)


def _rel_bucket(rel):
    half = N_BUCKETS // 2
    max_exact = half // 2
    n = jnp.abs(rel)
    nf = jnp.maximum(n, 1).astype(F32)
    large = max_exact + (jnp.log(nf / max_exact) / math.log(MAX_DISTANCE / max_exact)
                         * (half - max_exact)).astype(jnp.int32)
    large = jnp.minimum(large, half - 1)
    return jnp.where(rel > 0, half, 0) + jnp.where(n < max_exact, n, large)


def _bias_from_buckets(bkt, tab_ref, head, vis, shift):
    bias = jnp.full(bkt.shape, tab_ref[0, head], F32)
    for t in range(1, N_BUCKETS):
        bias = jnp.where(bkt == t, tab_ref[t, head], bias)
    return jnp.where(vis, (bias - shift) * LOG2E, MASK_VALUE)


def _split_maps(qh):
    lane = lax.broadcasted_iota(jnp.int32, qh.shape, 1)
    zero = jnp.zeros_like(qh)
    return jnp.concatenate([jnp.where(lane < DA_HD, qh, zero), jnp.where(lane >= DA_HD, qh, zero)], axis=0)


def _ones_column(n):
    return jnp.where(lax.broadcasted_iota(jnp.int32, (n, DA_V), 1) == 0, 1.0, 0.0).astype(BF16)


def _diff_finish(o, lam, g, tq, lambda_init):
    d = o[:tq] - lam * o[tq:]
    return _rms(d, g) * (1.0 - lambda_init)


def _attn_self_kernel(tab_ref, lam_ref, qt_ref, k_ref, vt_ref, bkt_ref, g_ref, o_ref,
                      bias_sc, q2t_sc, m_sc, l_sc, acc_sc, s_sc, *, tq, lambda_init):
    b = pl.program_id(0)
    i = pl.program_id(1)

    @pl.when((b == 0) & (i == 0))
    def _build_bias():
        bkt = bkt_ref[...]
        kpos = lax.broadcasted_iota(jnp.int32, bkt.shape, 0)
        qpos = lax.broadcasted_iota(jnp.int32, bkt.shape, 1)
        vis = lax.shift_right_arithmetic(kpos - tq, CHUNK_SHIFT) <= lax.shift_right_arithmetic(qpos, CHUNK_SHIFT)
        for h in range(DA_HEADS):
            bias_sc[h] = _bias_from_buckets(bkt, tab_ref, h, vis, tab_ref[FAR_BUCKET, h])

    feat = lax.broadcasted_iota(jnp.int32, (2 * DA_HD, tq), 0)
    for h in range(DA_HEADS):
        qt = qt_ref[h * 2 * DA_HD:(h + 1) * 2 * DA_HD, :]
        zero = jnp.zeros_like(qt)
        q2t_sc[h] = jnp.concatenate([jnp.where(feat < DA_HD, qt, zero), jnp.where(feat >= DA_HD, qt, zero)], axis=1)
    m_sc[...] = jnp.full(m_sc.shape, MASK_VALUE, F32)
    l_sc[...] = jnp.zeros(l_sc.shape, F32)
    acc_sc[...] = jnp.zeros(acc_sc.shape, F32)

    def logits(h, start):
        kj = k_ref[pl.ds(start, tq), h * DA_V:(h + 1) * DA_V]
        return _dot(kj, q2t_sc[h])

    def update(h, start, s, bias):
        vtj = vt_ref[h * DA_V:(h + 1) * DA_V, pl.ds(start, tq)]
        if bias is not None:
            s = s + jnp.concatenate([bias, bias], axis=1)
        m_prev = m_sc[h]
        m_new = jnp.maximum(m_prev, jnp.max(s, axis=0, keepdims=True))
        alpha = jnp.exp2(m_prev - m_new)
        p = jnp.exp2(s - m_new[0:1])
        l_sc[h] = alpha * l_sc[h] + jnp.sum(p, axis=0, keepdims=True)
        acc_sc[h] = alpha[0:1] * acc_sc[h] + _dot(vtj, p.astype(BF16))
        m_sc[h] = m_new

    def key_tile(start, bias_of, after=None):
        nslot = ATTN_LOOKAHEAD + 1
        for h in range(ATTN_LOOKAHEAD):
            s_sc[h % nslot] = logits(h, start)
        for h in range(DA_HEADS):
            if h + ATTN_LOOKAHEAD < DA_HEADS:
                s_sc[(h + ATTN_LOOKAHEAD) % nslot] = logits(h + ATTN_LOOKAHEAD, start)
            update(h, start, s_sc[h % nslot], bias_of(h))
            if after is not None:
                after(h)

    def far_body(j, carry):
        key_tile(pl.multiple_of(j * tq, tq), lambda h: None)
        return carry

    lax.fori_loop(0, jnp.maximum(i - 1, 0), far_body, 0)

    @pl.when(i >= 1)
    def _prev_tile():
        key_tile(pl.multiple_of((i - 1) * tq, tq), lambda h: bias_sc[h, :tq, :])

    lam = lam_ref[0]
    g = g_ref[...]

    def finish(h):
        o = acc_sc[h] / l_sc[h][0:1]
        d = o[:, :tq] - lam * o[:, tq:]
        y = d * lax.rsqrt(jnp.mean(d * d, axis=0, keepdims=True) + EPS) * g * (1.0 - lambda_init)
        o_ref[:, h * DA_V:(h + 1) * DA_V] = y.T.astype(BF16)

    key_tile(pl.multiple_of(i * tq, tq), lambda h: bias_sc[h, tq:, :], after=finish)


def _attn_self(qtb, kb, vtb, rel_table, lam, subln_g, batch, seq, lambda_init, tq):
    assert seq % tq == 0 and tq % 128 == 0 and tq >= MAX_DISTANCE and tq % CHUNK == 0
    nq = seq // tq
    rel = jnp.arange(2 * tq, dtype=jnp.int32)[:, None] - (tq + jnp.arange(tq, dtype=jnp.int32))[None, :]
    bkt = _rel_bucket(rel)
    smem = pl.BlockSpec(memory_space=pltpu.SMEM)
    return pl.pallas_call(
        functools.partial(_attn_self_kernel, tq=tq, lambda_init=lambda_init),
        grid=(batch, nq),
        in_specs=[smem, smem,
                  pl.BlockSpec((DA_QK_W, tq), lambda b, i: (b, i)),
                  pl.BlockSpec((seq, DA_QK_W), lambda b, i: (b, 0)),
                  pl.BlockSpec((DA_V_W, seq), lambda b, i: (b, 0)),
                  _resident((2 * tq, tq)), _resident((DA_V, 1))],
        out_specs=pl.BlockSpec((tq, DA_V_W), lambda b, i: (b * nq + i, 0)),
        out_shape=jax.ShapeDtypeStruct((batch * seq, DA_V_W), BF16),
        scratch_shapes=[pltpu.VMEM((DA_HEADS, 2 * tq, tq), F32),
                        pltpu.VMEM((DA_HEADS, 2 * DA_HD, 2 * tq), BF16),
                        pltpu.VMEM((DA_HEADS, 8, 2 * tq), F32),
                        pltpu.VMEM((DA_HEADS, 8, 2 * tq), F32),
                        pltpu.VMEM((DA_HEADS, DA_V, 2 * tq), F32),
                        pltpu.VMEM((ATTN_LOOKAHEAD + 1, tq, 2 * tq), F32)],
        compiler_params=_cparams("arbitrary", "arbitrary"),
        name="attn_self",
    )(rel_table, lam, qtb, kb, vtb, bkt, subln_g.reshape(DA_V, 1))


def _attn_cached_kernel(tab_ref, lam_ref, q_ref, ckt_ref, cv_ref, nk_ref, nv_ref, bktc_ref, bktn_ref, g_ref, o_ref,
                        biasc_sc, biasn_sc, *, sq, past, lambda_init):
    @pl.when(pl.program_id(0) == 0)
    def _build_bias():
        for bkt_ref, sc, k0 in ((bktc_ref, biasc_sc, 0), (bktn_ref, biasn_sc, past)):
            bkt = bkt_ref[...]
            qpos = past + lax.broadcasted_iota(jnp.int32, bkt.shape, 0)
            kpos = k0 + lax.broadcasted_iota(jnp.int32, bkt.shape, 1)
            vis = lax.shift_right_arithmetic(kpos, CHUNK_SHIFT) <= lax.shift_right_arithmetic(qpos, CHUNK_SHIFT)
            for h in range(DA_HEADS):
                sc[h] = _bias_from_buckets(bkt, tab_ref, h, vis, 0.0)

    q = q_ref[...]
    lam = lam_ref[0]
    g = g_ref[...]
    for h in range(DA_HEADS):
        cols = slice(h * DA_V, (h + 1) * DA_V)
        q2 = _split_maps(q[:, cols])
        s_c = _dot(q2, ckt_ref[cols, :].astype(BF16))
        s_c = (s_c.reshape(2, sq, past) + biasc_sc[h][None]).reshape(2 * sq, past)
        s_n = lax.dot_general(q2, nk_ref[:, cols].astype(BF16), NT_DIMS, preferred_element_type=F32)
        s_n = (s_n.reshape(2, sq, sq) + biasn_sc[h][None]).reshape(2 * sq, sq)
        m = jnp.maximum(jnp.max(s_c, axis=1, keepdims=True), jnp.max(s_n, axis=1, keepdims=True))
        p_c = jnp.exp2(s_c - m)
        p_n = jnp.exp2(s_n - m)
        l = jnp.sum(p_c, axis=1, keepdims=True) + jnp.sum(p_n, axis=1, keepdims=True)
        v_c = cv_ref[pl.ds(h, past, stride=DA_HEADS), :]
        o = (_dot(p_c.astype(BF16), v_c.astype(BF16))
             + _dot(p_n.astype(BF16), nv_ref[:, cols].astype(BF16))) / l
        o_ref[:, cols] = _diff_finish(o, lam, g, sq, lambda_init).astype(BF16)


def _attn_cached(q, cache_kt2d, cache_v2d, nk, nv, rel_table, lam, subln_g, batch, sq, past, lambda_init):
    qpos = past + jnp.arange(sq, dtype=jnp.int32)
    bkt_c = _rel_bucket(jnp.arange(past, dtype=jnp.int32)[None, :] - qpos[:, None])
    bkt_n = _rel_bucket(qpos[None, :] - qpos[:, None])
    smem = pl.BlockSpec(memory_space=pltpu.SMEM)
    new_spec = pl.BlockSpec((sq, DA_QK_W), lambda b: (b, 0))
    return pl.pallas_call(
        functools.partial(_attn_cached_kernel, sq=sq, past=past, lambda_init=lambda_init),
        grid=(batch,),
        in_specs=[smem, smem, new_spec,
                  pl.BlockSpec((DA_QK_W, past), lambda b: (b, 0)),
                  pl.BlockSpec((past * DA_HEADS, DA_V), lambda b: (b, 0)),
                  new_spec, new_spec,
                  _resident((sq, past)), _resident((sq, sq)), _resident((1, DA_V))],
        out_specs=new_spec,
        out_shape=jax.ShapeDtypeStruct((batch * sq, DA_V_W), BF16),
        scratch_shapes=[pltpu.VMEM((DA_HEADS, sq, past), F32), pltpu.VMEM((DA_HEADS, sq, sq), F32)],
        compiler_params=_cparams("arbitrary"),
        name="attn_cached",
    )(rel_table, lam, q, cache_kt2d, cache_v2d, nk, nv, bkt_c, bkt_n, subln_g)


def _xattn_kernel(q_ref, mk_ref, mv_ref, o_ref):
    q = q_ref[...]
    for h in range(XA_HEADS):
        cols = slice(h * XA_HD, (h + 1) * XA_HD)
        s = lax.dot_general(q[:, cols], mk_ref[:, cols].astype(BF16), NT_DIMS, preferred_element_type=F32)
        p = jnp.exp(s - jnp.max(s, axis=1, keepdims=True))
        l = jnp.sum(p, axis=1, keepdims=True)
        o_ref[:, cols] = (_dot(p.astype(BF16), mv_ref[:, cols].astype(BF16)) / l).astype(BF16)


def _xattn(qc, mk2d, mv2d, batch, seq, tq):
    nq = seq // tq
    n_mem = mk2d.shape[0] // batch
    row = pl.BlockSpec((tq, XA_W), lambda b, i: (b * nq + i, 0))
    mem = pl.BlockSpec((n_mem, XA_W), lambda b, i: (b, 0))
    return pl.pallas_call(
        _xattn_kernel,
        grid=(batch, nq),
        in_specs=[row, mem, mem],
        out_specs=row,
        out_shape=jax.ShapeDtypeStruct((batch * seq, XA_W), BF16),
        compiler_params=_cparams("parallel", "parallel"),
        name="xattn",
    )(qc, mk2d, mv2d)


def _mix_ffn_kernel(x_ref, a_ref, b_ref, c_ref, gmix_ref, wg_ref, bg_ref, wpa_ref, wpb_ref, wpc_ref, wo_ref,
                    gffn_ref, wfi_ref, wfo_ref, gfin_ref, y_ref):
    x = x_ref[...]
    h = _rms(x, gmix_ref[...]).astype(BF16)
    merged = None
    for n, (br_ref, wp_ref) in enumerate(((a_ref, wpa_ref), (b_ref, wpb_ref), (c_ref, wpc_ref))):
        cols = slice(n * D_MODEL, (n + 1) * D_MODEL)
        gate = jax.nn.sigmoid(_dot(h, wg_ref[:, cols]) + bg_ref[:, cols])
        term = gate * _dot(br_ref[...], wp_ref[...])
        merged = term if merged is None else merged + term
    x1 = x + _dot(merged.astype(BF16), wo_ref[...])
    h2 = _rms(x1, gffn_ref[...]).astype(BF16)
    ff = None
    for c in range(D_FF // FFN_CHUNK):
        gcols = slice(c * FFN_CHUNK, (c + 1) * FFN_CHUNK)
        ucols = slice(D_FF + c * FFN_CHUNK, D_FF + (c + 1) * FFN_CHUNK)
        act = (jax.nn.silu(_dot(h2, wfi_ref[:, gcols])) * _dot(h2, wfi_ref[:, ucols])).astype(BF16)
        term = _dot(act, wfo_ref[gcols, :])
        ff = term if ff is None else ff + term
    y_ref[...] = _rms(x1 + ff, gfin_ref[...])


def _mix_ffn(x2d, a_out, b_out, c_out, p, tm):
    t = x2d.shape[0]
    row = pl.BlockSpec((tm, D_MODEL), lambda i: (i, 0))
    vec = _resident((1, D_MODEL))
    sq_w = _resident((D_MODEL, D_MODEL))
    return pl.pallas_call(
        _mix_ffn_kernel,
        grid=(t // tm,),
        in_specs=[row, row, row, row, vec, _resident((D_MODEL, N_BRANCH * D_MODEL)),
                  _resident((1, N_BRANCH * D_MODEL)), sq_w, sq_w, sq_w, sq_w, vec,
                  _resident((D_MODEL, 2 * D_FF)), _resident((D_FF, D_MODEL)), vec],
        out_specs=row,
        out_shape=jax.ShapeDtypeStruct((t, D_MODEL), F32),
        compiler_params=_cparams("parallel"),
        name="mix_ffn",
    )(x2d, a_out, b_out, c_out, p["norm_mix"], p["w_gate"], p["b_gate"], p["w_proj_a"], p["w_proj_b"],
      p["w_proj_c"], p["w_out"], p["norm_ffn"], p["w_ffn_in"], p["w_ffn_out"], p["norm_final"])


def _layer(x, mk2d, mv2d, cache_kt2d, cache_v2d, conv_state, h0, rel_table, lam, p, lambda_init, tm, tq):
    batch, seq, _ = x.shape
    x2d = x.reshape(batch * seq, D_MODEL)
    no_history = cache_kt2d is None
    q, k, v, kb, vb, qc, b_out, new_conv, h_last = _in_proj(x2d, conv_state, h0, p, tm, seq, no_history)
    if no_history:
        a_out = _attn_self(q, kb, vb, rel_table, lam, p["subln_g"], batch, seq, lambda_init, tq)
    else:
        past = cache_kt2d.shape[1]
        a_out = _attn_cached(q, cache_kt2d, cache_v2d, k, v, rel_table, lam, p["subln_g"], batch, seq, past,
                             lambda_init)
    c_out = _xattn(qc, mk2d, mv2d, batch, seq, min(seq, tm))
    y = _mix_ffn(x2d, a_out, b_out, c_out, p, tm)
    return y.reshape(batch, seq, D_MODEL), k, v, new_conv, h_last.reshape(batch, LRU_W)


def kernel(x_prompt, x_sample, mem_prompt, cache_k, cache_v, state_conv, state_lru, cache_mem_k, cache_mem_v,
           rel_table, norm_mix, w_in, lambda_q1, lambda_k1, lambda_q2, lambda_k2, subln_g, conv_w, conv_b,
           w_rg_a, b_rg_a, w_rg_x, b_rg_x, rg_lambda, norm_mem, w_mem_kv, w_proj_a, w_proj_b, w_proj_c,
           w_gate, b_gate, w_out, norm_ffn, w_ffn_in, w_ffn_out, norm_final):
    depth = w_in.shape[0]
    assert depth == 1, "the final norm is fused into the layer's last kernel"
    l = 0
    lambda_init = 0.8 - 0.6 * math.exp(-0.3 * l)
    bp, sp_, _ = x_prompt.shape
    bs, ss, _ = x_sample.shape
    past = cache_k.shape[2]
    n_mem = mem_prompt.shape[1]
    row = lambda a: a.reshape(1, -1).astype(F32)
    p = dict(norm_mix=row(norm_mix[l]), w_in=w_in[l].astype(BF16), subln_g=row(subln_g[l]),
             conv_w=conv_w[l], conv_b=row(conv_b[l]), w_rg_a=w_rg_a[l].astype(BF16), b_rg_a=row(b_rg_a[l]),
             w_rg_x=w_rg_x[l].astype(BF16), b_rg_x=row(b_rg_x[l]), rg_lambda=row(rg_lambda[l]),
             w_proj_a=w_proj_a[l].astype(BF16), w_proj_b=w_proj_b[l].astype(BF16),
             w_proj_c=w_proj_c[l].astype(BF16), w_gate=w_gate[l].astype(BF16), b_gate=row(b_gate[l]),
             w_out=w_out[l].astype(BF16), norm_ffn=row(norm_ffn[l]), w_ffn_in=w_ffn_in[l].astype(BF16),
             w_ffn_out=w_ffn_out[l].astype(BF16), norm_final=row(norm_final))
    lam = (jnp.exp(jnp.sum(lambda_q1[l] * lambda_k1[l]).astype(F32))
           - jnp.exp(jnp.sum(lambda_q2[l] * lambda_k2[l]).astype(F32)) + lambda_init).reshape(1)

    mk, mv, mkb, mvb = _mem_kv(mem_prompt.reshape(bp * n_mem, D_MODEL), row(norm_mem[l]),
                               w_mem_kv[l].astype(BF16), 256)
    zeros_conv = jnp.zeros((bp, CONV_W - 1, LRU_W), F32)
    zeros_h = jnp.zeros((bp, 1, LRU_W), F32)
    yp, kp, vp, cp, hp = _layer(x_prompt, mkb, mvb, None, None, zeros_conv, zeros_h, rel_table, lam, p,
                                lambda_init, tm=512, tq=256)
    cache_kt = jnp.transpose(cache_k[l], (0, 2, 3, 4, 1)).reshape(bs * DA_QK_W, past)
    ys, ks, vs, cs, hs = _layer(x_sample, cache_mem_k[l].reshape(bs * n_mem, XA_W),
                                cache_mem_v[l].reshape(bs * n_mem, XA_W),
                                cache_kt, cache_v[l].reshape(bs * past * DA_HEADS, DA_V),
                                state_conv[l], state_lru[l].reshape(bs, 1, LRU_W), rel_table, lam, p,
                                lambda_init, tm=256, tq=ss)
    kp = jnp.transpose(kp.reshape(bp, DA_HEADS, 2, DA_HD, sp_), (0, 4, 1, 2, 3))
    return (yp, ys,
            kp[None], vp.reshape(1, bp, sp_, DA_HEADS, DA_V),
            cp[None], hp[None],
            mk.reshape(1, bp, n_mem, XA_HEADS, XA_HD), mv.reshape(1, bp, n_mem, XA_HEADS, XA_HD),
            ks.reshape(1, bs, ss, DA_HEADS, 2, DA_HD), vs.reshape(1, bs, ss, DA_HEADS, DA_V),
            cs[None], hs[None])
```

```python
import functools
import math

import jax
import jax.numpy as jnp
from jax import lax
from jax.experimental import pallas as pl
from jax.experimental.pallas import tpu as pltpu

F32 = jnp.float32
BF16 = jnp.bfloat16

D_MODEL = 1024
CHUNK = 64
CHUNK_SHIFT = 6
DA_HEADS = 8
DA_HD = 64
DA_V = 2 * DA_HD
LRU_W = D_MODEL
LRU_BLOCKS = 8
LRU_BW = LRU_W // LRU_BLOCKS
CONV_W = 4
LRU_C = 8.0
XA_HEADS = 4
XA_HD = 256
N_BUCKETS = 32
MAX_DISTANCE = 128
D_FF = ((8 * D_MODEL + 3 * 256 - 1) // (3 * 256)) * 256
N_BRANCH = 3
EPS = 1e-6
DA_QK_W = DA_HEADS * 2 * DA_HD
DA_V_W = DA_HEADS * DA_V
XA_W = XA_HEADS * XA_HD
IN_OFFS = (0, DA_QK_W, 2 * DA_QK_W, 2 * DA_QK_W + DA_V_W, 2 * DA_QK_W + DA_V_W + LRU_W,
           2 * DA_QK_W + DA_V_W + 2 * LRU_W)
IN_W = 2 * DA_QK_W + DA_V_W + 2 * LRU_W + XA_W
LOG2E = math.log2(math.e)
DA_Q_SCALE = DA_HD ** -0.5 * LOG2E
XA_SCALE = XA_HD ** -0.5
MASK_VALUE = -1e30
FAR_BUCKET = N_BUCKETS // 2 - 1
FFN_CHUNK = D_FF // 2
PROJ_CHUNK = 256
ATTN_LOOKAHEAD = 3

V7X_VMEM_BYTES = 64 * 1024 * 1024
VMEM_LIMIT = V7X_VMEM_BYTES - 8 * 1024 * 1024

NT_DIMS = (((1,), (1,)), ((), ()))


def _cparams(*sem):
    return pltpu.CompilerParams(dimension_semantics=sem, vmem_limit_bytes=VMEM_LIMIT)


def _resident(shape):
    nd = len(shape)
    return pl.BlockSpec(shape, lambda *_: (0,) * nd, pipeline_mode=pl.Buffered(1))


def _rms(x, g):
    return x * lax.rsqrt(jnp.mean(x * x, axis=-1, keepdims=True) + EPS) * g


def _dot(a, b):
    return jnp.dot(a, b, preferred_element_type=F32)


def _in_proj_kernel(x_ref, g_ref, w_ref, cs_ref, h0_ref, cw_ref, cb_ref, wa_ref, ba_ref, wx_ref, bx_ref, lam_ref,
                    q_ref, k_ref, v_ref, kb_ref, vb_ref, qc_ref, bo_ref, nc_ref, hl_ref,
                    xpad_sc, a_sc, u_sc, hs_sc, gg_sc, hc_sc, *, nt, seg, feature_major):
    tm = x_ref.shape[0]
    nseg = tm // seg
    halo = CONV_W - 1
    base = 8 - halo

    @pl.when(pl.program_id(0) % nt == 0)
    def _load_state():
        for s in range(nseg):
            xpad_sc[s, base:8, :] = cs_ref[s]
        hc_sc[...] = h0_ref[...]

    h = _rms(x_ref[...], g_ref[...]).astype(BF16)

    def proj(n, c0, width):
        return _dot(h, w_ref[:, IN_OFFS[n] + c0:IN_OFFS[n] + c0 + width])

    xb = proj(3, 0, LRU_W)
    for s in range(nseg):
        xpad_sc[s, 8:8 + seg, :] = xb[s * seg:(s + 1) * seg]

    def proj_chunk(kind, c0):
        cols = slice(c0, c0 + PROJ_CHUNK)
        if kind == "g":
            gg_sc[:, cols] = jax.nn.gelu(proj(4, c0, PROJ_CHUNK))
        elif kind == "q":
            qs = proj(0, c0, PROJ_CHUNK) * DA_Q_SCALE
            if feature_major:
                q_ref[cols, :] = qs.T.astype(BF16)
            else:
                q_ref[:, cols] = qs.astype(BF16)
        elif kind == "k":
            kc = proj(1, c0, PROJ_CHUNK)
            kb_ref[:, cols] = kc.astype(BF16)
            if feature_major:
                k_ref[cols, :] = kc.T
            else:
                k_ref[:, cols] = kc
        elif kind == "v":
            vc = proj(2, c0, PROJ_CHUNK)
            v_ref[:, cols] = vc
            if feature_major:
                vb_ref[cols, :] = vc.T.astype(BF16)
            else:
                vb_ref[:, cols] = vc.astype(BF16)
        else:
            qc_ref[:, cols] = (proj(5, c0, PROJ_CHUNK) * XA_SCALE).astype(BF16)

    chunks = [(kind, c0) for kind, width in (("g", LRU_W), ("q", DA_QK_W), ("k", DA_QK_W), ("v", DA_V_W), ("c", XA_W))
              for c0 in range(0, width, PROJ_CHUNK)]

    cw = cw_ref[...]
    sp = jax.nn.softplus(-lam_ref[...])
    for n in range(LRU_BLOCKS):
        for kind, c0 in chunks[n * len(chunks) // LRU_BLOCKS:(n + 1) * len(chunks) // LRU_BLOCKS]:
            proj_chunk(kind, c0)
        cols = slice(n * LRU_BW, (n + 1) * LRU_BW)
        xc_segs = []
        for s in range(nseg):
            xc = xpad_sc[s, base:base + seg, cols] * cw[0:1, cols]
            for j in range(1, CONV_W):
                xc = xc + xpad_sc[s, base + j:base + j + seg, cols] * cw[j:j + 1, cols]
            xc_segs.append(cb_ref[:, cols] + xc)
        xc = xc_segs[0] if nseg == 1 else jnp.concatenate(xc_segs, axis=0)
        xcb = xc.astype(BF16)
        r = jax.nn.sigmoid(_dot(xcb, wa_ref[n]) + ba_ref[:, cols])
        gate = jax.nn.sigmoid(_dot(xcb, wx_ref[n]) + bx_ref[:, cols])
        log_a = -LRU_C * r * sp[:, cols]
        a = jnp.exp(log_a)
        a_sc[:, cols] = a
        u_sc[:, cols] = jnp.sqrt(-jnp.tanh(log_a) * (a * a + 1.0)) * (gate * xc)
    for s in range(nseg):
        tail = xpad_sc[s, base + seg:8 + seg, :]
        nc_ref[s] = tail
        xpad_sc[s, base:8, :] = tail

    for s in range(nseg):
        def step(t, hprev, row0=s * seg):
            hnew = a_sc[pl.ds(row0 + t, 1), :] * hprev + u_sc[pl.ds(row0 + t, 1), :]
            hs_sc[pl.ds(row0 + t, 1), :] = hnew
            return hnew

        hlast = lax.fori_loop(0, seg, step, hc_sc[s], unroll=8)
        hc_sc[s] = hlast
        hl_ref[s] = hlast
    bo_ref[...] = (hs_sc[...] * gg_sc[...]).astype(BF16)


def _in_proj(x2d, conv_state, h0, p, tm, seq, feature_major):
    t = x2d.shape[0]
    seg = min(tm, seq)
    nseg = tm // seg
    nt = seq // seg
    assert tm % seg == 0 and seq % seg == 0 and seg % 8 == 0 and seg >= CONV_W - 1
    batch_block = lambda i: (i // nt, 0, 0)
    row = lambda width: pl.BlockSpec((tm, width), lambda i: (i, 0))
    sds = lambda width, dt: jax.ShapeDtypeStruct((t, width), dt)
    if feature_major:
        assert nseg == 1
        fm_spec = lambda width: pl.BlockSpec((width, tm), lambda i: (i // nt, i % nt))
        fm_sds = lambda width, dt: jax.ShapeDtypeStruct((t // seq * width, seq), dt)
    else:
        fm_spec, fm_sds = row, sds
    vec = _resident((1, LRU_W))
    gate_w = _resident((LRU_BLOCKS, LRU_BW, LRU_BW))
    conv_spec = pl.BlockSpec((nseg, CONV_W - 1, LRU_W), batch_block)
    state_spec = pl.BlockSpec((nseg, 1, LRU_W), batch_block)
    return pl.pallas_call(
        functools.partial(_in_proj_kernel, nt=nt, seg=seg, feature_major=feature_major),
        grid=(t // tm,),
        in_specs=[row(D_MODEL), _resident((1, D_MODEL)), _resident((D_MODEL, IN_W)), conv_spec, state_spec,
                  _resident((CONV_W, LRU_W)), vec, gate_w, vec, gate_w, vec, vec],
        out_specs=[fm_spec(DA_QK_W), fm_spec(DA_QK_W), row(DA_V_W), row(DA_QK_W), fm_spec(DA_V_W), row(XA_W),
                   row(LRU_W), conv_spec, state_spec],
        out_shape=[fm_sds(DA_QK_W, BF16), fm_sds(DA_QK_W, F32), sds(DA_V_W, F32), sds(DA_QK_W, BF16),
                   fm_sds(DA_V_W, BF16), sds(XA_W, BF16), sds(LRU_W, BF16),
                   jax.ShapeDtypeStruct(conv_state.shape, F32), jax.ShapeDtypeStruct(h0.shape, F32)],
        scratch_shapes=[pltpu.VMEM((nseg, 8 + seg, LRU_W), F32), pltpu.VMEM((tm, LRU_W), F32),
                        pltpu.VMEM((tm, LRU_W), F32), pltpu.VMEM((tm, LRU_W), F32), pltpu.VMEM((tm, LRU_W), F32),
                        pltpu.VMEM((nseg, 1, LRU_W), F32)],
        compiler_params=_cparams("arbitrary"),
        name="in_proj",
    )(x2d, p["norm_mix"], p["w_in"], conv_state, h0, p["conv_w"], p["conv_b"], p["w_rg_a"], p["b_rg_a"],
      p["w_rg_x"], p["b_rg_x"], p["rg_lambda"])


def _mem_kv_kernel(m_ref, g_ref, w_ref, mk_ref, mv_ref, mkb_ref, mvb_ref):
    h = _rms(m_ref[...], g_ref[...]).astype(BF16)
    mk = _dot(h, w_ref[:, :XA_W])
    mv = _dot(h, w_ref[:, XA_W:])
    mk_ref[...] = mk
    mv_ref[...] = mv
    mkb_ref[...] = mk.astype(BF16)
    mvb_ref[...] = mv.astype(BF16)


def _mem_kv(mem2d, g, w_bf16, tm):
    t = mem2d.shape[0]
    row = pl.BlockSpec((tm, XA_W), lambda i: (i, 0))
    return pl.pallas_call(
        _mem_kv_kernel,
        grid=(t // tm,),
        in_specs=[pl.BlockSpec((tm, D_MODEL), lambda i: (i, 0)), _resident((1, D_MODEL)),
                  _resident((D_MODEL, 2 * XA_W))],
        out_specs=[row, row, row, row],
        out_shape=[jax.ShapeDtypeStruct((t, XA_W), F32), jax.ShapeDtypeStruct((t, XA_W), F32),
                   jax.ShapeDtypeStruct((t, XA_W), BF16), jax.ShapeDtypeStruct((t, XA_W), BF16)],
        compiler_params=_cparams("parallel"),
        name="mem_kv",
    )(mem2d, g, w_bf16)


def _rel_bucket(rel):
    half = N_BUCKETS // 2
    max_exact = half // 2
    n = jnp.abs(rel)
    nf = jnp.maximum(n, 1).astype(F32)
    large = max_exact + (jnp.log(nf / max_exact) / math.log(MAX_DISTANCE / max_exact)
                         * (half - max_exact)).astype(jnp.int32)
    large = jnp.minimum(large, half - 1)
    return jnp.where(rel > 0, half, 0) + jnp.where(n < max_exact, n, large)


def _table_lanes(rel_table):
    return jnp.pad(rel_table.T, ((0, 0), (0, 128 - N_BUCKETS)))


def _bias_from_buckets(bkt, tabt_ref, head, vis, shift_bucket):
    rows, width = bkt.shape
    table = jnp.broadcast_to(tabt_ref[head:head + 1, :], (rows, 128))
    bias = jnp.concatenate([jnp.take_along_axis(table, bkt[:, c:c + 128], axis=1) for c in range(0, width, 128)],
                           axis=1)
    if shift_bucket is not None:
        bias = bias - tabt_ref[head:head + 1, shift_bucket:shift_bucket + 1]
    return jnp.where(vis, bias * LOG2E, MASK_VALUE)


def _split_maps(qh):
    lane = lax.broadcasted_iota(jnp.int32, qh.shape, 1)
    zero = jnp.zeros_like(qh)
    return jnp.concatenate([jnp.where(lane < DA_HD, qh, zero), jnp.where(lane >= DA_HD, qh, zero)], axis=0)


def _ones_column(n):
    return jnp.where(lax.broadcasted_iota(jnp.int32, (n, DA_V), 1) == 0, 1.0, 0.0).astype(BF16)


def _diff_finish(o, lam, g, tq, lambda_init):
    d = o[:tq] - lam * o[tq:]
    return _rms(d, g) * (1.0 - lambda_init)


def _attn_self_kernel(tab_ref, lam_ref, qt_ref, k_ref, vt_ref, bkt_ref, g_ref, o_ref,
                      bias_sc, q2t_sc, m_sc, l_sc, acc_sc, s_sc, *, tq, lambda_init):
    b = pl.program_id(0)
    i = pl.program_id(1)

    @pl.when((b == 0) & (i == 0))
    def _build_bias():
        bkt = bkt_ref[...]
        kpos = lax.broadcasted_iota(jnp.int32, bkt.shape, 0)
        qpos = lax.broadcasted_iota(jnp.int32, bkt.shape, 1)
        vis = jnp.right_shift(kpos - tq, CHUNK_SHIFT) <= jnp.right_shift(qpos, CHUNK_SHIFT)
        for h in range(DA_HEADS):
            bias_sc[h] = _bias_from_buckets(bkt, tab_ref, h, vis, FAR_BUCKET)

    feat = lax.broadcasted_iota(jnp.int32, (2 * DA_HD, tq), 0)
    for h in range(DA_HEADS):
        qt = qt_ref[h * 2 * DA_HD:(h + 1) * 2 * DA_HD, :]
        zero = jnp.zeros_like(qt)
        q2t_sc[h] = jnp.concatenate([jnp.where(feat < DA_HD, qt, zero), jnp.where(feat >= DA_HD, qt, zero)], axis=1)
    m_sc[...] = jnp.full(m_sc.shape, MASK_VALUE, F32)
    l_sc[...] = jnp.zeros(l_sc.shape, F32)
    acc_sc[...] = jnp.zeros(acc_sc.shape, F32)

    def logits(h, start):
        kj = k_ref[pl.ds(start, tq), h * DA_V:(h + 1) * DA_V]
        return _dot(kj, q2t_sc[h])

    def update(h, start, s, bias):
        vtj = vt_ref[h * DA_V:(h + 1) * DA_V, pl.ds(start, tq)]
        if bias is not None:
            s = s + jnp.concatenate([bias, bias], axis=1)
        m_prev = m_sc[h]
        m_new = jnp.maximum(m_prev, jnp.max(s, axis=0, keepdims=True))
        alpha = jnp.exp2(m_prev - m_new)
        p = jnp.exp2(s - m_new[0:1])
        l_sc[h] = alpha * l_sc[h] + jnp.sum(p, axis=0, keepdims=True)
        acc_sc[h] = alpha[0:1] * acc_sc[h] + _dot(vtj, p.astype(BF16))
        m_sc[h] = m_new

    def key_tile(start, bias_of, after=None):
        nslot = ATTN_LOOKAHEAD + 1
        for h in range(ATTN_LOOKAHEAD):
            s_sc[h % nslot] = logits(h, start)
        for h in range(DA_HEADS):
            if h + ATTN_LOOKAHEAD < DA_HEADS:
                s_sc[(h + ATTN_LOOKAHEAD) % nslot] = logits(h + ATTN_LOOKAHEAD, start)
            update(h, start, s_sc[h % nslot], bias_of(h))
            if after is not None:
                after(h)

    def far_body(j, carry):
        key_tile(pl.multiple_of(j * tq, tq), lambda h: None)
        return carry

    lax.fori_loop(0, jnp.maximum(i - 1, 0), far_body, 0)

    @pl.when(i >= 1)
    def _prev_tile():
        key_tile(pl.multiple_of((i - 1) * tq, tq), lambda h: bias_sc[h, :tq, :])

    lam = lam_ref[0]
    g = g_ref[...]

    def finish(h):
        o = acc_sc[h] / l_sc[h][0:1]
        d = o[:, :tq] - lam * o[:, tq:]
        y = d * lax.rsqrt(jnp.mean(d * d, axis=0, keepdims=True) + EPS) * g * (1.0 - lambda_init)
        o_ref[:, h * DA_V:(h + 1) * DA_V] = y.T.astype(BF16)

    key_tile(pl.multiple_of(i * tq, tq), lambda h: bias_sc[h, tq:, :], after=finish)


def _attn_self(qtb, kb, vtb, rel_table, lam, subln_g, batch, seq, lambda_init, tq):
    assert seq % tq == 0 and tq % 128 == 0 and tq >= MAX_DISTANCE and tq % CHUNK == 0
    nq = seq // tq
    rel = jnp.arange(2 * tq, dtype=jnp.int32)[:, None] - (tq + jnp.arange(tq, dtype=jnp.int32))[None, :]
    bkt = _rel_bucket(rel)
    smem = pl.BlockSpec(memory_space=pltpu.SMEM)
    return pl.pallas_call(
        functools.partial(_attn_self_kernel, tq=tq, lambda_init=lambda_init),
        grid=(batch, nq),
        in_specs=[_resident((DA_HEADS, 128)), smem,
                  pl.BlockSpec((DA_QK_W, tq), lambda b, i: (b, i)),
                  pl.BlockSpec((seq, DA_QK_W), lambda b, i: (b, 0)),
                  pl.BlockSpec((DA_V_W, seq), lambda b, i: (b, 0)),
                  _resident((2 * tq, tq)), _resident((DA_V, 1))],
        out_specs=pl.BlockSpec((tq, DA_V_W), lambda b, i: (b * nq + i, 0)),
        out_shape=jax.ShapeDtypeStruct((batch * seq, DA_V_W), BF16),
        scratch_shapes=[pltpu.VMEM((DA_HEADS, 2 * tq, tq), F32),
                        pltpu.VMEM((DA_HEADS, 2 * DA_HD, 2 * tq), BF16),
                        pltpu.VMEM((DA_HEADS, 8, 2 * tq), F32),
                        pltpu.VMEM((DA_HEADS, 8, 2 * tq), F32),
                        pltpu.VMEM((DA_HEADS, DA_V, 2 * tq), F32),
                        pltpu.VMEM((ATTN_LOOKAHEAD + 1, tq, 2 * tq), F32)],
        compiler_params=_cparams("arbitrary", "arbitrary"),
        name="attn_self",
    )(_table_lanes(rel_table), lam, qtb, kb, vtb, bkt, subln_g.reshape(DA_V, 1))


def _attn_cached_kernel(tab_ref, lam_ref, q_ref, ckt_ref, cv_ref, nk_ref, nv_ref, bktc_ref, bktn_ref, g_ref, o_ref,
                        biasc_sc, biasn_sc, *, sq, past, lambda_init):
    @pl.when(pl.program_id(0) == 0)
    def _build_bias():
        for bkt_ref, sc, k0 in ((bktc_ref, biasc_sc, 0), (bktn_ref, biasn_sc, past)):
            bkt = bkt_ref[...]
            qpos = past + lax.broadcasted_iota(jnp.int32, bkt.shape, 0)
            kpos = k0 + lax.broadcasted_iota(jnp.int32, bkt.shape, 1)
            vis = jnp.right_shift(kpos, CHUNK_SHIFT) <= jnp.right_shift(qpos, CHUNK_SHIFT)
            for h in range(DA_HEADS):
                sc[h] = _bias_from_buckets(bkt, tab_ref, h, vis, None)[:, :sc.shape[2]]

    q = q_ref[...]
    lam = lam_ref[0]
    g = g_ref[...]
    for h in range(DA_HEADS):
        cols = slice(h * DA_V, (h + 1) * DA_V)
        q2 = _split_maps(q[:, cols])
        s_c = _dot(q2, ckt_ref[cols, :].astype(BF16))
        s_c = (s_c.reshape(2, sq, past) + biasc_sc[h][None]).reshape(2 * sq, past)
        s_n = lax.dot_general(q2, nk_ref[:, cols].astype(BF16), NT_DIMS, preferred_element_type=F32)
        s_n = (s_n.reshape(2, sq, sq) + biasn_sc[h][None]).reshape(2 * sq, sq)
        m = jnp.maximum(jnp.max(s_c, axis=1, keepdims=True), jnp.max(s_n, axis=1, keepdims=True))
        p_c = jnp.exp2(s_c - m)
        p_n = jnp.exp2(s_n - m)
        l = jnp.sum(p_c, axis=1, keepdims=True) + jnp.sum(p_n, axis=1, keepdims=True)
        v_c = cv_ref[pl.ds(h, past, stride=DA_HEADS), :]
        o = (_dot(p_c.astype(BF16), v_c.astype(BF16))
             + _dot(p_n.astype(BF16), nv_ref[:, cols].astype(BF16))) / l
        o_ref[:, cols] = _diff_finish(o, lam, g, sq, lambda_init).astype(BF16)


def _attn_cached(q, cache_kt2d, cache_v2d, nk, nv, rel_table, lam, subln_g, batch, sq, past, lambda_init):
    qpos = past + jnp.arange(sq, dtype=jnp.int32)
    bkt_c = _rel_bucket(jnp.arange(past, dtype=jnp.int32)[None, :] - qpos[:, None])
    assert sq <= 128 and past % 128 == 0
    bkt_n = jnp.pad(_rel_bucket(qpos[None, :] - qpos[:, None]), ((0, 0), (0, 128 - sq)))
    smem = pl.BlockSpec(memory_space=pltpu.SMEM)
    new_spec = pl.BlockSpec((sq, DA_QK_W), lambda b: (b, 0))
    return pl.pallas_call(
        functools.partial(_attn_cached_kernel, sq=sq, past=past, lambda_init=lambda_init),
        grid=(batch,),
        in_specs=[_resident((DA_HEADS, 128)), smem, new_spec,
                  pl.BlockSpec((DA_QK_W, past), lambda b: (b, 0)),
                  pl.BlockSpec((past * DA_HEADS, DA_V), lambda b: (b, 0)),
                  new_spec, new_spec,
                  _resident((sq, past)), _resident((sq, 128)), _resident((1, DA_V))],
        out_specs=new_spec,
        out_shape=jax.ShapeDtypeStruct((batch * sq, DA_V_W), BF16),
        scratch_shapes=[pltpu.VMEM((DA_HEADS, sq, past), F32), pltpu.VMEM((DA_HEADS, sq, sq), F32)],
        compiler_params=_cparams("arbitrary"),
        name="attn_cached",
    )(_table_lanes(rel_table), lam, q, cache_kt2d, cache_v2d, nk, nv, bkt_c, bkt_n, subln_g)


def _mix_ffn_kernel(x_ref, a_ref, b_ref, qc_ref, mk_ref, mv_ref, gmix_ref, wg_ref, bg_ref, wpa_ref, wpb_ref, wpc_ref,
                    wo_ref, gffn_ref, wfi_ref, wfo_ref, gfin_ref, y_ref, *, seg):
    tm = x_ref.shape[0]
    nseg = tm // seg
    n_mem = mk_ref.shape[0] // nseg

    def xattn_head(hh):
        cols = slice(hh * XA_HD, (hh + 1) * XA_HD)
        outs = []
        for s in range(nseg):
            mem = slice(s * n_mem, (s + 1) * n_mem)
            logit = lax.dot_general(qc_ref[s * seg:(s + 1) * seg, cols], mk_ref[mem, cols].astype(BF16), NT_DIMS,
                                    preferred_element_type=F32)
            p = jnp.exp(logit - jnp.max(logit, axis=1, keepdims=True))
            l = jnp.sum(p, axis=1, keepdims=True)
            outs.append((_dot(p.astype(BF16), mv_ref[mem, cols].astype(BF16)) / l).astype(BF16))
        return outs[0] if nseg == 1 else jnp.concatenate(outs, axis=0)

    x = x_ref[...]
    h = _rms(x, gmix_ref[...]).astype(BF16)

    def branch(n, br, wp_ref):
        cols = slice(n * D_MODEL, (n + 1) * D_MODEL)
        gate = jax.nn.sigmoid(_dot(h, wg_ref[:, cols]) + bg_ref[:, cols])
        return gate * _dot(br, wp_ref[...])

    merged = branch(0, a_ref[...], wpa_ref)
    c_heads = [xattn_head(hh) for hh in range(XA_HEADS // 2)]
    merged = merged + branch(1, b_ref[...], wpb_ref)
    c_heads += [xattn_head(hh) for hh in range(XA_HEADS // 2, XA_HEADS)]
    merged = merged + branch(2, jnp.concatenate(c_heads, axis=1), wpc_ref)
    x1 = x + _dot(merged.astype(BF16), wo_ref[...])
    h2 = _rms(x1, gffn_ref[...]).astype(BF16)
    ff = None
    for c in range(D_FF // FFN_CHUNK):
        gcols = slice(c * FFN_CHUNK, (c + 1) * FFN_CHUNK)
        ucols = slice(D_FF + c * FFN_CHUNK, D_FF + (c + 1) * FFN_CHUNK)
        act = (jax.nn.silu(_dot(h2, wfi_ref[:, gcols])) * _dot(h2, wfi_ref[:, ucols])).astype(BF16)
        term = _dot(act, wfo_ref[gcols, :])
        ff = term if ff is None else ff + term
    y_ref[...] = _rms(x1 + ff, gfin_ref[...])


def _mix_ffn(x2d, a_out, b_out, qc, mk2d, mv2d, p, tm, seq):
    t = x2d.shape[0]
    seg = min(tm, seq)
    nseg = tm // seg
    nt = seq // seg
    n_mem = mk2d.shape[0] // (t // seq)
    assert tm % seg == 0 and seq % seg == 0
    row = pl.BlockSpec((tm, D_MODEL), lambda i: (i, 0))
    mem = pl.BlockSpec((nseg * n_mem, XA_W), lambda i: (i // nt, 0))
    vec = _resident((1, D_MODEL))
    sq_w = _resident((D_MODEL, D_MODEL))
    return pl.pallas_call(
        functools.partial(_mix_ffn_kernel, seg=seg),
        grid=(t // tm,),
        in_specs=[row, row, row, row, mem, mem, vec, _resident((D_MODEL, N_BRANCH * D_MODEL)),
                  _resident((1, N_BRANCH * D_MODEL)), sq_w, sq_w, sq_w, sq_w, vec,
                  _resident((D_MODEL, 2 * D_FF)), _resident((D_FF, D_MODEL)), vec],
        out_specs=row,
        out_shape=jax.ShapeDtypeStruct((t, D_MODEL), F32),
        compiler_params=_cparams("parallel"),
        name="mix_ffn",
    )(x2d, a_out, b_out, qc, mk2d, mv2d, p["norm_mix"], p["w_gate"], p["b_gate"], p["w_proj_a"], p["w_proj_b"],
      p["w_proj_c"], p["w_out"], p["norm_ffn"], p["w_ffn_in"], p["w_ffn_out"], p["norm_final"])


def _layer(x, mk2d, mv2d, cache_kt2d, cache_v2d, conv_state, h0, rel_table, lam, p, lambda_init, tm, tq, tm_out):
    batch, seq, _ = x.shape
    x2d = x.reshape(batch * seq, D_MODEL)
    no_history = cache_kt2d is None
    q, k, v, kb, vb, qc, b_out, new_conv, h_last = _in_proj(x2d, conv_state, h0, p, tm, seq, no_history)
    if no_history:
        a_out = _attn_self(q, kb, vb, rel_table, lam, p["subln_g"], batch, seq, lambda_init, tq)
    else:
        past = cache_kt2d.shape[1]
        a_out = _attn_cached(q, cache_kt2d, cache_v2d, k, v, rel_table, lam, p["subln_g"], batch, seq, past,
                             lambda_init)
    y = _mix_ffn(x2d, a_out, b_out, qc, mk2d, mv2d, p, tm_out, seq)
    return y.reshape(batch, seq, D_MODEL), k, v, new_conv, h_last.reshape(batch, LRU_W)


def kernel(x_prompt, x_sample, mem_prompt, cache_k, cache_v, state_conv, state_lru, cache_mem_k, cache_mem_v,
           rel_table, norm_mix, w_in, lambda_q1, lambda_k1, lambda_q2, lambda_k2, subln_g, conv_w, conv_b,
           w_rg_a, b_rg_a, w_rg_x, b_rg_x, rg_lambda, norm_mem, w_mem_kv, w_proj_a, w_proj_b, w_proj_c,
           w_gate, b_gate, w_out, norm_ffn, w_ffn_in, w_ffn_out, norm_final):
    depth = w_in.shape[0]
    assert depth == 1, "the final norm is fused into the layer's last kernel"
    l = 0
    lambda_init = 0.8 - 0.6 * math.exp(-0.3 * l)
    bp, sp_, _ = x_prompt.shape
    bs, ss, _ = x_sample.shape
    past = cache_k.shape[2]
    n_mem = mem_prompt.shape[1]
    row = lambda a: a.reshape(1, -1).astype(F32)
    p = dict(norm_mix=row(norm_mix[l]), w_in=w_in[l].astype(BF16), subln_g=row(subln_g[l]),
             conv_w=conv_w[l], conv_b=row(conv_b[l]), w_rg_a=w_rg_a[l].astype(BF16), b_rg_a=row(b_rg_a[l]),
             w_rg_x=w_rg_x[l].astype(BF16), b_rg_x=row(b_rg_x[l]), rg_lambda=row(rg_lambda[l]),
             w_proj_a=w_proj_a[l].astype(BF16), w_proj_b=w_proj_b[l].astype(BF16),
             w_proj_c=w_proj_c[l].astype(BF16), w_gate=w_gate[l].astype(BF16), b_gate=row(b_gate[l]),
             w_out=w_out[l].astype(BF16), norm_ffn=row(norm_ffn[l]), w_ffn_in=w_ffn_in[l].astype(BF16),
             w_ffn_out=w_ffn_out[l].astype(BF16), norm_final=row(norm_final))
    lam = (jnp.exp(jnp.sum(lambda_q1[l] * lambda_k1[l]).astype(F32))
           - jnp.exp(jnp.sum(lambda_q2[l] * lambda_k2[l]).astype(F32)) + lambda_init).reshape(1)

    mk, mv, mkb, mvb = _mem_kv(mem_prompt.reshape(bp * n_mem, D_MODEL), row(norm_mem[l]),
                               w_mem_kv[l].astype(BF16), 256)
    zeros_conv = jnp.zeros((bp, CONV_W - 1, LRU_W), F32)
    zeros_h = jnp.zeros((bp, 1, LRU_W), F32)
    yp, kp, vp, cp, hp = _layer(x_prompt, mkb, mvb, None, None, zeros_conv, zeros_h, rel_table, lam, p,
                                lambda_init, tm=512, tq=256, tm_out=512)
    cache_kt = jnp.transpose(cache_k[l], (0, 2, 3, 4, 1)).reshape(bs * DA_QK_W, past)
    ys, ks, vs, cs, hs = _layer(x_sample, cache_mem_k[l].reshape(bs * n_mem, XA_W),
                                cache_mem_v[l].reshape(bs * n_mem, XA_W),
                                cache_kt, cache_v[l].reshape(bs * past * DA_HEADS, DA_V),
                                state_conv[l], state_lru[l].reshape(bs, 1, LRU_W), rel_table, lam, p,
                                lambda_init, tm=256, tq=ss, tm_out=128)
    kp = jnp.transpose(kp.reshape(bp, DA_HEADS, 2, DA_HD, sp_), (0, 4, 1, 2, 3))
    return (yp, ys,
            kp[None], vp.reshape(1, bp, sp_, DA_HEADS, DA_V),
            cp[None], hp[None],
            mk.reshape(1, bp, n_mem, XA_HEADS, XA_HD), mv.reshape(1, bp, n_mem, XA_HEADS, XA_HD),
            ks.reshape(1, bs, ss, DA_HEADS, 2, DA_HD), vs.reshape(1, bs, ss, DA_HEADS, DA_V),
            cs[None], hs[None])
```

```python
import functools
import math

import jax
import jax.numpy as jnp
from jax import lax
from jax.experimental import pallas as pl
from jax.experimental.pallas import tpu as pltpu

F32 = jnp.float32
BF16 = jnp.bfloat16

D_MODEL = 1024
CHUNK = 64
CHUNK_SHIFT = 6
DA_HEADS = 8
DA_HD = 64
DA_V = 2 * DA_HD
LRU_W = D_MODEL
LRU_BLOCKS = 8
LRU_BW = LRU_W // LRU_BLOCKS
CONV_W = 4
LRU_C = 8.0
XA_HEADS = 4
XA_HD = 256
N_BUCKETS = 32
MAX_DISTANCE = 128
D_FF = ((8 * D_MODEL + 3 * 256 - 1) // (3 * 256)) * 256
N_BRANCH = 3
EPS = 1e-6
DA_QK_W = DA_HEADS * 2 * DA_HD
DA_V_W = DA_HEADS * DA_V
XA_W = XA_HEADS * XA_HD
IN_OFFS = (0, DA_QK_W, 2 * DA_QK_W, 2 * DA_QK_W + DA_V_W, 2 * DA_QK_W + DA_V_W + LRU_W,
           2 * DA_QK_W + DA_V_W + 2 * LRU_W)
IN_W = 2 * DA_QK_W + DA_V_W + 2 * LRU_W + XA_W
LOG2E = math.log2(math.e)
DA_Q_SCALE = DA_HD ** -0.5 * LOG2E
XA_SCALE = XA_HD ** -0.5
MASK_VALUE = -1e30
FAR_BUCKET = N_BUCKETS // 2 - 1
FFN_CHUNK = D_FF // 2
PROJ_CHUNK = 256
OUT_WEIGHTS = ("w_gate", "w_proj_a", "w_proj_b", "w_proj_c", "w_out", "w_ffn_in", "w_ffn_out")
BF16_SUBLANES = 16
ATTN_LOOKAHEAD = 3

V7X_VMEM_BYTES = 64 * 1024 * 1024
VMEM_LIMIT = V7X_VMEM_BYTES - 8 * 1024 * 1024

NT_DIMS = (((1,), (1,)), ((), ()))


def _cparams(*sem):
    return pltpu.CompilerParams(dimension_semantics=sem, vmem_limit_bytes=VMEM_LIMIT)


def _resident(shape):
    nd = len(shape)
    return pl.BlockSpec(shape, lambda *_: (0,) * nd, pipeline_mode=pl.Buffered(1))


def _rms(x, g):
    return x * lax.rsqrt(jnp.mean(x * x, axis=-1, keepdims=True) + EPS) * g


def _dot(a, b):
    return jnp.dot(a, b, preferred_element_type=F32)


def _in_proj_kernel(x_ref, g_ref, w_ref, cs_ref, h0_ref, cw_ref, cb_ref, wa_ref, ba_ref, wx_ref, bx_ref, lam_ref,
                    q_ref, k_ref, v_ref, kb_ref, vb_ref, qc_ref, bo_ref, nc_ref, hl_ref,
                    xpad_sc, a_sc, u_sc, hs_sc, gg_sc, hc_sc, *, nt, seg, feature_major):
    tm = x_ref.shape[0]
    nseg = tm // seg
    halo = CONV_W - 1
    base = 8 - halo

    @pl.when(pl.program_id(0) % nt == 0)
    def _load_state():
        for s in range(nseg):
            xpad_sc[s, base:8, :] = cs_ref[s]
        hc_sc[...] = h0_ref[...]

    h = _rms(x_ref[...], g_ref[...]).astype(BF16)

    def proj(n, c0, width):
        return _dot(h, w_ref[:, IN_OFFS[n] + c0:IN_OFFS[n] + c0 + width])

    xb = proj(3, 0, LRU_W)
    for s in range(nseg):
        xpad_sc[s, 8:8 + seg, :] = xb[s * seg:(s + 1) * seg]

    def proj_chunk(kind, c0):
        cols = slice(c0, c0 + PROJ_CHUNK)
        if kind == "g":
            gg_sc[:, cols] = jax.nn.gelu(proj(4, c0, PROJ_CHUNK))
        elif kind == "q":
            qs = proj(0, c0, PROJ_CHUNK) * DA_Q_SCALE
            if feature_major:
                q_ref[cols, :] = qs.T.astype(BF16)
            else:
                q_ref[:, cols] = qs.astype(BF16)
        elif kind == "k":
            kc = proj(1, c0, PROJ_CHUNK)
            kb_ref[:, cols] = kc.astype(BF16)
            if feature_major:
                k_ref[cols, :] = kc.T
            else:
                k_ref[:, cols] = kc
        elif kind == "v":
            vc = proj(2, c0, PROJ_CHUNK)
            v_ref[:, cols] = vc
            if feature_major:
                vb_ref[cols, :] = vc.T.astype(BF16)
            else:
                vb_ref[:, cols] = vc.astype(BF16)
        else:
            qc_ref[:, cols] = (proj(5, c0, PROJ_CHUNK) * XA_SCALE).astype(BF16)

    chunks = [(kind, c0) for kind, width in (("g", LRU_W), ("q", DA_QK_W), ("k", DA_QK_W), ("v", DA_V_W), ("c", XA_W))
              for c0 in range(0, width, PROJ_CHUNK)]

    cw = cw_ref[...]
    sp = jax.nn.softplus(-lam_ref[...])
    for n in range(LRU_BLOCKS):
        for kind, c0 in chunks[n * len(chunks) // LRU_BLOCKS:(n + 1) * len(chunks) // LRU_BLOCKS]:
            proj_chunk(kind, c0)
        cols = slice(n * LRU_BW, (n + 1) * LRU_BW)
        xc_segs = []
        for s in range(nseg):
            xc = xpad_sc[s, base:base + seg, cols] * cw[0:1, cols]
            for j in range(1, CONV_W):
                xc = xc + xpad_sc[s, base + j:base + j + seg, cols] * cw[j:j + 1, cols]
            xc_segs.append(cb_ref[:, cols] + xc)
        xc = xc_segs[0] if nseg == 1 else jnp.concatenate(xc_segs, axis=0)
        xcb = xc.astype(BF16)
        r = jax.nn.sigmoid(_dot(xcb, wa_ref[n]) + ba_ref[:, cols])
        gate = jax.nn.sigmoid(_dot(xcb, wx_ref[n]) + bx_ref[:, cols])
        log_a = -LRU_C * r * sp[:, cols]
        a = jnp.exp(log_a)
        a_sc[:, cols] = a
        u_sc[:, cols] = jnp.sqrt(-jnp.tanh(log_a) * (a * a + 1.0)) * (gate * xc)
    for s in range(nseg):
        tail = xpad_sc[s, base + seg:8 + seg, :]
        nc_ref[s] = tail
        xpad_sc[s, base:8, :] = tail

    for s in range(nseg):
        def step(t, hprev, row0=s * seg):
            hnew = a_sc[pl.ds(row0 + t, 1), :] * hprev + u_sc[pl.ds(row0 + t, 1), :]
            hs_sc[pl.ds(row0 + t, 1), :] = hnew
            return hnew

        hlast = lax.fori_loop(0, seg, step, hc_sc[s], unroll=8)
        hc_sc[s] = hlast
        hl_ref[s] = hlast
    bo_ref[...] = (hs_sc[...] * gg_sc[...]).astype(BF16)


def _in_proj(x2d, conv_state, h0, p, tm, seq, feature_major):
    t = x2d.shape[0]
    seg = min(tm, seq)
    nseg = tm // seg
    nt = seq // seg
    assert tm % seg == 0 and seq % seg == 0 and seg % 8 == 0 and seg >= CONV_W - 1
    batch_block = lambda i: (i // nt, 0, 0)
    row = lambda width: pl.BlockSpec((tm, width), lambda i: (i, 0))
    sds = lambda width, dt: jax.ShapeDtypeStruct((t, width), dt)
    if feature_major:
        assert nseg == 1
        fm_spec = lambda width: pl.BlockSpec((width, tm), lambda i: (i // nt, i % nt))
        fm_sds = lambda width, dt: jax.ShapeDtypeStruct((t // seq * width, seq), dt)
    else:
        fm_spec, fm_sds = row, sds
    vec = _resident((1, LRU_W))
    gate_w = _resident((LRU_BLOCKS, LRU_BW, LRU_BW))
    conv_spec = pl.BlockSpec((nseg, CONV_W - 1, LRU_W), batch_block)
    state_spec = pl.BlockSpec((nseg, 1, LRU_W), batch_block)
    return pl.pallas_call(
        functools.partial(_in_proj_kernel, nt=nt, seg=seg, feature_major=feature_major),
        grid=(t // tm,),
        in_specs=[row(D_MODEL), _resident((1, D_MODEL)), _resident((D_MODEL, IN_W)), conv_spec, state_spec,
                  _resident((CONV_W, LRU_W)), vec, gate_w, vec, gate_w, vec, vec],
        out_specs=[fm_spec(DA_QK_W), fm_spec(DA_QK_W), row(DA_V_W), row(DA_QK_W), fm_spec(DA_V_W), row(XA_W),
                   row(LRU_W), conv_spec, state_spec],
        out_shape=[fm_sds(DA_QK_W, BF16), fm_sds(DA_QK_W, F32), sds(DA_V_W, F32), sds(DA_QK_W, BF16),
                   fm_sds(DA_V_W, BF16), sds(XA_W, BF16), sds(LRU_W, BF16),
                   jax.ShapeDtypeStruct(conv_state.shape, F32), jax.ShapeDtypeStruct(h0.shape, F32)],
        scratch_shapes=[pltpu.VMEM((nseg, 8 + seg, LRU_W), F32), pltpu.VMEM((tm, LRU_W), F32),
                        pltpu.VMEM((tm, LRU_W), F32), pltpu.VMEM((tm, LRU_W), F32), pltpu.VMEM((tm, LRU_W), F32),
                        pltpu.VMEM((nseg, 1, LRU_W), F32)],
        compiler_params=_cparams("arbitrary"),
        name="in_proj",
    )(x2d, p["norm_mix"], p["w_in"], conv_state, h0, p["conv_w"], p["conv_b"], p["w_rg_a"], p["b_rg_a"],
      p["w_rg_x"], p["b_rg_x"], p["rg_lambda"])


def _mem_kv_kernel(m_ref, g_ref, w_ref, mk_ref, mv_ref, mkb_ref, mvb_ref):
    h = _rms(m_ref[...], g_ref[...]).astype(BF16)
    mk = _dot(h, w_ref[:, :XA_W])
    mv = _dot(h, w_ref[:, XA_W:])
    mk_ref[...] = mk
    mv_ref[...] = mv
    mkb_ref[...] = mk.astype(BF16)
    mvb_ref[...] = mv.astype(BF16)


def _mem_kv(mem2d, g, w_bf16, tm):
    t = mem2d.shape[0]
    row = pl.BlockSpec((tm, XA_W), lambda i: (i, 0))
    return pl.pallas_call(
        _mem_kv_kernel,
        grid=(t // tm,),
        in_specs=[pl.BlockSpec((tm, D_MODEL), lambda i: (i, 0)), _resident((1, D_MODEL)),
                  _resident((D_MODEL, 2 * XA_W))],
        out_specs=[row, row, row, row],
        out_shape=[jax.ShapeDtypeStruct((t, XA_W), F32), jax.ShapeDtypeStruct((t, XA_W), F32),
                   jax.ShapeDtypeStruct((t, XA_W), BF16), jax.ShapeDtypeStruct((t, XA_W), BF16)],
        compiler_params=_cparams("parallel"),
        name="mem_kv",
    )(mem2d, g, w_bf16)


def _rel_bucket(rel):
    half = N_BUCKETS // 2
    max_exact = half // 2
    n = jnp.abs(rel)
    nf = jnp.maximum(n, 1).astype(F32)
    large = max_exact + (jnp.log(nf / max_exact) / math.log(MAX_DISTANCE / max_exact)
                         * (half - max_exact)).astype(jnp.int32)
    large = jnp.minimum(large, half - 1)
    return jnp.where(rel > 0, half, 0) + jnp.where(n < max_exact, n, large)


def _table_lanes(rel_table):
    return jnp.pad(rel_table.T, ((0, 0), (0, 128 - N_BUCKETS)))


def _bias_from_buckets(bkt, tabt_ref, head, vis, shift_bucket):
    rows, width = bkt.shape
    table = jnp.broadcast_to(tabt_ref[head:head + 1, :], (rows, 128))
    bias = jnp.concatenate([jnp.take_along_axis(table, bkt[:, c:c + 128], axis=1) for c in range(0, width, 128)],
                           axis=1)
    if shift_bucket is not None:
        bias = bias - tabt_ref[head:head + 1, shift_bucket:shift_bucket + 1]
    return jnp.where(vis, bias * LOG2E, MASK_VALUE)


def _split_maps(qh):
    lane = lax.broadcasted_iota(jnp.int32, qh.shape, 1)
    zero = jnp.zeros_like(qh)
    return jnp.concatenate([jnp.where(lane < DA_HD, qh, zero), jnp.where(lane >= DA_HD, qh, zero)], axis=0)


def _ones_column(n):
    return jnp.where(lax.broadcasted_iota(jnp.int32, (n, DA_V), 1) == 0, 1.0, 0.0).astype(BF16)


def _diff_finish(o, lam, g, tq, lambda_init):
    d = o[:tq] - lam * o[tq:]
    return _rms(d, g) * (1.0 - lambda_init)


def _attn_self_kernel(tab_ref, lam_ref, qt_ref, k_ref, vt_ref, bkt_ref, g_ref, *rest, tq, lambda_init, n_cast):
    w_refs, o_ref, wb_refs = rest[:n_cast], rest[n_cast], rest[n_cast + 1:2 * n_cast + 1]
    bias_sc, q2t_sc, m_sc, l_sc, acc_sc, s_sc = rest[2 * n_cast + 1:]
    for w_ref, wb_ref in zip(w_refs, wb_refs):
        wb_ref[...] = w_ref[...].astype(BF16)

    b = pl.program_id(0)
    i = pl.program_id(1)

    @pl.when((b == 0) & (i == 0))
    def _build_bias():
        bkt = bkt_ref[...]
        kpos = lax.broadcasted_iota(jnp.int32, bkt.shape, 0)
        qpos = lax.broadcasted_iota(jnp.int32, bkt.shape, 1)
        vis = jnp.right_shift(kpos - tq, CHUNK_SHIFT) <= jnp.right_shift(qpos, CHUNK_SHIFT)
        for h in range(DA_HEADS):
            bias_sc[h] = _bias_from_buckets(bkt, tab_ref, h, vis, FAR_BUCKET)

    feat = lax.broadcasted_iota(jnp.int32, (2 * DA_HD, tq), 0)
    for h in range(DA_HEADS):
        qt = qt_ref[h * 2 * DA_HD:(h + 1) * 2 * DA_HD, :]
        zero = jnp.zeros_like(qt)
        q2t_sc[h] = jnp.concatenate([jnp.where(feat < DA_HD, qt, zero), jnp.where(feat >= DA_HD, qt, zero)], axis=1)
    m_sc[...] = jnp.full(m_sc.shape, MASK_VALUE, F32)
    l_sc[...] = jnp.zeros(l_sc.shape, F32)
    acc_sc[...] = jnp.zeros(acc_sc.shape, F32)

    def logits(h, start):
        kj = k_ref[pl.ds(start, tq), h * DA_V:(h + 1) * DA_V]
        return _dot(kj, q2t_sc[h])

    def update(h, start, s, bias):
        vtj = vt_ref[h * DA_V:(h + 1) * DA_V, pl.ds(start, tq)]
        if bias is not None:
            s = s + jnp.concatenate([bias, bias], axis=1)
        m_prev = m_sc[h]
        m_new = jnp.maximum(m_prev, jnp.max(s, axis=0, keepdims=True))
        alpha = jnp.exp2(m_prev - m_new)
        p = jnp.exp2(s - m_new[0:1])
        l_sc[h] = alpha * l_sc[h] + jnp.sum(p, axis=0, keepdims=True)
        acc_sc[h] = alpha[0:1] * acc_sc[h] + _dot(vtj, p.astype(BF16))
        m_sc[h] = m_new

    def key_tile(start, bias_of, after=None):
        nslot = ATTN_LOOKAHEAD + 1
        for h in range(ATTN_LOOKAHEAD):
            s_sc[h % nslot] = logits(h, start)
        for h in range(DA_HEADS):
            if h + ATTN_LOOKAHEAD < DA_HEADS:
                s_sc[(h + ATTN_LOOKAHEAD) % nslot] = logits(h + ATTN_LOOKAHEAD, start)
            update(h, start, s_sc[h % nslot], bias_of(h))
            if after is not None:
                after(h)

    def far_body(j, carry):
        key_tile(pl.multiple_of(j * tq, tq), lambda h: None)
        return carry

    lax.fori_loop(0, jnp.maximum(i - 1, 0), far_body, 0)

    @pl.when(i >= 1)
    def _prev_tile():
        key_tile(pl.multiple_of((i - 1) * tq, tq), lambda h: bias_sc[h, :tq, :])

    lam = lam_ref[0]
    g = g_ref[...]

    def finish(h):
        o = acc_sc[h] / l_sc[h][0:1]
        d = o[:, :tq] - lam * o[:, tq:]
        y = d * lax.rsqrt(jnp.mean(d * d, axis=0, keepdims=True) + EPS) * g * (1.0 - lambda_init)
        o_ref[:, h * DA_V:(h + 1) * DA_V] = y.T.astype(BF16)

    key_tile(pl.multiple_of(i * tq, tq), lambda h: bias_sc[h, tq:, :], after=finish)


def _attn_self(qtb, kb, vtb, rel_table, lam, subln_g, batch, seq, lambda_init, tq, cast_weights):
    assert seq % tq == 0 and tq % 128 == 0 and tq >= MAX_DISTANCE and tq % CHUNK == 0
    nq = seq // tq
    rel = jnp.arange(2 * tq, dtype=jnp.int32)[:, None] - (tq + jnp.arange(tq, dtype=jnp.int32))[None, :]
    bkt = _rel_bucket(rel)
    smem = pl.BlockSpec(memory_space=pltpu.SMEM)

    steps = batch * nq
    cast_specs = []
    for w in cast_weights:
        rows, cols = w.shape
        nblk = max(d for d in range(1, steps + 1)
                   if steps % d == 0 and rows % d == 0 and (rows // d) % BF16_SUBLANES == 0)
        cast_specs.append(pl.BlockSpec((rows // nblk, cols),
                                       lambda b, i, per=steps // nblk: ((b * nq + i) // per, 0)))
    n_cast = len(cast_weights)
    outs = pl.pallas_call(
        functools.partial(_attn_self_kernel, tq=tq, lambda_init=lambda_init, n_cast=n_cast),
        grid=(batch, nq),
        in_specs=[_resident((DA_HEADS, 128)), smem,
                  pl.BlockSpec((DA_QK_W, tq), lambda b, i: (b, i)),
                  pl.BlockSpec((seq, DA_QK_W), lambda b, i: (b, 0)),
                  pl.BlockSpec((DA_V_W, seq), lambda b, i: (b, 0)),
                  _resident((2 * tq, tq)), _resident((DA_V, 1))] + cast_specs,
        out_specs=[pl.BlockSpec((tq, DA_V_W), lambda b, i: (b * nq + i, 0))] + cast_specs,
        out_shape=[jax.ShapeDtypeStruct((batch * seq, DA_V_W), BF16)]
                  + [jax.ShapeDtypeStruct(w.shape, BF16) for w in cast_weights],
        scratch_shapes=[pltpu.VMEM((DA_HEADS, 2 * tq, tq), F32),
                        pltpu.VMEM((DA_HEADS, 2 * DA_HD, 2 * tq), BF16),
                        pltpu.VMEM((DA_HEADS, 8, 2 * tq), F32),
                        pltpu.VMEM((DA_HEADS, 8, 2 * tq), F32),
                        pltpu.VMEM((DA_HEADS, DA_V, 2 * tq), F32),
                        pltpu.VMEM((ATTN_LOOKAHEAD + 1, tq, 2 * tq), F32)],
        compiler_params=_cparams("arbitrary", "arbitrary"),
        name="attn_self",
    )(_table_lanes(rel_table), lam, qtb, kb, vtb, bkt, subln_g.reshape(DA_V, 1), *cast_weights)
    return outs[0], outs[1:]


def _attn_cached_kernel(tab_ref, lam_ref, q_ref, ckt_ref, cv_ref, nk_ref, nv_ref, bktc_ref, bktn_ref, g_ref, o_ref,
                        biasc_sc, biasn_sc, *, sq, past, lambda_init):
    @pl.when(pl.program_id(0) == 0)
    def _build_bias():
        for bkt_ref, sc, k0 in ((bktc_ref, biasc_sc, 0), (bktn_ref, biasn_sc, past)):
            bkt = bkt_ref[...]
            qpos = past + lax.broadcasted_iota(jnp.int32, bkt.shape, 0)
            kpos = k0 + lax.broadcasted_iota(jnp.int32, bkt.shape, 1)
            vis = jnp.right_shift(kpos, CHUNK_SHIFT) <= jnp.right_shift(qpos, CHUNK_SHIFT)
            for h in range(DA_HEADS):
                sc[h] = _bias_from_buckets(bkt, tab_ref, h, vis, None)[:, :sc.shape[2]]

    q = q_ref[...]
    lam = lam_ref[0]
    g = g_ref[...]
    for h in range(DA_HEADS):
        cols = slice(h * DA_V, (h + 1) * DA_V)
        q2 = _split_maps(q[:, cols])
        s_c = _dot(q2, ckt_ref[cols, :].astype(BF16))
        s_c = (s_c.reshape(2, sq, past) + biasc_sc[h][None]).reshape(2 * sq, past)
        s_n = lax.dot_general(q2, nk_ref[:, cols].astype(BF16), NT_DIMS, preferred_element_type=F32)
        s_n = (s_n.reshape(2, sq, sq) + biasn_sc[h][None]).reshape(2 * sq, sq)
        m = jnp.maximum(jnp.max(s_c, axis=1, keepdims=True), jnp.max(s_n, axis=1, keepdims=True))
        p_c = jnp.exp2(s_c - m)
        p_n = jnp.exp2(s_n - m)
        l = jnp.sum(p_c, axis=1, keepdims=True) + jnp.sum(p_n, axis=1, keepdims=True)
        v_c = cv_ref[pl.ds(h, past, stride=DA_HEADS), :]
        o = (_dot(p_c.astype(BF16), v_c.astype(BF16))
             + _dot(p_n.astype(BF16), nv_ref[:, cols].astype(BF16))) / l
        o_ref[:, cols] = _diff_finish(o, lam, g, sq, lambda_init).astype(BF16)


def _attn_cached(q, cache_kt2d, cache_v2d, nk, nv, rel_table, lam, subln_g, batch, sq, past, lambda_init):
    qpos = past + jnp.arange(sq, dtype=jnp.int32)
    bkt_c = _rel_bucket(jnp.arange(past, dtype=jnp.int32)[None, :] - qpos[:, None])
    assert sq <= 128 and past % 128 == 0
    bkt_n = jnp.pad(_rel_bucket(qpos[None, :] - qpos[:, None]), ((0, 0), (0, 128 - sq)))
    smem = pl.BlockSpec(memory_space=pltpu.SMEM)
    new_spec = pl.BlockSpec((sq, DA_QK_W), lambda b: (b, 0))
    return pl.pallas_call(
        functools.partial(_attn_cached_kernel, sq=sq, past=past, lambda_init=lambda_init),
        grid=(batch,),
        in_specs=[_resident((DA_HEADS, 128)), smem, new_spec,
                  pl.BlockSpec((DA_QK_W, past), lambda b: (b, 0)),
                  pl.BlockSpec((past * DA_HEADS, DA_V), lambda b: (b, 0)),
                  new_spec, new_spec,
                  _resident((sq, past)), _resident((sq, 128)), _resident((1, DA_V))],
        out_specs=new_spec,
        out_shape=jax.ShapeDtypeStruct((batch * sq, DA_V_W), BF16),
        scratch_shapes=[pltpu.VMEM((DA_HEADS, sq, past), F32), pltpu.VMEM((DA_HEADS, sq, sq), F32)],
        compiler_params=_cparams("arbitrary"),
        name="attn_cached",
    )(_table_lanes(rel_table), lam, q, cache_kt2d, cache_v2d, nk, nv, bkt_c, bkt_n, subln_g)


def _mix_ffn_kernel(x_ref, a_ref, b_ref, qc_ref, mk_ref, mv_ref, gmix_ref, wg_ref, bg_ref, wpa_ref, wpb_ref, wpc_ref,
                    wo_ref, gffn_ref, wfi_ref, wfo_ref, gfin_ref, y_ref, *, seg):
    tm = x_ref.shape[0]
    nseg = tm // seg
    n_mem = mk_ref.shape[0] // nseg

    def xattn_head(hh):
        cols = slice(hh * XA_HD, (hh + 1) * XA_HD)
        outs = []
        for s in range(nseg):
            mem = slice(s * n_mem, (s + 1) * n_mem)
            logit = lax.dot_general(qc_ref[s * seg:(s + 1) * seg, cols], mk_ref[mem, cols].astype(BF16), NT_DIMS,
                                    preferred_element_type=F32)
            p = jnp.exp(logit - jnp.max(logit, axis=1, keepdims=True))
            l = jnp.sum(p, axis=1, keepdims=True)
            outs.append((_dot(p.astype(BF16), mv_ref[mem, cols].astype(BF16)) / l).astype(BF16))
        return outs[0] if nseg == 1 else jnp.concatenate(outs, axis=0)

    x = x_ref[...]
    h = _rms(x, gmix_ref[...]).astype(BF16)

    def branch(n, br, wp_ref):
        cols = slice(n * D_MODEL, (n + 1) * D_MODEL)
        gate = jax.nn.sigmoid(_dot(h, wg_ref[:, cols]) + bg_ref[:, cols])
        return gate * _dot(br, wp_ref[...])

    merged = branch(0, a_ref[...], wpa_ref)
    c_heads = [xattn_head(hh) for hh in range(XA_HEADS // 2)]
    merged = merged + branch(1, b_ref[...], wpb_ref)
    c_heads += [xattn_head(hh) for hh in range(XA_HEADS // 2, XA_HEADS)]
    merged = merged + branch(2, jnp.concatenate(c_heads, axis=1), wpc_ref)
    x1 = x + _dot(merged.astype(BF16), wo_ref[...])
    h2 = _rms(x1, gffn_ref[...]).astype(BF16)
    ff = None
    for c in range(D_FF // FFN_CHUNK):
        gcols = slice(c * FFN_CHUNK, (c + 1) * FFN_CHUNK)
        ucols = slice(D_FF + c * FFN_CHUNK, D_FF + (c + 1) * FFN_CHUNK)
        act = (jax.nn.silu(_dot(h2, wfi_ref[:, gcols])) * _dot(h2, wfi_ref[:, ucols])).astype(BF16)
        term = _dot(act, wfo_ref[gcols, :])
        ff = term if ff is None else ff + term
    y_ref[...] = _rms(x1 + ff, gfin_ref[...])


def _mix_ffn(x2d, a_out, b_out, qc, mk2d, mv2d, p, tm, seq):
    t = x2d.shape[0]
    seg = min(tm, seq)
    nseg = tm // seg
    nt = seq // seg
    n_mem = mk2d.shape[0] // (t // seq)
    assert tm % seg == 0 and seq % seg == 0
    row = pl.BlockSpec((tm, D_MODEL), lambda i: (i, 0))
    mem = pl.BlockSpec((nseg * n_mem, XA_W), lambda i: (i // nt, 0))
    vec = _resident((1, D_MODEL))
    sq_w = _resident((D_MODEL, D_MODEL))
    return pl.pallas_call(
        functools.partial(_mix_ffn_kernel, seg=seg),
        grid=(t // tm,),
        in_specs=[row, row, row, row, mem, mem, vec, _resident((D_MODEL, N_BRANCH * D_MODEL)),
                  _resident((1, N_BRANCH * D_MODEL)), sq_w, sq_w, sq_w, sq_w, vec,
                  _resident((D_MODEL, 2 * D_FF)), _resident((D_FF, D_MODEL)), vec],
        out_specs=row,
        out_shape=jax.ShapeDtypeStruct((t, D_MODEL), F32),
        compiler_params=_cparams("parallel"),
        name="mix_ffn",
    )(x2d, a_out, b_out, qc, mk2d, mv2d, p["norm_mix"], p["w_gate"], p["b_gate"], p["w_proj_a"], p["w_proj_b"],
      p["w_proj_c"], p["w_out"], p["norm_ffn"], p["w_ffn_in"], p["w_ffn_out"], p["norm_final"])


def _layer(x, mk2d, mv2d, cache_kt2d, cache_v2d, conv_state, h0, rel_table, lam, p, lambda_init, tm, tq, tm_out):
    batch, seq, _ = x.shape
    x2d = x.reshape(batch * seq, D_MODEL)
    no_history = cache_kt2d is None
    q, k, v, kb, vb, qc, b_out, new_conv, h_last = _in_proj(x2d, conv_state, h0, p, tm, seq, no_history)
    if no_history:
        a_out, out_w = _attn_self(q, kb, vb, rel_table, lam, p["subln_g"], batch, seq, lambda_init, tq,
                                  [p[n] for n in OUT_WEIGHTS])
        p = {**p, **dict(zip(OUT_WEIGHTS, out_w))}
    else:
        past = cache_kt2d.shape[1]
        a_out = _attn_cached(q, cache_kt2d, cache_v2d, k, v, rel_table, lam, p["subln_g"], batch, seq, past,
                             lambda_init)
    y = _mix_ffn(x2d, a_out, b_out, qc, mk2d, mv2d, p, tm_out, seq)
    return y.reshape(batch, seq, D_MODEL), k, v, new_conv, h_last.reshape(batch, LRU_W), p


def kernel(x_prompt, x_sample, mem_prompt, cache_k, cache_v, state_conv, state_lru, cache_mem_k, cache_mem_v,
           rel_table, norm_mix, w_in, lambda_q1, lambda_k1, lambda_q2, lambda_k2, subln_g, conv_w, conv_b,
           w_rg_a, b_rg_a, w_rg_x, b_rg_x, rg_lambda, norm_mem, w_mem_kv, w_proj_a, w_proj_b, w_proj_c,
           w_gate, b_gate, w_out, norm_ffn, w_ffn_in, w_ffn_out, norm_final):
    depth = w_in.shape[0]
    assert depth == 1, "the final norm is fused into the layer's last kernel"
    l = 0
    lambda_init = 0.8 - 0.6 * math.exp(-0.3 * l)
    bp, sp_, _ = x_prompt.shape
    bs, ss, _ = x_sample.shape
    past = cache_k.shape[2]
    n_mem = mem_prompt.shape[1]
    row = lambda a: a.reshape(1, -1).astype(F32)
    p = dict(norm_mix=row(norm_mix[l]), w_in=w_in[l].astype(BF16), subln_g=row(subln_g[l]),
             conv_w=conv_w[l], conv_b=row(conv_b[l]), w_rg_a=w_rg_a[l].astype(BF16), b_rg_a=row(b_rg_a[l]),
             w_rg_x=w_rg_x[l].astype(BF16), b_rg_x=row(b_rg_x[l]), rg_lambda=row(rg_lambda[l]),
             w_proj_a=w_proj_a[l], w_proj_b=w_proj_b[l], w_proj_c=w_proj_c[l], w_gate=w_gate[l],
             b_gate=row(b_gate[l]), w_out=w_out[l], norm_ffn=row(norm_ffn[l]), w_ffn_in=w_ffn_in[l],
             w_ffn_out=w_ffn_out[l], norm_final=row(norm_final))
    lam = (jnp.exp(jnp.sum(lambda_q1[l] * lambda_k1[l]).astype(F32))
           - jnp.exp(jnp.sum(lambda_q2[l] * lambda_k2[l]).astype(F32)) + lambda_init).reshape(1)

    mk, mv, mkb, mvb = _mem_kv(mem_prompt.reshape(bp * n_mem, D_MODEL), row(norm_mem[l]),
                               w_mem_kv[l].astype(BF16), 256)
    zeros_conv = jnp.zeros((bp, CONV_W - 1, LRU_W), F32)
    zeros_h = jnp.zeros((bp, 1, LRU_W), F32)
    yp, kp, vp, cp, hp, p = _layer(x_prompt, mkb, mvb, None, None, zeros_conv, zeros_h, rel_table, lam, p,
                                   lambda_init, tm=512, tq=256, tm_out=512)
    cache_kt = jnp.transpose(cache_k[l], (0, 2, 3, 4, 1)).reshape(bs * DA_QK_W, past)
    ys, ks, vs, cs, hs, _ = _layer(x_sample, cache_mem_k[l].reshape(bs * n_mem, XA_W),
                                   cache_mem_v[l].reshape(bs * n_mem, XA_W),
                                   cache_kt, cache_v[l].reshape(bs * past * DA_HEADS, DA_V),
                                   state_conv[l], state_lru[l].reshape(bs, 1, LRU_W), rel_table, lam, p,
                                   lambda_init, tm=512, tq=ss, tm_out=128)
    kp = jnp.transpose(kp.reshape(bp, DA_HEADS, 2, DA_HD, sp_), (0, 4, 1, 2, 3))
    return (yp, ys,
            kp[None], vp.reshape(1, bp, sp_, DA_HEADS, DA_V),
            cp[None], hp[None],
            mk.reshape(1, bp, n_mem, XA_HEADS, XA_HD), mv.reshape(1, bp, n_mem, XA_HEADS, XA_HD),
            ks.reshape(1, bs, ss, DA_HEADS, 2, DA_HD), vs.reshape(1, bs, ss, DA_HEADS, DA_V),
            cs[None], hs[None])
```

```python
import functools
import math

import jax
import jax.numpy as jnp
from jax import lax
from jax.experimental import pallas as pl
from jax.experimental.pallas import tpu as pltpu

F32 = jnp.float32
BF16 = jnp.bfloat16

D_MODEL = 1024
CHUNK = 64
CHUNK_SHIFT = 6
DA_HEADS = 8
DA_HD = 64
DA_V = 2 * DA_HD
LRU_W = D_MODEL
LRU_BLOCKS = 8
LRU_BW = LRU_W // LRU_BLOCKS
CONV_W = 4
LRU_C = 8.0
XA_HEADS = 4
XA_HD = 256
N_BUCKETS = 32
MAX_DISTANCE = 128
D_FF = ((8 * D_MODEL + 3 * 256 - 1) // (3 * 256)) * 256
N_BRANCH = 3
EPS = 1e-6
DA_QK_W = DA_HEADS * 2 * DA_HD
DA_V_W = DA_HEADS * DA_V
XA_W = XA_HEADS * XA_HD
IN_OFFS = (0, DA_QK_W, 2 * DA_QK_W, 2 * DA_QK_W + DA_V_W, 2 * DA_QK_W + DA_V_W + LRU_W,
           2 * DA_QK_W + DA_V_W + 2 * LRU_W)
IN_W = 2 * DA_QK_W + DA_V_W + 2 * LRU_W + XA_W
LOG2E = math.log2(math.e)
DA_Q_SCALE = DA_HD ** -0.5 * LOG2E
XA_SCALE = XA_HD ** -0.5
MASK_VALUE = -1e30
FAR_BUCKET = N_BUCKETS // 2 - 1
FFN_CHUNK = D_FF // 2
PROJ_CHUNK = 256
OUT_WEIGHTS = ("w_gate", "w_proj_a", "w_proj_b", "w_proj_c", "w_out", "w_ffn_in", "w_ffn_out")
BF16_SUBLANES = 16
ATTN_LOOKAHEAD = 3

V7X_VMEM_BYTES = 64 * 1024 * 1024
VMEM_LIMIT = V7X_VMEM_BYTES - 8 * 1024 * 1024

NT_DIMS = (((1,), (1,)), ((), ()))


def _cparams(*sem):
    return pltpu.CompilerParams(dimension_semantics=sem, vmem_limit_bytes=VMEM_LIMIT)


def _resident(shape):
    nd = len(shape)
    return pl.BlockSpec(shape, lambda *_: (0,) * nd, pipeline_mode=pl.Buffered(1))


def _rms(x, g):
    return x * lax.rsqrt(jnp.mean(x * x, axis=-1, keepdims=True) + EPS) * g


def _dot(a, b):
    return jnp.dot(a, b, preferred_element_type=F32)


def _in_proj_kernel(x_ref, g_ref, w_ref, cs_ref, h0_ref, cw_ref, cb_ref, wa_ref, ba_ref, wx_ref, bx_ref, lam_ref,
                    q_ref, k_ref, v_ref, kb_ref, vb_ref, qc_ref, bo_ref, nc_ref, hl_ref,
                    xpad_sc, a_sc, u_sc, hs_sc, gg_sc, hc_sc, *, nt, seg, feature_major):
    tm = x_ref.shape[0]
    nseg = tm // seg
    halo = CONV_W - 1
    base = 8 - halo

    @pl.when(pl.program_id(0) % nt == 0)
    def _load_state():
        for s in range(nseg):
            xpad_sc[s, base:8, :] = cs_ref[s]
        hc_sc[...] = h0_ref[...]

    h = _rms(x_ref[...], g_ref[...]).astype(BF16)

    def proj(n, c0, width):
        return _dot(h, w_ref[:, IN_OFFS[n] + c0:IN_OFFS[n] + c0 + width])

    xb = proj(3, 0, LRU_W)
    for s in range(nseg):
        xpad_sc[s, 8:8 + seg, :] = xb[s * seg:(s + 1) * seg]

    def proj_chunk(kind, c0):
        cols = slice(c0, c0 + PROJ_CHUNK)
        if kind == "g":
            gg_sc[:, cols] = jax.nn.gelu(proj(4, c0, PROJ_CHUNK))
        elif kind == "q":
            qs = proj(0, c0, PROJ_CHUNK) * DA_Q_SCALE
            if feature_major:
                q_ref[cols, :] = qs.T.astype(BF16)
            else:
                q_ref[:, cols] = qs.astype(BF16)
        elif kind == "k":
            kc = proj(1, c0, PROJ_CHUNK)
            kb_ref[:, cols] = kc.astype(BF16)
            if feature_major:
                k_ref[cols, :] = kc.T
            else:
                k_ref[:, cols] = kc
        elif kind == "v":
            vc = proj(2, c0, PROJ_CHUNK)
            v_ref[:, cols] = vc
            if feature_major:
                vb_ref[cols, :] = vc.T.astype(BF16)
            else:
                vb_ref[:, cols] = vc.astype(BF16)
        else:
            qc_ref[:, cols] = (proj(5, c0, PROJ_CHUNK) * XA_SCALE).astype(BF16)

    chunks = [(kind, c0) for kind, width in (("g", LRU_W), ("q", DA_QK_W), ("k", DA_QK_W), ("v", DA_V_W), ("c", XA_W))
              for c0 in range(0, width, PROJ_CHUNK)]

    cw = cw_ref[...]
    sp = jax.nn.softplus(-lam_ref[...])
    for n in range(LRU_BLOCKS):
        for kind, c0 in chunks[n * len(chunks) // LRU_BLOCKS:(n + 1) * len(chunks) // LRU_BLOCKS]:
            proj_chunk(kind, c0)
        cols = slice(n * LRU_BW, (n + 1) * LRU_BW)
        xc_segs = []
        for s in range(nseg):
            xc = xpad_sc[s, base:base + seg, cols] * cw[0:1, cols]
            for j in range(1, CONV_W):
                xc = xc + xpad_sc[s, base + j:base + j + seg, cols] * cw[j:j + 1, cols]
            xc_segs.append(cb_ref[:, cols] + xc)
        xc = xc_segs[0] if nseg == 1 else jnp.concatenate(xc_segs, axis=0)
        xcb = xc.astype(BF16)
        tanh_r = jnp.tanh(0.5 * (_dot(xcb, wa_ref[n]) + ba_ref[:, cols]))
        tanh_i = jnp.tanh(0.5 * (_dot(xcb, wx_ref[n]) + bx_ref[:, cols]))
        half_c_sp = (-0.5 * LRU_C) * sp[:, cols]
        log_a = half_c_sp * tanh_r + half_c_sp
        a = jnp.exp(log_a)
        a_sc[:, cols] = a
        one_minus_a2 = -jnp.tanh(log_a) * (a * a + 1.0)
        mult = jnp.where(one_minus_a2 > 0.0, one_minus_a2 * lax.rsqrt(one_minus_a2), 0.0)
        u_sc[:, cols] = mult * ((0.5 * tanh_i + 0.5) * xc)
    for s in range(nseg):
        tail = xpad_sc[s, base + seg:8 + seg, :]
        nc_ref[s] = tail
        xpad_sc[s, base:8, :] = tail

    for s in range(nseg):
        def step(t, hprev, row0=s * seg):
            hnew = a_sc[pl.ds(row0 + t, 1), :] * hprev + u_sc[pl.ds(row0 + t, 1), :]
            hs_sc[pl.ds(row0 + t, 1), :] = hnew
            return hnew

        hlast = lax.fori_loop(0, seg, step, hc_sc[s], unroll=8)
        hc_sc[s] = hlast
        hl_ref[s] = hlast
    bo_ref[...] = (hs_sc[...] * gg_sc[...]).astype(BF16)


def _in_proj(x2d, conv_state, h0, p, tm, seq, feature_major):
    t = x2d.shape[0]
    seg = min(tm, seq)
    nseg = tm // seg
    nt = seq // seg
    assert tm % seg == 0 and seq % seg == 0 and seg % 8 == 0 and seg >= CONV_W - 1
    batch_block = lambda i: (i // nt, 0, 0)
    row = lambda width: pl.BlockSpec((tm, width), lambda i: (i, 0))
    sds = lambda width, dt: jax.ShapeDtypeStruct((t, width), dt)
    if feature_major:
        assert nseg == 1
        fm_spec = lambda width: pl.BlockSpec((width, tm), lambda i: (i // nt, i % nt))
        fm_sds = lambda width, dt: jax.ShapeDtypeStruct((t // seq * width, seq), dt)
    else:
        fm_spec, fm_sds = row, sds
    vec = _resident((1, LRU_W))
    gate_w = _resident((LRU_BLOCKS, LRU_BW, LRU_BW))
    conv_spec = pl.BlockSpec((nseg, CONV_W - 1, LRU_W), batch_block)
    state_spec = pl.BlockSpec((nseg, 1, LRU_W), batch_block)
    return pl.pallas_call(
        functools.partial(_in_proj_kernel, nt=nt, seg=seg, feature_major=feature_major),
        grid=(t // tm,),
        in_specs=[row(D_MODEL), _resident((1, D_MODEL)), _resident((D_MODEL, IN_W)), conv_spec, state_spec,
                  _resident((CONV_W, LRU_W)), vec, gate_w, vec, gate_w, vec, vec],
        out_specs=[fm_spec(DA_QK_W), fm_spec(DA_QK_W), row(DA_V_W), row(DA_QK_W), fm_spec(DA_V_W), row(XA_W),
                   row(LRU_W), conv_spec, state_spec],
        out_shape=[fm_sds(DA_QK_W, BF16), fm_sds(DA_QK_W, F32), sds(DA_V_W, F32), sds(DA_QK_W, BF16),
                   fm_sds(DA_V_W, BF16), sds(XA_W, BF16), sds(LRU_W, BF16),
                   jax.ShapeDtypeStruct(conv_state.shape, F32), jax.ShapeDtypeStruct(h0.shape, F32)],
        scratch_shapes=[pltpu.VMEM((nseg, 8 + seg, LRU_W), F32), pltpu.VMEM((tm, LRU_W), F32),
                        pltpu.VMEM((tm, LRU_W), F32), pltpu.VMEM((tm, LRU_W), F32), pltpu.VMEM((tm, LRU_W), F32),
                        pltpu.VMEM((nseg, 1, LRU_W), F32)],
        compiler_params=_cparams("arbitrary"),
        name="in_proj",
    )(x2d, p["norm_mix"], p["w_in"], conv_state, h0, p["conv_w"], p["conv_b"], p["w_rg_a"], p["b_rg_a"],
      p["w_rg_x"], p["b_rg_x"], p["rg_lambda"])


def _cast_block_specs(weights, steps, step_index):
    specs = []
    for w in weights:
        rows, cols = w.shape
        nblk = max(d for d in range(1, steps + 1)
                   if steps % d == 0 and rows % d == 0 and (rows // d) % BF16_SUBLANES == 0)
        specs.append(pl.BlockSpec((rows // nblk, cols), lambda *g, per=steps // nblk: (step_index(*g) // per, 0)))
    return specs


def _mem_kv_kernel(m_ref, g_ref, w_ref, wc_ref, mk_ref, mv_ref, mkb_ref, mvb_ref, wcb_ref):
    wcb_ref[...] = wc_ref[...].astype(BF16)
    h = _rms(m_ref[...], g_ref[...]).astype(BF16)
    mk = _dot(h, w_ref[:, :XA_W])
    mv = _dot(h, w_ref[:, XA_W:])
    mk_ref[...] = mk
    mv_ref[...] = mv
    mkb_ref[...] = mk.astype(BF16)
    mvb_ref[...] = mv.astype(BF16)


def _mem_kv(mem2d, g, w_bf16, tm, cast_weight):
    t = mem2d.shape[0]
    row = pl.BlockSpec((tm, XA_W), lambda i: (i, 0))
    cast_spec, = _cast_block_specs([cast_weight], t // tm, lambda i: i)
    return pl.pallas_call(
        _mem_kv_kernel,
        grid=(t // tm,),
        in_specs=[pl.BlockSpec((tm, D_MODEL), lambda i: (i, 0)), _resident((1, D_MODEL)),
                  _resident((D_MODEL, 2 * XA_W)), cast_spec],
        out_specs=[row, row, row, row, cast_spec],
        out_shape=[jax.ShapeDtypeStruct((t, XA_W), F32), jax.ShapeDtypeStruct((t, XA_W), F32),
                   jax.ShapeDtypeStruct((t, XA_W), BF16), jax.ShapeDtypeStruct((t, XA_W), BF16),
                   jax.ShapeDtypeStruct(cast_weight.shape, BF16)],
        compiler_params=_cparams("parallel"),
        name="mem_kv",
    )(mem2d, g, w_bf16, cast_weight)


def _rel_bucket(rel):
    half = N_BUCKETS // 2
    max_exact = half // 2
    n = jnp.abs(rel)
    nf = jnp.maximum(n, 1).astype(F32)
    large = max_exact + (jnp.log(nf / max_exact) / math.log(MAX_DISTANCE / max_exact)
                         * (half - max_exact)).astype(jnp.int32)
    large = jnp.minimum(large, half - 1)
    return jnp.where(rel > 0, half, 0) + jnp.where(n < max_exact, n, large)


def _table_lanes(rel_table):
    return jnp.pad(rel_table.T, ((0, 0), (0, 128 - N_BUCKETS)))


def _bias_from_buckets(bkt, tabt_ref, head, vis, shift_bucket):
    rows, width = bkt.shape
    table = jnp.broadcast_to(tabt_ref[head:head + 1, :], (rows, 128))
    bias = jnp.concatenate([jnp.take_along_axis(table, bkt[:, c:c + 128], axis=1) for c in range(0, width, 128)],
                           axis=1)
    if shift_bucket is not None:
        bias = bias - tabt_ref[head:head + 1, shift_bucket:shift_bucket + 1]
    return jnp.where(vis, bias * LOG2E, MASK_VALUE)


def _split_maps(qh):
    lane = lax.broadcasted_iota(jnp.int32, qh.shape, 1)
    zero = jnp.zeros_like(qh)
    return jnp.concatenate([jnp.where(lane < DA_HD, qh, zero), jnp.where(lane >= DA_HD, qh, zero)], axis=0)


def _ones_column(n):
    return jnp.where(lax.broadcasted_iota(jnp.int32, (n, DA_V), 1) == 0, 1.0, 0.0).astype(BF16)


def _diff_finish(o, lam, g, tq, lambda_init):
    d = o[:tq] - lam * o[tq:]
    return _rms(d, g) * (1.0 - lambda_init)


def _attn_self_kernel(tab_ref, lam_ref, qt_ref, k_ref, vt_ref, bkt_ref, g_ref, *rest, tq, lambda_init, n_cast):
    w_refs, o_ref, wb_refs = rest[:n_cast], rest[n_cast], rest[n_cast + 1:2 * n_cast + 1]
    bias_sc, q2t_sc, m_sc, l_sc, acc_sc, s_sc = rest[2 * n_cast + 1:]
    for w_ref, wb_ref in zip(w_refs, wb_refs):
        wb_ref[...] = w_ref[...].astype(BF16)

    b = pl.program_id(0)
    i = pl.program_id(1)

    @pl.when((b == 0) & (i == 0))
    def _build_bias():
        bkt = bkt_ref[...]
        kpos = lax.broadcasted_iota(jnp.int32, bkt.shape, 0)
        qpos = lax.broadcasted_iota(jnp.int32, bkt.shape, 1)
        vis = jnp.right_shift(kpos - tq, CHUNK_SHIFT) <= jnp.right_shift(qpos, CHUNK_SHIFT)
        for h in range(DA_HEADS):
            bias_sc[h] = _bias_from_buckets(bkt, tab_ref, h, vis, FAR_BUCKET)

    feat = lax.broadcasted_iota(jnp.int32, (2 * DA_HD, tq), 0)
    for h in range(DA_HEADS):
        qt = qt_ref[h * 2 * DA_HD:(h + 1) * 2 * DA_HD, :]
        zero = jnp.zeros_like(qt)
        q2t_sc[h] = jnp.concatenate([jnp.where(feat < DA_HD, qt, zero), jnp.where(feat >= DA_HD, qt, zero)], axis=1)
    m_sc[...] = jnp.full(m_sc.shape, MASK_VALUE, F32)
    l_sc[...] = jnp.zeros(l_sc.shape, F32)
    acc_sc[...] = jnp.zeros(acc_sc.shape, F32)

    def logits(h, start):
        kj = k_ref[pl.ds(start, tq), h * DA_V:(h + 1) * DA_V]
        return _dot(kj, q2t_sc[h])

    def update(h, start, s, bias):
        vtj = vt_ref[h * DA_V:(h + 1) * DA_V, pl.ds(start, tq)]
        if bias is not None:
            s = s + jnp.concatenate([bias, bias], axis=1)
        m_prev = m_sc[h]
        m_new = jnp.maximum(m_prev, jnp.max(s, axis=0, keepdims=True))
        alpha = jnp.exp2(m_prev - m_new)
        p = jnp.exp2(s - m_new[0:1])
        l_sc[h] = alpha * l_sc[h] + jnp.sum(p, axis=0, keepdims=True)
        acc_sc[h] = alpha[0:1] * acc_sc[h] + _dot(vtj, p.astype(BF16))
        m_sc[h] = m_new

    def key_tile(start, bias_of, after=None):
        nslot = ATTN_LOOKAHEAD + 1
        for h in range(ATTN_LOOKAHEAD):
            s_sc[h % nslot] = logits(h, start)
        for h in range(DA_HEADS):
            if h + ATTN_LOOKAHEAD < DA_HEADS:
                s_sc[(h + ATTN_LOOKAHEAD) % nslot] = logits(h + ATTN_LOOKAHEAD, start)
            update(h, start, s_sc[h % nslot], bias_of(h))
            if after is not None:
                after(h)

    def far_body(j, carry):
        key_tile(pl.multiple_of(j * tq, tq), lambda h: None)
        return carry

    lax.fori_loop(0, jnp.maximum(i - 1, 0), far_body, 0)

    @pl.when(i >= 1)
    def _prev_tile():
        key_tile(pl.multiple_of((i - 1) * tq, tq), lambda h: bias_sc[h, :tq, :])

    lam = lam_ref[0]
    g = g_ref[...]

    def finish(h):
        o = acc_sc[h] / l_sc[h][0:1]
        d = o[:, :tq] - lam * o[:, tq:]
        y = d * lax.rsqrt(jnp.mean(d * d, axis=0, keepdims=True) + EPS) * g * (1.0 - lambda_init)
        o_ref[:, h * DA_V:(h + 1) * DA_V] = y.T.astype(BF16)

    key_tile(pl.multiple_of(i * tq, tq), lambda h: bias_sc[h, tq:, :], after=finish)


def _attn_self(qtb, kb, vtb, rel_table, lam, subln_g, batch, seq, lambda_init, tq, cast_weights):
    assert seq % tq == 0 and tq % 128 == 0 and tq >= MAX_DISTANCE and tq % CHUNK == 0
    nq = seq // tq
    rel = jnp.arange(2 * tq, dtype=jnp.int32)[:, None] - (tq + jnp.arange(tq, dtype=jnp.int32))[None, :]
    bkt = _rel_bucket(rel)
    smem = pl.BlockSpec(memory_space=pltpu.SMEM)

    cast_specs = _cast_block_specs(cast_weights, batch * nq, lambda b, i: b * nq + i)
    n_cast = len(cast_weights)
    outs = pl.pallas_call(
        functools.partial(_attn_self_kernel, tq=tq, lambda_init=lambda_init, n_cast=n_cast),
        grid=(batch, nq),
        in_specs=[_resident((DA_HEADS, 128)), smem,
                  pl.BlockSpec((DA_QK_W, tq), lambda b, i: (b, i)),
                  pl.BlockSpec((seq, DA_QK_W), lambda b, i: (b, 0)),
                  pl.BlockSpec((DA_V_W, seq), lambda b, i: (b, 0)),
                  _resident((2 * tq, tq)), _resident((DA_V, 1))] + cast_specs,
        out_specs=[pl.BlockSpec((tq, DA_V_W), lambda b, i: (b * nq + i, 0))] + cast_specs,
        out_shape=[jax.ShapeDtypeStruct((batch * seq, DA_V_W), BF16)]
                  + [jax.ShapeDtypeStruct(w.shape, BF16) for w in cast_weights],
        scratch_shapes=[pltpu.VMEM((DA_HEADS, 2 * tq, tq), F32),
                        pltpu.VMEM((DA_HEADS, 2 * DA_HD, 2 * tq), BF16),
                        pltpu.VMEM((DA_HEADS, 8, 2 * tq), F32),
                        pltpu.VMEM((DA_HEADS, 8, 2 * tq), F32),
                        pltpu.VMEM((DA_HEADS, DA_V, 2 * tq), F32),
                        pltpu.VMEM((ATTN_LOOKAHEAD + 1, tq, 2 * tq), F32)],
        compiler_params=_cparams("arbitrary", "arbitrary"),
        name="attn_self",
    )(_table_lanes(rel_table), lam, qtb, kb, vtb, bkt, subln_g.reshape(DA_V, 1), *cast_weights)
    return outs[0], outs[1:]


def _attn_cached_kernel(tab_ref, lam_ref, q_ref, ckt_ref, cv_ref, nk_ref, nv_ref, bktc_ref, bktn_ref, g_ref, o_ref,
                        biasc_sc, biasn_sc, *, sq, past, lambda_init):
    @pl.when(pl.program_id(0) == 0)
    def _build_bias():
        for bkt_ref, sc, k0 in ((bktc_ref, biasc_sc, 0), (bktn_ref, biasn_sc, past)):
            bkt = bkt_ref[...]
            qpos = past + lax.broadcasted_iota(jnp.int32, bkt.shape, 0)
            kpos = k0 + lax.broadcasted_iota(jnp.int32, bkt.shape, 1)
            vis = jnp.right_shift(kpos, CHUNK_SHIFT) <= jnp.right_shift(qpos, CHUNK_SHIFT)
            for h in range(DA_HEADS):
                sc[h] = _bias_from_buckets(bkt, tab_ref, h, vis, None)[:, :sc.shape[2]]

    q = q_ref[...]
    lam = lam_ref[0]
    g = g_ref[...]
    for h in range(DA_HEADS):
        cols = slice(h * DA_V, (h + 1) * DA_V)
        q2 = _split_maps(q[:, cols])
        s_c = _dot(q2, ckt_ref[cols, :].astype(BF16))
        s_c = (s_c.reshape(2, sq, past) + biasc_sc[h][None]).reshape(2 * sq, past)
        s_n = lax.dot_general(q2, nk_ref[:, cols].astype(BF16), NT_DIMS, preferred_element_type=F32)
        s_n = (s_n.reshape(2, sq, sq) + biasn_sc[h][None]).reshape(2 * sq, sq)
        m = jnp.maximum(jnp.max(s_c, axis=1, keepdims=True), jnp.max(s_n, axis=1, keepdims=True))
        p_c = jnp.exp2(s_c - m)
        p_n = jnp.exp2(s_n - m)
        l = jnp.sum(p_c, axis=1, keepdims=True) + jnp.sum(p_n, axis=1, keepdims=True)
        v_c = cv_ref[pl.ds(h, past, stride=DA_HEADS), :]
        o = (_dot(p_c.astype(BF16), v_c.astype(BF16))
             + _dot(p_n.astype(BF16), nv_ref[:, cols].astype(BF16))) / l
        o_ref[:, cols] = _diff_finish(o, lam, g, sq, lambda_init).astype(BF16)


def _attn_cached(q, cache_kt2d, cache_v2d, nk, nv, rel_table, lam, subln_g, batch, sq, past, lambda_init):
    qpos = past + jnp.arange(sq, dtype=jnp.int32)
    bkt_c = _rel_bucket(jnp.arange(past, dtype=jnp.int32)[None, :] - qpos[:, None])
    assert sq <= 128 and past % 128 == 0
    bkt_n = jnp.pad(_rel_bucket(qpos[None, :] - qpos[:, None]), ((0, 0), (0, 128 - sq)))
    smem = pl.BlockSpec(memory_space=pltpu.SMEM)
    new_spec = pl.BlockSpec((sq, DA_QK_W), lambda b: (b, 0))
    return pl.pallas_call(
        functools.partial(_attn_cached_kernel, sq=sq, past=past, lambda_init=lambda_init),
        grid=(batch,),
        in_specs=[_resident((DA_HEADS, 128)), smem, new_spec,
                  pl.BlockSpec((DA_QK_W, past), lambda b: (b, 0)),
                  pl.BlockSpec((past * DA_HEADS, DA_V), lambda b: (b, 0)),
                  new_spec, new_spec,
                  _resident((sq, past)), _resident((sq, 128)), _resident((1, DA_V))],
        out_specs=new_spec,
        out_shape=jax.ShapeDtypeStruct((batch * sq, DA_V_W), BF16),
        scratch_shapes=[pltpu.VMEM((DA_HEADS, sq, past), F32), pltpu.VMEM((DA_HEADS, sq, sq), F32)],
        compiler_params=_cparams("arbitrary"),
        name="attn_cached",
    )(_table_lanes(rel_table), lam, q, cache_kt2d, cache_v2d, nk, nv, bkt_c, bkt_n, subln_g)


def _mix_ffn_kernel(x_ref, a_ref, b_ref, qc_ref, mk_ref, mv_ref, gmix_ref, wg_ref, bg_ref, wpa_ref, wpb_ref, wpc_ref,
                    wo_ref, gffn_ref, wfi_ref, wfo_ref, gfin_ref, y_ref, *, seg):
    tm = x_ref.shape[0]
    nseg = tm // seg
    n_mem = mk_ref.shape[0] // nseg

    def xattn_head(hh):
        cols = slice(hh * XA_HD, (hh + 1) * XA_HD)
        outs = []
        for s in range(nseg):
            mem = slice(s * n_mem, (s + 1) * n_mem)
            logit = lax.dot_general(qc_ref[s * seg:(s + 1) * seg, cols], mk_ref[mem, cols].astype(BF16), NT_DIMS,
                                    preferred_element_type=F32)
            p = jnp.exp(logit - jnp.max(logit, axis=1, keepdims=True))
            l = jnp.sum(p, axis=1, keepdims=True)
            outs.append((_dot(p.astype(BF16), mv_ref[mem, cols].astype(BF16)) / l).astype(BF16))
        return outs[0] if nseg == 1 else jnp.concatenate(outs, axis=0)

    x = x_ref[...]
    h = _rms(x, gmix_ref[...]).astype(BF16)

    def branch(n, br, wp_ref):
        cols = slice(n * D_MODEL, (n + 1) * D_MODEL)
        gate = jax.nn.sigmoid(_dot(h, wg_ref[:, cols]) + bg_ref[:, cols])
        return gate * _dot(br, wp_ref[...])

    merged = branch(0, a_ref[...], wpa_ref)
    c_heads = [xattn_head(hh) for hh in range(XA_HEADS // 2)]
    merged = merged + branch(1, b_ref[...], wpb_ref)
    c_heads += [xattn_head(hh) for hh in range(XA_HEADS // 2, XA_HEADS)]
    merged = merged + branch(2, jnp.concatenate(c_heads, axis=1), wpc_ref)
    x1 = x + _dot(merged.astype(BF16), wo_ref[...])
    h2 = _rms(x1, gffn_ref[...]).astype(BF16)
    ff = None
    for c in range(D_FF // FFN_CHUNK):
        gcols = slice(c * FFN_CHUNK, (c + 1) * FFN_CHUNK)
        ucols = slice(D_FF + c * FFN_CHUNK, D_FF + (c + 1) * FFN_CHUNK)
        act = (jax.nn.silu(_dot(h2, wfi_ref[:, gcols])) * _dot(h2, wfi_ref[:, ucols])).astype(BF16)
        term = _dot(act, wfo_ref[gcols, :])
        ff = term if ff is None else ff + term
    y_ref[...] = _rms(x1 + ff, gfin_ref[...])


def _mix_ffn(x2d, a_out, b_out, qc, mk2d, mv2d, p, tm, seq):
    t = x2d.shape[0]
    seg = min(tm, seq)
    nseg = tm // seg
    nt = seq // seg
    n_mem = mk2d.shape[0] // (t // seq)
    assert tm % seg == 0 and seq % seg == 0
    row = pl.BlockSpec((tm, D_MODEL), lambda i: (i, 0))
    mem = pl.BlockSpec((nseg * n_mem, XA_W), lambda i: (i // nt, 0))
    vec = _resident((1, D_MODEL))
    sq_w = _resident((D_MODEL, D_MODEL))
    return pl.pallas_call(
        functools.partial(_mix_ffn_kernel, seg=seg),
        grid=(t // tm,),
        in_specs=[row, row, row, row, mem, mem, vec, _resident((D_MODEL, N_BRANCH * D_MODEL)),
                  _resident((1, N_BRANCH * D_MODEL)), sq_w, sq_w, sq_w, sq_w, vec,
                  _resident((D_MODEL, 2 * D_FF)), _resident((D_FF, D_MODEL)), vec],
        out_specs=row,
        out_shape=jax.ShapeDtypeStruct((t, D_MODEL), F32),
        compiler_params=_cparams("parallel"),
        name="mix_ffn",
    )(x2d, a_out, b_out, qc, mk2d, mv2d, p["norm_mix"], p["w_gate"], p["b_gate"], p["w_proj_a"], p["w_proj_b"],
      p["w_proj_c"], p["w_out"], p["norm_ffn"], p["w_ffn_in"], p["w_ffn_out"], p["norm_final"])


def _layer(x, mk2d, mv2d, cache_kt2d, cache_v2d, conv_state, h0, rel_table, lam, p, lambda_init, tm, tq, tm_out):
    batch, seq, _ = x.shape
    x2d = x.reshape(batch * seq, D_MODEL)
    no_history = cache_kt2d is None
    q, k, v, kb, vb, qc, b_out, new_conv, h_last = _in_proj(x2d, conv_state, h0, p, tm, seq, no_history)
    if no_history:
        a_out, out_w = _attn_self(q, kb, vb, rel_table, lam, p["subln_g"], batch, seq, lambda_init, tq,
                                  [p[n] for n in OUT_WEIGHTS])
        p = {**p, **dict(zip(OUT_WEIGHTS, out_w))}
    else:
        past = cache_kt2d.shape[1]
        a_out = _attn_cached(q, cache_kt2d, cache_v2d, k, v, rel_table, lam, p["subln_g"], batch, seq, past,
                             lambda_init)
    y = _mix_ffn(x2d, a_out, b_out, qc, mk2d, mv2d, p, tm_out, seq)
    return y.reshape(batch, seq, D_MODEL), k, v, new_conv, h_last.reshape(batch, LRU_W), p


def kernel(x_prompt, x_sample, mem_prompt, cache_k, cache_v, state_conv, state_lru, cache_mem_k, cache_mem_v,
           rel_table, norm_mix, w_in, lambda_q1, lambda_k1, lambda_q2, lambda_k2, subln_g, conv_w, conv_b,
           w_rg_a, b_rg_a, w_rg_x, b_rg_x, rg_lambda, norm_mem, w_mem_kv, w_proj_a, w_proj_b, w_proj_c,
           w_gate, b_gate, w_out, norm_ffn, w_ffn_in, w_ffn_out, norm_final):
    depth = w_in.shape[0]
    assert depth == 1, "the final norm is fused into the layer's last kernel"
    l = 0
    lambda_init = 0.8 - 0.6 * math.exp(-0.3 * l)
    bp, sp_, _ = x_prompt.shape
    bs, ss, _ = x_sample.shape
    past = cache_k.shape[2]
    n_mem = mem_prompt.shape[1]
    row = lambda a: a.reshape(1, -1).astype(F32)
    mk, mv, mkb, mvb, w_in_bf16 = _mem_kv(mem_prompt.reshape(bp * n_mem, D_MODEL), row(norm_mem[l]),
                                          w_mem_kv[l].astype(BF16), 256, w_in[l])
    p = dict(norm_mix=row(norm_mix[l]), w_in=w_in_bf16, subln_g=row(subln_g[l]),
             conv_w=conv_w[l], conv_b=row(conv_b[l]), w_rg_a=w_rg_a[l].astype(BF16), b_rg_a=row(b_rg_a[l]),
             w_rg_x=w_rg_x[l].astype(BF16), b_rg_x=row(b_rg_x[l]), rg_lambda=row(rg_lambda[l]),
             w_proj_a=w_proj_a[l], w_proj_b=w_proj_b[l], w_proj_c=w_proj_c[l], w_gate=w_gate[l],
             b_gate=row(b_gate[l]), w_out=w_out[l], norm_ffn=row(norm_ffn[l]), w_ffn_in=w_ffn_in[l],
             w_ffn_out=w_ffn_out[l], norm_final=row(norm_final))
    lam = (jnp.exp(jnp.sum(lambda_q1[l] * lambda_k1[l]).astype(F32))
           - jnp.exp(jnp.sum(lambda_q2[l] * lambda_k2[l]).astype(F32)) + lambda_init).reshape(1)

    zeros_conv = jnp.zeros((bp, CONV_W - 1, LRU_W), F32)
    zeros_h = jnp.zeros((bp, 1, LRU_W), F32)
    yp, kp, vp, cp, hp, p = _layer(x_prompt, mkb, mvb, None, None, zeros_conv, zeros_h, rel_table, lam, p,
                                   lambda_init, tm=512, tq=256, tm_out=512)
    cache_kt = jnp.transpose(cache_k[l], (0, 2, 3, 4, 1)).reshape(bs * DA_QK_W, past)
    ys, ks, vs, cs, hs, _ = _layer(x_sample, cache_mem_k[l].reshape(bs * n_mem, XA_W),
                                   cache_mem_v[l].reshape(bs * n_mem, XA_W),
                                   cache_kt, cache_v[l].reshape(bs * past * DA_HEADS, DA_V),
                                   state_conv[l], state_lru[l].reshape(bs, 1, LRU_W), rel_table, lam, p,
                                   lambda_init, tm=256, tq=ss, tm_out=128)
    kp = jnp.transpose(kp.reshape(bp, DA_HEADS, 2, DA_HD, sp_), (0, 4, 1, 2, 3))
    return (yp, ys,
            kp[None], vp.reshape(1, bp, sp_, DA_HEADS, DA_V),
            cp[None], hp[None],
            mk.reshape(1, bp, n_mem, XA_HEADS, XA_HD), mv.reshape(1, bp, n_mem, XA_HEADS, XA_HD),
            ks.reshape(1, bs, ss, DA_HEADS, 2, DA_HD), vs.reshape(1, bs, ss, DA_HEADS, DA_V),
            cs[None], hs[None])
```

```python
import functools
import math

import jax
import jax.numpy as jnp
from jax import lax
from jax.experimental import pallas as pl
from jax.experimental.pallas import tpu as pltpu

F32 = jnp.float32
BF16 = jnp.bfloat16

D_MODEL = 1024
CHUNK = 64
CHUNK_SHIFT = 6
DA_HEADS = 8
DA_HD = 64
DA_V = 2 * DA_HD
LRU_W = D_MODEL
LRU_BLOCKS = 8
LRU_BW = LRU_W // LRU_BLOCKS
CONV_W = 4
LRU_C = 8.0
XA_HEADS = 4
XA_HD = 256
N_BUCKETS = 32
MAX_DISTANCE = 128
D_FF = ((8 * D_MODEL + 3 * 256 - 1) // (3 * 256)) * 256
N_BRANCH = 3
EPS = 1e-6
DA_QK_W = DA_HEADS * 2 * DA_HD
DA_V_W = DA_HEADS * DA_V
XA_W = XA_HEADS * XA_HD
IN_OFFS = (0, DA_QK_W, 2 * DA_QK_W, 2 * DA_QK_W + DA_V_W, 2 * DA_QK_W + DA_V_W + LRU_W,
           2 * DA_QK_W + DA_V_W + 2 * LRU_W)
IN_W = 2 * DA_QK_W + DA_V_W + 2 * LRU_W + XA_W
LOG2E = math.log2(math.e)
DA_Q_SCALE = DA_HD ** -0.5 * LOG2E
XA_SCALE = XA_HD ** -0.5
MASK_VALUE = -1e30
FAR_BUCKET = N_BUCKETS // 2 - 1
FFN_CHUNK = D_FF // 2
PROJ_CHUNK = 256
OUT_WEIGHTS = ("w_gate", "w_proj_a", "w_proj_b", "w_proj_c", "w_out", "w_ffn_in", "w_ffn_out")
BF16_SUBLANES = 16
ATTN_LOOKAHEAD = 3

V7X_VMEM_BYTES = 64 * 1024 * 1024
VMEM_LIMIT = V7X_VMEM_BYTES - 8 * 1024 * 1024

NT_DIMS = (((1,), (1,)), ((), ()))


def _cparams(*sem):
    return pltpu.CompilerParams(dimension_semantics=sem, vmem_limit_bytes=VMEM_LIMIT)


def _resident(shape):
    nd = len(shape)
    return pl.BlockSpec(shape, lambda *_: (0,) * nd, pipeline_mode=pl.Buffered(1))


def _rms(x, g):
    return x * lax.rsqrt(jnp.mean(x * x, axis=-1, keepdims=True) + EPS) * g


def _dot(a, b):
    return jnp.dot(a, b, preferred_element_type=F32)


def _in_proj_kernel(x_ref, g_ref, w_ref, cs_ref, h0_ref, cw_ref, cb_ref, wrg_ref, ba_ref, bx_ref, lam_ref,
                    q_ref, k_ref, v_ref, kb_ref, vb_ref, qc_ref, bo_ref, nc_ref, hl_ref,
                    xpad_sc, a_sc, u_sc, hs_sc, gg_sc, hc_sc, *, nt, seg, feature_major):
    tm = x_ref.shape[0]
    nseg = tm // seg
    halo = CONV_W - 1
    base = 8 - halo

    @pl.when(pl.program_id(0) % nt == 0)
    def _load_state():
        for s in range(nseg):
            xpad_sc[s, base:8, :] = cs_ref[s]
        hc_sc[...] = h0_ref[...]

    h = _rms(x_ref[...], g_ref[...]).astype(BF16)

    def proj(n, c0, width):
        return _dot(h, w_ref[:, IN_OFFS[n] + c0:IN_OFFS[n] + c0 + width])

    xb = proj(3, 0, LRU_W)
    for s in range(nseg):
        xpad_sc[s, 8:8 + seg, :] = xb[s * seg:(s + 1) * seg]

    def proj_chunk(kind, c0):
        cols = slice(c0, c0 + PROJ_CHUNK)
        if kind == "g":
            gg_sc[:, cols] = jax.nn.gelu(proj(4, c0, PROJ_CHUNK))
        elif kind == "q":
            qs = proj(0, c0, PROJ_CHUNK) * DA_Q_SCALE
            if feature_major:
                q_ref[cols, :] = qs.T.astype(BF16)
            else:
                q_ref[:, cols] = qs.astype(BF16)
        elif kind == "k":
            kc = proj(1, c0, PROJ_CHUNK)
            kb_ref[:, cols] = kc.astype(BF16)
            if feature_major:
                k_ref[cols, :] = kc.T
            else:
                k_ref[:, cols] = kc
        elif kind == "v":
            vc = proj(2, c0, PROJ_CHUNK)
            v_ref[:, cols] = vc
            if feature_major:
                vb_ref[cols, :] = vc.T.astype(BF16)
            else:
                vb_ref[:, cols] = vc.astype(BF16)
        else:
            qc_ref[:, cols] = (proj(5, c0, PROJ_CHUNK) * XA_SCALE).astype(BF16)

    chunks = [(kind, c0) for kind, width in (("g", LRU_W), ("q", DA_QK_W), ("k", DA_QK_W), ("v", DA_V_W), ("c", XA_W))
              for c0 in range(0, width, PROJ_CHUNK)]

    cw = cw_ref[...]
    sp = jax.nn.softplus(-lam_ref[...])
    for n in range(LRU_BLOCKS):
        for kind, c0 in chunks[n * len(chunks) // LRU_BLOCKS:(n + 1) * len(chunks) // LRU_BLOCKS]:
            proj_chunk(kind, c0)
        cols = slice(n * LRU_BW, (n + 1) * LRU_BW)
        xc_segs = []
        for s in range(nseg):
            xc = xpad_sc[s, base:base + seg, cols] * cw[0:1, cols]
            for j in range(1, CONV_W):
                xc = xc + xpad_sc[s, base + j:base + j + seg, cols] * cw[j:j + 1, cols]
            xc_segs.append(cb_ref[:, cols] + xc)
        xc = xc_segs[0] if nseg == 1 else jnp.concatenate(xc_segs, axis=0)
        xcb = xc.astype(BF16)
        pre = _dot(xcb, wrg_ref[n])
        tanh_r = jnp.tanh(0.5 * (pre[:, :LRU_BW] + ba_ref[:, cols]))
        tanh_i = jnp.tanh(0.5 * (pre[:, LRU_BW:] + bx_ref[:, cols]))
        half_c_sp = (-0.5 * LRU_C) * sp[:, cols]
        log_a = half_c_sp * tanh_r + half_c_sp
        a = jnp.exp(log_a)
        a_sc[:, cols] = a
        one_minus_a2 = -jnp.tanh(log_a) * (a * a + 1.0)
        mult = jnp.where(one_minus_a2 > 0.0, one_minus_a2 * lax.rsqrt(one_minus_a2), 0.0)
        u_sc[:, cols] = mult * ((0.5 * tanh_i + 0.5) * xc)
    for s in range(nseg):
        tail = xpad_sc[s, base + seg:8 + seg, :]
        nc_ref[s] = tail
        xpad_sc[s, base:8, :] = tail

    for s in range(nseg):
        def step(t, hprev, row0=s * seg):
            hnew = a_sc[pl.ds(row0 + t, 1), :] * hprev + u_sc[pl.ds(row0 + t, 1), :]
            hs_sc[pl.ds(row0 + t, 1), :] = hnew
            return hnew

        hlast = lax.fori_loop(0, seg, step, hc_sc[s], unroll=8)
        hc_sc[s] = hlast
        hl_ref[s] = hlast
    bo_ref[...] = (hs_sc[...] * gg_sc[...]).astype(BF16)


def _in_proj(x2d, conv_state, h0, p, tm, seq, feature_major):
    t = x2d.shape[0]
    seg = min(tm, seq)
    nseg = tm // seg
    nt = seq // seg
    assert tm % seg == 0 and seq % seg == 0 and seg % 8 == 0 and seg >= CONV_W - 1
    batch_block = lambda i: (i // nt, 0, 0)
    row = lambda width: pl.BlockSpec((tm, width), lambda i: (i, 0))
    sds = lambda width, dt: jax.ShapeDtypeStruct((t, width), dt)
    if feature_major:
        assert nseg == 1
        fm_spec = lambda width: pl.BlockSpec((width, tm), lambda i: (i // nt, i % nt))
        fm_sds = lambda width, dt: jax.ShapeDtypeStruct((t // seq * width, seq), dt)
    else:
        fm_spec, fm_sds = row, sds
    vec = _resident((1, LRU_W))
    gate_w = _resident((LRU_BLOCKS, LRU_BW, 2 * LRU_BW))
    conv_spec = pl.BlockSpec((nseg, CONV_W - 1, LRU_W), batch_block)
    state_spec = pl.BlockSpec((nseg, 1, LRU_W), batch_block)
    return pl.pallas_call(
        functools.partial(_in_proj_kernel, nt=nt, seg=seg, feature_major=feature_major),
        grid=(t // tm,),
        in_specs=[row(D_MODEL), _resident((1, D_MODEL)), _resident((D_MODEL, IN_W)), conv_spec, state_spec,
                  _resident((CONV_W, LRU_W)), vec, gate_w, vec, vec, vec],
        out_specs=[fm_spec(DA_QK_W), fm_spec(DA_QK_W), row(DA_V_W), row(DA_QK_W), fm_spec(DA_V_W), row(XA_W),
                   row(LRU_W), conv_spec, state_spec],
        out_shape=[fm_sds(DA_QK_W, BF16), fm_sds(DA_QK_W, F32), sds(DA_V_W, F32), sds(DA_QK_W, BF16),
                   fm_sds(DA_V_W, BF16), sds(XA_W, BF16), sds(LRU_W, BF16),
                   jax.ShapeDtypeStruct(conv_state.shape, F32), jax.ShapeDtypeStruct(h0.shape, F32)],
        scratch_shapes=[pltpu.VMEM((nseg, 8 + seg, LRU_W), F32), pltpu.VMEM((tm, LRU_W), F32),
                        pltpu.VMEM((tm, LRU_W), F32), pltpu.VMEM((tm, LRU_W), F32), pltpu.VMEM((tm, LRU_W), F32),
                        pltpu.VMEM((nseg, 1, LRU_W), F32)],
        compiler_params=_cparams("arbitrary"),
        name="in_proj",
    )(x2d, p["norm_mix"], p["w_in"], conv_state, h0, p["conv_w"], p["conv_b"], p["w_rg"], p["b_rg_a"],
      p["b_rg_x"], p["rg_lambda"])


def _cast_block_specs(weights, steps, step_index):
    specs = []
    for w in weights:
        rows, cols = w.shape
        nblk = max(d for d in range(1, steps + 1)
                   if steps % d == 0 and rows % d == 0 and (rows // d) % BF16_SUBLANES == 0)
        specs.append(pl.BlockSpec((rows // nblk, cols), lambda *g, per=steps // nblk: (step_index(*g) // per, 0)))
    return specs


def _mem_kv_kernel(m_ref, g_ref, w_ref, mk_ref, mv_ref, mkb_ref, mvb_ref):
    h = _rms(m_ref[...], g_ref[...]).astype(BF16)
    mk = _dot(h, w_ref[:, :XA_W])
    mv = _dot(h, w_ref[:, XA_W:])
    mk_ref[...] = mk
    mv_ref[...] = mv
    mkb_ref[...] = mk.astype(BF16)
    mvb_ref[...] = mv.astype(BF16)


def _mem_kv(mem2d, g, w_bf16, tm):
    t = mem2d.shape[0]
    row = pl.BlockSpec((tm, XA_W), lambda i: (i, 0))
    return pl.pallas_call(
        _mem_kv_kernel,
        grid=(t // tm,),
        in_specs=[pl.BlockSpec((tm, D_MODEL), lambda i: (i, 0)), _resident((1, D_MODEL)),
                  _resident((D_MODEL, 2 * XA_W))],
        out_specs=[row, row, row, row],
        out_shape=[jax.ShapeDtypeStruct((t, XA_W), F32), jax.ShapeDtypeStruct((t, XA_W), F32),
                   jax.ShapeDtypeStruct((t, XA_W), BF16), jax.ShapeDtypeStruct((t, XA_W), BF16)],
        compiler_params=_cparams("parallel"),
        name="mem_kv",
    )(mem2d, g, w_bf16)


def _rel_bucket(rel):
    half = N_BUCKETS // 2
    max_exact = half // 2
    n = jnp.abs(rel)
    nf = jnp.maximum(n, 1).astype(F32)
    large = max_exact + (jnp.log(nf / max_exact) / math.log(MAX_DISTANCE / max_exact)
                         * (half - max_exact)).astype(jnp.int32)
    large = jnp.minimum(large, half - 1)
    return jnp.where(rel > 0, half, 0) + jnp.where(n < max_exact, n, large)


def _table_lanes(rel_table):
    return jnp.pad(rel_table.T, ((0, 0), (0, 128 - N_BUCKETS)))


def _bias_from_buckets(bkt, tabt_ref, head, vis, shift_bucket):
    rows, width = bkt.shape
    table = jnp.broadcast_to(tabt_ref[head:head + 1, :], (rows, 128))
    bias = jnp.concatenate([jnp.take_along_axis(table, bkt[:, c:c + 128], axis=1) for c in range(0, width, 128)],
                           axis=1)
    if shift_bucket is not None:
        bias = bias - tabt_ref[head:head + 1, shift_bucket:shift_bucket + 1]
    return jnp.where(vis, bias * LOG2E, MASK_VALUE)


def _split_maps(qh):
    lane = lax.broadcasted_iota(jnp.int32, qh.shape, 1)
    zero = jnp.zeros_like(qh)
    return jnp.concatenate([jnp.where(lane < DA_HD, qh, zero), jnp.where(lane >= DA_HD, qh, zero)], axis=0)


def _ones_column(n):
    return jnp.where(lax.broadcasted_iota(jnp.int32, (n, DA_V), 1) == 0, 1.0, 0.0).astype(BF16)


def _diff_finish(o, lam, g, tq, lambda_init):
    d = o[:tq] - lam * o[tq:]
    return _rms(d, g) * (1.0 - lambda_init)


def _attn_self_kernel(tab_ref, lam_ref, qt_ref, k_ref, vt_ref, bkt_ref, g_ref, *rest, tq, lambda_init, n_cast):
    w_refs, o_ref, wb_refs = rest[:n_cast], rest[n_cast], rest[n_cast + 1:2 * n_cast + 1]
    bias_sc, q2t_sc, m_sc, l_sc, acc_sc, s_sc = rest[2 * n_cast + 1:]
    for w_ref, wb_ref in zip(w_refs, wb_refs):
        wb_ref[...] = w_ref[...].astype(BF16)

    b = pl.program_id(0)
    i = pl.program_id(1)

    @pl.when((b == 0) & (i == 0))
    def _build_bias():
        bkt = bkt_ref[...]
        kpos = lax.broadcasted_iota(jnp.int32, bkt.shape, 0)
        qpos = lax.broadcasted_iota(jnp.int32, bkt.shape, 1)
        vis = jnp.right_shift(kpos - tq, CHUNK_SHIFT) <= jnp.right_shift(qpos, CHUNK_SHIFT)
        for h in range(DA_HEADS):
            bias_sc[h] = _bias_from_buckets(bkt, tab_ref, h, vis, FAR_BUCKET)

    feat = lax.broadcasted_iota(jnp.int32, (2 * DA_HD, tq), 0)
    for h in range(DA_HEADS):
        qt = qt_ref[h * 2 * DA_HD:(h + 1) * 2 * DA_HD, :]
        zero = jnp.zeros_like(qt)
        q2t_sc[h] = jnp.concatenate([jnp.where(feat < DA_HD, qt, zero), jnp.where(feat >= DA_HD, qt, zero)], axis=1)
    m_sc[...] = jnp.full(m_sc.shape, MASK_VALUE, F32)
    l_sc[...] = jnp.zeros(l_sc.shape, F32)
    acc_sc[...] = jnp.zeros(acc_sc.shape, F32)

    def logits(h, start, nk):
        kj = k_ref[pl.ds(start, nk), h * DA_V:(h + 1) * DA_V]
        return _dot(kj, q2t_sc[h])

    def update(h, start, nk, s, bias):
        vtj = vt_ref[h * DA_V:(h + 1) * DA_V, pl.ds(start, nk)]
        if bias is not None:
            s = s + jnp.concatenate([bias, bias], axis=1)
        m_prev = m_sc[h]
        m_new = jnp.maximum(m_prev, jnp.max(s, axis=0, keepdims=True))
        alpha = jnp.exp2(m_prev - m_new)
        p = jnp.exp2(s - m_new[0:1])
        l_sc[h] = alpha * l_sc[h] + jnp.sum(p, axis=0, keepdims=True)
        acc_sc[h] = alpha[0:1] * acc_sc[h] + _dot(vtj, p.astype(BF16))
        m_sc[h] = m_new

    def key_tile(start, nk, bias_of, after=None):
        nslot = ATTN_LOOKAHEAD + 1
        for h in range(ATTN_LOOKAHEAD):
            s_sc[h % nslot, :nk] = logits(h, start, nk)
        for h in range(DA_HEADS):
            if h + ATTN_LOOKAHEAD < DA_HEADS:
                s_sc[(h + ATTN_LOOKAHEAD) % nslot, :nk] = logits(h + ATTN_LOOKAHEAD, start, nk)
            update(h, start, nk, s_sc[h % nslot, :nk], bias_of(h))
            if after is not None:
                after(h)

    n_far = jnp.maximum(i - 1, 0)

    def far_pair(j, carry):
        key_tile(pl.multiple_of(j * 2 * tq, 2 * tq), 2 * tq, lambda h: None)
        return carry

    lax.fori_loop(0, n_far // 2, far_pair, 0)

    @pl.when(n_far % 2 == 1)
    def _odd_far_tile():
        key_tile(pl.multiple_of((n_far - 1) * tq, tq), tq, lambda h: None)

    @pl.when(i >= 1)
    def _prev_tile():
        key_tile(pl.multiple_of((i - 1) * tq, tq), tq, lambda h: bias_sc[h, :tq, :])

    lam = lam_ref[0]
    g = g_ref[...]

    def finish(h):
        o = acc_sc[h] / l_sc[h][0:1]
        d = o[:, :tq] - lam * o[:, tq:]
        y = d * lax.rsqrt(jnp.mean(d * d, axis=0, keepdims=True) + EPS) * g * (1.0 - lambda_init)
        o_ref[:, h * DA_V:(h + 1) * DA_V] = y.T.astype(BF16)

    key_tile(pl.multiple_of(i * tq, tq), tq, lambda h: bias_sc[h, tq:, :], after=finish)


def _attn_self(qtb, kb, vtb, rel_table, lam, subln_g, batch, seq, lambda_init, tq, cast_weights):
    assert seq % tq == 0 and tq % 128 == 0 and tq >= MAX_DISTANCE and tq % CHUNK == 0
    nq = seq // tq
    rel = jnp.arange(2 * tq, dtype=jnp.int32)[:, None] - (tq + jnp.arange(tq, dtype=jnp.int32))[None, :]
    bkt = _rel_bucket(rel)
    smem = pl.BlockSpec(memory_space=pltpu.SMEM)

    cast_specs = _cast_block_specs(cast_weights, batch * nq, lambda b, i: b * nq + i)
    n_cast = len(cast_weights)
    outs = pl.pallas_call(
        functools.partial(_attn_self_kernel, tq=tq, lambda_init=lambda_init, n_cast=n_cast),
        grid=(batch, nq),
        in_specs=[_resident((DA_HEADS, 128)), smem,
                  pl.BlockSpec((DA_QK_W, tq), lambda b, i: (b, i)),
                  pl.BlockSpec((seq, DA_QK_W), lambda b, i: (b, 0)),
                  pl.BlockSpec((DA_V_W, seq), lambda b, i: (b, 0)),
                  _resident((2 * tq, tq)), _resident((DA_V, 1))] + cast_specs,
        out_specs=[pl.BlockSpec((tq, DA_V_W), lambda b, i: (b * nq + i, 0))] + cast_specs,
        out_shape=[jax.ShapeDtypeStruct((batch * seq, DA_V_W), BF16)]
                  + [jax.ShapeDtypeStruct(w.shape, BF16) for w in cast_weights],
        scratch_shapes=[pltpu.VMEM((DA_HEADS, 2 * tq, tq), F32),
                        pltpu.VMEM((DA_HEADS, 2 * DA_HD, 2 * tq), BF16),
                        pltpu.VMEM((DA_HEADS, 8, 2 * tq), F32),
                        pltpu.VMEM((DA_HEADS, 8, 2 * tq), F32),
                        pltpu.VMEM((DA_HEADS, DA_V, 2 * tq), F32),
                        pltpu.VMEM((ATTN_LOOKAHEAD + 1, 2 * tq, 2 * tq), F32)],
        compiler_params=_cparams("arbitrary", "arbitrary"),
        name="attn_self",
    )(_table_lanes(rel_table), lam, qtb, kb, vtb, bkt, subln_g.reshape(DA_V, 1), *cast_weights)
    return outs[0], outs[1:]


def _attn_cached_kernel(tab_ref, lam_ref, q_ref, ckt_ref, cv_ref, nk_ref, nv_ref, bktc_ref, bktn_ref, g_ref, o_ref,
                        biasc_sc, biasn_sc, *, sq, past, lambda_init):
    @pl.when(pl.program_id(0) == 0)
    def _build_bias():
        for bkt_ref, sc, k0 in ((bktc_ref, biasc_sc, 0), (bktn_ref, biasn_sc, past)):
            bkt = bkt_ref[...]
            qpos = past + lax.broadcasted_iota(jnp.int32, bkt.shape, 0)
            kpos = k0 + lax.broadcasted_iota(jnp.int32, bkt.shape, 1)
            vis = jnp.right_shift(kpos, CHUNK_SHIFT) <= jnp.right_shift(qpos, CHUNK_SHIFT)
            for h in range(DA_HEADS):
                sc[h] = _bias_from_buckets(bkt, tab_ref, h, vis, None)[:, :sc.shape[2]]

    q = q_ref[...]
    lam = lam_ref[0]
    g = g_ref[...]
    for h in range(DA_HEADS):
        cols = slice(h * DA_V, (h + 1) * DA_V)
        q2 = _split_maps(q[:, cols])
        s_c = _dot(q2, ckt_ref[cols, :].astype(BF16))
        s_c = (s_c.reshape(2, sq, past) + biasc_sc[h][None]).reshape(2 * sq, past)
        s_n = lax.dot_general(q2, nk_ref[:, cols].astype(BF16), NT_DIMS, preferred_element_type=F32)
        s_n = (s_n.reshape(2, sq, sq) + biasn_sc[h][None]).reshape(2 * sq, sq)
        m = jnp.maximum(jnp.max(s_c, axis=1, keepdims=True), jnp.max(s_n, axis=1, keepdims=True))
        p_c = jnp.exp2(s_c - m)
        p_n = jnp.exp2(s_n - m)
        l = jnp.sum(p_c, axis=1, keepdims=True) + jnp.sum(p_n, axis=1, keepdims=True)
        v_c = cv_ref[pl.ds(h, past, stride=DA_HEADS), :]
        o = (_dot(p_c.astype(BF16), v_c.astype(BF16))
             + _dot(p_n.astype(BF16), nv_ref[:, cols].astype(BF16))) / l
        o_ref[:, cols] = _diff_finish(o, lam, g, sq, lambda_init).astype(BF16)


def _attn_cached(q, cache_kt2d, cache_v2d, nk, nv, rel_table, lam, subln_g, batch, sq, past, lambda_init):
    qpos = past + jnp.arange(sq, dtype=jnp.int32)
    bkt_c = _rel_bucket(jnp.arange(past, dtype=jnp.int32)[None, :] - qpos[:, None])
    assert sq <= 128 and past % 128 == 0
    bkt_n = jnp.pad(_rel_bucket(qpos[None, :] - qpos[:, None]), ((0, 0), (0, 128 - sq)))
    smem = pl.BlockSpec(memory_space=pltpu.SMEM)
    new_spec = pl.BlockSpec((sq, DA_QK_W), lambda b: (b, 0))
    return pl.pallas_call(
        functools.partial(_attn_cached_kernel, sq=sq, past=past, lambda_init=lambda_init),
        grid=(batch,),
        in_specs=[_resident((DA_HEADS, 128)), smem, new_spec,
                  pl.BlockSpec((DA_QK_W, past), lambda b: (b, 0)),
                  pl.BlockSpec((past * DA_HEADS, DA_V), lambda b: (b, 0)),
                  new_spec, new_spec,
                  _resident((sq, past)), _resident((sq, 128)), _resident((1, DA_V))],
        out_specs=new_spec,
        out_shape=jax.ShapeDtypeStruct((batch * sq, DA_V_W), BF16),
        scratch_shapes=[pltpu.VMEM((DA_HEADS, sq, past), F32), pltpu.VMEM((DA_HEADS, sq, sq), F32)],
        compiler_params=_cparams("arbitrary"),
        name="attn_cached",
    )(_table_lanes(rel_table), lam, q, cache_kt2d, cache_v2d, nk, nv, bkt_c, bkt_n, subln_g)


def _mix_ffn_kernel(x_ref, a_ref, b_ref, qc_ref, mk_ref, mv_ref, gmix_ref, wg_ref, bg_ref, wpa_ref, wpb_ref, wpc_ref,
                    wo_ref, gffn_ref, wfi_ref, wfo_ref, gfin_ref, y_ref, *, seg):
    tm = x_ref.shape[0]
    nseg = tm // seg
    n_mem = mk_ref.shape[0] // nseg

    def xattn_head(hh):
        cols = slice(hh * XA_HD, (hh + 1) * XA_HD)
        outs = []
        for s in range(nseg):
            mem = slice(s * n_mem, (s + 1) * n_mem)
            logit = lax.dot_general(qc_ref[s * seg:(s + 1) * seg, cols], mk_ref[mem, cols].astype(BF16), NT_DIMS,
                                    preferred_element_type=F32)
            p = jnp.exp(logit - jnp.max(logit, axis=1, keepdims=True))
            l = jnp.sum(p, axis=1, keepdims=True)
            outs.append((_dot(p.astype(BF16), mv_ref[mem, cols].astype(BF16)) / l).astype(BF16))
        return outs[0] if nseg == 1 else jnp.concatenate(outs, axis=0)

    x = x_ref[...]
    h = _rms(x, gmix_ref[...]).astype(BF16)

    def branch(n, br, wp_ref):
        cols = slice(n * D_MODEL, (n + 1) * D_MODEL)
        gate = jax.nn.sigmoid(_dot(h, wg_ref[:, cols]) + bg_ref[:, cols])
        return gate * _dot(br, wp_ref[...])

    merged = branch(0, a_ref[...], wpa_ref)
    c_heads = [xattn_head(hh) for hh in range(XA_HEADS // 2)]
    merged = merged + branch(1, b_ref[...], wpb_ref)
    c_heads += [xattn_head(hh) for hh in range(XA_HEADS // 2, XA_HEADS)]
    merged = merged + branch(2, jnp.concatenate(c_heads, axis=1), wpc_ref)
    x1 = x + _dot(merged.astype(BF16), wo_ref[...])
    h2 = _rms(x1, gffn_ref[...]).astype(BF16)
    ff = None
    for c in range(D_FF // FFN_CHUNK):
        gcols = slice(c * FFN_CHUNK, (c + 1) * FFN_CHUNK)
        ucols = slice(D_FF + c * FFN_CHUNK, D_FF + (c + 1) * FFN_CHUNK)
        act = (jax.nn.silu(_dot(h2, wfi_ref[:, gcols])) * _dot(h2, wfi_ref[:, ucols])).astype(BF16)
        term = _dot(act, wfo_ref[gcols, :])
        ff = term if ff is None else ff + term
    y_ref[...] = _rms(x1 + ff, gfin_ref[...])


def _mix_ffn(x2d, a_out, b_out, qc, mk2d, mv2d, p, tm, seq):
    t = x2d.shape[0]
    seg = min(tm, seq)
    nseg = tm // seg
    nt = seq // seg
    n_mem = mk2d.shape[0] // (t // seq)
    assert tm % seg == 0 and seq % seg == 0
    row = pl.BlockSpec((tm, D_MODEL), lambda i: (i, 0))
    mem = pl.BlockSpec((nseg * n_mem, XA_W), lambda i: (i // nt, 0))
    vec = _resident((1, D_MODEL))
    sq_w = _resident((D_MODEL, D_MODEL))
    return pl.pallas_call(
        functools.partial(_mix_ffn_kernel, seg=seg),
        grid=(t // tm,),
        in_specs=[row, row, row, row, mem, mem, vec, _resident((D_MODEL, N_BRANCH * D_MODEL)),
                  _resident((1, N_BRANCH * D_MODEL)), sq_w, sq_w, sq_w, sq_w, vec,
                  _resident((D_MODEL, 2 * D_FF)), _resident((D_FF, D_MODEL)), vec],
        out_specs=row,
        out_shape=jax.ShapeDtypeStruct((t, D_MODEL), F32),
        compiler_params=_cparams("parallel"),
        name="mix_ffn",
    )(x2d, a_out, b_out, qc, mk2d, mv2d, p["norm_mix"], p["w_gate"], p["b_gate"], p["w_proj_a"], p["w_proj_b"],
      p["w_proj_c"], p["w_out"], p["norm_ffn"], p["w_ffn_in"], p["w_ffn_out"], p["norm_final"])


def _layer(x, mk2d, mv2d, cache_kt2d, cache_v2d, conv_state, h0, rel_table, lam, p, lambda_init, tm, tq, tm_out):
    batch, seq, _ = x.shape
    x2d = x.reshape(batch * seq, D_MODEL)
    no_history = cache_kt2d is None
    q, k, v, kb, vb, qc, b_out, new_conv, h_last = _in_proj(x2d, conv_state, h0, p, tm, seq, no_history)
    if no_history:
        a_out, out_w = _attn_self(q, kb, vb, rel_table, lam, p["subln_g"], batch, seq, lambda_init, tq,
                                  [p[n] for n in OUT_WEIGHTS])
        p = {**p, **dict(zip(OUT_WEIGHTS, out_w))}
    else:
        past = cache_kt2d.shape[1]
        a_out = _attn_cached(q, cache_kt2d, cache_v2d, k, v, rel_table, lam, p["subln_g"], batch, seq, past,
                             lambda_init)
    y = _mix_ffn(x2d, a_out, b_out, qc, mk2d, mv2d, p, tm_out, seq)
    return y.reshape(batch, seq, D_MODEL), k, v, new_conv, h_last.reshape(batch, LRU_W), p


def kernel(x_prompt, x_sample, mem_prompt, cache_k, cache_v, state_conv, state_lru, cache_mem_k, cache_mem_v,
           rel_table, norm_mix, w_in, lambda_q1, lambda_k1, lambda_q2, lambda_k2, subln_g, conv_w, conv_b,
           w_rg_a, b_rg_a, w_rg_x, b_rg_x, rg_lambda, norm_mem, w_mem_kv, w_proj_a, w_proj_b, w_proj_c,
           w_gate, b_gate, w_out, norm_ffn, w_ffn_in, w_ffn_out, norm_final):
    depth = w_in.shape[0]
    assert depth == 1, "the final norm is fused into the layer's last kernel"
    l = 0
    lambda_init = 0.8 - 0.6 * math.exp(-0.3 * l)
    bp, sp_, _ = x_prompt.shape
    bs, ss, _ = x_sample.shape
    past = cache_k.shape[2]
    n_mem = mem_prompt.shape[1]
    row = lambda a: a.reshape(1, -1).astype(F32)
    mk, mv, mkb, mvb = _mem_kv(mem_prompt.reshape(bp * n_mem, D_MODEL), row(norm_mem[l]),
                               w_mem_kv[l].astype(BF16), 256)
    p = dict(norm_mix=row(norm_mix[l]), w_in=w_in[l].astype(BF16), subln_g=row(subln_g[l]),
             conv_w=conv_w[l], conv_b=row(conv_b[l]),
             w_rg=jnp.concatenate([w_rg_a[l], w_rg_x[l]], axis=-1).astype(BF16),
             b_rg_a=row(b_rg_a[l]), b_rg_x=row(b_rg_x[l]), rg_lambda=row(rg_lambda[l]),
             w_proj_a=w_proj_a[l], w_proj_b=w_proj_b[l], w_proj_c=w_proj_c[l], w_gate=w_gate[l],
             b_gate=row(b_gate[l]), w_out=w_out[l], norm_ffn=row(norm_ffn[l]), w_ffn_in=w_ffn_in[l],
             w_ffn_out=w_ffn_out[l], norm_final=row(norm_final))
    lam = (jnp.exp(jnp.sum(lambda_q1[l] * lambda_k1[l]).astype(F32))
           - jnp.exp(jnp.sum(lambda_q2[l] * lambda_k2[l]).astype(F32)) + lambda_init).reshape(1)

    zeros_conv = jnp.zeros((bp, CONV_W - 1, LRU_W), F32)
    zeros_h = jnp.zeros((bp, 1, LRU_W), F32)
    yp, kp, vp, cp, hp, p = _layer(x_prompt, mkb, mvb, None, None, zeros_conv, zeros_h, rel_table, lam, p,
                                   lambda_init, tm=512, tq=256, tm_out=512)
    cache_kt = jnp.transpose(cache_k[l], (0, 2, 3, 4, 1)).reshape(bs * DA_QK_W, past)
    ys, ks, vs, cs, hs, _ = _layer(x_sample, cache_mem_k[l].reshape(bs * n_mem, XA_W),
                                   cache_mem_v[l].reshape(bs * n_mem, XA_W),
                                   cache_kt, cache_v[l].reshape(bs * past * DA_HEADS, DA_V),
                                   state_conv[l], state_lru[l].reshape(bs, 1, LRU_W), rel_table, lam, p,
                                   lambda_init, tm=256, tq=ss, tm_out=128)
    kp = jnp.transpose(kp.reshape(bp, DA_HEADS, 2, DA_HD, sp_), (0, 4, 1, 2, 3))
    return (yp, ys,
            kp[None], vp.reshape(1, bp, sp_, DA_HEADS, DA_V),
            cp[None], hp[None],
            mk.reshape(1, bp, n_mem, XA_HEADS, XA_HD), mv.reshape(1, bp, n_mem, XA_HEADS, XA_HD),
            ks.reshape(1, bs, ss, DA_HEADS, 2, DA_HD), vs.reshape(1, bs, ss, DA_HEADS, DA_V),
            cs[None], hs[None])
```

```python
import functools
import math

import jax
import jax.numpy as jnp
from jax import lax
from jax.experimental import pallas as pl
from jax.experimental.pallas import tpu as pltpu

F32 = jnp.float32
BF16 = jnp.bfloat16

D_MODEL = 1024
CHUNK = 64
CHUNK_SHIFT = 6
DA_HEADS = 8
DA_HD = 64
DA_V = 2 * DA_HD
LRU_W = D_MODEL
LRU_BLOCKS = 8
LRU_BW = LRU_W // LRU_BLOCKS
CONV_W = 4
LRU_C = 8.0
XA_HEADS = 4
XA_HD = 256
N_BUCKETS = 32
MAX_DISTANCE = 128
D_FF = ((8 * D_MODEL + 3 * 256 - 1) // (3 * 256)) * 256
N_BRANCH = 3
EPS = 1e-6
DA_QK_W = DA_HEADS * 2 * DA_HD
DA_V_W = DA_HEADS * DA_V
XA_W = XA_HEADS * XA_HD
IN_OFFS = (0, DA_QK_W, 2 * DA_QK_W, 2 * DA_QK_W + DA_V_W, 2 * DA_QK_W + DA_V_W + LRU_W,
           2 * DA_QK_W + DA_V_W + 2 * LRU_W)
IN_W = 2 * DA_QK_W + DA_V_W + 2 * LRU_W + XA_W
LOG2E = math.log2(math.e)
DA_Q_SCALE = DA_HD ** -0.5 * LOG2E
XA_SCALE = XA_HD ** -0.5
MASK_VALUE = -1e30
FAR_BUCKET = N_BUCKETS // 2 - 1
FFN_CHUNK = D_FF // 2
PROJ_CHUNK = 256
OUT_WEIGHTS = ("w_gate", "w_proj_a", "w_proj_b", "w_proj_c", "w_out", "w_ffn_in", "w_ffn_out")
BF16_SUBLANES = 16
ATTN_LOOKAHEAD = 3

V7X_VMEM_BYTES = 64 * 1024 * 1024
VMEM_LIMIT = V7X_VMEM_BYTES - 8 * 1024 * 1024

NT_DIMS = (((1,), (1,)), ((), ()))


def _cparams(*sem):
    return pltpu.CompilerParams(dimension_semantics=sem, vmem_limit_bytes=VMEM_LIMIT)


def _resident(shape):
    nd = len(shape)
    return pl.BlockSpec(shape, lambda *_: (0,) * nd, pipeline_mode=pl.Buffered(1))


def _rms(x, g):
    return x * lax.rsqrt(jnp.mean(x * x, axis=-1, keepdims=True) + EPS) * g


def _dot(a, b):
    return jnp.dot(a, b, preferred_element_type=F32)


def _in_proj_kernel(x_ref, g_ref, w_ref, cs_ref, h0_ref, cw_ref, cb_ref, wrg_ref, ba_ref, bx_ref, lam_ref,
                    q_ref, k_ref, v_ref, kb_ref, vb_ref, qc_ref, bo_ref, nc_ref, hl_ref,
                    xpad_sc, a_sc, u_sc, hs_sc, gg_sc, hc_sc, *, nt, seg, feature_major):
    tm = x_ref.shape[0]
    nseg = tm // seg
    halo = CONV_W - 1
    base = 8 - halo

    @pl.when(pl.program_id(0) % nt == 0)
    def _load_state():
        for s in range(nseg):
            xpad_sc[s, base:8, :] = cs_ref[s]
        hc_sc[...] = h0_ref[...]

    h = _rms(x_ref[...], g_ref[...]).astype(BF16)

    def proj(n, c0, width):
        return _dot(h, w_ref[:, IN_OFFS[n] + c0:IN_OFFS[n] + c0 + width])

    xb = proj(3, 0, LRU_W)
    for s in range(nseg):
        xpad_sc[s, 8:8 + seg, :] = xb[s * seg:(s + 1) * seg]

    def proj_chunk(kind, c0):
        cols = slice(c0, c0 + PROJ_CHUNK)
        if kind == "g":
            gg_sc[:, cols] = jax.nn.gelu(proj(4, c0, PROJ_CHUNK))
        elif kind == "q":
            qs = proj(0, c0, PROJ_CHUNK) * DA_Q_SCALE
            if feature_major:
                q_ref[cols, :] = qs.T.astype(BF16)
            else:
                q_ref[:, cols] = qs.astype(BF16)
        elif kind == "k":
            kc = proj(1, c0, PROJ_CHUNK)
            kb_ref[:, cols] = kc.astype(BF16)
            if feature_major:
                k_ref[cols, :] = kc.T
            else:
                k_ref[:, cols] = kc
        elif kind == "v":
            vc = proj(2, c0, PROJ_CHUNK)
            v_ref[:, cols] = vc
            if feature_major:
                vb_ref[cols, :] = vc.T.astype(BF16)
            else:
                vb_ref[:, cols] = vc.astype(BF16)
        else:
            qc_ref[:, cols] = (proj(5, c0, PROJ_CHUNK) * XA_SCALE).astype(BF16)

    chunks = [(kind, c0) for kind, width in (("g", LRU_W), ("q", DA_QK_W), ("k", DA_QK_W), ("v", DA_V_W), ("c", XA_W))
              for c0 in range(0, width, PROJ_CHUNK)]

    cw = cw_ref[...]
    sp = jax.nn.softplus(-lam_ref[...])
    for n in range(LRU_BLOCKS):
        for kind, c0 in chunks[n * len(chunks) // LRU_BLOCKS:(n + 1) * len(chunks) // LRU_BLOCKS]:
            proj_chunk(kind, c0)
        cols = slice(n * LRU_BW, (n + 1) * LRU_BW)
        xc_segs = []
        for s in range(nseg):
            xc = xpad_sc[s, base:base + seg, cols] * cw[0:1, cols]
            for j in range(1, CONV_W):
                xc = xc + xpad_sc[s, base + j:base + j + seg, cols] * cw[j:j + 1, cols]
            xc_segs.append(cb_ref[:, cols] + xc)
        xc = xc_segs[0] if nseg == 1 else jnp.concatenate(xc_segs, axis=0)
        xcb = xc.astype(BF16)
        pre = _dot(xcb, wrg_ref[n])
        tanh_r = jnp.tanh(0.5 * (pre[:, :LRU_BW] + ba_ref[:, cols]))
        tanh_i = jnp.tanh(0.5 * (pre[:, LRU_BW:] + bx_ref[:, cols]))
        half_c_sp = (-0.5 * LRU_C) * sp[:, cols]
        log_a = half_c_sp * tanh_r + half_c_sp
        a = jnp.exp(log_a)
        a_sc[:, cols] = a
        one_minus_a2 = -jnp.tanh(log_a) * (a * a + 1.0)
        mult = jnp.where(one_minus_a2 > 0.0, one_minus_a2 * lax.rsqrt(one_minus_a2), 0.0)
        u_sc[:, cols] = mult * ((0.5 * tanh_i + 0.5) * xc)
    for s in range(nseg):
        tail = xpad_sc[s, base + seg:8 + seg, :]
        nc_ref[s] = tail
        xpad_sc[s, base:8, :] = tail

    for s in range(nseg):
        def step(t, hprev, row0=s * seg):
            hnew = a_sc[pl.ds(row0 + t, 1), :] * hprev + u_sc[pl.ds(row0 + t, 1), :]
            hs_sc[pl.ds(row0 + t, 1), :] = hnew
            return hnew

        hlast = lax.fori_loop(0, seg, step, hc_sc[s], unroll=8)
        hc_sc[s] = hlast
        hl_ref[s] = hlast
    bo_ref[...] = (hs_sc[...] * gg_sc[...]).astype(BF16)


def _in_proj(x2d, conv_state, h0, p, tm, seq, feature_major):
    t = x2d.shape[0]
    seg = min(tm, seq)
    nseg = tm // seg
    nt = seq // seg
    assert tm % seg == 0 and seq % seg == 0 and seg % 8 == 0 and seg >= CONV_W - 1
    batch_block = lambda i: (i // nt, 0, 0)
    row = lambda width: pl.BlockSpec((tm, width), lambda i: (i, 0))
    sds = lambda width, dt: jax.ShapeDtypeStruct((t, width), dt)
    if feature_major:
        assert nseg == 1
        fm_spec = lambda width: pl.BlockSpec((width, tm), lambda i: (i // nt, i % nt))
        fm_sds = lambda width, dt: jax.ShapeDtypeStruct((t // seq * width, seq), dt)
    else:
        fm_spec, fm_sds = row, sds
    vec = _resident((1, LRU_W))
    gate_w = _resident((LRU_BLOCKS, LRU_BW, 2 * LRU_BW))
    conv_spec = pl.BlockSpec((nseg, CONV_W - 1, LRU_W), batch_block)
    state_spec = pl.BlockSpec((nseg, 1, LRU_W), batch_block)
    return pl.pallas_call(
        functools.partial(_in_proj_kernel, nt=nt, seg=seg, feature_major=feature_major),
        grid=(t // tm,),
        in_specs=[row(D_MODEL), _resident((1, D_MODEL)), _resident((D_MODEL, IN_W)), conv_spec, state_spec,
                  _resident((CONV_W, LRU_W)), vec, gate_w, vec, vec, vec],
        out_specs=[fm_spec(DA_QK_W), fm_spec(DA_QK_W), row(DA_V_W), row(DA_QK_W), fm_spec(DA_V_W), row(XA_W),
                   row(LRU_W), conv_spec, state_spec],
        out_shape=[fm_sds(DA_QK_W, BF16), fm_sds(DA_QK_W, F32), sds(DA_V_W, F32), sds(DA_QK_W, BF16),
                   fm_sds(DA_V_W, BF16), sds(XA_W, BF16), sds(LRU_W, BF16),
                   jax.ShapeDtypeStruct(conv_state.shape, F32), jax.ShapeDtypeStruct(h0.shape, F32)],
        scratch_shapes=[pltpu.VMEM((nseg, 8 + seg, LRU_W), F32), pltpu.VMEM((tm, LRU_W), F32),
                        pltpu.VMEM((tm, LRU_W), F32), pltpu.VMEM((tm, LRU_W), F32), pltpu.VMEM((tm, LRU_W), F32),
                        pltpu.VMEM((nseg, 1, LRU_W), F32)],
        compiler_params=_cparams("arbitrary"),
        name="in_proj",
    )(x2d, p["norm_mix"], p["w_in"], conv_state, h0, p["conv_w"], p["conv_b"], p["w_rg"], p["b_rg_a"],
      p["b_rg_x"], p["rg_lambda"])


def _cast_block_specs(weights, steps, step_index):
    specs = []
    for w in weights:
        rows, cols = w.shape
        nblk = max(d for d in range(1, steps + 1)
                   if steps % d == 0 and rows % d == 0 and (rows // d) % BF16_SUBLANES == 0)
        specs.append(pl.BlockSpec((rows // nblk, cols), lambda *g, per=steps // nblk: (step_index(*g) // per, 0)))
    return specs


def _mem_kv_kernel(m_ref, g_ref, w_ref, mk_ref, mv_ref, mkb_ref, mvb_ref):
    h = _rms(m_ref[...], g_ref[...]).astype(BF16)
    mk = _dot(h, w_ref[:, :XA_W])
    mv = _dot(h, w_ref[:, XA_W:])
    mk_ref[...] = mk
    mv_ref[...] = mv
    mkb_ref[...] = mk.astype(BF16)
    mvb_ref[...] = mv.astype(BF16)


def _mem_kv(mem2d, g, w_bf16, tm):
    t = mem2d.shape[0]
    row = pl.BlockSpec((tm, XA_W), lambda i: (i, 0))
    return pl.pallas_call(
        _mem_kv_kernel,
        grid=(t // tm,),
        in_specs=[pl.BlockSpec((tm, D_MODEL), lambda i: (i, 0)), _resident((1, D_MODEL)),
                  _resident((D_MODEL, 2 * XA_W))],
        out_specs=[row, row, row, row],
        out_shape=[jax.ShapeDtypeStruct((t, XA_W), F32), jax.ShapeDtypeStruct((t, XA_W), F32),
                   jax.ShapeDtypeStruct((t, XA_W), BF16), jax.ShapeDtypeStruct((t, XA_W), BF16)],
        compiler_params=_cparams("parallel"),
        name="mem_kv",
    )(mem2d, g, w_bf16)


def _rel_bucket(rel):
    half = N_BUCKETS // 2
    max_exact = half // 2
    n = jnp.abs(rel)
    nf = jnp.maximum(n, 1).astype(F32)
    large = max_exact + (jnp.log(nf / max_exact) / math.log(MAX_DISTANCE / max_exact)
                         * (half - max_exact)).astype(jnp.int32)
    large = jnp.minimum(large, half - 1)
    return jnp.where(rel > 0, half, 0) + jnp.where(n < max_exact, n, large)


def _table_lanes(rel_table):
    return jnp.pad(rel_table.T, ((0, 0), (0, 128 - N_BUCKETS)))


def _bias_from_buckets(bkt, tabt_ref, head, vis, shift_bucket):
    rows, width = bkt.shape
    table = jnp.broadcast_to(tabt_ref[head:head + 1, :], (rows, 128))
    bias = jnp.concatenate([jnp.take_along_axis(table, bkt[:, c:c + 128], axis=1) for c in range(0, width, 128)],
                           axis=1)
    if shift_bucket is not None:
        bias = bias - tabt_ref[head:head + 1, shift_bucket:shift_bucket + 1]
    return jnp.where(vis, bias * LOG2E, MASK_VALUE)


def _split_maps(qh):
    lane = lax.broadcasted_iota(jnp.int32, qh.shape, 1)
    zero = jnp.zeros_like(qh)
    return jnp.concatenate([jnp.where(lane < DA_HD, qh, zero), jnp.where(lane >= DA_HD, qh, zero)], axis=0)


def _ones_column(n):
    return jnp.where(lax.broadcasted_iota(jnp.int32, (n, DA_V), 1) == 0, 1.0, 0.0).astype(BF16)


def _diff_finish(o, lam, g, tq, lambda_init):
    d = o[:tq] - lam * o[tq:]
    return _rms(d, g) * (1.0 - lambda_init)


def _attn_self_kernel(tab_ref, lam_ref, qt_ref, k_ref, vt_ref, bkt_ref, g_ref, *rest, tq, lambda_init, n_cast):
    w_refs, o_ref, wb_refs = rest[:n_cast], rest[n_cast], rest[n_cast + 1:2 * n_cast + 1]
    bias_sc, q2t_sc, m_sc, l_sc, acc_sc, s_sc = rest[2 * n_cast + 1:]
    for w_ref, wb_ref in zip(w_refs, wb_refs):
        wb_ref[...] = w_ref[...].astype(BF16)

    b = pl.program_id(0)
    i = pl.program_id(1)

    @pl.when((b == 0) & (i == 0))
    def _build_bias():
        bkt = bkt_ref[...]
        kpos = lax.broadcasted_iota(jnp.int32, bkt.shape, 0)
        qpos = lax.broadcasted_iota(jnp.int32, bkt.shape, 1)
        vis = jnp.right_shift(kpos - tq, CHUNK_SHIFT) <= jnp.right_shift(qpos, CHUNK_SHIFT)
        for h in range(DA_HEADS):
            bias_sc[h] = _bias_from_buckets(bkt, tab_ref, h, vis, FAR_BUCKET)

    feat = lax.broadcasted_iota(jnp.int32, (2 * DA_HD, tq), 0)
    for h in range(DA_HEADS):
        qt = qt_ref[h * 2 * DA_HD:(h + 1) * 2 * DA_HD, :]
        zero = jnp.zeros_like(qt)
        q2t_sc[h] = jnp.concatenate([jnp.where(feat < DA_HD, qt, zero), jnp.where(feat >= DA_HD, qt, zero)], axis=1)
    m_sc[...] = jnp.full(m_sc.shape, MASK_VALUE, F32)
    l_sc[...] = jnp.zeros(l_sc.shape, F32)
    acc_sc[...] = jnp.zeros(acc_sc.shape, F32)

    def logits(h, start, nk):
        kj = k_ref[pl.ds(start, nk), h * DA_V:(h + 1) * DA_V]
        return _dot(kj, q2t_sc[h])

    def update(h, start, nk, s, bias):
        vtj = vt_ref[h * DA_V:(h + 1) * DA_V, pl.ds(start, nk)]
        if bias is not None:
            s = s + jnp.concatenate([bias, bias], axis=1)
        m_prev = m_sc[h]
        m_new = jnp.maximum(m_prev, jnp.max(s, axis=0, keepdims=True))
        alpha = jnp.exp2(m_prev - m_new)
        p = jnp.exp2(s - m_new[0:1])
        l_sc[h] = alpha * l_sc[h] + jnp.sum(p, axis=0, keepdims=True)
        acc_sc[h] = alpha[0:1] * acc_sc[h] + _dot(vtj, p.astype(BF16))
        m_sc[h] = m_new

    def key_tile(start, nk, bias_of, after=None):
        nslot = ATTN_LOOKAHEAD + 1
        for h in range(ATTN_LOOKAHEAD):
            s_sc[h % nslot, :nk] = logits(h, start, nk)
        for h in range(DA_HEADS):
            if h + ATTN_LOOKAHEAD < DA_HEADS:
                s_sc[(h + ATTN_LOOKAHEAD) % nslot, :nk] = logits(h + ATTN_LOOKAHEAD, start, nk)
            update(h, start, nk, s_sc[h % nslot, :nk], bias_of(h))
            if after is not None:
                after(h)

    n_far = jnp.maximum(i - 1, 0)

    def far_pair(j, carry):
        key_tile(pl.multiple_of(j * 2 * tq, 2 * tq), 2 * tq, lambda h: None)
        return carry

    lax.fori_loop(0, n_far // 2, far_pair, 0)

    @pl.when(n_far % 2 == 1)
    def _odd_far_tile():
        key_tile(pl.multiple_of((n_far - 1) * tq, tq), tq, lambda h: None)

    lam = lam_ref[0]
    g = g_ref[...]

    def finish(h):
        o = acc_sc[h] / l_sc[h][0:1]
        d = o[:, :tq] - lam * o[:, tq:]
        y = d * lax.rsqrt(jnp.mean(d * d, axis=0, keepdims=True) + EPS) * g * (1.0 - lambda_init)
        o_ref[:, h * DA_V:(h + 1) * DA_V] = y.T.astype(BF16)

    @pl.when(i >= 1)
    def _prev_and_diagonal():
        key_tile(pl.multiple_of((i - 1) * tq, tq), 2 * tq, lambda h: bias_sc[h], after=finish)

    @pl.when(i == 0)
    def _diagonal_only():
        key_tile(0, tq, lambda h: bias_sc[h, tq:, :], after=finish)


def _attn_self(qtb, kb, vtb, rel_table, lam, subln_g, batch, seq, lambda_init, tq, cast_weights):
    assert seq % tq == 0 and tq % 128 == 0 and tq >= MAX_DISTANCE and tq % CHUNK == 0
    nq = seq // tq
    rel = jnp.arange(2 * tq, dtype=jnp.int32)[:, None] - (tq + jnp.arange(tq, dtype=jnp.int32))[None, :]
    bkt = _rel_bucket(rel)
    smem = pl.BlockSpec(memory_space=pltpu.SMEM)

    cast_specs = _cast_block_specs(cast_weights, batch * nq, lambda b, i: b * nq + i)
    n_cast = len(cast_weights)
    outs = pl.pallas_call(
        functools.partial(_attn_self_kernel, tq=tq, lambda_init=lambda_init, n_cast=n_cast),
        grid=(batch, nq),
        in_specs=[_resident((DA_HEADS, 128)), smem,
                  pl.BlockSpec((DA_QK_W, tq), lambda b, i: (b, i)),
                  pl.BlockSpec((seq, DA_QK_W), lambda b, i: (b, 0)),
                  pl.BlockSpec((DA_V_W, seq), lambda b, i: (b, 0)),
                  _resident((2 * tq, tq)), _resident((DA_V, 1))] + cast_specs,
        out_specs=[pl.BlockSpec((tq, DA_V_W), lambda b, i: (b * nq + i, 0))] + cast_specs,
        out_shape=[jax.ShapeDtypeStruct((batch * seq, DA_V_W), BF16)]
                  + [jax.ShapeDtypeStruct(w.shape, BF16) for w in cast_weights],
        scratch_shapes=[pltpu.VMEM((DA_HEADS, 2 * tq, tq), F32),
                        pltpu.VMEM((DA_HEADS, 2 * DA_HD, 2 * tq), BF16),
                        pltpu.VMEM((DA_HEADS, 8, 2 * tq), F32),
                        pltpu.VMEM((DA_HEADS, 8, 2 * tq), F32),
                        pltpu.VMEM((DA_HEADS, DA_V, 2 * tq), F32),
                        pltpu.VMEM((ATTN_LOOKAHEAD + 1, 2 * tq, 2 * tq), F32)],
        compiler_params=_cparams("arbitrary", "arbitrary"),
        name="attn_self",
    )(_table_lanes(rel_table), lam, qtb, kb, vtb, bkt, subln_g.reshape(DA_V, 1), *cast_weights)
    return outs[0], outs[1:]


def _attn_cached_kernel(tab_ref, lam_ref, q_ref, ckt_ref, cv_ref, nk_ref, nv_ref, bktc_ref, bktn_ref, g_ref, o_ref,
                        biasc_sc, biasn_sc, *, sq, past, lambda_init):
    @pl.when(pl.program_id(0) == 0)
    def _build_bias():
        for bkt_ref, sc, k0 in ((bktc_ref, biasc_sc, 0), (bktn_ref, biasn_sc, past)):
            bkt = bkt_ref[...]
            qpos = past + lax.broadcasted_iota(jnp.int32, bkt.shape, 0)
            kpos = k0 + lax.broadcasted_iota(jnp.int32, bkt.shape, 1)
            vis = jnp.right_shift(kpos, CHUNK_SHIFT) <= jnp.right_shift(qpos, CHUNK_SHIFT)
            for h in range(DA_HEADS):
                sc[h] = _bias_from_buckets(bkt, tab_ref, h, vis, None)[:, :sc.shape[2]]

    q = q_ref[...]
    lam = lam_ref[0]
    g = g_ref[...]
    for h in range(DA_HEADS):
        cols = slice(h * DA_V, (h + 1) * DA_V)
        q2 = _split_maps(q[:, cols])
        s_c = _dot(q2, ckt_ref[cols, :].astype(BF16))
        s_c = (s_c.reshape(2, sq, past) + biasc_sc[h][None]).reshape(2 * sq, past)
        s_n = lax.dot_general(q2, nk_ref[:, cols].astype(BF16), NT_DIMS, preferred_element_type=F32)
        s_n = (s_n.reshape(2, sq, sq) + biasn_sc[h][None]).reshape(2 * sq, sq)
        m = jnp.maximum(jnp.max(s_c, axis=1, keepdims=True), jnp.max(s_n, axis=1, keepdims=True))
        p_c = jnp.exp2(s_c - m)
        p_n = jnp.exp2(s_n - m)
        l = jnp.sum(p_c, axis=1, keepdims=True) + jnp.sum(p_n, axis=1, keepdims=True)
        v_c = cv_ref[pl.ds(h, past, stride=DA_HEADS), :]
        o = (_dot(p_c.astype(BF16), v_c.astype(BF16))
             + _dot(p_n.astype(BF16), nv_ref[:, cols].astype(BF16))) / l
        o_ref[:, cols] = _diff_finish(o, lam, g, sq, lambda_init).astype(BF16)


def _attn_cached(q, cache_kt2d, cache_v2d, nk, nv, rel_table, lam, subln_g, batch, sq, past, lambda_init):
    qpos = past + jnp.arange(sq, dtype=jnp.int32)
    bkt_c = _rel_bucket(jnp.arange(past, dtype=jnp.int32)[None, :] - qpos[:, None])
    assert sq <= 128 and past % 128 == 0
    bkt_n = jnp.pad(_rel_bucket(qpos[None, :] - qpos[:, None]), ((0, 0), (0, 128 - sq)))
    smem = pl.BlockSpec(memory_space=pltpu.SMEM)
    new_spec = pl.BlockSpec((sq, DA_QK_W), lambda b: (b, 0))
    return pl.pallas_call(
        functools.partial(_attn_cached_kernel, sq=sq, past=past, lambda_init=lambda_init),
        grid=(batch,),
        in_specs=[_resident((DA_HEADS, 128)), smem, new_spec,
                  pl.BlockSpec((DA_QK_W, past), lambda b: (b, 0)),
                  pl.BlockSpec((past * DA_HEADS, DA_V), lambda b: (b, 0)),
                  new_spec, new_spec,
                  _resident((sq, past)), _resident((sq, 128)), _resident((1, DA_V))],
        out_specs=new_spec,
        out_shape=jax.ShapeDtypeStruct((batch * sq, DA_V_W), BF16),
        scratch_shapes=[pltpu.VMEM((DA_HEADS, sq, past), F32), pltpu.VMEM((DA_HEADS, sq, sq), F32)],
        compiler_params=_cparams("arbitrary"),
        name="attn_cached",
    )(_table_lanes(rel_table), lam, q, cache_kt2d, cache_v2d, nk, nv, bkt_c, bkt_n, subln_g)


def _mix_ffn_kernel(x_ref, a_ref, b_ref, qc_ref, mk_ref, mv_ref, gmix_ref, wg_ref, bg_ref, wpa_ref, wpb_ref, wpc_ref,
                    wo_ref, gffn_ref, wfi_ref, wfo_ref, gfin_ref, y_ref, *, seg):
    tm = x_ref.shape[0]
    nseg = tm // seg
    n_mem = mk_ref.shape[0] // nseg

    def xattn_head(hh):
        cols = slice(hh * XA_HD, (hh + 1) * XA_HD)
        outs = []
        for s in range(nseg):
            mem = slice(s * n_mem, (s + 1) * n_mem)
            logit = lax.dot_general(qc_ref[s * seg:(s + 1) * seg, cols], mk_ref[mem, cols].astype(BF16), NT_DIMS,
                                    preferred_element_type=F32)
            p = jnp.exp(logit - jnp.max(logit, axis=1, keepdims=True))
            l = jnp.sum(p, axis=1, keepdims=True)
            outs.append((_dot(p.astype(BF16), mv_ref[mem, cols].astype(BF16)) / l).astype(BF16))
        return outs[0] if nseg == 1 else jnp.concatenate(outs, axis=0)

    x = x_ref[...]
    h = _rms(x, gmix_ref[...]).astype(BF16)

    def branch(n, br, wp_ref):
        cols = slice(n * D_MODEL, (n + 1) * D_MODEL)
        gate = jax.nn.sigmoid(_dot(h, wg_ref[:, cols]) + bg_ref[:, cols])
        return gate * _dot(br, wp_ref[...])

    merged = branch(0, a_ref[...], wpa_ref)
    c_heads = [xattn_head(hh) for hh in range(XA_HEADS // 2)]
    merged = merged + branch(1, b_ref[...], wpb_ref)
    c_heads += [xattn_head(hh) for hh in range(XA_HEADS // 2, XA_HEADS)]
    merged = merged + branch(2, jnp.concatenate(c_heads, axis=1), wpc_ref)
    x1 = x + _dot(merged.astype(BF16), wo_ref[...])
    h2 = _rms(x1, gffn_ref[...]).astype(BF16)
    ff = None
    for c in range(D_FF // FFN_CHUNK):
        gcols = slice(c * FFN_CHUNK, (c + 1) * FFN_CHUNK)
        ucols = slice(D_FF + c * FFN_CHUNK, D_FF + (c + 1) * FFN_CHUNK)
        act = (jax.nn.silu(_dot(h2, wfi_ref[:, gcols])) * _dot(h2, wfi_ref[:, ucols])).astype(BF16)
        term = _dot(act, wfo_ref[gcols, :])
        ff = term if ff is None else ff + term
    y_ref[...] = _rms(x1 + ff, gfin_ref[...])


def _mix_ffn(x2d, a_out, b_out, qc, mk2d, mv2d, p, tm, seq):
    t = x2d.shape[0]
    seg = min(tm, seq)
    nseg = tm // seg
    nt = seq // seg
    n_mem = mk2d.shape[0] // (t // seq)
    assert tm % seg == 0 and seq % seg == 0
    row = pl.BlockSpec((tm, D_MODEL), lambda i: (i, 0))
    mem = pl.BlockSpec((nseg * n_mem, XA_W), lambda i: (i // nt, 0))
    vec = _resident((1, D_MODEL))
    sq_w = _resident((D_MODEL, D_MODEL))
    return pl.pallas_call(
        functools.partial(_mix_ffn_kernel, seg=seg),
        grid=(t // tm,),
        in_specs=[row, row, row, row, mem, mem, vec, _resident((D_MODEL, N_BRANCH * D_MODEL)),
                  _resident((1, N_BRANCH * D_MODEL)), sq_w, sq_w, sq_w, sq_w, vec,
                  _resident((D_MODEL, 2 * D_FF)), _resident((D_FF, D_MODEL)), vec],
        out_specs=row,
        out_shape=jax.ShapeDtypeStruct((t, D_MODEL), F32),
        compiler_params=_cparams("parallel"),
        name="mix_ffn",
    )(x2d, a_out, b_out, qc, mk2d, mv2d, p["norm_mix"], p["w_gate"], p["b_gate"], p["w_proj_a"], p["w_proj_b"],
      p["w_proj_c"], p["w_out"], p["norm_ffn"], p["w_ffn_in"], p["w_ffn_out"], p["norm_final"])


def _layer(x, mk2d, mv2d, cache_kt2d, cache_v2d, conv_state, h0, rel_table, lam, p, lambda_init, tm, tq, tm_out):
    batch, seq, _ = x.shape
    x2d = x.reshape(batch * seq, D_MODEL)
    no_history = cache_kt2d is None
    q, k, v, kb, vb, qc, b_out, new_conv, h_last = _in_proj(x2d, conv_state, h0, p, tm, seq, no_history)
    if no_history:
        a_out, out_w = _attn_self(q, kb, vb, rel_table, lam, p["subln_g"], batch, seq, lambda_init, tq,
                                  [p[n] for n in OUT_WEIGHTS])
        p = {**p, **dict(zip(OUT_WEIGHTS, out_w))}
    else:
        past = cache_kt2d.shape[1]
        a_out = _attn_cached(q, cache_kt2d, cache_v2d, k, v, rel_table, lam, p["subln_g"], batch, seq, past,
                             lambda_init)
    y = _mix_ffn(x2d, a_out, b_out, qc, mk2d, mv2d, p, tm_out, seq)
    return y.reshape(batch, seq, D_MODEL), k, v, new_conv, h_last.reshape(batch, LRU_W), p


def kernel(x_prompt, x_sample, mem_prompt, cache_k, cache_v, state_conv, state_lru, cache_mem_k, cache_mem_v,
           rel_table, norm_mix, w_in, lambda_q1, lambda_k1, lambda_q2, lambda_k2, subln_g, conv_w, conv_b,
           w_rg_a, b_rg_a, w_rg_x, b_rg_x, rg_lambda, norm_mem, w_mem_kv, w_proj_a, w_proj_b, w_proj_c,
           w_gate, b_gate, w_out, norm_ffn, w_ffn_in, w_ffn_out, norm_final):
    depth = w_in.shape[0]
    assert depth == 1, "the final norm is fused into the layer's last kernel"
    l = 0
    lambda_init = 0.8 - 0.6 * math.exp(-0.3 * l)
    bp, sp_, _ = x_prompt.shape
    bs, ss, _ = x_sample.shape
    past = cache_k.shape[2]
    n_mem = mem_prompt.shape[1]
    row = lambda a: a.reshape(1, -1).astype(F32)
    mk, mv, mkb, mvb = _mem_kv(mem_prompt.reshape(bp * n_mem, D_MODEL), row(norm_mem[l]),
                               w_mem_kv[l].astype(BF16), 256)
    p = dict(norm_mix=row(norm_mix[l]), w_in=w_in[l].astype(BF16), subln_g=row(subln_g[l]),
             conv_w=conv_w[l], conv_b=row(conv_b[l]),
             w_rg=jnp.concatenate([w_rg_a[l], w_rg_x[l]], axis=-1).astype(BF16),
             b_rg_a=row(b_rg_a[l]), b_rg_x=row(b_rg_x[l]), rg_lambda=row(rg_lambda[l]),
             w_proj_a=w_proj_a[l], w_proj_b=w_proj_b[l], w_proj_c=w_proj_c[l], w_gate=w_gate[l],
             b_gate=row(b_gate[l]), w_out=w_out[l], norm_ffn=row(norm_ffn[l]), w_ffn_in=w_ffn_in[l],
             w_ffn_out=w_ffn_out[l], norm_final=row(norm_final))
    lam = (jnp.exp(jnp.sum(lambda_q1[l] * lambda_k1[l]).astype(F32))
           - jnp.exp(jnp.sum(lambda_q2[l] * lambda_k2[l]).astype(F32)) + lambda_init).reshape(1)

    zeros_conv = jnp.zeros((bp, CONV_W - 1, LRU_W), F32)
    zeros_h = jnp.zeros((bp, 1, LRU_W), F32)
    yp, kp, vp, cp, hp, p = _layer(x_prompt, mkb, mvb, None, None, zeros_conv, zeros_h, rel_table, lam, p,
                                   lambda_init, tm=512, tq=256, tm_out=512)
    cache_kt = jnp.transpose(cache_k[l], (0, 2, 3, 4, 1)).reshape(bs * DA_QK_W, past)
    ys, ks, vs, cs, hs, _ = _layer(x_sample, cache_mem_k[l].reshape(bs * n_mem, XA_W),
                                   cache_mem_v[l].reshape(bs * n_mem, XA_W),
                                   cache_kt, cache_v[l].reshape(bs * past * DA_HEADS, DA_V),
                                   state_conv[l], state_lru[l].reshape(bs, 1, LRU_W), rel_table, lam, p,
                                   lambda_init, tm=256, tq=ss, tm_out=128)
    kp = jnp.transpose(kp.reshape(bp, DA_HEADS, 2, DA_HD, sp_), (0, 4, 1, 2, 3))
    return (yp, ys,
            kp[None], vp.reshape(1, bp, sp_, DA_HEADS, DA_V),
            cp[None], hp[None],
            mk.reshape(1, bp, n_mem, XA_HEADS, XA_HD), mv.reshape(1, bp, n_mem, XA_HEADS, XA_HD),
            ks.reshape(1, bs, ss, DA_HEADS, 2, DA_HD), vs.reshape(1, bs, ss, DA_HEADS, DA_V),
            cs[None], hs[None])
```

```python
import functools
import math

import jax
import jax.numpy as jnp
from jax import lax
from jax.experimental import pallas as pl
from jax.experimental.pallas import tpu as pltpu

F32 = jnp.float32
BF16 = jnp.bfloat16

D_MODEL = 1024
CHUNK = 64
CHUNK_SHIFT = 6
DA_HEADS = 8
DA_HD = 64
DA_V = 2 * DA_HD
LRU_W = D_MODEL
LRU_BLOCKS = 8
LRU_BW = LRU_W // LRU_BLOCKS
CONV_W = 4
LRU_C = 8.0
XA_HEADS = 4
XA_HD = 256
N_BUCKETS = 32
MAX_DISTANCE = 128
D_FF = ((8 * D_MODEL + 3 * 256 - 1) // (3 * 256)) * 256
N_BRANCH = 3
EPS = 1e-6
DA_QK_W = DA_HEADS * 2 * DA_HD
DA_V_W = DA_HEADS * DA_V
XA_W = XA_HEADS * XA_HD
IN_OFFS = (0, DA_QK_W, 2 * DA_QK_W, 2 * DA_QK_W + DA_V_W, 2 * DA_QK_W + DA_V_W + LRU_W,
           2 * DA_QK_W + DA_V_W + 2 * LRU_W)
IN_W = 2 * DA_QK_W + DA_V_W + 2 * LRU_W + XA_W
LOG2E = math.log2(math.e)
DA_Q_SCALE = DA_HD ** -0.5 * LOG2E
XA_SCALE = XA_HD ** -0.5
MASK_VALUE = -1e30
FAR_BUCKET = N_BUCKETS // 2 - 1
V7X_MXU_DIM = 256
FFN_SPLITS = (0, -(-(D_FF // V7X_MXU_DIM) // 2) * V7X_MXU_DIM, D_FF)
PROJ_CHUNK = V7X_MXU_DIM
OUT_WEIGHTS = ("w_gate", "w_proj_a", "w_proj_b", "w_proj_c", "w_out", "w_ffn_in", "w_ffn_out")
BF16_SUBLANES = 16
ATTN_LOOKAHEAD = 3

V7X_VMEM_BYTES = 64 * 1024 * 1024
VMEM_LIMIT = V7X_VMEM_BYTES - 8 * 1024 * 1024

NT_DIMS = (((1,), (1,)), ((), ()))


def _cparams(*sem):
    return pltpu.CompilerParams(dimension_semantics=sem, vmem_limit_bytes=VMEM_LIMIT)


def _resident(shape):
    nd = len(shape)
    return pl.BlockSpec(shape, lambda *_: (0,) * nd, pipeline_mode=pl.Buffered(1))


def _rms(x, g):
    return x * lax.rsqrt(jnp.mean(x * x, axis=-1, keepdims=True) + EPS) * g


def _dot(a, b):
    return jnp.dot(a, b, preferred_element_type=F32)


def _in_proj_kernel(x_ref, g_ref, w_ref, cs_ref, h0_ref, cw_ref, cb_ref, wrg_ref, ba_ref, bx_ref, lam_ref,
                    q_ref, k_ref, v_ref, kb_ref, vb_ref, qc_ref, bo_ref, nc_ref, hl_ref,
                    xpad_sc, a_sc, u_sc, hs_sc, gg_sc, hc_sc, *, nt, seg, feature_major):
    tm = x_ref.shape[0]
    nseg = tm // seg
    halo = CONV_W - 1
    base = 8 - halo

    @pl.when(pl.program_id(0) % nt == 0)
    def _load_state():
        for s in range(nseg):
            xpad_sc[s, base:8, :] = cs_ref[s]
        hc_sc[...] = h0_ref[...]

    h = _rms(x_ref[...], g_ref[...]).astype(BF16)

    def proj(n, c0, width):
        return _dot(h, w_ref[:, IN_OFFS[n] + c0:IN_OFFS[n] + c0 + width])

    xb = proj(3, 0, LRU_W)
    for s in range(nseg):
        xpad_sc[s, 8:8 + seg, :] = xb[s * seg:(s + 1) * seg]

    def proj_chunk(kind, c0):
        cols = slice(c0, c0 + PROJ_CHUNK)
        if kind == "g":
            gg_sc[:, cols] = jax.nn.gelu(proj(4, c0, PROJ_CHUNK))
        elif kind == "q":
            qs = proj(0, c0, PROJ_CHUNK) * DA_Q_SCALE
            if feature_major:
                q_ref[cols, :] = qs.T.astype(BF16)
            else:
                q_ref[:, cols] = qs.astype(BF16)
        elif kind == "k":
            kc = proj(1, c0, PROJ_CHUNK)
            kb_ref[:, cols] = kc.astype(BF16)
            if feature_major:
                k_ref[cols, :] = kc.T
            else:
                k_ref[:, cols] = kc
        elif kind == "v":
            vc = proj(2, c0, PROJ_CHUNK)
            v_ref[:, cols] = vc
            if feature_major:
                vb_ref[cols, :] = vc.T.astype(BF16)
            else:
                vb_ref[:, cols] = vc.astype(BF16)
        else:
            qc_ref[:, cols] = (proj(5, c0, PROJ_CHUNK) * XA_SCALE).astype(BF16)

    chunks = [(kind, c0) for kind, width in (("g", LRU_W), ("q", DA_QK_W), ("k", DA_QK_W), ("v", DA_V_W), ("c", XA_W))
              for c0 in range(0, width, PROJ_CHUNK)]

    cw = cw_ref[...]
    sp = jax.nn.softplus(-lam_ref[...])
    for n in range(LRU_BLOCKS):
        for kind, c0 in chunks[n * len(chunks) // LRU_BLOCKS:(n + 1) * len(chunks) // LRU_BLOCKS]:
            proj_chunk(kind, c0)
        cols = slice(n * LRU_BW, (n + 1) * LRU_BW)
        xc_segs = []
        for s in range(nseg):
            xc = xpad_sc[s, base:base + seg, cols] * cw[0:1, cols]
            for j in range(1, CONV_W):
                xc = xc + xpad_sc[s, base + j:base + j + seg, cols] * cw[j:j + 1, cols]
            xc_segs.append(cb_ref[:, cols] + xc)
        xc = xc_segs[0] if nseg == 1 else jnp.concatenate(xc_segs, axis=0)
        xcb = xc.astype(BF16)
        pre = _dot(xcb, wrg_ref[n])
        tanh_r = jnp.tanh(0.5 * (pre[:, :LRU_BW] + ba_ref[:, cols]))
        tanh_i = jnp.tanh(0.5 * (pre[:, LRU_BW:] + bx_ref[:, cols]))
        half_c_sp = (-0.5 * LRU_C) * sp[:, cols]
        log_a = half_c_sp * tanh_r + half_c_sp
        a = jnp.exp(log_a)
        a_sc[:, cols] = a
        one_minus_a2 = -jnp.tanh(log_a) * (a * a + 1.0)
        mult = jnp.where(one_minus_a2 > 0.0, one_minus_a2 * lax.rsqrt(one_minus_a2), 0.0)
        u_sc[:, cols] = mult * ((0.5 * tanh_i + 0.5) * xc)
    for s in range(nseg):
        tail = xpad_sc[s, base + seg:8 + seg, :]
        nc_ref[s] = tail
        xpad_sc[s, base:8, :] = tail

    for s in range(nseg):
        def step(t, hprev, row0=s * seg):
            hnew = a_sc[pl.ds(row0 + t, 1), :] * hprev + u_sc[pl.ds(row0 + t, 1), :]
            hs_sc[pl.ds(row0 + t, 1), :] = hnew
            return hnew

        hlast = lax.fori_loop(0, seg, step, hc_sc[s], unroll=8)
        hc_sc[s] = hlast
        hl_ref[s] = hlast
    bo_ref[...] = (hs_sc[...] * gg_sc[...]).astype(BF16)


def _in_proj(x2d, conv_state, h0, p, tm, seq, feature_major):
    t = x2d.shape[0]
    seg = min(tm, seq)
    nseg = tm // seg
    nt = seq // seg
    assert tm % seg == 0 and seq % seg == 0 and seg % 8 == 0 and seg >= CONV_W - 1
    batch_block = lambda i: (i // nt, 0, 0)
    row = lambda width: pl.BlockSpec((tm, width), lambda i: (i, 0))
    sds = lambda width, dt: jax.ShapeDtypeStruct((t, width), dt)
    if feature_major:
        assert nseg == 1
        fm_spec = lambda width: pl.BlockSpec((width, tm), lambda i: (i // nt, i % nt))
        fm_sds = lambda width, dt: jax.ShapeDtypeStruct((t // seq * width, seq), dt)
    else:
        fm_spec, fm_sds = row, sds
    vec = _resident((1, LRU_W))
    gate_w = _resident((LRU_BLOCKS, LRU_BW, 2 * LRU_BW))
    conv_spec = pl.BlockSpec((nseg, CONV_W - 1, LRU_W), batch_block)
    state_spec = pl.BlockSpec((nseg, 1, LRU_W), batch_block)
    return pl.pallas_call(
        functools.partial(_in_proj_kernel, nt=nt, seg=seg, feature_major=feature_major),
        grid=(t // tm,),
        in_specs=[row(D_MODEL), _resident((1, D_MODEL)), _resident((D_MODEL, IN_W)), conv_spec, state_spec,
                  _resident((CONV_W, LRU_W)), vec, gate_w, vec, vec, vec],
        out_specs=[fm_spec(DA_QK_W), fm_spec(DA_QK_W), row(DA_V_W), row(DA_QK_W), fm_spec(DA_V_W), row(XA_W),
                   row(LRU_W), conv_spec, state_spec],
        out_shape=[fm_sds(DA_QK_W, BF16), fm_sds(DA_QK_W, F32), sds(DA_V_W, F32), sds(DA_QK_W, BF16),
                   fm_sds(DA_V_W, BF16), sds(XA_W, BF16), sds(LRU_W, BF16),
                   jax.ShapeDtypeStruct(conv_state.shape, F32), jax.ShapeDtypeStruct(h0.shape, F32)],
        scratch_shapes=[pltpu.VMEM((nseg, 8 + seg, LRU_W), F32), pltpu.VMEM((tm, LRU_W), F32),
                        pltpu.VMEM((tm, LRU_W), F32), pltpu.VMEM((tm, LRU_W), F32), pltpu.VMEM((tm, LRU_W), F32),
                        pltpu.VMEM((nseg, 1, LRU_W), F32)],
        compiler_params=_cparams("arbitrary"),
        name="in_proj",
    )(x2d, p["norm_mix"], p["w_in"], conv_state, h0, p["conv_w"], p["conv_b"], p["w_rg"], p["b_rg_a"],
      p["b_rg_x"], p["rg_lambda"])


def _cast_block_specs(weights, steps, step_index):
    specs = []
    for w in weights:
        rows, cols = w.shape
        nblk = max(d for d in range(1, steps + 1)
                   if steps % d == 0 and rows % d == 0 and (rows // d) % BF16_SUBLANES == 0)
        specs.append(pl.BlockSpec((rows // nblk, cols), lambda *g, per=steps // nblk: (step_index(*g) // per, 0)))
    return specs


def _mem_kv_kernel(m_ref, g_ref, w_ref, mk_ref, mv_ref, mkb_ref, mvb_ref):
    h = _rms(m_ref[...], g_ref[...]).astype(BF16)
    mk = _dot(h, w_ref[:, :XA_W])
    mv = _dot(h, w_ref[:, XA_W:])
    mk_ref[...] = mk
    mv_ref[...] = mv
    mkb_ref[...] = mk.astype(BF16)
    mvb_ref[...] = mv.astype(BF16)


def _mem_kv(mem2d, g, w_bf16, tm):
    t = mem2d.shape[0]
    row = pl.BlockSpec((tm, XA_W), lambda i: (i, 0))
    return pl.pallas_call(
        _mem_kv_kernel,
        grid=(t // tm,),
        in_specs=[pl.BlockSpec((tm, D_MODEL), lambda i: (i, 0)), _resident((1, D_MODEL)),
                  _resident((D_MODEL, 2 * XA_W))],
        out_specs=[row, row, row, row],
        out_shape=[jax.ShapeDtypeStruct((t, XA_W), F32), jax.ShapeDtypeStruct((t, XA_W), F32),
                   jax.ShapeDtypeStruct((t, XA_W), BF16), jax.ShapeDtypeStruct((t, XA_W), BF16)],
        compiler_params=_cparams("parallel"),
        name="mem_kv",
    )(mem2d, g, w_bf16)


def _rel_bucket(rel):
    half = N_BUCKETS // 2
    max_exact = half // 2
    n = jnp.abs(rel)
    nf = jnp.maximum(n, 1).astype(F32)
    large = max_exact + (jnp.log(nf / max_exact) / math.log(MAX_DISTANCE / max_exact)
                         * (half - max_exact)).astype(jnp.int32)
    large = jnp.minimum(large, half - 1)
    return jnp.where(rel > 0, half, 0) + jnp.where(n < max_exact, n, large)


def _table_lanes(rel_table):
    return jnp.pad(rel_table.T, ((0, 0), (0, 128 - N_BUCKETS)))


def _bias_from_buckets(bkt, tabt_ref, head, vis, shift_bucket):
    rows, width = bkt.shape
    table = jnp.broadcast_to(tabt_ref[head:head + 1, :], (rows, 128))
    bias = jnp.concatenate([jnp.take_along_axis(table, bkt[:, c:c + 128], axis=1) for c in range(0, width, 128)],
                           axis=1)
    if shift_bucket is not None:
        bias = bias - tabt_ref[head:head + 1, shift_bucket:shift_bucket + 1]
    return jnp.where(vis, bias * LOG2E, MASK_VALUE)


def _split_maps(qh):
    lane = lax.broadcasted_iota(jnp.int32, qh.shape, 1)
    zero = jnp.zeros_like(qh)
    return jnp.concatenate([jnp.where(lane < DA_HD, qh, zero), jnp.where(lane >= DA_HD, qh, zero)], axis=0)


def _ones_column(n):
    return jnp.where(lax.broadcasted_iota(jnp.int32, (n, DA_V), 1) == 0, 1.0, 0.0).astype(BF16)


def _diff_finish(o, lam, g, tq, lambda_init):
    d = o[:tq] - lam * o[tq:]
    return _rms(d, g) * (1.0 - lambda_init)


def _attn_self_kernel(tab_ref, lam_ref, qt_ref, k_ref, vt_ref, bkt_ref, g_ref, *rest, tq, lambda_init, n_cast):
    w_refs, o_ref, wb_refs = rest[:n_cast], rest[n_cast], rest[n_cast + 1:2 * n_cast + 1]
    bias_sc, q2t_sc, m_sc, l_sc, acc_sc, s_sc = rest[2 * n_cast + 1:]
    for w_ref, wb_ref in zip(w_refs, wb_refs):
        wb_ref[...] = w_ref[...].astype(BF16)

    b = pl.program_id(0)
    i = pl.program_id(1)

    @pl.when((b == 0) & (i == 0))
    def _build_bias():
        bkt = bkt_ref[...]
        kpos = lax.broadcasted_iota(jnp.int32, bkt.shape, 0)
        qpos = lax.broadcasted_iota(jnp.int32, bkt.shape, 1)
        vis = jnp.right_shift(kpos - tq, CHUNK_SHIFT) <= jnp.right_shift(qpos, CHUNK_SHIFT)
        for h in range(DA_HEADS):
            bias_sc[h] = _bias_from_buckets(bkt, tab_ref, h, vis, FAR_BUCKET)

    feat = lax.broadcasted_iota(jnp.int32, (2 * DA_HD, tq), 0)
    for h in range(DA_HEADS):
        qt = qt_ref[h * 2 * DA_HD:(h + 1) * 2 * DA_HD, :]
        zero = jnp.zeros_like(qt)
        q2t_sc[h] = jnp.concatenate([jnp.where(feat < DA_HD, qt, zero), jnp.where(feat >= DA_HD, qt, zero)], axis=1)
    m_sc[...] = jnp.full(m_sc.shape, MASK_VALUE, F32)
    l_sc[...] = jnp.zeros(l_sc.shape, F32)
    acc_sc[...] = jnp.zeros(acc_sc.shape, F32)

    def logits(h, start, nk):
        kj = k_ref[pl.ds(start, nk), h * DA_V:(h + 1) * DA_V]
        return _dot(kj, q2t_sc[h])

    def update(h, start, nk, s, bias):
        vtj = vt_ref[h * DA_V:(h + 1) * DA_V, pl.ds(start, nk)]
        if bias is not None:
            s = s + jnp.concatenate([bias, bias], axis=1)
        m_prev = m_sc[h]
        m_new = jnp.maximum(m_prev, jnp.max(s, axis=0, keepdims=True))
        alpha = jnp.exp2(m_prev - m_new)
        p = jnp.exp2(s - m_new[0:1])
        l_sc[h] = alpha * l_sc[h] + jnp.sum(p, axis=0, keepdims=True)
        acc_sc[h] = alpha[0:1] * acc_sc[h] + _dot(vtj, p.astype(BF16))
        m_sc[h] = m_new

    def key_tile(start, nk, bias_of, after=None):
        nslot = ATTN_LOOKAHEAD + 1
        for h in range(ATTN_LOOKAHEAD):
            s_sc[h % nslot, :nk] = logits(h, start, nk)
        for h in range(DA_HEADS):
            if h + ATTN_LOOKAHEAD < DA_HEADS:
                s_sc[(h + ATTN_LOOKAHEAD) % nslot, :nk] = logits(h + ATTN_LOOKAHEAD, start, nk)
            update(h, start, nk, s_sc[h % nslot, :nk], bias_of(h))
            if after is not None:
                after(h)

    n_far = jnp.maximum(i - 1, 0)

    def far_pair(j, carry):
        key_tile(pl.multiple_of(j * 2 * tq, 2 * tq), 2 * tq, lambda h: None)
        return carry

    lax.fori_loop(0, n_far // 2, far_pair, 0)

    @pl.when(n_far % 2 == 1)
    def _odd_far_tile():
        key_tile(pl.multiple_of((n_far - 1) * tq, tq), tq, lambda h: None)

    lam = lam_ref[0]
    g = g_ref[...]

    def finish(h):
        o = acc_sc[h] / l_sc[h][0:1]
        d = o[:, :tq] - lam * o[:, tq:]
        y = d * lax.rsqrt(jnp.mean(d * d, axis=0, keepdims=True) + EPS) * g * (1.0 - lambda_init)
        o_ref[:, h * DA_V:(h + 1) * DA_V] = y.T.astype(BF16)

    @pl.when(i >= 1)
    def _prev_and_diagonal():
        key_tile(pl.multiple_of((i - 1) * tq, tq), 2 * tq, lambda h: bias_sc[h], after=finish)

    @pl.when(i == 0)
    def _diagonal_only():
        key_tile(0, tq, lambda h: bias_sc[h, tq:, :], after=finish)


def _attn_self(qtb, kb, vtb, rel_table, lam, subln_g, batch, seq, lambda_init, tq, cast_weights):
    assert seq % tq == 0 and tq % 128 == 0 and tq >= MAX_DISTANCE and tq % CHUNK == 0
    nq = seq // tq
    rel = jnp.arange(2 * tq, dtype=jnp.int32)[:, None] - (tq + jnp.arange(tq, dtype=jnp.int32))[None, :]
    bkt = _rel_bucket(rel)
    smem = pl.BlockSpec(memory_space=pltpu.SMEM)

    cast_specs = _cast_block_specs(cast_weights, batch * nq, lambda b, i: b * nq + i)
    n_cast = len(cast_weights)
    outs = pl.pallas_call(
        functools.partial(_attn_self_kernel, tq=tq, lambda_init=lambda_init, n_cast=n_cast),
        grid=(batch, nq),
        in_specs=[_resident((DA_HEADS, 128)), smem,
                  pl.BlockSpec((DA_QK_W, tq), lambda b, i: (b, i)),
                  pl.BlockSpec((seq, DA_QK_W), lambda b, i: (b, 0)),
                  pl.BlockSpec((DA_V_W, seq), lambda b, i: (b, 0)),
                  _resident((2 * tq, tq)), _resident((DA_V, 1))] + cast_specs,
        out_specs=[pl.BlockSpec((tq, DA_V_W), lambda b, i: (b * nq + i, 0))] + cast_specs,
        out_shape=[jax.ShapeDtypeStruct((batch * seq, DA_V_W), BF16)]
                  + [jax.ShapeDtypeStruct(w.shape, BF16) for w in cast_weights],
        scratch_shapes=[pltpu.VMEM((DA_HEADS, 2 * tq, tq), F32),
                        pltpu.VMEM((DA_HEADS, 2 * DA_HD, 2 * tq), BF16),
                        pltpu.VMEM((DA_HEADS, 8, 2 * tq), F32),
                        pltpu.VMEM((DA_HEADS, 8, 2 * tq), F32),
                        pltpu.VMEM((DA_HEADS, DA_V, 2 * tq), F32),
                        pltpu.VMEM((ATTN_LOOKAHEAD + 1, 2 * tq, 2 * tq), F32)],
        compiler_params=_cparams("arbitrary", "arbitrary"),
        name="attn_self",
    )(_table_lanes(rel_table), lam, qtb, kb, vtb, bkt, subln_g.reshape(DA_V, 1), *cast_weights)
    return outs[0], outs[1:]


def _attn_cached_kernel(tab_ref, lam_ref, q_ref, ckt_ref, cv_ref, nk_ref, nv_ref, bktc_ref, bktn_ref, g_ref, o_ref,
                        biasc_sc, biasn_sc, *, sq, past, lambda_init):
    @pl.when(pl.program_id(0) == 0)
    def _build_bias():
        for bkt_ref, sc, k0 in ((bktc_ref, biasc_sc, 0), (bktn_ref, biasn_sc, past)):
            bkt = bkt_ref[...]
            qpos = past + lax.broadcasted_iota(jnp.int32, bkt.shape, 0)
            kpos = k0 + lax.broadcasted_iota(jnp.int32, bkt.shape, 1)
            vis = jnp.right_shift(kpos, CHUNK_SHIFT) <= jnp.right_shift(qpos, CHUNK_SHIFT)
            for h in range(DA_HEADS):
                sc[h] = _bias_from_buckets(bkt, tab_ref, h, vis, None)[:, :sc.shape[2]]

    q = q_ref[...]
    lam = lam_ref[0]
    g = g_ref[...]
    for h in range(DA_HEADS):
        cols = slice(h * DA_V, (h + 1) * DA_V)
        q2 = _split_maps(q[:, cols])
        s_c = _dot(q2, ckt_ref[cols, :].astype(BF16))
        s_c = (s_c.reshape(2, sq, past) + biasc_sc[h][None]).reshape(2 * sq, past)
        s_n = lax.dot_general(q2, nk_ref[:, cols].astype(BF16), NT_DIMS, preferred_element_type=F32)
        s_n = (s_n.reshape(2, sq, sq) + biasn_sc[h][None]).reshape(2 * sq, sq)
        m = jnp.maximum(jnp.max(s_c, axis=1, keepdims=True), jnp.max(s_n, axis=1, keepdims=True))
        p_c = jnp.exp2(s_c - m)
        p_n = jnp.exp2(s_n - m)
        l = jnp.sum(p_c, axis=1, keepdims=True) + jnp.sum(p_n, axis=1, keepdims=True)
        v_c = cv_ref[pl.ds(h, past, stride=DA_HEADS), :]
        o = (_dot(p_c.astype(BF16), v_c.astype(BF16))
             + _dot(p_n.astype(BF16), nv_ref[:, cols].astype(BF16))) / l
        o_ref[:, cols] = _diff_finish(o, lam, g, sq, lambda_init).astype(BF16)


def _attn_cached(q, cache_kt2d, cache_v2d, nk, nv, rel_table, lam, subln_g, batch, sq, past, lambda_init):
    qpos = past + jnp.arange(sq, dtype=jnp.int32)
    bkt_c = _rel_bucket(jnp.arange(past, dtype=jnp.int32)[None, :] - qpos[:, None])
    assert sq <= 128 and past % 128 == 0
    bkt_n = jnp.pad(_rel_bucket(qpos[None, :] - qpos[:, None]), ((0, 0), (0, 128 - sq)))
    smem = pl.BlockSpec(memory_space=pltpu.SMEM)
    new_spec = pl.BlockSpec((sq, DA_QK_W), lambda b: (b, 0))
    return pl.pallas_call(
        functools.partial(_attn_cached_kernel, sq=sq, past=past, lambda_init=lambda_init),
        grid=(batch,),
        in_specs=[_resident((DA_HEADS, 128)), smem, new_spec,
                  pl.BlockSpec((DA_QK_W, past), lambda b: (b, 0)),
                  pl.BlockSpec((past * DA_HEADS, DA_V), lambda b: (b, 0)),
                  new_spec, new_spec,
                  _resident((sq, past)), _resident((sq, 128)), _resident((1, DA_V))],
        out_specs=new_spec,
        out_shape=jax.ShapeDtypeStruct((batch * sq, DA_V_W), BF16),
        scratch_shapes=[pltpu.VMEM((DA_HEADS, sq, past), F32), pltpu.VMEM((DA_HEADS, sq, sq), F32)],
        compiler_params=_cparams("arbitrary"),
        name="attn_cached",
    )(_table_lanes(rel_table), lam, q, cache_kt2d, cache_v2d, nk, nv, bkt_c, bkt_n, subln_g)


def _mix_ffn_kernel(x_ref, a_ref, b_ref, qc_ref, mk_ref, mv_ref, gmix_ref, wg_ref, bg_ref, wpa_ref, wpb_ref, wpc_ref,
                    wo_ref, gffn_ref, wfi_ref, wfo_ref, gfin_ref, y_ref, *, seg):
    tm = x_ref.shape[0]
    nseg = tm // seg
    n_mem = mk_ref.shape[0] // nseg

    def xattn_head(hh):
        cols = slice(hh * XA_HD, (hh + 1) * XA_HD)
        outs = []
        for s in range(nseg):
            mem = slice(s * n_mem, (s + 1) * n_mem)
            logit = lax.dot_general(qc_ref[s * seg:(s + 1) * seg, cols], mk_ref[mem, cols].astype(BF16), NT_DIMS,
                                    preferred_element_type=F32)
            p = jnp.exp(logit - jnp.max(logit, axis=1, keepdims=True))
            l = jnp.sum(p, axis=1, keepdims=True)
            outs.append((_dot(p.astype(BF16), mv_ref[mem, cols].astype(BF16)) / l).astype(BF16))
        return outs[0] if nseg == 1 else jnp.concatenate(outs, axis=0)

    x = x_ref[...]
    h = _rms(x, gmix_ref[...]).astype(BF16)

    def branch(n, br, wp_ref):
        cols = slice(n * D_MODEL, (n + 1) * D_MODEL)
        gate = jax.nn.sigmoid(_dot(h, wg_ref[:, cols]) + bg_ref[:, cols])
        return gate * _dot(br, wp_ref[...])

    merged = branch(0, a_ref[...], wpa_ref)
    c_heads = [xattn_head(hh) for hh in range(XA_HEADS // 2)]
    merged = merged + branch(1, b_ref[...], wpb_ref)
    c_heads += [xattn_head(hh) for hh in range(XA_HEADS // 2, XA_HEADS)]
    merged = merged + branch(2, jnp.concatenate(c_heads, axis=1), wpc_ref)
    x1 = x + _dot(merged.astype(BF16), wo_ref[...])
    h2 = _rms(x1, gffn_ref[...]).astype(BF16)
    ff = None
    for c0, c1 in zip(FFN_SPLITS[:-1], FFN_SPLITS[1:]):
        gcols = slice(c0, c1)
        ucols = slice(D_FF + c0, D_FF + c1)
        act = (jax.nn.silu(_dot(h2, wfi_ref[:, gcols])) * _dot(h2, wfi_ref[:, ucols])).astype(BF16)
        term = _dot(act, wfo_ref[gcols, :])
        ff = term if ff is None else ff + term
    y_ref[...] = _rms(x1 + ff, gfin_ref[...])


def _mix_ffn(x2d, a_out, b_out, qc, mk2d, mv2d, p, tm, seq):
    t = x2d.shape[0]
    seg = min(tm, seq)
    nseg = tm // seg
    nt = seq // seg
    n_mem = mk2d.shape[0] // (t // seq)
    assert tm % seg == 0 and seq % seg == 0
    row = pl.BlockSpec((tm, D_MODEL), lambda i: (i, 0))
    mem = pl.BlockSpec((nseg * n_mem, XA_W), lambda i: (i // nt, 0))
    vec = _resident((1, D_MODEL))
    sq_w = _resident((D_MODEL, D_MODEL))
    return pl.pallas_call(
        functools.partial(_mix_ffn_kernel, seg=seg),
        grid=(t // tm,),
        in_specs=[row, row, row, row, mem, mem, vec, _resident((D_MODEL, N_BRANCH * D_MODEL)),
                  _resident((1, N_BRANCH * D_MODEL)), sq_w, sq_w, sq_w, sq_w, vec,
                  _resident((D_MODEL, 2 * D_FF)), _resident((D_FF, D_MODEL)), vec],
        out_specs=row,
        out_shape=jax.ShapeDtypeStruct((t, D_MODEL), F32),
        compiler_params=_cparams("parallel"),
        name="mix_ffn",
    )(x2d, a_out, b_out, qc, mk2d, mv2d, p["norm_mix"], p["w_gate"], p["b_gate"], p["w_proj_a"], p["w_proj_b"],
      p["w_proj_c"], p["w_out"], p["norm_ffn"], p["w_ffn_in"], p["w_ffn_out"], p["norm_final"])


def _layer(x, mk2d, mv2d, cache_kt2d, cache_v2d, conv_state, h0, rel_table, lam, p, lambda_init, tm, tq, tm_out):
    batch, seq, _ = x.shape
    x2d = x.reshape(batch * seq, D_MODEL)
    no_history = cache_kt2d is None
    q, k, v, kb, vb, qc, b_out, new_conv, h_last = _in_proj(x2d, conv_state, h0, p, tm, seq, no_history)
    if no_history:
        a_out, out_w = _attn_self(q, kb, vb, rel_table, lam, p["subln_g"], batch, seq, lambda_init, tq,
                                  [p[n] for n in OUT_WEIGHTS])
        p = {**p, **dict(zip(OUT_WEIGHTS, out_w))}
    else:
        past = cache_kt2d.shape[1]
        a_out = _attn_cached(q, cache_kt2d, cache_v2d, k, v, rel_table, lam, p["subln_g"], batch, seq, past,
                             lambda_init)
    y = _mix_ffn(x2d, a_out, b_out, qc, mk2d, mv2d, p, tm_out, seq)
    return y.reshape(batch, seq, D_MODEL), k, v, new_conv, h_last.reshape(batch, LRU_W), p


def kernel(x_prompt, x_sample, mem_prompt, cache_k, cache_v, state_conv, state_lru, cache_mem_k, cache_mem_v,
           rel_table, norm_mix, w_in, lambda_q1, lambda_k1, lambda_q2, lambda_k2, subln_g, conv_w, conv_b,
           w_rg_a, b_rg_a, w_rg_x, b_rg_x, rg_lambda, norm_mem, w_mem_kv, w_proj_a, w_proj_b, w_proj_c,
           w_gate, b_gate, w_out, norm_ffn, w_ffn_in, w_ffn_out, norm_final):
    depth = w_in.shape[0]
    assert depth == 1, "the final norm is fused into the layer's last kernel"
    l = 0
    lambda_init = 0.8 - 0.6 * math.exp(-0.3 * l)
    bp, sp_, _ = x_prompt.shape
    bs, ss, _ = x_sample.shape
    past = cache_k.shape[2]
    n_mem = mem_prompt.shape[1]
    row = lambda a: a.reshape(1, -1).astype(F32)
    mk, mv, mkb, mvb = _mem_kv(mem_prompt.reshape(bp * n_mem, D_MODEL), row(norm_mem[l]),
                               w_mem_kv[l].astype(BF16), 256)
    p = dict(norm_mix=row(norm_mix[l]), w_in=w_in[l].astype(BF16), subln_g=row(subln_g[l]),
             conv_w=conv_w[l], conv_b=row(conv_b[l]),
             w_rg=jnp.concatenate([w_rg_a[l], w_rg_x[l]], axis=-1).astype(BF16),
             b_rg_a=row(b_rg_a[l]), b_rg_x=row(b_rg_x[l]), rg_lambda=row(rg_lambda[l]),
             w_proj_a=w_proj_a[l], w_proj_b=w_proj_b[l], w_proj_c=w_proj_c[l], w_gate=w_gate[l],
             b_gate=row(b_gate[l]), w_out=w_out[l], norm_ffn=row(norm_ffn[l]), w_ffn_in=w_ffn_in[l],
             w_ffn_out=w_ffn_out[l], norm_final=row(norm_final))
    lam = (jnp.exp(jnp.sum(lambda_q1[l] * lambda_k1[l]).astype(F32))
           - jnp.exp(jnp.sum(lambda_q2[l] * lambda_k2[l]).astype(F32)) + lambda_init).reshape(1)

    zeros_conv = jnp.zeros((bp, CONV_W - 1, LRU_W), F32)
    zeros_h = jnp.zeros((bp, 1, LRU_W), F32)
    yp, kp, vp, cp, hp, p = _layer(x_prompt, mkb, mvb, None, None, zeros_conv, zeros_h, rel_table, lam, p,
                                   lambda_init, tm=512, tq=256, tm_out=512)
    cache_kt = jnp.transpose(cache_k[l], (0, 2, 3, 4, 1)).reshape(bs * DA_QK_W, past)
    ys, ks, vs, cs, hs, _ = _layer(x_sample, cache_mem_k[l].reshape(bs * n_mem, XA_W),
                                   cache_mem_v[l].reshape(bs * n_mem, XA_W),
                                   cache_kt, cache_v[l].reshape(bs * past * DA_HEADS, DA_V),
                                   state_conv[l], state_lru[l].reshape(bs, 1, LRU_W), rel_table, lam, p,
                                   lambda_init, tm=256, tq=ss, tm_out=128)
    kp = jnp.transpose(kp.reshape(bp, DA_HEADS, 2, DA_HD, sp_), (0, 4, 1, 2, 3))
    return (yp, ys,
            kp[None], vp.reshape(1, bp, sp_, DA_HEADS, DA_V),
            cp[None], hp[None],
            mk.reshape(1, bp, n_mem, XA_HEADS, XA_HD), mv.reshape(1, bp, n_mem, XA_HEADS, XA_HD),
            ks.reshape(1, bs, ss, DA_HEADS, 2, DA_HD), vs.reshape(1, bs, ss, DA_HEADS, DA_V),
            cs[None], hs[None])
```

```python
import functools
import math

import jax
import jax.numpy as jnp
from jax import lax
from jax.experimental import pallas as pl
from jax.experimental.pallas import tpu as pltpu

F32 = jnp.float32
BF16 = jnp.bfloat16

D_MODEL = 1024
CHUNK = 64
CHUNK_SHIFT = 6
DA_HEADS = 8
DA_HD = 64
DA_V = 2 * DA_HD
LRU_W = D_MODEL
LRU_BLOCKS = 8
LRU_BW = LRU_W // LRU_BLOCKS
CONV_W = 4
LRU_C = 8.0
XA_HEADS = 4
XA_HD = 256
N_BUCKETS = 32
MAX_DISTANCE = 128
D_FF = ((8 * D_MODEL + 3 * 256 - 1) // (3 * 256)) * 256
N_BRANCH = 3
EPS = 1e-6
DA_QK_W = DA_HEADS * 2 * DA_HD
DA_V_W = DA_HEADS * DA_V
XA_W = XA_HEADS * XA_HD
IN_OFFS = (0, DA_QK_W, 2 * DA_QK_W, 2 * DA_QK_W + DA_V_W, 2 * DA_QK_W + DA_V_W + LRU_W,
           2 * DA_QK_W + DA_V_W + 2 * LRU_W)
IN_W = 2 * DA_QK_W + DA_V_W + 2 * LRU_W + XA_W
LOG2E = math.log2(math.e)
DA_Q_SCALE = DA_HD ** -0.5 * LOG2E
XA_SCALE = XA_HD ** -0.5
MASK_VALUE = -1e30
FAR_BUCKET = N_BUCKETS // 2 - 1
V7X_MXU_DIM = 256
FFN_SPLITS = (0, -(-(D_FF // V7X_MXU_DIM) // 2) * V7X_MXU_DIM, D_FF)
PROJ_CHUNK = V7X_MXU_DIM
OUT_WEIGHTS = ("w_gate", "w_proj_a", "w_proj_b", "w_proj_c", "w_out", "w_ffn_in", "w_ffn_out")
SCAN_GROUP = 8
BF16_SUBLANES = 16
ATTN_LOOKAHEAD = 3

V7X_VMEM_BYTES = 64 * 1024 * 1024
VMEM_LIMIT = V7X_VMEM_BYTES - 8 * 1024 * 1024

NT_DIMS = (((1,), (1,)), ((), ()))


def _cparams(*sem):
    return pltpu.CompilerParams(dimension_semantics=sem, vmem_limit_bytes=VMEM_LIMIT)


def _resident(shape):
    nd = len(shape)
    return pl.BlockSpec(shape, lambda *_: (0,) * nd, pipeline_mode=pl.Buffered(1))


def _rms(x, g):
    return x * lax.rsqrt(jnp.mean(x * x, axis=-1, keepdims=True) + EPS) * g


def _dot(a, b):
    return jnp.dot(a, b, preferred_element_type=F32)


def _in_proj_kernel(x_ref, g_ref, w_ref, cs_ref, h0_ref, cw_ref, cb_ref, wrg_ref, ba_ref, bx_ref, lam_ref,
                    q_ref, k_ref, v_ref, kb_ref, vb_ref, qc_ref, bo_ref, nc_ref, hl_ref,
                    xpad_sc, a_sc, u_sc, hs_sc, gg_sc, hc_sc, p_sc, q_sc, hin_sc, *, nt, seg, feature_major):
    tm = x_ref.shape[0]
    nseg = tm // seg
    two_level = nseg == 1
    groups = tm // SCAN_GROUP
    halo = CONV_W - 1
    base = 8 - halo

    @pl.when(pl.program_id(0) % nt == 0)
    def _load_state():
        for s in range(nseg):
            xpad_sc[s, base:8, :] = cs_ref[s]
            for n in range(LRU_BLOCKS):
                hc_sc[s, n] = h0_ref[s][:, n * LRU_BW:(n + 1) * LRU_BW]

    h = _rms(x_ref[...], g_ref[...]).astype(BF16)

    def proj(n, c0, width):
        return _dot(h, w_ref[:, IN_OFFS[n] + c0:IN_OFFS[n] + c0 + width])

    xb = proj(3, 0, LRU_W)
    for s in range(nseg):
        xpad_sc[s, 8:8 + seg, :] = xb[s * seg:(s + 1) * seg]

    def proj_chunk(kind, c0):
        cols = slice(c0, c0 + PROJ_CHUNK)
        if kind == "g":
            gg_sc[:, cols] = jax.nn.gelu(proj(4, c0, PROJ_CHUNK))
        elif kind == "q":
            qs = proj(0, c0, PROJ_CHUNK) * DA_Q_SCALE
            if feature_major:
                q_ref[cols, :] = qs.T.astype(BF16)
            else:
                q_ref[:, cols] = qs.astype(BF16)
        elif kind == "k":
            kc = proj(1, c0, PROJ_CHUNK)
            kb_ref[:, cols] = kc.astype(BF16)
            if feature_major:
                k_ref[cols, :] = kc.T
            else:
                k_ref[:, cols] = kc
        elif kind == "v":
            vc = proj(2, c0, PROJ_CHUNK)
            v_ref[:, cols] = vc
            if feature_major:
                vb_ref[cols, :] = vc.T.astype(BF16)
            else:
                vb_ref[:, cols] = vc.astype(BF16)
        else:
            qc_ref[:, cols] = (proj(5, c0, PROJ_CHUNK) * XA_SCALE).astype(BF16)

    chunks = [(kind, c0) for kind, width in (("g", LRU_W), ("q", DA_QK_W), ("k", DA_QK_W), ("v", DA_V_W), ("c", XA_W))
              for c0 in range(0, width, PROJ_CHUNK)]

    cw = cw_ref[...]
    sp = jax.nn.softplus(-lam_ref[...])
    for n in range(LRU_BLOCKS):
        for kind, c0 in chunks[n * len(chunks) // LRU_BLOCKS:(n + 1) * len(chunks) // LRU_BLOCKS]:
            proj_chunk(kind, c0)
        cols = slice(n * LRU_BW, (n + 1) * LRU_BW)
        xc_segs = []
        for s in range(nseg):
            xc = xpad_sc[s, base:base + seg, cols] * cw[0:1, cols]
            for j in range(1, CONV_W):
                xc = xc + xpad_sc[s, base + j:base + j + seg, cols] * cw[j:j + 1, cols]
            xc_segs.append(cb_ref[:, cols] + xc)
        xc = xc_segs[0] if nseg == 1 else jnp.concatenate(xc_segs, axis=0)
        xcb = xc.astype(BF16)
        pre = _dot(xcb, wrg_ref[n])
        tanh_r = jnp.tanh(0.5 * (pre[:, :LRU_BW] + ba_ref[:, cols]))
        tanh_i = jnp.tanh(0.5 * (pre[:, LRU_BW:] + bx_ref[:, cols]))
        half_c_sp = (-0.5 * LRU_C) * sp[:, cols]
        log_a = half_c_sp * tanh_r + half_c_sp
        a = jnp.exp(log_a)
        a_sc[n] = a
        one_minus_a2 = -jnp.tanh(log_a) * (a * a + 1.0)
        mult = jnp.where(one_minus_a2 > 0.0, one_minus_a2 * lax.rsqrt(one_minus_a2), 0.0)
        u_sc[n] = mult * ((0.5 * tanh_i + 0.5) * xc)
        if two_level:
            for r in range(SCAN_GROUP):
                a_r = a_sc[n, pl.ds(r, groups, stride=SCAN_GROUP), :]
                u_r = u_sc[n, pl.ds(r, groups, stride=SCAN_GROUP), :]
                p_r, q_r = (a_r, u_r) if r == 0 else (a_r * p_r, a_r * q_r + u_r)
                p_sc[n, r] = p_r
                q_sc[n, r] = q_r
    for s in range(nseg):
        tail = xpad_sc[s, base + seg:8 + seg, :]
        nc_ref[s] = tail
        xpad_sc[s, base:8, :] = tail

    if two_level:
        def group_step(gi, h_in):
            hin_sc[:, pl.ds(gi, 1), :] = h_in
            last = SCAN_GROUP - 1
            return p_sc[:, last, pl.ds(gi, 1), :] * h_in + q_sc[:, last, pl.ds(gi, 1), :]

        hlast = lax.fori_loop(0, groups, group_step, hc_sc[0], unroll=8)
        hc_sc[0] = hlast
        hl_ref[0] = jnp.concatenate([hlast[n] for n in range(LRU_BLOCKS)], axis=1)
        for n in range(LRU_BLOCKS):
            h_in = hin_sc[n]
            for r in range(SCAN_GROUP):
                hs_sc[n, pl.ds(r, groups, stride=SCAN_GROUP), :] = p_sc[n, r] * h_in + q_sc[n, r]
    else:
        for s in range(nseg):
            def step(t, hprev, row0=s * seg):
                hnew = a_sc[:, pl.ds(row0 + t, 1), :] * hprev + u_sc[:, pl.ds(row0 + t, 1), :]
                hs_sc[:, pl.ds(row0 + t, 1), :] = hnew
                return hnew

            hlast = lax.fori_loop(0, seg, step, hc_sc[s], unroll=8)
            hc_sc[s] = hlast
            hl_ref[s] = jnp.concatenate([hlast[n] for n in range(LRU_BLOCKS)], axis=1)
    for n in range(LRU_BLOCKS):
        cols = slice(n * LRU_BW, (n + 1) * LRU_BW)
        bo_ref[:, cols] = (hs_sc[n] * gg_sc[:, cols]).astype(BF16)


def _in_proj(x2d, conv_state, h0, p, tm, seq, feature_major):
    t = x2d.shape[0]
    seg = min(tm, seq)
    nseg = tm // seg
    nt = seq // seg
    assert tm % seg == 0 and seq % seg == 0 and seg % 8 == 0 and seg >= CONV_W - 1
    batch_block = lambda i: (i // nt, 0, 0)
    row = lambda width: pl.BlockSpec((tm, width), lambda i: (i, 0))
    sds = lambda width, dt: jax.ShapeDtypeStruct((t, width), dt)
    if feature_major:
        assert nseg == 1
        fm_spec = lambda width: pl.BlockSpec((width, tm), lambda i: (i // nt, i % nt))
        fm_sds = lambda width, dt: jax.ShapeDtypeStruct((t // seq * width, seq), dt)
    else:
        fm_spec, fm_sds = row, sds
    vec = _resident((1, LRU_W))
    gate_w = _resident((LRU_BLOCKS, LRU_BW, 2 * LRU_BW))
    slab = pltpu.VMEM((LRU_BLOCKS, tm, LRU_BW), F32)
    conv_spec = pl.BlockSpec((nseg, CONV_W - 1, LRU_W), batch_block)
    state_spec = pl.BlockSpec((nseg, 1, LRU_W), batch_block)
    return pl.pallas_call(
        functools.partial(_in_proj_kernel, nt=nt, seg=seg, feature_major=feature_major),
        grid=(t // tm,),
        in_specs=[row(D_MODEL), _resident((1, D_MODEL)), _resident((D_MODEL, IN_W)), conv_spec, state_spec,
                  _resident((CONV_W, LRU_W)), vec, gate_w, vec, vec, vec],
        out_specs=[fm_spec(DA_QK_W), fm_spec(DA_QK_W), row(DA_V_W), row(DA_QK_W), fm_spec(DA_V_W), row(XA_W),
                   row(LRU_W), conv_spec, state_spec],
        out_shape=[fm_sds(DA_QK_W, BF16), fm_sds(DA_QK_W, F32), sds(DA_V_W, F32), sds(DA_QK_W, BF16),
                   fm_sds(DA_V_W, BF16), sds(XA_W, BF16), sds(LRU_W, BF16),
                   jax.ShapeDtypeStruct(conv_state.shape, F32), jax.ShapeDtypeStruct(h0.shape, F32)],
        scratch_shapes=[pltpu.VMEM((nseg, 8 + seg, LRU_W), F32), slab, slab, slab, pltpu.VMEM((tm, LRU_W), F32),
                        pltpu.VMEM((nseg, LRU_BLOCKS, 1, LRU_BW), F32),
                        pltpu.VMEM((LRU_BLOCKS, SCAN_GROUP, tm // SCAN_GROUP, LRU_BW), F32),
                        pltpu.VMEM((LRU_BLOCKS, SCAN_GROUP, tm // SCAN_GROUP, LRU_BW), F32),
                        pltpu.VMEM((LRU_BLOCKS, tm // SCAN_GROUP, LRU_BW), F32)],
        compiler_params=_cparams("arbitrary"),
        name="in_proj",
    )(x2d, p["norm_mix"], p["w_in"], conv_state, h0, p["conv_w"], p["conv_b"], p["w_rg"], p["b_rg_a"],
      p["b_rg_x"], p["rg_lambda"])


def _cast_block_specs(weights, steps, step_index):
    specs = []
    for w in weights:
        rows, cols = w.shape
        nblk = max(d for d in range(1, steps + 1)
                   if steps % d == 0 and rows % d == 0 and (rows // d) % BF16_SUBLANES == 0)
        specs.append(pl.BlockSpec((rows // nblk, cols), lambda *g, per=steps // nblk: (step_index(*g) // per, 0)))
    return specs


def _mem_kv_kernel(m_ref, g_ref, w_ref, mk_ref, mv_ref, mkb_ref, mvb_ref):
    h = _rms(m_ref[...], g_ref[...]).astype(BF16)
    mk = _dot(h, w_ref[:, :XA_W])
    mv = _dot(h, w_ref[:, XA_W:])
    mk_ref[...] = mk
    mv_ref[...] = mv
    mkb_ref[...] = mk.astype(BF16)
    mvb_ref[...] = mv.astype(BF16)


def _mem_kv(mem2d, g, w_bf16, tm):
    t = mem2d.shape[0]
    row = pl.BlockSpec((tm, XA_W), lambda i: (i, 0))
    return pl.pallas_call(
        _mem_kv_kernel,
        grid=(t // tm,),
        in_specs=[pl.BlockSpec((tm, D_MODEL), lambda i: (i, 0)), _resident((1, D_MODEL)),
                  _resident((D_MODEL, 2 * XA_W))],
        out_specs=[row, row, row, row],
        out_shape=[jax.ShapeDtypeStruct((t, XA_W), F32), jax.ShapeDtypeStruct((t, XA_W), F32),
                   jax.ShapeDtypeStruct((t, XA_W), BF16), jax.ShapeDtypeStruct((t, XA_W), BF16)],
        compiler_params=_cparams("parallel"),
        name="mem_kv",
    )(mem2d, g, w_bf16)


def _rel_bucket(rel):
    half = N_BUCKETS // 2
    max_exact = half // 2
    n = jnp.abs(rel)
    nf = jnp.maximum(n, 1).astype(F32)
    large = max_exact + (jnp.log(nf / max_exact) / math.log(MAX_DISTANCE / max_exact)
                         * (half - max_exact)).astype(jnp.int32)
    large = jnp.minimum(large, half - 1)
    return jnp.where(rel > 0, half, 0) + jnp.where(n < max_exact, n, large)


def _table_lanes(rel_table):
    return jnp.pad(rel_table.T, ((0, 0), (0, 128 - N_BUCKETS)))


def _bias_from_buckets(bkt, tabt_ref, head, vis, shift_bucket):
    rows, width = bkt.shape
    table = jnp.broadcast_to(tabt_ref[head:head + 1, :], (rows, 128))
    bias = jnp.concatenate([jnp.take_along_axis(table, bkt[:, c:c + 128], axis=1) for c in range(0, width, 128)],
                           axis=1)
    if shift_bucket is not None:
        bias = bias - tabt_ref[head:head + 1, shift_bucket:shift_bucket + 1]
    return jnp.where(vis, bias * LOG2E, MASK_VALUE)


def _split_maps(qh):
    lane = lax.broadcasted_iota(jnp.int32, qh.shape, 1)
    zero = jnp.zeros_like(qh)
    return jnp.concatenate([jnp.where(lane < DA_HD, qh, zero), jnp.where(lane >= DA_HD, qh, zero)], axis=0)


def _ones_column(n):
    return jnp.where(lax.broadcasted_iota(jnp.int32, (n, DA_V), 1) == 0, 1.0, 0.0).astype(BF16)


def _diff_finish(o, lam, g, tq, lambda_init):
    d = o[:tq] - lam * o[tq:]
    return _rms(d, g) * (1.0 - lambda_init)


def _attn_self_kernel(tab_ref, lam_ref, qt_ref, k_ref, vt_ref, bkt_ref, g_ref, *rest, tq, lambda_init, n_cast):
    w_refs, o_ref, wb_refs = rest[:n_cast], rest[n_cast], rest[n_cast + 1:2 * n_cast + 1]
    bias_sc, q2t_sc, m_sc, l_sc, acc_sc, s_sc = rest[2 * n_cast + 1:]
    for w_ref, wb_ref in zip(w_refs, wb_refs):
        wb_ref[...] = w_ref[...].astype(BF16)

    b = pl.program_id(0)
    i = pl.program_id(1)

    @pl.when((b == 0) & (i == 0))
    def _build_bias():
        bkt = bkt_ref[...]
        kpos = lax.broadcasted_iota(jnp.int32, bkt.shape, 0)
        qpos = lax.broadcasted_iota(jnp.int32, bkt.shape, 1)
        vis = jnp.right_shift(kpos - tq, CHUNK_SHIFT) <= jnp.right_shift(qpos, CHUNK_SHIFT)
        for h in range(DA_HEADS):
            bias_sc[h] = _bias_from_buckets(bkt, tab_ref, h, vis, FAR_BUCKET)

    feat = lax.broadcasted_iota(jnp.int32, (2 * DA_HD, tq), 0)
    for h in range(DA_HEADS):
        qt = qt_ref[h * 2 * DA_HD:(h + 1) * 2 * DA_HD, :]
        zero = jnp.zeros_like(qt)
        q2t_sc[h] = jnp.concatenate([jnp.where(feat < DA_HD, qt, zero), jnp.where(feat >= DA_HD, qt, zero)], axis=1)
    m_sc[...] = jnp.full(m_sc.shape, MASK_VALUE, F32)
    l_sc[...] = jnp.zeros(l_sc.shape, F32)
    acc_sc[...] = jnp.zeros(acc_sc.shape, F32)

    def logits(h, start, nk):
        kj = k_ref[pl.ds(start, nk), h * DA_V:(h + 1) * DA_V]
        return _dot(kj, q2t_sc[h])

    def update(h, start, nk, s, bias):
        vtj = vt_ref[h * DA_V:(h + 1) * DA_V, pl.ds(start, nk)]
        if bias is not None:
            s = s + jnp.concatenate([bias, bias], axis=1)
        m_prev = m_sc[h]
        m_new = jnp.maximum(m_prev, jnp.max(s, axis=0, keepdims=True))
        alpha = jnp.exp2(m_prev - m_new)
        p = jnp.exp2(s - m_new[0:1])
        l_sc[h] = alpha * l_sc[h] + jnp.sum(p, axis=0, keepdims=True)
        acc_sc[h] = alpha[0:1] * acc_sc[h] + _dot(vtj, p.astype(BF16))
        m_sc[h] = m_new

    def key_tile(start, nk, bias_of, after=None):
        nslot = ATTN_LOOKAHEAD + 1
        for h in range(ATTN_LOOKAHEAD):
            s_sc[h % nslot, :nk] = logits(h, start, nk)
        for h in range(DA_HEADS):
            if h + ATTN_LOOKAHEAD < DA_HEADS:
                s_sc[(h + ATTN_LOOKAHEAD) % nslot, :nk] = logits(h + ATTN_LOOKAHEAD, start, nk)
            update(h, start, nk, s_sc[h % nslot, :nk], bias_of(h))
            if after is not None:
                after(h)

    n_far = jnp.maximum(i - 1, 0)

    def far_pair(j, carry):
        key_tile(pl.multiple_of(j * 2 * tq, 2 * tq), 2 * tq, lambda h: None)
        return carry

    lax.fori_loop(0, n_far // 2, far_pair, 0)

    @pl.when(n_far % 2 == 1)
    def _odd_far_tile():
        key_tile(pl.multiple_of((n_far - 1) * tq, tq), tq, lambda h: None)

    lam = lam_ref[0]
    g = g_ref[...]

    def finish(h):
        o = acc_sc[h] / l_sc[h][0:1]
        d = o[:, :tq] - lam * o[:, tq:]
        y = d * lax.rsqrt(jnp.mean(d * d, axis=0, keepdims=True) + EPS) * g * (1.0 - lambda_init)
        o_ref[:, h * DA_V:(h + 1) * DA_V] = y.T.astype(BF16)

    @pl.when(i >= 1)
    def _prev_and_diagonal():
        key_tile(pl.multiple_of((i - 1) * tq, tq), 2 * tq, lambda h: bias_sc[h], after=finish)

    @pl.when(i == 0)
    def _diagonal_only():
        key_tile(0, tq, lambda h: bias_sc[h, tq:, :], after=finish)


def _attn_self(qtb, kb, vtb, rel_table, lam, subln_g, batch, seq, lambda_init, tq, cast_weights):
    assert seq % tq == 0 and tq % 128 == 0 and tq >= MAX_DISTANCE and tq % CHUNK == 0
    nq = seq // tq
    rel = jnp.arange(2 * tq, dtype=jnp.int32)[:, None] - (tq + jnp.arange(tq, dtype=jnp.int32))[None, :]
    bkt = _rel_bucket(rel)
    smem = pl.BlockSpec(memory_space=pltpu.SMEM)

    cast_specs = _cast_block_specs(cast_weights, batch * nq, lambda b, i: b * nq + i)
    n_cast = len(cast_weights)
    outs = pl.pallas_call(
        functools.partial(_attn_self_kernel, tq=tq, lambda_init=lambda_init, n_cast=n_cast),
        grid=(batch, nq),
        in_specs=[_resident((DA_HEADS, 128)), smem,
                  pl.BlockSpec((DA_QK_W, tq), lambda b, i: (b, i)),
                  pl.BlockSpec((seq, DA_QK_W), lambda b, i: (b, 0)),
                  pl.BlockSpec((DA_V_W, seq), lambda b, i: (b, 0)),
                  _resident((2 * tq, tq)), _resident((DA_V, 1))] + cast_specs,
        out_specs=[pl.BlockSpec((tq, DA_V_W), lambda b, i: (b * nq + i, 0))] + cast_specs,
        out_shape=[jax.ShapeDtypeStruct((batch * seq, DA_V_W), BF16)]
                  + [jax.ShapeDtypeStruct(w.shape, BF16) for w in cast_weights],
        scratch_shapes=[pltpu.VMEM((DA_HEADS, 2 * tq, tq), F32),
                        pltpu.VMEM((DA_HEADS, 2 * DA_HD, 2 * tq), BF16),
                        pltpu.VMEM((DA_HEADS, 8, 2 * tq), F32),
                        pltpu.VMEM((DA_HEADS, 8, 2 * tq), F32),
                        pltpu.VMEM((DA_HEADS, DA_V, 2 * tq), F32),
                        pltpu.VMEM((ATTN_LOOKAHEAD + 1, 2 * tq, 2 * tq), F32)],
        compiler_params=_cparams("arbitrary", "arbitrary"),
        name="attn_self",
    )(_table_lanes(rel_table), lam, qtb, kb, vtb, bkt, subln_g.reshape(DA_V, 1), *cast_weights)
    return outs[0], outs[1:]


def _attn_cached_kernel(tab_ref, lam_ref, q_ref, ckt_ref, cv_ref, nk_ref, nv_ref, bktc_ref, bktn_ref, g_ref, o_ref,
                        biasc_sc, biasn_sc, *, sq, past, lambda_init):
    @pl.when(pl.program_id(0) == 0)
    def _build_bias():
        for bkt_ref, sc, k0 in ((bktc_ref, biasc_sc, 0), (bktn_ref, biasn_sc, past)):
            bkt = bkt_ref[...]
            qpos = past + lax.broadcasted_iota(jnp.int32, bkt.shape, 0)
            kpos = k0 + lax.broadcasted_iota(jnp.int32, bkt.shape, 1)
            vis = jnp.right_shift(kpos, CHUNK_SHIFT) <= jnp.right_shift(qpos, CHUNK_SHIFT)
            for h in range(DA_HEADS):
                sc[h] = _bias_from_buckets(bkt, tab_ref, h, vis, None)[:, :sc.shape[2]]

    q = q_ref[...]
    lam = lam_ref[0]
    g = g_ref[...]
    for h in range(DA_HEADS):
        cols = slice(h * DA_V, (h + 1) * DA_V)
        q2 = _split_maps(q[:, cols])
        s_c = _dot(q2, ckt_ref[cols, :].astype(BF16))
        s_c = (s_c.reshape(2, sq, past) + biasc_sc[h][None]).reshape(2 * sq, past)
        s_n = lax.dot_general(q2, nk_ref[:, cols].astype(BF16), NT_DIMS, preferred_element_type=F32)
        s_n = (s_n.reshape(2, sq, sq) + biasn_sc[h][None]).reshape(2 * sq, sq)
        m = jnp.maximum(jnp.max(s_c, axis=1, keepdims=True), jnp.max(s_n, axis=1, keepdims=True))
        p_c = jnp.exp2(s_c - m)
        p_n = jnp.exp2(s_n - m)
        l = jnp.sum(p_c, axis=1, keepdims=True) + jnp.sum(p_n, axis=1, keepdims=True)
        v_c = cv_ref[pl.ds(h, past, stride=DA_HEADS), :]
        o = (_dot(p_c.astype(BF16), v_c.astype(BF16))
             + _dot(p_n.astype(BF16), nv_ref[:, cols].astype(BF16))) / l
        o_ref[:, cols] = _diff_finish(o, lam, g, sq, lambda_init).astype(BF16)


def _attn_cached(q, cache_kt2d, cache_v2d, nk, nv, rel_table, lam, subln_g, batch, sq, past, lambda_init):
    qpos = past + jnp.arange(sq, dtype=jnp.int32)
    bkt_c = _rel_bucket(jnp.arange(past, dtype=jnp.int32)[None, :] - qpos[:, None])
    assert sq <= 128 and past % 128 == 0
    bkt_n = jnp.pad(_rel_bucket(qpos[None, :] - qpos[:, None]), ((0, 0), (0, 128 - sq)))
    smem = pl.BlockSpec(memory_space=pltpu.SMEM)
    new_spec = pl.BlockSpec((sq, DA_QK_W), lambda b: (b, 0))
    return pl.pallas_call(
        functools.partial(_attn_cached_kernel, sq=sq, past=past, lambda_init=lambda_init),
        grid=(batch,),
        in_specs=[_resident((DA_HEADS, 128)), smem, new_spec,
                  pl.BlockSpec((DA_QK_W, past), lambda b: (b, 0)),
                  pl.BlockSpec((past * DA_HEADS, DA_V), lambda b: (b, 0)),
                  new_spec, new_spec,
                  _resident((sq, past)), _resident((sq, 128)), _resident((1, DA_V))],
        out_specs=new_spec,
        out_shape=jax.ShapeDtypeStruct((batch * sq, DA_V_W), BF16),
        scratch_shapes=[pltpu.VMEM((DA_HEADS, sq, past), F32), pltpu.VMEM((DA_HEADS, sq, sq), F32)],
        compiler_params=_cparams("arbitrary"),
        name="attn_cached",
    )(_table_lanes(rel_table), lam, q, cache_kt2d, cache_v2d, nk, nv, bkt_c, bkt_n, subln_g)


def _mix_ffn_kernel(x_ref, a_ref, b_ref, qc_ref, mk_ref, mv_ref, gmix_ref, wg_ref, bg_ref, wpa_ref, wpb_ref, wpc_ref,
                    wo_ref, gffn_ref, wfi_ref, wfo_ref, gfin_ref, y_ref, *, seg):
    tm = x_ref.shape[0]
    nseg = tm // seg
    n_mem = mk_ref.shape[0] // nseg

    def xattn_head(hh):
        cols = slice(hh * XA_HD, (hh + 1) * XA_HD)
        outs = []
        for s in range(nseg):
            mem = slice(s * n_mem, (s + 1) * n_mem)
            logit = lax.dot_general(qc_ref[s * seg:(s + 1) * seg, cols], mk_ref[mem, cols].astype(BF16), NT_DIMS,
                                    preferred_element_type=F32)
            p = jnp.exp(logit - jnp.max(logit, axis=1, keepdims=True))
            l = jnp.sum(p, axis=1, keepdims=True)
            outs.append((_dot(p.astype(BF16), mv_ref[mem, cols].astype(BF16)) / l).astype(BF16))
        return outs[0] if nseg == 1 else jnp.concatenate(outs, axis=0)

    x = x_ref[...]
    h = _rms(x, gmix_ref[...]).astype(BF16)

    def branch(n, br, wp_ref):
        cols = slice(n * D_MODEL, (n + 1) * D_MODEL)
        gate = jax.nn.sigmoid(_dot(h, wg_ref[:, cols]) + bg_ref[:, cols])
        return gate * _dot(br, wp_ref[...])

    merged = branch(0, a_ref[...], wpa_ref)
    c_heads = [xattn_head(hh) for hh in range(XA_HEADS // 2)]
    merged = merged + branch(1, b_ref[...], wpb_ref)
    c_heads += [xattn_head(hh) for hh in range(XA_HEADS // 2, XA_HEADS)]
    merged = merged + branch(2, jnp.concatenate(c_heads, axis=1), wpc_ref)
    x1 = x + _dot(merged.astype(BF16), wo_ref[...])
    h2 = _rms(x1, gffn_ref[...]).astype(BF16)
    ff = None
    for c0, c1 in zip(FFN_SPLITS[:-1], FFN_SPLITS[1:]):
        gcols = slice(c0, c1)
        ucols = slice(D_FF + c0, D_FF + c1)
        act = (jax.nn.silu(_dot(h2, wfi_ref[:, gcols])) * _dot(h2, wfi_ref[:, ucols])).astype(BF16)
        term = _dot(act, wfo_ref[gcols, :])
        ff = term if ff is None else ff + term
    y_ref[...] = _rms(x1 + ff, gfin_ref[...])


def _mix_ffn(x2d, a_out, b_out, qc, mk2d, mv2d, p, tm, seq):
    t = x2d.shape[0]
    seg = min(tm, seq)
    nseg = tm // seg
    nt = seq // seg
    n_mem = mk2d.shape[0] // (t // seq)
    assert tm % seg == 0 and seq % seg == 0
    row = pl.BlockSpec((tm, D_MODEL), lambda i: (i, 0))
    mem = pl.BlockSpec((nseg * n_mem, XA_W), lambda i: (i // nt, 0))
    vec = _resident((1, D_MODEL))
    sq_w = _resident((D_MODEL, D_MODEL))
    return pl.pallas_call(
        functools.partial(_mix_ffn_kernel, seg=seg),
        grid=(t // tm,),
        in_specs=[row, row, row, row, mem, mem, vec, _resident((D_MODEL, N_BRANCH * D_MODEL)),
                  _resident((1, N_BRANCH * D_MODEL)), sq_w, sq_w, sq_w, sq_w, vec,
                  _resident((D_MODEL, 2 * D_FF)), _resident((D_FF, D_MODEL)), vec],
        out_specs=row,
        out_shape=jax.ShapeDtypeStruct((t, D_MODEL), F32),
        compiler_params=_cparams("parallel"),
        name="mix_ffn",
    )(x2d, a_out, b_out, qc, mk2d, mv2d, p["norm_mix"], p["w_gate"], p["b_gate"], p["w_proj_a"], p["w_proj_b"],
      p["w_proj_c"], p["w_out"], p["norm_ffn"], p["w_ffn_in"], p["w_ffn_out"], p["norm_final"])


def _layer(x, mk2d, mv2d, cache_kt2d, cache_v2d, conv_state, h0, rel_table, lam, p, lambda_init, tm, tq, tm_out):
    batch, seq, _ = x.shape
    x2d = x.reshape(batch * seq, D_MODEL)
    no_history = cache_kt2d is None
    q, k, v, kb, vb, qc, b_out, new_conv, h_last = _in_proj(x2d, conv_state, h0, p, tm, seq, no_history)
    if no_history:
        a_out, out_w = _attn_self(q, kb, vb, rel_table, lam, p["subln_g"], batch, seq, lambda_init, tq,
                                  [p[n] for n in OUT_WEIGHTS])
        p = {**p, **dict(zip(OUT_WEIGHTS, out_w))}
    else:
        past = cache_kt2d.shape[1]
        a_out = _attn_cached(q, cache_kt2d, cache_v2d, k, v, rel_table, lam, p["subln_g"], batch, seq, past,
                             lambda_init)
    y = _mix_ffn(x2d, a_out, b_out, qc, mk2d, mv2d, p, tm_out, seq)
    return y.reshape(batch, seq, D_MODEL), k, v, new_conv, h_last.reshape(batch, LRU_W), p


def kernel(x_prompt, x_sample, mem_prompt, cache_k, cache_v, state_conv, state_lru, cache_mem_k, cache_mem_v,
           rel_table, norm_mix, w_in, lambda_q1, lambda_k1, lambda_q2, lambda_k2, subln_g, conv_w, conv_b,
           w_rg_a, b_rg_a, w_rg_x, b_rg_x, rg_lambda, norm_mem, w_mem_kv, w_proj_a, w_proj_b, w_proj_c,
           w_gate, b_gate, w_out, norm_ffn, w_ffn_in, w_ffn_out, norm_final):
    depth = w_in.shape[0]
    assert depth == 1, "the final norm is fused into the layer's last kernel"
    l = 0
    lambda_init = 0.8 - 0.6 * math.exp(-0.3 * l)
    bp, sp_, _ = x_prompt.shape
    bs, ss, _ = x_sample.shape
    past = cache_k.shape[2]
    n_mem = mem_prompt.shape[1]
    row = lambda a: a.reshape(1, -1).astype(F32)
    mk, mv, mkb, mvb = _mem_kv(mem_prompt.reshape(bp * n_mem, D_MODEL), row(norm_mem[l]),
                               w_mem_kv[l].astype(BF16), 256)
    p = dict(norm_mix=row(norm_mix[l]), w_in=w_in[l].astype(BF16), subln_g=row(subln_g[l]),
             conv_w=conv_w[l], conv_b=row(conv_b[l]),
             w_rg=jnp.concatenate([w_rg_a[l], w_rg_x[l]], axis=-1).astype(BF16),
             b_rg_a=row(b_rg_a[l]), b_rg_x=row(b_rg_x[l]), rg_lambda=row(rg_lambda[l]),
             w_proj_a=w_proj_a[l], w_proj_b=w_proj_b[l], w_proj_c=w_proj_c[l], w_gate=w_gate[l],
             b_gate=row(b_gate[l]), w_out=w_out[l], norm_ffn=row(norm_ffn[l]), w_ffn_in=w_ffn_in[l],
             w_ffn_out=w_ffn_out[l], norm_final=row(norm_final))
    lam = (jnp.exp(jnp.sum(lambda_q1[l] * lambda_k1[l]).astype(F32))
           - jnp.exp(jnp.sum(lambda_q2[l] * lambda_k2[l]).astype(F32)) + lambda_init).reshape(1)

    zeros_conv = jnp.zeros((bp, CONV_W - 1, LRU_W), F32)
    zeros_h = jnp.zeros((bp, 1, LRU_W), F32)
    yp, kp, vp, cp, hp, p = _layer(x_prompt, mkb, mvb, None, None, zeros_conv, zeros_h, rel_table, lam, p,
                                   lambda_init, tm=512, tq=256, tm_out=512)
    cache_kt = jnp.transpose(cache_k[l], (0, 2, 3, 4, 1)).reshape(bs * DA_QK_W, past)
    ys, ks, vs, cs, hs, _ = _layer(x_sample, cache_mem_k[l].reshape(bs * n_mem, XA_W),
                                   cache_mem_v[l].reshape(bs * n_mem, XA_W),
                                   cache_kt, cache_v[l].reshape(bs * past * DA_HEADS, DA_V),
                                   state_conv[l], state_lru[l].reshape(bs, 1, LRU_W), rel_table, lam, p,
                                   lambda_init, tm=256, tq=ss, tm_out=128)
    kp = jnp.transpose(kp.reshape(bp, DA_HEADS, 2, DA_HD, sp_), (0, 4, 1, 2, 3))
    return (yp, ys,
            kp[None], vp.reshape(1, bp, sp_, DA_HEADS, DA_V),
            cp[None], hp[None],
            mk.reshape(1, bp, n_mem, XA_HEADS, XA_HD), mv.reshape(1, bp, n_mem, XA_HEADS, XA_HD),
            ks.reshape(1, bs, ss, DA_HEADS, 2, DA_HD), vs.reshape(1, bs, ss, DA_HEADS, DA_V),
            cs[None], hs[None])
```

```python
import functools
import math

import jax
import jax.numpy as jnp
from jax import lax
from jax.experimental import pallas as pl
from jax.experimental.pallas import tpu as pltpu

F32 = jnp.float32
BF16 = jnp.bfloat16

D_MODEL = 1024
CHUNK = 64
CHUNK_SHIFT = 6
DA_HEADS = 8
DA_HD = 64
DA_V = 2 * DA_HD
LRU_W = D_MODEL
LRU_BLOCKS = 8
LRU_BW = LRU_W // LRU_BLOCKS
CONV_W = 4
LRU_C = 8.0
XA_HEADS = 4
XA_HD = 256
N_BUCKETS = 32
MAX_DISTANCE = 128
D_FF = ((8 * D_MODEL + 3 * 256 - 1) // (3 * 256)) * 256
N_BRANCH = 3
EPS = 1e-6
DA_QK_W = DA_HEADS * 2 * DA_HD
DA_V_W = DA_HEADS * DA_V
XA_W = XA_HEADS * XA_HD
IN_OFFS = (0, DA_QK_W, 2 * DA_QK_W, 2 * DA_QK_W + DA_V_W, 2 * DA_QK_W + DA_V_W + LRU_W,
           2 * DA_QK_W + DA_V_W + 2 * LRU_W)
IN_W = 2 * DA_QK_W + DA_V_W + 2 * LRU_W + XA_W
LOG2E = math.log2(math.e)
DA_Q_SCALE = DA_HD ** -0.5 * LOG2E
XA_SCALE = XA_HD ** -0.5
MASK_VALUE = -1e30
FAR_BUCKET = N_BUCKETS // 2 - 1
V7X_MXU_DIM = 256
FFN_SPLITS = (0, -(-(D_FF // V7X_MXU_DIM) // 2) * V7X_MXU_DIM, D_FF)
PROJ_CHUNK = V7X_MXU_DIM
OUT_WEIGHTS = ("w_gate", "w_proj_a", "w_proj_b", "w_proj_c", "w_out", "w_ffn_in", "w_ffn_out")
SCAN_GROUP = 8
BF16_SUBLANES = 16
ATTN_LOOKAHEAD = 3

V7X_VMEM_BYTES = 64 * 1024 * 1024
VMEM_LIMIT = V7X_VMEM_BYTES - 8 * 1024 * 1024

NT_DIMS = (((1,), (1,)), ((), ()))


def _cparams(*sem):
    return pltpu.CompilerParams(dimension_semantics=sem, vmem_limit_bytes=VMEM_LIMIT)


def _resident(shape):
    nd = len(shape)
    return pl.BlockSpec(shape, lambda *_: (0,) * nd, pipeline_mode=pl.Buffered(1))


def _rms(x, g):
    return x * lax.rsqrt(jnp.mean(x * x, axis=-1, keepdims=True) + EPS) * g


def _dot(a, b):
    return jnp.dot(a, b, preferred_element_type=F32)


def _in_proj_kernel(x_ref, g_ref, w_ref, cs_ref, h0_ref, cw_ref, cb_ref, wrg_ref, ba_ref, bx_ref, lam_ref,
                    q_ref, k_ref, v_ref, kb_ref, vb_ref, qc_ref, bo_ref, nc_ref, hl_ref,
                    xpad_sc, a_sc, u_sc, hs_sc, gg_sc, hc_sc, p_sc, q_sc, hin_sc, *, nt, seg, feature_major):
    tm = x_ref.shape[0]
    nseg = tm // seg
    two_level = nseg == 1
    groups = tm // SCAN_GROUP
    halo = CONV_W - 1
    base = 8 - halo

    @pl.when(pl.program_id(0) % nt == 0)
    def _load_state():
        for s in range(nseg):
            xpad_sc[s, base:8, :] = cs_ref[s]
            for n in range(LRU_BLOCKS):
                hc_sc[s, n] = h0_ref[s][:, n * LRU_BW:(n + 1) * LRU_BW]

    h = _rms(x_ref[...], g_ref[...]).astype(BF16)

    def proj(n, c0, width):
        return _dot(h, w_ref[:, IN_OFFS[n] + c0:IN_OFFS[n] + c0 + width])

    xb = proj(3, 0, LRU_W)
    for s in range(nseg):
        xpad_sc[s, 8:8 + seg, :] = xb[s * seg:(s + 1) * seg]

    def proj_chunk(kind, c0):
        cols = slice(c0, c0 + PROJ_CHUNK)
        if kind == "g":
            gg_sc[:, cols] = jax.nn.gelu(proj(4, c0, PROJ_CHUNK))
        elif kind == "q":
            qs = proj(0, c0, PROJ_CHUNK) * DA_Q_SCALE
            if feature_major:
                q_ref[cols, :] = qs.T.astype(BF16)
            else:
                q_ref[:, cols] = qs.astype(BF16)
        elif kind == "k":
            kc = proj(1, c0, PROJ_CHUNK)
            kb_ref[:, cols] = kc.astype(BF16)
            if feature_major:
                k_ref[cols, :] = kc.T
            else:
                k_ref[:, cols] = kc
        elif kind == "v":
            vc = proj(2, c0, PROJ_CHUNK)
            v_ref[:, cols] = vc
            if feature_major:
                vb_ref[cols, :] = vc.T.astype(BF16)
            else:
                vb_ref[:, cols] = vc.astype(BF16)
        else:
            qc_ref[:, cols] = (proj(5, c0, PROJ_CHUNK) * XA_SCALE).astype(BF16)

    chunks = [(kind, c0) for kind, width in (("g", LRU_W), ("q", DA_QK_W), ("k", DA_QK_W), ("v", DA_V_W), ("c", XA_W))
              for c0 in range(0, width, PROJ_CHUNK)]
    n_early = (LRU_W + 2 * DA_QK_W) // PROJ_CHUNK
    early, late = chunks[:n_early], chunks[n_early:]

    def share(items, n):
        return items[n * len(items) // LRU_BLOCKS:(n + 1) * len(items) // LRU_BLOCKS]

    cw = cw_ref[...]
    sp = jax.nn.softplus(-lam_ref[...])
    for n in range(LRU_BLOCKS):
        for kind, c0 in share(early, n):
            proj_chunk(kind, c0)
        cols = slice(n * LRU_BW, (n + 1) * LRU_BW)
        xc_segs = []
        for s in range(nseg):
            xc = xpad_sc[s, base:base + seg, cols] * cw[0:1, cols]
            for j in range(1, CONV_W):
                xc = xc + xpad_sc[s, base + j:base + j + seg, cols] * cw[j:j + 1, cols]
            xc_segs.append(cb_ref[:, cols] + xc)
        xc = xc_segs[0] if nseg == 1 else jnp.concatenate(xc_segs, axis=0)
        xcb = xc.astype(BF16)
        pre = _dot(xcb, wrg_ref[n])
        tanh_r = jnp.tanh(0.5 * (pre[:, :LRU_BW] + ba_ref[:, cols]))
        tanh_i = jnp.tanh(0.5 * (pre[:, LRU_BW:] + bx_ref[:, cols]))
        half_c_sp = (-0.5 * LRU_C) * sp[:, cols]
        log_a = half_c_sp * tanh_r + half_c_sp
        a = jnp.exp(log_a)
        a_sc[n] = a
        one_minus_a2 = -jnp.tanh(log_a) * (a * a + 1.0)
        mult = jnp.where(one_minus_a2 > 0.0, one_minus_a2 * lax.rsqrt(one_minus_a2), 0.0)
        u_sc[n] = mult * ((0.5 * tanh_i + 0.5) * xc)
        if two_level:
            for r in range(SCAN_GROUP):
                a_r = a_sc[n, pl.ds(r, groups, stride=SCAN_GROUP), :]
                u_r = u_sc[n, pl.ds(r, groups, stride=SCAN_GROUP), :]
                p_r, q_r = (a_r, u_r) if r == 0 else (a_r * p_r, a_r * q_r + u_r)
                p_sc[n, r] = p_r
                q_sc[n, r] = q_r
    for s in range(nseg):
        tail = xpad_sc[s, base + seg:8 + seg, :]
        nc_ref[s] = tail
        xpad_sc[s, base:8, :] = tail

    if two_level:
        def group_step(gi, h_in):
            hin_sc[:, pl.ds(gi, 1), :] = h_in
            last = SCAN_GROUP - 1
            return p_sc[:, last, pl.ds(gi, 1), :] * h_in + q_sc[:, last, pl.ds(gi, 1), :]

        hlast = lax.fori_loop(0, groups, group_step, hc_sc[0], unroll=8)
        hc_sc[0] = hlast
        hl_ref[0] = jnp.concatenate([hlast[n] for n in range(LRU_BLOCKS)], axis=1)
    else:
        for s in range(nseg):
            def step(t, hprev, row0=s * seg):
                hnew = a_sc[:, pl.ds(row0 + t, 1), :] * hprev + u_sc[:, pl.ds(row0 + t, 1), :]
                hs_sc[:, pl.ds(row0 + t, 1), :] = hnew
                return hnew

            hlast = lax.fori_loop(0, seg, step, hc_sc[s], unroll=8)
            hc_sc[s] = hlast
            hl_ref[s] = jnp.concatenate([hlast[n] for n in range(LRU_BLOCKS)], axis=1)
    for n in range(LRU_BLOCKS):
        for kind, c0 in share(late, n):
            proj_chunk(kind, c0)
        if two_level:
            h_in = hin_sc[n]
            for r in range(SCAN_GROUP):
                hs_sc[n, pl.ds(r, groups, stride=SCAN_GROUP), :] = p_sc[n, r] * h_in + q_sc[n, r]
        cols = slice(n * LRU_BW, (n + 1) * LRU_BW)
        bo_ref[:, cols] = (hs_sc[n] * gg_sc[:, cols]).astype(BF16)


def _in_proj(x2d, conv_state, h0, p, tm, seq, feature_major):
    t = x2d.shape[0]
    seg = min(tm, seq)
    nseg = tm // seg
    nt = seq // seg
    assert tm % seg == 0 and seq % seg == 0 and seg % 8 == 0 and seg >= CONV_W - 1
    batch_block = lambda i: (i // nt, 0, 0)
    row = lambda width: pl.BlockSpec((tm, width), lambda i: (i, 0))
    sds = lambda width, dt: jax.ShapeDtypeStruct((t, width), dt)
    if feature_major:
        assert nseg == 1
        fm_spec = lambda width: pl.BlockSpec((width, tm), lambda i: (i // nt, i % nt))
        fm_sds = lambda width, dt: jax.ShapeDtypeStruct((t // seq * width, seq), dt)
    else:
        fm_spec, fm_sds = row, sds
    vec = _resident((1, LRU_W))
    gate_w = _resident((LRU_BLOCKS, LRU_BW, 2 * LRU_BW))
    slab = pltpu.VMEM((LRU_BLOCKS, tm, LRU_BW), F32)
    conv_spec = pl.BlockSpec((nseg, CONV_W - 1, LRU_W), batch_block)
    state_spec = pl.BlockSpec((nseg, 1, LRU_W), batch_block)
    return pl.pallas_call(
        functools.partial(_in_proj_kernel, nt=nt, seg=seg, feature_major=feature_major),
        grid=(t // tm,),
        in_specs=[row(D_MODEL), _resident((1, D_MODEL)), _resident((D_MODEL, IN_W)), conv_spec, state_spec,
                  _resident((CONV_W, LRU_W)), vec, gate_w, vec, vec, vec],
        out_specs=[fm_spec(DA_QK_W), fm_spec(DA_QK_W), row(DA_V_W), row(DA_QK_W), fm_spec(DA_V_W), row(XA_W),
                   row(LRU_W), conv_spec, state_spec],
        out_shape=[fm_sds(DA_QK_W, BF16), fm_sds(DA_QK_W, F32), sds(DA_V_W, F32), sds(DA_QK_W, BF16),
                   fm_sds(DA_V_W, BF16), sds(XA_W, BF16), sds(LRU_W, BF16),
                   jax.ShapeDtypeStruct(conv_state.shape, F32), jax.ShapeDtypeStruct(h0.shape, F32)],
        scratch_shapes=[pltpu.VMEM((nseg, 8 + seg, LRU_W), F32), slab, slab, slab, pltpu.VMEM((tm, LRU_W), F32),
                        pltpu.VMEM((nseg, LRU_BLOCKS, 1, LRU_BW), F32),
                        pltpu.VMEM((LRU_BLOCKS, SCAN_GROUP, tm // SCAN_GROUP, LRU_BW), F32),
                        pltpu.VMEM((LRU_BLOCKS, SCAN_GROUP, tm // SCAN_GROUP, LRU_BW), F32),
                        pltpu.VMEM((LRU_BLOCKS, tm // SCAN_GROUP, LRU_BW), F32)],
        compiler_params=_cparams("arbitrary"),
        name="in_proj",
    )(x2d, p["norm_mix"], p["w_in"], conv_state, h0, p["conv_w"], p["conv_b"], p["w_rg"], p["b_rg_a"],
      p["b_rg_x"], p["rg_lambda"])


def _cast_block_specs(weights, steps, step_index):
    specs = []
    for w in weights:
        rows, cols = w.shape
        nblk = max(d for d in range(1, steps + 1)
                   if steps % d == 0 and rows % d == 0 and (rows // d) % BF16_SUBLANES == 0)
        specs.append(pl.BlockSpec((rows // nblk, cols), lambda *g, per=steps // nblk: (step_index(*g) // per, 0)))
    return specs


def _mem_kv_kernel(m_ref, g_ref, w_ref, mk_ref, mv_ref, mkb_ref, mvb_ref):
    h = _rms(m_ref[...], g_ref[...]).astype(BF16)
    mk = _dot(h, w_ref[:, :XA_W])
    mv = _dot(h, w_ref[:, XA_W:])
    mk_ref[...] = mk
    mv_ref[...] = mv
    mkb_ref[...] = mk.astype(BF16)
    mvb_ref[...] = mv.astype(BF16)


def _mem_kv(mem2d, g, w_bf16, tm):
    t = mem2d.shape[0]
    row = pl.BlockSpec((tm, XA_W), lambda i: (i, 0))
    return pl.pallas_call(
        _mem_kv_kernel,
        grid=(t // tm,),
        in_specs=[pl.BlockSpec((tm, D_MODEL), lambda i: (i, 0)), _resident((1, D_MODEL)),
                  _resident((D_MODEL, 2 * XA_W))],
        out_specs=[row, row, row, row],
        out_shape=[jax.ShapeDtypeStruct((t, XA_W), F32), jax.ShapeDtypeStruct((t, XA_W), F32),
                   jax.ShapeDtypeStruct((t, XA_W), BF16), jax.ShapeDtypeStruct((t, XA_W), BF16)],
        compiler_params=_cparams("parallel"),
        name="mem_kv",
    )(mem2d, g, w_bf16)


def _rel_bucket(rel):
    half = N_BUCKETS // 2
    max_exact = half // 2
    n = jnp.abs(rel)
    nf = jnp.maximum(n, 1).astype(F32)
    large = max_exact + (jnp.log(nf / max_exact) / math.log(MAX_DISTANCE / max_exact)
                         * (half - max_exact)).astype(jnp.int32)
    large = jnp.minimum(large, half - 1)
    return jnp.where(rel > 0, half, 0) + jnp.where(n < max_exact, n, large)


def _table_lanes(rel_table):
    return jnp.pad(rel_table.T, ((0, 0), (0, 128 - N_BUCKETS)))


def _bias_from_buckets(bkt, tabt_ref, head, vis, shift_bucket):
    rows, width = bkt.shape
    table = jnp.broadcast_to(tabt_ref[head:head + 1, :], (rows, 128))
    bias = jnp.concatenate([jnp.take_along_axis(table, bkt[:, c:c + 128], axis=1) for c in range(0, width, 128)],
                           axis=1)
    if shift_bucket is not None:
        bias = bias - tabt_ref[head:head + 1, shift_bucket:shift_bucket + 1]
    return jnp.where(vis, bias * LOG2E, MASK_VALUE)


def _split_maps(qh):
    lane = lax.broadcasted_iota(jnp.int32, qh.shape, 1)
    zero = jnp.zeros_like(qh)
    return jnp.concatenate([jnp.where(lane < DA_HD, qh, zero), jnp.where(lane >= DA_HD, qh, zero)], axis=0)


def _ones_column(n):
    return jnp.where(lax.broadcasted_iota(jnp.int32, (n, DA_V), 1) == 0, 1.0, 0.0).astype(BF16)


def _diff_finish(o, lam, g, tq, lambda_init):
    d = o[:tq] - lam * o[tq:]
    return _rms(d, g) * (1.0 - lambda_init)


def _attn_self_kernel(tab_ref, lam_ref, qt_ref, k_ref, vt_ref, bkt_ref, g_ref, *rest, tq, lambda_init, n_cast):
    w_refs, o_ref, wb_refs = rest[:n_cast], rest[n_cast], rest[n_cast + 1:2 * n_cast + 1]
    bias_sc, q2t_sc, m_sc, l_sc, acc_sc, s_sc = rest[2 * n_cast + 1:]
    for w_ref, wb_ref in zip(w_refs, wb_refs):
        wb_ref[...] = w_ref[...].astype(BF16)

    b = pl.program_id(0)
    i = pl.program_id(1)

    @pl.when((b == 0) & (i == 0))
    def _build_bias():
        bkt = bkt_ref[...]
        kpos = lax.broadcasted_iota(jnp.int32, bkt.shape, 0)
        qpos = lax.broadcasted_iota(jnp.int32, bkt.shape, 1)
        vis = jnp.right_shift(kpos - tq, CHUNK_SHIFT) <= jnp.right_shift(qpos, CHUNK_SHIFT)
        for h in range(DA_HEADS):
            bias_sc[h] = _bias_from_buckets(bkt, tab_ref, h, vis, FAR_BUCKET)

    feat = lax.broadcasted_iota(jnp.int32, (2 * DA_HD, tq), 0)
    for h in range(DA_HEADS):
        qt = qt_ref[h * 2 * DA_HD:(h + 1) * 2 * DA_HD, :]
        zero = jnp.zeros_like(qt)
        q2t_sc[h] = jnp.concatenate([jnp.where(feat < DA_HD, qt, zero), jnp.where(feat >= DA_HD, qt, zero)], axis=1)
    m_sc[...] = jnp.full(m_sc.shape, MASK_VALUE, F32)
    l_sc[...] = jnp.zeros(l_sc.shape, F32)
    acc_sc[...] = jnp.zeros(acc_sc.shape, F32)

    def logits(h, start, nk):
        kj = k_ref[pl.ds(start, nk), h * DA_V:(h + 1) * DA_V]
        return _dot(kj, q2t_sc[h])

    def update(h, start, nk, s, bias):
        vtj = vt_ref[h * DA_V:(h + 1) * DA_V, pl.ds(start, nk)]
        if bias is not None:
            s = s + jnp.concatenate([bias, bias], axis=1)
        m_prev = m_sc[h]
        m_new = jnp.maximum(m_prev, jnp.max(s, axis=0, keepdims=True))
        alpha = jnp.exp2(m_prev - m_new)
        p = jnp.exp2(s - m_new[0:1])
        l_sc[h] = alpha * l_sc[h] + jnp.sum(p, axis=0, keepdims=True)
        acc_sc[h] = alpha[0:1] * acc_sc[h] + _dot(vtj, p.astype(BF16))
        m_sc[h] = m_new

    def key_tile(start, nk, bias_of, after=None):
        nslot = ATTN_LOOKAHEAD + 1
        for h in range(ATTN_LOOKAHEAD):
            s_sc[h % nslot, :nk] = logits(h, start, nk)
        for h in range(DA_HEADS):
            if h + ATTN_LOOKAHEAD < DA_HEADS:
                s_sc[(h + ATTN_LOOKAHEAD) % nslot, :nk] = logits(h + ATTN_LOOKAHEAD, start, nk)
            update(h, start, nk, s_sc[h % nslot, :nk], bias_of(h))
            if after is not None:
                after(h)

    n_far = jnp.maximum(i - 1, 0)

    def far_pair(j, carry):
        key_tile(pl.multiple_of(j * 2 * tq, 2 * tq), 2 * tq, lambda h: None)
        return carry

    lax.fori_loop(0, n_far // 2, far_pair, 0)

    @pl.when(n_far % 2 == 1)
    def _odd_far_tile():
        key_tile(pl.multiple_of((n_far - 1) * tq, tq), tq, lambda h: None)

    lam = lam_ref[0]
    g = g_ref[...]

    def finish(h):
        o = acc_sc[h] / l_sc[h][0:1]
        d = o[:, :tq] - lam * o[:, tq:]
        y = d * lax.rsqrt(jnp.mean(d * d, axis=0, keepdims=True) + EPS) * g * (1.0 - lambda_init)
        o_ref[:, h * DA_V:(h + 1) * DA_V] = y.T.astype(BF16)

    @pl.when(i >= 1)
    def _prev_and_diagonal():
        key_tile(pl.multiple_of((i - 1) * tq, tq), 2 * tq, lambda h: bias_sc[h], after=finish)

    @pl.when(i == 0)
    def _diagonal_only():
        key_tile(0, tq, lambda h: bias_sc[h, tq:, :], after=finish)


def _attn_self(qtb, kb, vtb, rel_table, lam, subln_g, batch, seq, lambda_init, tq, cast_weights):
    assert seq % tq == 0 and tq % 128 == 0 and tq >= MAX_DISTANCE and tq % CHUNK == 0
    nq = seq // tq
    rel = jnp.arange(2 * tq, dtype=jnp.int32)[:, None] - (tq + jnp.arange(tq, dtype=jnp.int32))[None, :]
    bkt = _rel_bucket(rel)
    smem = pl.BlockSpec(memory_space=pltpu.SMEM)

    cast_specs = _cast_block_specs(cast_weights, batch * nq, lambda b, i: b * nq + i)
    n_cast = len(cast_weights)
    outs = pl.pallas_call(
        functools.partial(_attn_self_kernel, tq=tq, lambda_init=lambda_init, n_cast=n_cast),
        grid=(batch, nq),
        in_specs=[_resident((DA_HEADS, 128)), smem,
                  pl.BlockSpec((DA_QK_W, tq), lambda b, i: (b, i)),
                  pl.BlockSpec((seq, DA_QK_W), lambda b, i: (b, 0)),
                  pl.BlockSpec((DA_V_W, seq), lambda b, i: (b, 0)),
                  _resident((2 * tq, tq)), _resident((DA_V, 1))] + cast_specs,
        out_specs=[pl.BlockSpec((tq, DA_V_W), lambda b, i: (b * nq + i, 0))] + cast_specs,
        out_shape=[jax.ShapeDtypeStruct((batch * seq, DA_V_W), BF16)]
                  + [jax.ShapeDtypeStruct(w.shape, BF16) for w in cast_weights],
        scratch_shapes=[pltpu.VMEM((DA_HEADS, 2 * tq, tq), F32),
                        pltpu.VMEM((DA_HEADS, 2 * DA_HD, 2 * tq), BF16),
                        pltpu.VMEM((DA_HEADS, 8, 2 * tq), F32),
                        pltpu.VMEM((DA_HEADS, 8, 2 * tq), F32),
                        pltpu.VMEM((DA_HEADS, DA_V, 2 * tq), F32),
                        pltpu.VMEM((ATTN_LOOKAHEAD + 1, 2 * tq, 2 * tq), F32)],
        compiler_params=_cparams("arbitrary", "arbitrary"),
        name="attn_self",
    )(_table_lanes(rel_table), lam, qtb, kb, vtb, bkt, subln_g.reshape(DA_V, 1), *cast_weights)
    return outs[0], outs[1:]


def _attn_cached_kernel(tab_ref, lam_ref, q_ref, ckt_ref, cv_ref, nk_ref, nv_ref, bktc_ref, bktn_ref, g_ref, o_ref,
                        biasc_sc, biasn_sc, *, sq, past, lambda_init):
    @pl.when(pl.program_id(0) == 0)
    def _build_bias():
        for bkt_ref, sc, k0 in ((bktc_ref, biasc_sc, 0), (bktn_ref, biasn_sc, past)):
            bkt = bkt_ref[...]
            qpos = past + lax.broadcasted_iota(jnp.int32, bkt.shape, 0)
            kpos = k0 + lax.broadcasted_iota(jnp.int32, bkt.shape, 1)
            vis = jnp.right_shift(kpos, CHUNK_SHIFT) <= jnp.right_shift(qpos, CHUNK_SHIFT)
            for h in range(DA_HEADS):
                sc[h] = _bias_from_buckets(bkt, tab_ref, h, vis, None)[:, :sc.shape[2]]

    q = q_ref[...]
    lam = lam_ref[0]
    g = g_ref[...]
    for h in range(DA_HEADS):
        cols = slice(h * DA_V, (h + 1) * DA_V)
        q2 = _split_maps(q[:, cols])
        s_c = _dot(q2, ckt_ref[cols, :].astype(BF16))
        s_c = (s_c.reshape(2, sq, past) + biasc_sc[h][None]).reshape(2 * sq, past)
        s_n = lax.dot_general(q2, nk_ref[:, cols].astype(BF16), NT_DIMS, preferred_element_type=F32)
        s_n = (s_n.reshape(2, sq, sq) + biasn_sc[h][None]).reshape(2 * sq, sq)
        m = jnp.maximum(jnp.max(s_c, axis=1, keepdims=True), jnp.max(s_n, axis=1, keepdims=True))
        p_c = jnp.exp2(s_c - m)
        p_n = jnp.exp2(s_n - m)
        l = jnp.sum(p_c, axis=1, keepdims=True) + jnp.sum(p_n, axis=1, keepdims=True)
        v_c = cv_ref[pl.ds(h, past, stride=DA_HEADS), :]
        o = (_dot(p_c.astype(BF16), v_c.astype(BF16))
             + _dot(p_n.astype(BF16), nv_ref[:, cols].astype(BF16))) / l
        o_ref[:, cols] = _diff_finish(o, lam, g, sq, lambda_init).astype(BF16)


def _attn_cached(q, cache_kt2d, cache_v2d, nk, nv, rel_table, lam, subln_g, batch, sq, past, lambda_init):
    qpos = past + jnp.arange(sq, dtype=jnp.int32)
    bkt_c = _rel_bucket(jnp.arange(past, dtype=jnp.int32)[None, :] - qpos[:, None])
    assert sq <= 128 and past % 128 == 0
    bkt_n = jnp.pad(_rel_bucket(qpos[None, :] - qpos[:, None]), ((0, 0), (0, 128 - sq)))
    smem = pl.BlockSpec(memory_space=pltpu.SMEM)
    new_spec = pl.BlockSpec((sq, DA_QK_W), lambda b: (b, 0))
    return pl.pallas_call(
        functools.partial(_attn_cached_kernel, sq=sq, past=past, lambda_init=lambda_init),
        grid=(batch,),
        in_specs=[_resident((DA_HEADS, 128)), smem, new_spec,
                  pl.BlockSpec((DA_QK_W, past), lambda b: (b, 0)),
                  pl.BlockSpec((past * DA_HEADS, DA_V), lambda b: (b, 0)),
                  new_spec, new_spec,
                  _resident((sq, past)), _resident((sq, 128)), _resident((1, DA_V))],
        out_specs=new_spec,
        out_shape=jax.ShapeDtypeStruct((batch * sq, DA_V_W), BF16),
        scratch_shapes=[pltpu.VMEM((DA_HEADS, sq, past), F32), pltpu.VMEM((DA_HEADS, sq, sq), F32)],
        compiler_params=_cparams("arbitrary"),
        name="attn_cached",
    )(_table_lanes(rel_table), lam, q, cache_kt2d, cache_v2d, nk, nv, bkt_c, bkt_n, subln_g)


def _mix_ffn_kernel(x_ref, a_ref, b_ref, qc_ref, mk_ref, mv_ref, gmix_ref, wg_ref, bg_ref, wpa_ref, wpb_ref, wpc_ref,
                    wo_ref, gffn_ref, wfi_ref, wfo_ref, gfin_ref, y_ref, *, seg):
    tm = x_ref.shape[0]
    nseg = tm // seg
    n_mem = mk_ref.shape[0] // nseg

    def xattn_head(hh):
        cols = slice(hh * XA_HD, (hh + 1) * XA_HD)
        outs = []
        for s in range(nseg):
            mem = slice(s * n_mem, (s + 1) * n_mem)
            logit = lax.dot_general(qc_ref[s * seg:(s + 1) * seg, cols], mk_ref[mem, cols].astype(BF16), NT_DIMS,
                                    preferred_element_type=F32)
            p = jnp.exp(logit - jnp.max(logit, axis=1, keepdims=True))
            l = jnp.sum(p, axis=1, keepdims=True)
            outs.append((_dot(p.astype(BF16), mv_ref[mem, cols].astype(BF16)) / l).astype(BF16))
        return outs[0] if nseg == 1 else jnp.concatenate(outs, axis=0)

    x = x_ref[...]
    h = _rms(x, gmix_ref[...]).astype(BF16)

    def branch(n, br, wp_ref):
        cols = slice(n * D_MODEL, (n + 1) * D_MODEL)
        gate = jax.nn.sigmoid(_dot(h, wg_ref[:, cols]) + bg_ref[:, cols])
        return gate * _dot(br, wp_ref[...])

    merged = branch(0, a_ref[...], wpa_ref)
    c_heads = [xattn_head(hh) for hh in range(XA_HEADS // 2)]
    merged = merged + branch(1, b_ref[...], wpb_ref)
    c_heads += [xattn_head(hh) for hh in range(XA_HEADS // 2, XA_HEADS)]
    merged = merged + branch(2, jnp.concatenate(c_heads, axis=1), wpc_ref)
    x1 = x + _dot(merged.astype(BF16), wo_ref[...])
    h2 = _rms(x1, gffn_ref[...]).astype(BF16)
    ff = None
    for c0, c1 in zip(FFN_SPLITS[:-1], FFN_SPLITS[1:]):
        gcols = slice(c0, c1)
        ucols = slice(D_FF + c0, D_FF + c1)
        act = (jax.nn.silu(_dot(h2, wfi_ref[:, gcols])) * _dot(h2, wfi_ref[:, ucols])).astype(BF16)
        term = _dot(act, wfo_ref[gcols, :])
        ff = term if ff is None else ff + term
    y_ref[...] = _rms(x1 + ff, gfin_ref[...])


def _mix_ffn(x2d, a_out, b_out, qc, mk2d, mv2d, p, tm, seq):
    t = x2d.shape[0]
    seg = min(tm, seq)
    nseg = tm // seg
    nt = seq // seg
    n_mem = mk2d.shape[0] // (t // seq)
    assert tm % seg == 0 and seq % seg == 0
    row = pl.BlockSpec((tm, D_MODEL), lambda i: (i, 0))
    mem = pl.BlockSpec((nseg * n_mem, XA_W), lambda i: (i // nt, 0))
    vec = _resident((1, D_MODEL))
    sq_w = _resident((D_MODEL, D_MODEL))
    return pl.pallas_call(
        functools.partial(_mix_ffn_kernel, seg=seg),
        grid=(t // tm,),
        in_specs=[row, row, row, row, mem, mem, vec, _resident((D_MODEL, N_BRANCH * D_MODEL)),
                  _resident((1, N_BRANCH * D_MODEL)), sq_w, sq_w, sq_w, sq_w, vec,
                  _resident((D_MODEL, 2 * D_FF)), _resident((D_FF, D_MODEL)), vec],
        out_specs=row,
        out_shape=jax.ShapeDtypeStruct((t, D_MODEL), F32),
        compiler_params=_cparams("parallel"),
        name="mix_ffn",
    )(x2d, a_out, b_out, qc, mk2d, mv2d, p["norm_mix"], p["w_gate"], p["b_gate"], p["w_proj_a"], p["w_proj_b"],
      p["w_proj_c"], p["w_out"], p["norm_ffn"], p["w_ffn_in"], p["w_ffn_out"], p["norm_final"])


def _layer(x, mk2d, mv2d, cache_kt2d, cache_v2d, conv_state, h0, rel_table, lam, p, lambda_init, tm, tq, tm_out):
    batch, seq, _ = x.shape
    x2d = x.reshape(batch * seq, D_MODEL)
    no_history = cache_kt2d is None
    q, k, v, kb, vb, qc, b_out, new_conv, h_last = _in_proj(x2d, conv_state, h0, p, tm, seq, no_history)
    if no_history:
        a_out, out_w = _attn_self(q, kb, vb, rel_table, lam, p["subln_g"], batch, seq, lambda_init, tq,
                                  [p[n] for n in OUT_WEIGHTS])
        p = {**p, **dict(zip(OUT_WEIGHTS, out_w))}
    else:
        past = cache_kt2d.shape[1]
        a_out = _attn_cached(q, cache_kt2d, cache_v2d, k, v, rel_table, lam, p["subln_g"], batch, seq, past,
                             lambda_init)
    y = _mix_ffn(x2d, a_out, b_out, qc, mk2d, mv2d, p, tm_out, seq)
    return y.reshape(batch, seq, D_MODEL), k, v, new_conv, h_last.reshape(batch, LRU_W), p


def kernel(x_prompt, x_sample, mem_prompt, cache_k, cache_v, state_conv, state_lru, cache_mem_k, cache_mem_v,
           rel_table, norm_mix, w_in, lambda_q1, lambda_k1, lambda_q2, lambda_k2, subln_g, conv_w, conv_b,
           w_rg_a, b_rg_a, w_rg_x, b_rg_x, rg_lambda, norm_mem, w_mem_kv, w_proj_a, w_proj_b, w_proj_c,
           w_gate, b_gate, w_out, norm_ffn, w_ffn_in, w_ffn_out, norm_final):
    depth = w_in.shape[0]
    assert depth == 1, "the final norm is fused into the layer's last kernel"
    l = 0
    lambda_init = 0.8 - 0.6 * math.exp(-0.3 * l)
    bp, sp_, _ = x_prompt.shape
    bs, ss, _ = x_sample.shape
    past = cache_k.shape[2]
    n_mem = mem_prompt.shape[1]
    row = lambda a: a.reshape(1, -1).astype(F32)
    mk, mv, mkb, mvb = _mem_kv(mem_prompt.reshape(bp * n_mem, D_MODEL), row(norm_mem[l]),
                               w_mem_kv[l].astype(BF16), 256)
    p = dict(norm_mix=row(norm_mix[l]), w_in=w_in[l].astype(BF16), subln_g=row(subln_g[l]),
             conv_w=conv_w[l], conv_b=row(conv_b[l]),
             w_rg=jnp.concatenate([w_rg_a[l], w_rg_x[l]], axis=-1).astype(BF16),
             b_rg_a=row(b_rg_a[l]), b_rg_x=row(b_rg_x[l]), rg_lambda=row(rg_lambda[l]),
             w_proj_a=w_proj_a[l], w_proj_b=w_proj_b[l], w_proj_c=w_proj_c[l], w_gate=w_gate[l],
             b_gate=row(b_gate[l]), w_out=w_out[l], norm_ffn=row(norm_ffn[l]), w_ffn_in=w_ffn_in[l],
             w_ffn_out=w_ffn_out[l], norm_final=row(norm_final))
    lam = (jnp.exp(jnp.sum(lambda_q1[l] * lambda_k1[l]).astype(F32))
           - jnp.exp(jnp.sum(lambda_q2[l] * lambda_k2[l]).astype(F32)) + lambda_init).reshape(1)

    zeros_conv = jnp.zeros((bp, CONV_W - 1, LRU_W), F32)
    zeros_h = jnp.zeros((bp, 1, LRU_W), F32)
    yp, kp, vp, cp, hp, p = _layer(x_prompt, mkb, mvb, None, None, zeros_conv, zeros_h, rel_table, lam, p,
                                   lambda_init, tm=512, tq=256, tm_out=512)
    cache_kt = jnp.transpose(cache_k[l], (0, 2, 3, 4, 1)).reshape(bs * DA_QK_W, past)
    ys, ks, vs, cs, hs, _ = _layer(x_sample, cache_mem_k[l].reshape(bs * n_mem, XA_W),
                                   cache_mem_v[l].reshape(bs * n_mem, XA_W),
                                   cache_kt, cache_v[l].reshape(bs * past * DA_HEADS, DA_V),
                                   state_conv[l], state_lru[l].reshape(bs, 1, LRU_W), rel_table, lam, p,
                                   lambda_init, tm=256, tq=ss, tm_out=128)
    kp = jnp.transpose(kp.reshape(bp, DA_HEADS, 2, DA_HD, sp_), (0, 4, 1, 2, 3))
    return (yp, ys,
            kp[None], vp.reshape(1, bp, sp_, DA_HEADS, DA_V),
            cp[None], hp[None],
            mk.reshape(1, bp, n_mem, XA_HEADS, XA_HD), mv.reshape(1, bp, n_mem, XA_HEADS, XA_HD),
            ks.reshape(1, bs, ss, DA_HEADS, 2, DA_HD), vs.reshape(1, bs, ss, DA_HEADS, DA_V),
            cs[None], hs[None])
```

```python
import functools
import math

import jax
import jax.numpy as jnp
from jax import lax
from jax.experimental import pallas as pl
from jax.experimental.pallas import tpu as pltpu

F32 = jnp.float32
BF16 = jnp.bfloat16

D_MODEL = 1024
CHUNK = 64
CHUNK_SHIFT = 6
DA_HEADS = 8
DA_HD = 64
DA_V = 2 * DA_HD
LRU_W = D_MODEL
LRU_BLOCKS = 8
LRU_BW = LRU_W // LRU_BLOCKS
CONV_W = 4
LRU_C = 8.0
XA_HEADS = 4
XA_HD = 256
N_BUCKETS = 32
MAX_DISTANCE = 128
D_FF = ((8 * D_MODEL + 3 * 256 - 1) // (3 * 256)) * 256
N_BRANCH = 3
EPS = 1e-6
DA_QK_W = DA_HEADS * 2 * DA_HD
DA_V_W = DA_HEADS * DA_V
XA_W = XA_HEADS * XA_HD
IN_OFFS = (0, DA_QK_W, 2 * DA_QK_W, 2 * DA_QK_W + DA_V_W, 2 * DA_QK_W + DA_V_W + LRU_W,
           2 * DA_QK_W + DA_V_W + 2 * LRU_W)
IN_W = 2 * DA_QK_W + DA_V_W + 2 * LRU_W + XA_W
LOG2E = math.log2(math.e)
DA_Q_SCALE = DA_HD ** -0.5 * LOG2E
XA_SCALE = XA_HD ** -0.5
MASK_VALUE = -1e30
FAR_BUCKET = N_BUCKETS // 2 - 1
V7X_MXU_DIM = 256
FFN_SPLITS = (0, -(-(D_FF // V7X_MXU_DIM) // 2) * V7X_MXU_DIM, D_FF)
PROJ_CHUNK = V7X_MXU_DIM
OUT_WEIGHTS = ("w_gate", "w_proj_a", "w_proj_b", "w_proj_c", "w_out", "w_ffn_in", "w_ffn_out")
SCAN_GROUP = 8
BF16_SUBLANES = 16
ATTN_LOOKAHEAD = 2

V7X_VMEM_BYTES = 64 * 1024 * 1024
VMEM_LIMIT = V7X_VMEM_BYTES - 8 * 1024 * 1024

NT_DIMS = (((1,), (1,)), ((), ()))


def _cparams(*sem):
    return pltpu.CompilerParams(dimension_semantics=sem, vmem_limit_bytes=VMEM_LIMIT)


def _resident(shape):
    nd = len(shape)
    return pl.BlockSpec(shape, lambda *_: (0,) * nd, pipeline_mode=pl.Buffered(1))


def _rms(x, g):
    return x * lax.rsqrt(jnp.mean(x * x, axis=-1, keepdims=True) + EPS) * g


def _dot(a, b):
    return jnp.dot(a, b, preferred_element_type=F32)


def _in_proj_kernel(x_ref, g_ref, w_ref, cs_ref, h0_ref, cw_ref, cb_ref, wrg_ref, ba_ref, bx_ref, lam_ref,
                    q_ref, k_ref, v_ref, kb_ref, vb_ref, qc_ref, bo_ref, nc_ref, hl_ref,
                    xpad_sc, a_sc, u_sc, hs_sc, gg_sc, hc_sc, p_sc, q_sc, hin_sc, *, nt, seg, feature_major):
    tm = x_ref.shape[0]
    nseg = tm // seg
    two_level = nseg == 1
    groups = tm // SCAN_GROUP
    halo = CONV_W - 1
    base = 8 - halo

    @pl.when(pl.program_id(0) % nt == 0)
    def _load_state():
        for s in range(nseg):
            xpad_sc[s, base:8, :] = cs_ref[s]
            for n in range(LRU_BLOCKS):
                hc_sc[s, n] = h0_ref[s][:, n * LRU_BW:(n + 1) * LRU_BW]

    h = _rms(x_ref[...], g_ref[...]).astype(BF16)

    def proj(n, c0, width):
        return _dot(h, w_ref[:, IN_OFFS[n] + c0:IN_OFFS[n] + c0 + width])

    xb = proj(3, 0, LRU_W)
    for s in range(nseg):
        xpad_sc[s, 8:8 + seg, :] = xb[s * seg:(s + 1) * seg]

    def proj_chunk(kind, c0):
        cols = slice(c0, c0 + PROJ_CHUNK)
        if kind == "g":
            gg_sc[:, cols] = jax.nn.gelu(proj(4, c0, PROJ_CHUNK))
        elif kind == "q":
            qs = proj(0, c0, PROJ_CHUNK) * DA_Q_SCALE
            if feature_major:
                q_ref[cols, :] = qs.T.astype(BF16)
            else:
                q_ref[:, cols] = qs.astype(BF16)
        elif kind == "k":
            kc = proj(1, c0, PROJ_CHUNK)
            kb_ref[:, cols] = kc.astype(BF16)
            if feature_major:
                k_ref[cols, :] = kc.T
            else:
                k_ref[:, cols] = kc
        elif kind == "v":
            vc = proj(2, c0, PROJ_CHUNK)
            v_ref[:, cols] = vc
            if feature_major:
                vb_ref[cols, :] = vc.T.astype(BF16)
            else:
                vb_ref[:, cols] = vc.astype(BF16)
        else:
            qc_ref[:, cols] = (proj(5, c0, PROJ_CHUNK) * XA_SCALE).astype(BF16)

    chunks = [(kind, c0) for kind, width in (("g", LRU_W), ("q", DA_QK_W), ("k", DA_QK_W), ("v", DA_V_W), ("c", XA_W))
              for c0 in range(0, width, PROJ_CHUNK)]

    cw = cw_ref[...]
    sp = jax.nn.softplus(-lam_ref[...])
    for n in range(LRU_BLOCKS):
        for kind, c0 in chunks[n * len(chunks) // LRU_BLOCKS:(n + 1) * len(chunks) // LRU_BLOCKS]:
            proj_chunk(kind, c0)
        cols = slice(n * LRU_BW, (n + 1) * LRU_BW)
        xc_segs = []
        for s in range(nseg):
            xc = xpad_sc[s, base:base + seg, cols] * cw[0:1, cols]
            for j in range(1, CONV_W):
                xc = xc + xpad_sc[s, base + j:base + j + seg, cols] * cw[j:j + 1, cols]
            xc_segs.append(cb_ref[:, cols] + xc)
        xc = xc_segs[0] if nseg == 1 else jnp.concatenate(xc_segs, axis=0)
        xcb = xc.astype(BF16)
        pre = _dot(xcb, wrg_ref[n])
        tanh_r = jnp.tanh(0.5 * (pre[:, :LRU_BW] + ba_ref[:, cols]))
        tanh_i = jnp.tanh(0.5 * (pre[:, LRU_BW:] + bx_ref[:, cols]))
        half_c_sp = (-0.5 * LRU_C) * sp[:, cols]
        log_a = half_c_sp * tanh_r + half_c_sp
        a = jnp.exp(log_a)
        a_sc[n] = a
        one_minus_a2 = -jnp.tanh(log_a) * (a * a + 1.0)
        mult = jnp.where(one_minus_a2 > 0.0, one_minus_a2 * lax.rsqrt(one_minus_a2), 0.0)
        u_sc[n] = mult * ((0.5 * tanh_i + 0.5) * xc)
        if two_level:
            for r in range(SCAN_GROUP):
                a_r = a_sc[n, pl.ds(r, groups, stride=SCAN_GROUP), :]
                u_r = u_sc[n, pl.ds(r, groups, stride=SCAN_GROUP), :]
                p_r, q_r = (a_r, u_r) if r == 0 else (a_r * p_r, a_r * q_r + u_r)
                p_sc[n, r] = p_r
                q_sc[n, r] = q_r
    for s in range(nseg):
        tail = xpad_sc[s, base + seg:8 + seg, :]
        nc_ref[s] = tail
        xpad_sc[s, base:8, :] = tail

    if two_level:
        def group_step(gi, h_in):
            hin_sc[:, pl.ds(gi, 1), :] = h_in
            last = SCAN_GROUP - 1
            return p_sc[:, last, pl.ds(gi, 1), :] * h_in + q_sc[:, last, pl.ds(gi, 1), :]

        hlast = lax.fori_loop(0, groups, group_step, hc_sc[0], unroll=8)
        hc_sc[0] = hlast
        hl_ref[0] = jnp.concatenate([hlast[n] for n in range(LRU_BLOCKS)], axis=1)
        for n in range(LRU_BLOCKS):
            h_in = hin_sc[n]
            for r in range(SCAN_GROUP):
                hs_sc[n, pl.ds(r, groups, stride=SCAN_GROUP), :] = p_sc[n, r] * h_in + q_sc[n, r]
    else:
        for s in range(nseg):
            def step(t, hprev, row0=s * seg):
                hnew = a_sc[:, pl.ds(row0 + t, 1), :] * hprev + u_sc[:, pl.ds(row0 + t, 1), :]
                hs_sc[:, pl.ds(row0 + t, 1), :] = hnew
                return hnew

            hlast = lax.fori_loop(0, seg, step, hc_sc[s], unroll=8)
            hc_sc[s] = hlast
            hl_ref[s] = jnp.concatenate([hlast[n] for n in range(LRU_BLOCKS)], axis=1)
    for n in range(LRU_BLOCKS):
        cols = slice(n * LRU_BW, (n + 1) * LRU_BW)
        bo_ref[:, cols] = (hs_sc[n] * gg_sc[:, cols]).astype(BF16)


def _in_proj(x2d, conv_state, h0, p, tm, seq, feature_major):
    t = x2d.shape[0]
    seg = min(tm, seq)
    nseg = tm // seg
    nt = seq // seg
    assert tm % seg == 0 and seq % seg == 0 and seg % 8 == 0 and seg >= CONV_W - 1
    batch_block = lambda i: (i // nt, 0, 0)
    row = lambda width: pl.BlockSpec((tm, width), lambda i: (i, 0))
    sds = lambda width, dt: jax.ShapeDtypeStruct((t, width), dt)
    if feature_major:
        assert nseg == 1
        fm_spec = lambda width: pl.BlockSpec((width, tm), lambda i: (i // nt, i % nt))
        fm_sds = lambda width, dt: jax.ShapeDtypeStruct((t // seq * width, seq), dt)
    else:
        fm_spec, fm_sds = row, sds
    vec = _resident((1, LRU_W))
    gate_w = _resident((LRU_BLOCKS, LRU_BW, 2 * LRU_BW))
    slab = pltpu.VMEM((LRU_BLOCKS, tm, LRU_BW), F32)
    conv_spec = pl.BlockSpec((nseg, CONV_W - 1, LRU_W), batch_block)
    state_spec = pl.BlockSpec((nseg, 1, LRU_W), batch_block)
    return pl.pallas_call(
        functools.partial(_in_proj_kernel, nt=nt, seg=seg, feature_major=feature_major),
        grid=(t // tm,),
        in_specs=[row(D_MODEL), _resident((1, D_MODEL)), _resident((D_MODEL, IN_W)), conv_spec, state_spec,
                  _resident((CONV_W, LRU_W)), vec, gate_w, vec, vec, vec],
        out_specs=[fm_spec(DA_QK_W), fm_spec(DA_QK_W), row(DA_V_W), row(DA_QK_W), fm_spec(DA_V_W), row(XA_W),
                   row(LRU_W), conv_spec, state_spec],
        out_shape=[fm_sds(DA_QK_W, BF16), fm_sds(DA_QK_W, F32), sds(DA_V_W, F32), sds(DA_QK_W, BF16),
                   fm_sds(DA_V_W, BF16), sds(XA_W, BF16), sds(LRU_W, BF16),
                   jax.ShapeDtypeStruct(conv_state.shape, F32), jax.ShapeDtypeStruct(h0.shape, F32)],
        scratch_shapes=[pltpu.VMEM((nseg, 8 + seg, LRU_W), F32), slab, slab, slab, pltpu.VMEM((tm, LRU_W), F32),
                        pltpu.VMEM((nseg, LRU_BLOCKS, 1, LRU_BW), F32),
                        pltpu.VMEM((LRU_BLOCKS, SCAN_GROUP, tm // SCAN_GROUP, LRU_BW), F32),
                        pltpu.VMEM((LRU_BLOCKS, SCAN_GROUP, tm // SCAN_GROUP, LRU_BW), F32),
                        pltpu.VMEM((LRU_BLOCKS, tm // SCAN_GROUP, LRU_BW), F32)],
        compiler_params=_cparams("arbitrary"),
        name="in_proj",
    )(x2d, p["norm_mix"], p["w_in"], conv_state, h0, p["conv_w"], p["conv_b"], p["w_rg"], p["b_rg_a"],
      p["b_rg_x"], p["rg_lambda"])


def _cast_block_specs(weights, steps, step_index):
    specs = []
    for w in weights:
        rows, cols = w.shape
        nblk = max(d for d in range(1, steps + 1)
                   if steps % d == 0 and rows % d == 0 and (rows // d) % BF16_SUBLANES == 0)
        specs.append(pl.BlockSpec((rows // nblk, cols), lambda *g, per=steps // nblk: (step_index(*g) // per, 0)))
    return specs


def _mem_kv_kernel(m_ref, g_ref, w_ref, mk_ref, mv_ref, mkb_ref, mvb_ref):
    h = _rms(m_ref[...], g_ref[...]).astype(BF16)
    mk = _dot(h, w_ref[:, :XA_W])
    mv = _dot(h, w_ref[:, XA_W:])
    mk_ref[...] = mk
    mv_ref[...] = mv
    mkb_ref[...] = mk.astype(BF16)
    mvb_ref[...] = mv.astype(BF16)


def _mem_kv(mem2d, g, w_bf16, tm):
    t = mem2d.shape[0]
    row = pl.BlockSpec((tm, XA_W), lambda i: (i, 0))
    return pl.pallas_call(
        _mem_kv_kernel,
        grid=(t // tm,),
        in_specs=[pl.BlockSpec((tm, D_MODEL), lambda i: (i, 0)), _resident((1, D_MODEL)),
                  _resident((D_MODEL, 2 * XA_W))],
        out_specs=[row, row, row, row],
        out_shape=[jax.ShapeDtypeStruct((t, XA_W), F32), jax.ShapeDtypeStruct((t, XA_W), F32),
                   jax.ShapeDtypeStruct((t, XA_W), BF16), jax.ShapeDtypeStruct((t, XA_W), BF16)],
        compiler_params=_cparams("parallel"),
        name="mem_kv",
    )(mem2d, g, w_bf16)


def _rel_bucket(rel):
    half = N_BUCKETS // 2
    max_exact = half // 2
    n = jnp.abs(rel)
    nf = jnp.maximum(n, 1).astype(F32)
    large = max_exact + (jnp.log(nf / max_exact) / math.log(MAX_DISTANCE / max_exact)
                         * (half - max_exact)).astype(jnp.int32)
    large = jnp.minimum(large, half - 1)
    return jnp.where(rel > 0, half, 0) + jnp.where(n < max_exact, n, large)


def _table_lanes(rel_table):
    return jnp.pad(rel_table.T, ((0, 0), (0, 128 - N_BUCKETS)))


def _bias_from_buckets(bkt, tabt_ref, head, vis, shift_bucket):
    rows, width = bkt.shape
    table = jnp.broadcast_to(tabt_ref[head:head + 1, :], (rows, 128))
    bias = jnp.concatenate([jnp.take_along_axis(table, bkt[:, c:c + 128], axis=1) for c in range(0, width, 128)],
                           axis=1)
    if shift_bucket is not None:
        bias = bias - tabt_ref[head:head + 1, shift_bucket:shift_bucket + 1]
    return jnp.where(vis, bias * LOG2E, MASK_VALUE)


def _split_maps(qh):
    lane = lax.broadcasted_iota(jnp.int32, qh.shape, 1)
    zero = jnp.zeros_like(qh)
    return jnp.concatenate([jnp.where(lane < DA_HD, qh, zero), jnp.where(lane >= DA_HD, qh, zero)], axis=0)


def _ones_column(n):
    return jnp.where(lax.broadcasted_iota(jnp.int32, (n, DA_V), 1) == 0, 1.0, 0.0).astype(BF16)


def _diff_finish(o, lam, g, tq, lambda_init):
    d = o[:tq] - lam * o[tq:]
    return _rms(d, g) * (1.0 - lambda_init)


def _attn_self_kernel(tab_ref, lam_ref, qt_ref, k_ref, vt_ref, bkt_ref, g_ref, *rest, tq, lambda_init, n_cast):
    w_refs, o_ref, wb_refs = rest[:n_cast], rest[n_cast], rest[n_cast + 1:2 * n_cast + 1]
    bias_sc, q2t_sc, m_sc, l_sc, acc_sc, s_sc = rest[2 * n_cast + 1:]
    for w_ref, wb_ref in zip(w_refs, wb_refs):
        wb_ref[...] = w_ref[...].astype(BF16)

    b = pl.program_id(0)
    i = pl.program_id(1)

    @pl.when((b == 0) & (i == 0))
    def _build_bias():
        bkt = bkt_ref[...]
        kpos = lax.broadcasted_iota(jnp.int32, bkt.shape, 0)
        qpos = lax.broadcasted_iota(jnp.int32, bkt.shape, 1)
        vis = jnp.right_shift(kpos - tq, CHUNK_SHIFT) <= jnp.right_shift(qpos, CHUNK_SHIFT)
        for h in range(DA_HEADS):
            bias_sc[h] = _bias_from_buckets(bkt, tab_ref, h, vis, FAR_BUCKET)

    feat = lax.broadcasted_iota(jnp.int32, (2 * DA_HD, tq), 0)
    for h in range(DA_HEADS):
        qt = qt_ref[h * 2 * DA_HD:(h + 1) * 2 * DA_HD, :]
        zero = jnp.zeros_like(qt)
        q2t_sc[h] = jnp.concatenate([jnp.where(feat < DA_HD, qt, zero), jnp.where(feat >= DA_HD, qt, zero)], axis=1)
    m_sc[...] = jnp.full(m_sc.shape, MASK_VALUE, F32)
    l_sc[...] = jnp.zeros(l_sc.shape, F32)
    acc_sc[...] = jnp.zeros(acc_sc.shape, F32)

    def logits(h, start, nk):
        kj = k_ref[pl.ds(start, nk), h * DA_V:(h + 1) * DA_V]
        return _dot(kj, q2t_sc[h])

    def update(h, start, nk, s, bias):
        vtj = vt_ref[h * DA_V:(h + 1) * DA_V, pl.ds(start, nk)]
        if bias is not None:
            s = s + jnp.concatenate([bias, bias], axis=1)
        m_prev = m_sc[h]
        m_new = jnp.maximum(m_prev, jnp.max(s, axis=0, keepdims=True))
        alpha = jnp.exp2(m_prev - m_new)
        p = jnp.exp2(s - m_new[0:1])
        l_sc[h] = alpha * l_sc[h] + jnp.sum(p, axis=0, keepdims=True)
        acc_sc[h] = alpha[0:1] * acc_sc[h] + _dot(vtj, p.astype(BF16))
        m_sc[h] = m_new

    def key_tile(start, nk, bias_of, after=None):
        nslot = ATTN_LOOKAHEAD + 1
        for h in range(ATTN_LOOKAHEAD):
            s_sc[h % nslot, :nk] = logits(h, start, nk)
        for h in range(DA_HEADS):
            if h + ATTN_LOOKAHEAD < DA_HEADS:
                s_sc[(h + ATTN_LOOKAHEAD) % nslot, :nk] = logits(h + ATTN_LOOKAHEAD, start, nk)
            update(h, start, nk, s_sc[h % nslot, :nk], bias_of(h))
            if after is not None:
                after(h)

    n_far = jnp.maximum(i - 1, 0)

    def far_pair(j, carry):
        key_tile(pl.multiple_of(j * 2 * tq, 2 * tq), 2 * tq, lambda h: None)
        return carry

    lax.fori_loop(0, n_far // 2, far_pair, 0)

    @pl.when(n_far % 2 == 1)
    def _odd_far_tile():
        key_tile(pl.multiple_of((n_far - 1) * tq, tq), tq, lambda h: None)

    lam = lam_ref[0]
    g = g_ref[...]

    def finish(h):
        o = acc_sc[h] / l_sc[h][0:1]
        d = o[:, :tq] - lam * o[:, tq:]
        y = d * lax.rsqrt(jnp.mean(d * d, axis=0, keepdims=True) + EPS) * g * (1.0 - lambda_init)
        o_ref[:, h * DA_V:(h + 1) * DA_V] = y.T.astype(BF16)

    @pl.when(i >= 1)
    def _prev_and_diagonal():
        key_tile(pl.multiple_of((i - 1) * tq, tq), 2 * tq, lambda h: bias_sc[h], after=finish)

    @pl.when(i == 0)
    def _diagonal_only():
        key_tile(0, tq, lambda h: bias_sc[h, tq:, :], after=finish)


def _attn_self(qtb, kb, vtb, rel_table, lam, subln_g, batch, seq, lambda_init, tq, cast_weights):
    assert seq % tq == 0 and tq % 128 == 0 and tq >= MAX_DISTANCE and tq % CHUNK == 0
    nq = seq // tq
    rel = jnp.arange(2 * tq, dtype=jnp.int32)[:, None] - (tq + jnp.arange(tq, dtype=jnp.int32))[None, :]
    bkt = _rel_bucket(rel)
    smem = pl.BlockSpec(memory_space=pltpu.SMEM)

    cast_specs = _cast_block_specs(cast_weights, batch * nq, lambda b, i: b * nq + i)
    n_cast = len(cast_weights)
    outs = pl.pallas_call(
        functools.partial(_attn_self_kernel, tq=tq, lambda_init=lambda_init, n_cast=n_cast),
        grid=(batch, nq),
        in_specs=[_resident((DA_HEADS, 128)), smem,
                  pl.BlockSpec((DA_QK_W, tq), lambda b, i: (b, i)),
                  pl.BlockSpec((seq, DA_QK_W), lambda b, i: (b, 0)),
                  pl.BlockSpec((DA_V_W, seq), lambda b, i: (b, 0)),
                  _resident((2 * tq, tq)), _resident((DA_V, 1))] + cast_specs,
        out_specs=[pl.BlockSpec((tq, DA_V_W), lambda b, i: (b * nq + i, 0))] + cast_specs,
        out_shape=[jax.ShapeDtypeStruct((batch * seq, DA_V_W), BF16)]
                  + [jax.ShapeDtypeStruct(w.shape, BF16) for w in cast_weights],
        scratch_shapes=[pltpu.VMEM((DA_HEADS, 2 * tq, tq), F32),
                        pltpu.VMEM((DA_HEADS, 2 * DA_HD, 2 * tq), BF16),
                        pltpu.VMEM((DA_HEADS, 8, 2 * tq), F32),
                        pltpu.VMEM((DA_HEADS, 8, 2 * tq), F32),
                        pltpu.VMEM((DA_HEADS, DA_V, 2 * tq), F32),
                        pltpu.VMEM((ATTN_LOOKAHEAD + 1, 2 * tq, 2 * tq), F32)],
        compiler_params=_cparams("arbitrary", "arbitrary"),
        name="attn_self",
    )(_table_lanes(rel_table), lam, qtb, kb, vtb, bkt, subln_g.reshape(DA_V, 1), *cast_weights)
    return outs[0], outs[1:]


def _attn_cached_kernel(tab_ref, lam_ref, q_ref, ckt_ref, cv_ref, nk_ref, nv_ref, bktc_ref, bktn_ref, g_ref, o_ref,
                        biasc_sc, biasn_sc, *, sq, past, lambda_init):
    @pl.when(pl.program_id(0) == 0)
    def _build_bias():
        for bkt_ref, sc, k0 in ((bktc_ref, biasc_sc, 0), (bktn_ref, biasn_sc, past)):
            bkt = bkt_ref[...]
            qpos = past + lax.broadcasted_iota(jnp.int32, bkt.shape, 0)
            kpos = k0 + lax.broadcasted_iota(jnp.int32, bkt.shape, 1)
            vis = jnp.right_shift(kpos, CHUNK_SHIFT) <= jnp.right_shift(qpos, CHUNK_SHIFT)
            for h in range(DA_HEADS):
                sc[h] = _bias_from_buckets(bkt, tab_ref, h, vis, None)[:, :sc.shape[2]]

    q = q_ref[...]
    lam = lam_ref[0]
    g = g_ref[...]
    for h in range(DA_HEADS):
        cols = slice(h * DA_V, (h + 1) * DA_V)
        q2 = _split_maps(q[:, cols])
        s_c = _dot(q2, ckt_ref[cols, :].astype(BF16))
        s_c = (s_c.reshape(2, sq, past) + biasc_sc[h][None]).reshape(2 * sq, past)
        s_n = lax.dot_general(q2, nk_ref[:, cols].astype(BF16), NT_DIMS, preferred_element_type=F32)
        s_n = (s_n.reshape(2, sq, sq) + biasn_sc[h][None]).reshape(2 * sq, sq)
        m = jnp.maximum(jnp.max(s_c, axis=1, keepdims=True), jnp.max(s_n, axis=1, keepdims=True))
        p_c = jnp.exp2(s_c - m)
        p_n = jnp.exp2(s_n - m)
        l = jnp.sum(p_c, axis=1, keepdims=True) + jnp.sum(p_n, axis=1, keepdims=True)
        v_c = cv_ref[pl.ds(h, past, stride=DA_HEADS), :]
        o = (_dot(p_c.astype(BF16), v_c.astype(BF16))
             + _dot(p_n.astype(BF16), nv_ref[:, cols].astype(BF16))) / l
        o_ref[:, cols] = _diff_finish(o, lam, g, sq, lambda_init).astype(BF16)


def _attn_cached(q, cache_kt2d, cache_v2d, nk, nv, rel_table, lam, subln_g, batch, sq, past, lambda_init):
    qpos = past + jnp.arange(sq, dtype=jnp.int32)
    bkt_c = _rel_bucket(jnp.arange(past, dtype=jnp.int32)[None, :] - qpos[:, None])
    assert sq <= 128 and past % 128 == 0
    bkt_n = jnp.pad(_rel_bucket(qpos[None, :] - qpos[:, None]), ((0, 0), (0, 128 - sq)))
    smem = pl.BlockSpec(memory_space=pltpu.SMEM)
    new_spec = pl.BlockSpec((sq, DA_QK_W), lambda b: (b, 0))
    return pl.pallas_call(
        functools.partial(_attn_cached_kernel, sq=sq, past=past, lambda_init=lambda_init),
        grid=(batch,),
        in_specs=[_resident((DA_HEADS, 128)), smem, new_spec,
                  pl.BlockSpec((DA_QK_W, past), lambda b: (b, 0)),
                  pl.BlockSpec((past * DA_HEADS, DA_V), lambda b: (b, 0)),
                  new_spec, new_spec,
                  _resident((sq, past)), _resident((sq, 128)), _resident((1, DA_V))],
        out_specs=new_spec,
        out_shape=jax.ShapeDtypeStruct((batch * sq, DA_V_W), BF16),
        scratch_shapes=[pltpu.VMEM((DA_HEADS, sq, past), F32), pltpu.VMEM((DA_HEADS, sq, sq), F32)],
        compiler_params=_cparams("arbitrary"),
        name="attn_cached",
    )(_table_lanes(rel_table), lam, q, cache_kt2d, cache_v2d, nk, nv, bkt_c, bkt_n, subln_g)


def _mix_ffn_kernel(x_ref, a_ref, b_ref, qc_ref, mk_ref, mv_ref, gmix_ref, wg_ref, bg_ref, wpa_ref, wpb_ref, wpc_ref,
                    wo_ref, gffn_ref, wfi_ref, wfo_ref, gfin_ref, y_ref, *, seg):
    tm = x_ref.shape[0]
    nseg = tm // seg
    n_mem = mk_ref.shape[0] // nseg

    def xattn_head(hh):
        cols = slice(hh * XA_HD, (hh + 1) * XA_HD)
        outs = []
        for s in range(nseg):
            mem = slice(s * n_mem, (s + 1) * n_mem)
            logit = lax.dot_general(qc_ref[s * seg:(s + 1) * seg, cols], mk_ref[mem, cols].astype(BF16), NT_DIMS,
                                    preferred_element_type=F32)
            p = jnp.exp(logit - jnp.max(logit, axis=1, keepdims=True))
            l = jnp.sum(p, axis=1, keepdims=True)
            outs.append((_dot(p.astype(BF16), mv_ref[mem, cols].astype(BF16)) / l).astype(BF16))
        return outs[0] if nseg == 1 else jnp.concatenate(outs, axis=0)

    x = x_ref[...]
    h = _rms(x, gmix_ref[...]).astype(BF16)

    def branch(n, br, wp_ref):
        cols = slice(n * D_MODEL, (n + 1) * D_MODEL)
        gate = jax.nn.sigmoid(_dot(h, wg_ref[:, cols]) + bg_ref[:, cols])
        return gate * _dot(br, wp_ref[...])

    merged = branch(0, a_ref[...], wpa_ref)
    c_heads = [xattn_head(hh) for hh in range(XA_HEADS // 2)]
    merged = merged + branch(1, b_ref[...], wpb_ref)
    c_heads += [xattn_head(hh) for hh in range(XA_HEADS // 2, XA_HEADS)]
    merged = merged + branch(2, jnp.concatenate(c_heads, axis=1), wpc_ref)
    x1 = x + _dot(merged.astype(BF16), wo_ref[...])
    h2 = _rms(x1, gffn_ref[...]).astype(BF16)
    ff = None
    for c0, c1 in zip(FFN_SPLITS[:-1], FFN_SPLITS[1:]):
        gcols = slice(c0, c1)
        ucols = slice(D_FF + c0, D_FF + c1)
        act = (jax.nn.silu(_dot(h2, wfi_ref[:, gcols])) * _dot(h2, wfi_ref[:, ucols])).astype(BF16)
        term = _dot(act, wfo_ref[gcols, :])
        ff = term if ff is None else ff + term
    y_ref[...] = _rms(x1 + ff, gfin_ref[...])


def _mix_ffn(x2d, a_out, b_out, qc, mk2d, mv2d, p, tm, seq):
    t = x2d.shape[0]
    seg = min(tm, seq)
    nseg = tm // seg
    nt = seq // seg
    n_mem = mk2d.shape[0] // (t // seq)
    assert tm % seg == 0 and seq % seg == 0
    row = pl.BlockSpec((tm, D_MODEL), lambda i: (i, 0))
    mem = pl.BlockSpec((nseg * n_mem, XA_W), lambda i: (i // nt, 0))
    vec = _resident((1, D_MODEL))
    sq_w = _resident((D_MODEL, D_MODEL))
    return pl.pallas_call(
        functools.partial(_mix_ffn_kernel, seg=seg),
        grid=(t // tm,),
        in_specs=[row, row, row, row, mem, mem, vec, _resident((D_MODEL, N_BRANCH * D_MODEL)),
                  _resident((1, N_BRANCH * D_MODEL)), sq_w, sq_w, sq_w, sq_w, vec,
                  _resident((D_MODEL, 2 * D_FF)), _resident((D_FF, D_MODEL)), vec],
        out_specs=row,
        out_shape=jax.ShapeDtypeStruct((t, D_MODEL), F32),
        compiler_params=_cparams("parallel"),
        name="mix_ffn",
    )(x2d, a_out, b_out, qc, mk2d, mv2d, p["norm_mix"], p["w_gate"], p["b_gate"], p["w_proj_a"], p["w_proj_b"],
      p["w_proj_c"], p["w_out"], p["norm_ffn"], p["w_ffn_in"], p["w_ffn_out"], p["norm_final"])


def _layer(x, mk2d, mv2d, cache_kt2d, cache_v2d, conv_state, h0, rel_table, lam, p, lambda_init, tm, tq, tm_out):
    batch, seq, _ = x.shape
    x2d = x.reshape(batch * seq, D_MODEL)
    no_history = cache_kt2d is None
    q, k, v, kb, vb, qc, b_out, new_conv, h_last = _in_proj(x2d, conv_state, h0, p, tm, seq, no_history)
    if no_history:
        a_out, out_w = _attn_self(q, kb, vb, rel_table, lam, p["subln_g"], batch, seq, lambda_init, tq,
                                  [p[n] for n in OUT_WEIGHTS])
        p = {**p, **dict(zip(OUT_WEIGHTS, out_w))}
    else:
        past = cache_kt2d.shape[1]
        a_out = _attn_cached(q, cache_kt2d, cache_v2d, k, v, rel_table, lam, p["subln_g"], batch, seq, past,
                             lambda_init)
    y = _mix_ffn(x2d, a_out, b_out, qc, mk2d, mv2d, p, tm_out, seq)
    return y.reshape(batch, seq, D_MODEL), k, v, new_conv, h_last.reshape(batch, LRU_W), p


def kernel(x_prompt, x_sample, mem_prompt, cache_k, cache_v, state_conv, state_lru, cache_mem_k, cache_mem_v,
           rel_table, norm_mix, w_in, lambda_q1, lambda_k1, lambda_q2, lambda_k2, subln_g, conv_w, conv_b,
           w_rg_a, b_rg_a, w_rg_x, b_rg_x, rg_lambda, norm_mem, w_mem_kv, w_proj_a, w_proj_b, w_proj_c,
           w_gate, b_gate, w_out, norm_ffn, w_ffn_in, w_ffn_out, norm_final):
    depth = w_in.shape[0]
    assert depth == 1, "the final norm is fused into the layer's last kernel"
    l = 0
    lambda_init = 0.8 - 0.6 * math.exp(-0.3 * l)
    bp, sp_, _ = x_prompt.shape
    bs, ss, _ = x_sample.shape
    past = cache_k.shape[2]
    n_mem = mem_prompt.shape[1]
    row = lambda a: a.reshape(1, -1).astype(F32)
    mk, mv, mkb, mvb = _mem_kv(mem_prompt.reshape(bp * n_mem, D_MODEL), row(norm_mem[l]),
                               w_mem_kv[l].astype(BF16), 256)
    p = dict(norm_mix=row(norm_mix[l]), w_in=w_in[l].astype(BF16), subln_g=row(subln_g[l]),
             conv_w=conv_w[l], conv_b=row(conv_b[l]),
             w_rg=jnp.concatenate([w_rg_a[l], w_rg_x[l]], axis=-1).astype(BF16),
             b_rg_a=row(b_rg_a[l]), b_rg_x=row(b_rg_x[l]), rg_lambda=row(rg_lambda[l]),
             w_proj_a=w_proj_a[l], w_proj_b=w_proj_b[l], w_proj_c=w_proj_c[l], w_gate=w_gate[l],
             b_gate=row(b_gate[l]), w_out=w_out[l], norm_ffn=row(norm_ffn[l]), w_ffn_in=w_ffn_in[l],
             w_ffn_out=w_ffn_out[l], norm_final=row(norm_final))
    lam = (jnp.exp(jnp.sum(lambda_q1[l] * lambda_k1[l]).astype(F32))
           - jnp.exp(jnp.sum(lambda_q2[l] * lambda_k2[l]).astype(F32)) + lambda_init).reshape(1)

    zeros_conv = jnp.zeros((bp, CONV_W - 1, LRU_W), F32)
    zeros_h = jnp.zeros((bp, 1, LRU_W), F32)
    yp, kp, vp, cp, hp, p = _layer(x_prompt, mkb, mvb, None, None, zeros_conv, zeros_h, rel_table, lam, p,
                                   lambda_init, tm=512, tq=256, tm_out=512)
    cache_kt = jnp.transpose(cache_k[l], (0, 2, 3, 4, 1)).reshape(bs * DA_QK_W, past)
    ys, ks, vs, cs, hs, _ = _layer(x_sample, cache_mem_k[l].reshape(bs * n_mem, XA_W),
                                   cache_mem_v[l].reshape(bs * n_mem, XA_W),
                                   cache_kt, cache_v[l].reshape(bs * past * DA_HEADS, DA_V),
                                   state_conv[l], state_lru[l].reshape(bs, 1, LRU_W), rel_table, lam, p,
                                   lambda_init, tm=256, tq=ss, tm_out=128)
    kp = jnp.transpose(kp.reshape(bp, DA_HEADS, 2, DA_HD, sp_), (0, 4, 1, 2, 3))
    return (yp, ys,
            kp[None], vp.reshape(1, bp, sp_, DA_HEADS, DA_V),
            cp[None], hp[None],
            mk.reshape(1, bp, n_mem, XA_HEADS, XA_HD), mv.reshape(1, bp, n_mem, XA_HEADS, XA_HD),
            ks.reshape(1, bs, ss, DA_HEADS, 2, DA_HD), vs.reshape(1, bs, ss, DA_HEADS, DA_V),
            cs[None], hs[None])
```

```python
import functools
import math

import jax
import jax.numpy as jnp
from jax import lax
from jax.experimental import pallas as pl
from jax.experimental.pallas import tpu as pltpu

F32 = jnp.float32
BF16 = jnp.bfloat16

D_MODEL = 1024
CHUNK = 64
CHUNK_SHIFT = 6
DA_HEADS = 8
DA_HD = 64
DA_V = 2 * DA_HD
LRU_W = D_MODEL
LRU_BLOCKS = 8
LRU_BW = LRU_W // LRU_BLOCKS
CONV_W = 4
LRU_C = 8.0
XA_HEADS = 4
XA_HD = 256
N_BUCKETS = 32
MAX_DISTANCE = 128
D_FF = ((8 * D_MODEL + 3 * 256 - 1) // (3 * 256)) * 256
N_BRANCH = 3
EPS = 1e-6
DA_QK_W = DA_HEADS * 2 * DA_HD
DA_V_W = DA_HEADS * DA_V
XA_W = XA_HEADS * XA_HD
IN_OFFS = (0, DA_QK_W, 2 * DA_QK_W, 2 * DA_QK_W + DA_V_W, 2 * DA_QK_W + DA_V_W + LRU_W,
           2 * DA_QK_W + DA_V_W + 2 * LRU_W)
IN_W = 2 * DA_QK_W + DA_V_W + 2 * LRU_W + XA_W
LOG2E = math.log2(math.e)
DA_Q_SCALE = DA_HD ** -0.5 * LOG2E
XA_SCALE = XA_HD ** -0.5
MASK_VALUE = -1e30
FAR_BUCKET = N_BUCKETS // 2 - 1
V7X_MXU_DIM = 256
FFN_SPLITS = (0, -(-(D_FF // V7X_MXU_DIM) // 2) * V7X_MXU_DIM, D_FF)
PROJ_CHUNK = V7X_MXU_DIM
OUT_WEIGHTS = ("w_gate", "w_proj_a", "w_proj_b", "w_proj_c", "w_out", "w_ffn_in", "w_ffn_out")
SCAN_GROUP = 8
BF16_SUBLANES = 16
ATTN_LOOKAHEAD = 2

V7X_VMEM_BYTES = 64 * 1024 * 1024
VMEM_LIMIT = V7X_VMEM_BYTES - 8 * 1024 * 1024

NT_DIMS = (((1,), (1,)), ((), ()))


def _cparams(*sem):
    return pltpu.CompilerParams(dimension_semantics=sem, vmem_limit_bytes=VMEM_LIMIT)


def _resident(shape):
    nd = len(shape)
    return pl.BlockSpec(shape, lambda *_: (0,) * nd, pipeline_mode=pl.Buffered(1))


def _rms(x, g):
    return x * lax.rsqrt(jnp.mean(x * x, axis=-1, keepdims=True) + EPS) * g


def _dot(a, b):
    return jnp.dot(a, b, preferred_element_type=F32)


def _in_proj_kernel(x_ref, g_ref, w_ref, cs_ref, h0_ref, cw_ref, cb_ref, wrg_ref, ba_ref, bx_ref, lam_ref,
                    q_ref, k_ref, v_ref, kb_ref, vb_ref, qc_ref, bo_ref, nc_ref, hl_ref,
                    xpad_sc, a_sc, u_sc, hs_sc, gg_sc, hc_sc, p_sc, q_sc, hin_sc, *, nt, seg, feature_major):
    tm = x_ref.shape[0]
    nseg = tm // seg
    two_level = nseg == 1
    groups = tm // SCAN_GROUP
    halo = CONV_W - 1
    base = 8 - halo

    @pl.when(pl.program_id(0) % nt == 0)
    def _load_state():
        for s in range(nseg):
            xpad_sc[s, base:8, :] = cs_ref[s]
            for n in range(LRU_BLOCKS):
                hc_sc[s, n] = h0_ref[s][:, n * LRU_BW:(n + 1) * LRU_BW]

    h = _rms(x_ref[...], g_ref[...]).astype(BF16)

    def proj(n, c0, width):
        return _dot(h, w_ref[:, IN_OFFS[n] + c0:IN_OFFS[n] + c0 + width])

    xb = proj(3, 0, LRU_W)
    for s in range(nseg):
        xpad_sc[s, 8:8 + seg, :] = xb[s * seg:(s + 1) * seg]

    def proj_chunk(kind, c0):
        cols = slice(c0, c0 + PROJ_CHUNK)
        if kind == "g":
            gg_sc[:, cols] = jax.nn.gelu(proj(4, c0, PROJ_CHUNK))
        elif kind == "q":
            qs = proj(0, c0, PROJ_CHUNK) * DA_Q_SCALE
            if feature_major:
                q_ref[cols, :] = qs.T.astype(BF16)
            else:
                q_ref[:, cols] = qs.astype(BF16)
        elif kind == "k":
            kc = proj(1, c0, PROJ_CHUNK)
            kb_ref[:, cols] = kc.astype(BF16)
            if feature_major:
                k_ref[cols, :] = kc.T
            else:
                k_ref[:, cols] = kc
        elif kind == "v":
            vc = proj(2, c0, PROJ_CHUNK)
            v_ref[:, cols] = vc
            if feature_major:
                vb_ref[cols, :] = vc.T.astype(BF16)
            else:
                vb_ref[:, cols] = vc.astype(BF16)
        else:
            qc_ref[:, cols] = (proj(5, c0, PROJ_CHUNK) * XA_SCALE).astype(BF16)

    chunks = [(kind, c0) for kind, width in (("g", LRU_W), ("q", DA_QK_W), ("k", DA_QK_W), ("v", DA_V_W), ("c", XA_W))
              for c0 in range(0, width, PROJ_CHUNK)]

    cw = cw_ref[...]
    sp = jax.nn.softplus(-lam_ref[...])
    for n in range(LRU_BLOCKS):
        for kind, c0 in chunks[n * len(chunks) // LRU_BLOCKS:(n + 1) * len(chunks) // LRU_BLOCKS]:
            proj_chunk(kind, c0)
        cols = slice(n * LRU_BW, (n + 1) * LRU_BW)
        xc_segs = []
        for s in range(nseg):
            xc = xpad_sc[s, base:base + seg, cols] * cw[0:1, cols]
            for j in range(1, CONV_W):
                xc = xc + xpad_sc[s, base + j:base + j + seg, cols] * cw[j:j + 1, cols]
            xc_segs.append(cb_ref[:, cols] + xc)
        xc = xc_segs[0] if nseg == 1 else jnp.concatenate(xc_segs, axis=0)
        xcb = xc.astype(BF16)
        pre = _dot(xcb, wrg_ref[n])
        tanh_r = jnp.tanh(0.5 * (pre[:, :LRU_BW] + ba_ref[:, cols]))
        tanh_i = jnp.tanh(0.5 * (pre[:, LRU_BW:] + bx_ref[:, cols]))
        half_c_sp = (-0.5 * LRU_C) * sp[:, cols]
        log_a = half_c_sp * tanh_r + half_c_sp
        a = jnp.exp(log_a)
        a_sc[n] = a
        one_minus_a2 = -jnp.tanh(log_a) * (a * a + 1.0)
        mult = jnp.where(one_minus_a2 > 0.0, one_minus_a2 * lax.rsqrt(one_minus_a2), 0.0)
        u_sc[n] = mult * ((0.5 * tanh_i + 0.5) * xc)
        if two_level:
            for r in range(SCAN_GROUP):
                a_r = a_sc[n, pl.ds(r, groups, stride=SCAN_GROUP), :]
                u_r = u_sc[n, pl.ds(r, groups, stride=SCAN_GROUP), :]
                p_r, q_r = (a_r, u_r) if r == 0 else (a_r * p_r, a_r * q_r + u_r)
                p_sc[n, r] = p_r
                q_sc[n, r] = q_r
    for s in range(nseg):
        tail = xpad_sc[s, base + seg:8 + seg, :]
        nc_ref[s] = tail
        xpad_sc[s, base:8, :] = tail

    if two_level:
        def group_step(gi, h_in):
            hin_sc[:, pl.ds(gi, 1), :] = h_in
            last = SCAN_GROUP - 1
            return p_sc[:, last, pl.ds(gi, 1), :] * h_in + q_sc[:, last, pl.ds(gi, 1), :]

        hlast = lax.fori_loop(0, groups, group_step, hc_sc[0], unroll=8)
        hc_sc[0] = hlast
        hl_ref[0] = jnp.concatenate([hlast[n] for n in range(LRU_BLOCKS)], axis=1)
        for n in range(LRU_BLOCKS):
            h_in = hin_sc[n]
            for r in range(SCAN_GROUP):
                hs_sc[n, pl.ds(r, groups, stride=SCAN_GROUP), :] = p_sc[n, r] * h_in + q_sc[n, r]
    else:
        for s in range(nseg):
            def step(t, hprev, row0=s * seg):
                hnew = a_sc[:, pl.ds(row0 + t, 1), :] * hprev + u_sc[:, pl.ds(row0 + t, 1), :]
                hs_sc[:, pl.ds(row0 + t, 1), :] = hnew
                return hnew

            hlast = lax.fori_loop(0, seg, step, hc_sc[s], unroll=8)
            hc_sc[s] = hlast
            hl_ref[s] = jnp.concatenate([hlast[n] for n in range(LRU_BLOCKS)], axis=1)
    for n in range(LRU_BLOCKS):
        cols = slice(n * LRU_BW, (n + 1) * LRU_BW)
        bo_ref[:, cols] = (hs_sc[n] * gg_sc[:, cols]).astype(BF16)


def _in_proj(x2d, conv_state, h0, p, tm, seq, feature_major):
    t = x2d.shape[0]
    seg = min(tm, seq)
    nseg = tm // seg
    nt = seq // seg
    assert tm % seg == 0 and seq % seg == 0 and seg % 8 == 0 and seg >= CONV_W - 1
    batch_block = lambda i: (i // nt, 0, 0)
    row = lambda width: pl.BlockSpec((tm, width), lambda i: (i, 0))
    sds = lambda width, dt: jax.ShapeDtypeStruct((t, width), dt)
    if feature_major:
        assert nseg == 1
        fm_spec = lambda width: pl.BlockSpec((width, tm), lambda i: (i // nt, i % nt))
        fm_sds = lambda width, dt: jax.ShapeDtypeStruct((t // seq * width, seq), dt)
    else:
        fm_spec, fm_sds = row, sds
    vec = _resident((1, LRU_W))
    gate_w = _resident((LRU_BLOCKS, LRU_BW, 2 * LRU_BW))
    slab = pltpu.VMEM((LRU_BLOCKS, tm, LRU_BW), F32)
    conv_spec = pl.BlockSpec((nseg, CONV_W - 1, LRU_W), batch_block)
    state_spec = pl.BlockSpec((nseg, 1, LRU_W), batch_block)
    return pl.pallas_call(
        functools.partial(_in_proj_kernel, nt=nt, seg=seg, feature_major=feature_major),
        grid=(t // tm,),
        in_specs=[row(D_MODEL), _resident((1, D_MODEL)), _resident((D_MODEL, IN_W)), conv_spec, state_spec,
                  _resident((CONV_W, LRU_W)), vec, gate_w, vec, vec, vec],
        out_specs=[fm_spec(DA_QK_W), fm_spec(DA_QK_W), row(DA_V_W), row(DA_QK_W), fm_spec(DA_V_W), row(XA_W),
                   row(LRU_W), conv_spec, state_spec],
        out_shape=[fm_sds(DA_QK_W, BF16), fm_sds(DA_QK_W, F32), sds(DA_V_W, F32), sds(DA_QK_W, BF16),
                   fm_sds(DA_V_W, BF16), sds(XA_W, BF16), sds(LRU_W, BF16),
                   jax.ShapeDtypeStruct(conv_state.shape, F32), jax.ShapeDtypeStruct(h0.shape, F32)],
        scratch_shapes=[pltpu.VMEM((nseg, 8 + seg, LRU_W), F32), slab, slab, slab, pltpu.VMEM((tm, LRU_W), F32),
                        pltpu.VMEM((nseg, LRU_BLOCKS, 1, LRU_BW), F32),
                        pltpu.VMEM((LRU_BLOCKS, SCAN_GROUP, tm // SCAN_GROUP, LRU_BW), F32),
                        pltpu.VMEM((LRU_BLOCKS, SCAN_GROUP, tm // SCAN_GROUP, LRU_BW), F32),
                        pltpu.VMEM((LRU_BLOCKS, tm // SCAN_GROUP, LRU_BW), F32)],
        compiler_params=_cparams("arbitrary"),
        name="in_proj",
    )(x2d, p["norm_mix"], p["w_in"], conv_state, h0, p["conv_w"], p["conv_b"], p["w_rg"], p["b_rg_a"],
      p["b_rg_x"], p["rg_lambda"])


def _cast_block_specs(weights, steps, step_index):
    specs = []
    for w in weights:
        rows, cols = w.shape
        nblk = max(d for d in range(1, steps + 1)
                   if steps % d == 0 and rows % d == 0 and (rows // d) % BF16_SUBLANES == 0)
        specs.append(pl.BlockSpec((rows // nblk, cols), lambda *g, per=steps // nblk: (step_index(*g) // per, 0)))
    return specs


def _mem_kv_kernel(m_ref, g_ref, w_ref, mk_ref, mv_ref, mkb_ref, mvb_ref):
    h = _rms(m_ref[...], g_ref[...]).astype(BF16)
    mk = _dot(h, w_ref[:, :XA_W])
    mv = _dot(h, w_ref[:, XA_W:])
    for hh in range(XA_HEADS):
        mk_ref[:, hh, :] = mk[:, hh * XA_HD:(hh + 1) * XA_HD]
        mv_ref[:, hh, :] = mv[:, hh * XA_HD:(hh + 1) * XA_HD]
    mkb_ref[...] = mk.astype(BF16)
    mvb_ref[...] = mv.astype(BF16)


def _mem_kv(mem2d, g, w_bf16, tm):
    t = mem2d.shape[0]
    row = pl.BlockSpec((tm, XA_W), lambda i: (i, 0))
    heads = pl.BlockSpec((tm, XA_HEADS, XA_HD), lambda i: (i, 0, 0))
    return pl.pallas_call(
        _mem_kv_kernel,
        grid=(t // tm,),
        in_specs=[pl.BlockSpec((tm, D_MODEL), lambda i: (i, 0)), _resident((1, D_MODEL)),
                  _resident((D_MODEL, 2 * XA_W))],
        out_specs=[heads, heads, row, row],
        out_shape=[jax.ShapeDtypeStruct((t, XA_HEADS, XA_HD), F32), jax.ShapeDtypeStruct((t, XA_HEADS, XA_HD), F32),
                   jax.ShapeDtypeStruct((t, XA_W), BF16), jax.ShapeDtypeStruct((t, XA_W), BF16)],
        compiler_params=_cparams("parallel"),
        name="mem_kv",
    )(mem2d, g, w_bf16)


def _rel_bucket(rel):
    half = N_BUCKETS // 2
    max_exact = half // 2
    n = jnp.abs(rel)
    nf = jnp.maximum(n, 1).astype(F32)
    large = max_exact + (jnp.log(nf / max_exact) / math.log(MAX_DISTANCE / max_exact)
                         * (half - max_exact)).astype(jnp.int32)
    large = jnp.minimum(large, half - 1)
    return jnp.where(rel > 0, half, 0) + jnp.where(n < max_exact, n, large)


def _table_lanes(rel_table):
    return jnp.pad(rel_table.T, ((0, 0), (0, 128 - N_BUCKETS)))


def _bias_from_buckets(bkt, tabt_ref, head, vis, shift_bucket):
    rows, width = bkt.shape
    table = jnp.broadcast_to(tabt_ref[head:head + 1, :], (rows, 128))
    bias = jnp.concatenate([jnp.take_along_axis(table, bkt[:, c:c + 128], axis=1) for c in range(0, width, 128)],
                           axis=1)
    if shift_bucket is not None:
        bias = bias - tabt_ref[head:head + 1, shift_bucket:shift_bucket + 1]
    return jnp.where(vis, bias * LOG2E, MASK_VALUE)


def _split_maps(qh):
    lane = lax.broadcasted_iota(jnp.int32, qh.shape, 1)
    zero = jnp.zeros_like(qh)
    return jnp.concatenate([jnp.where(lane < DA_HD, qh, zero), jnp.where(lane >= DA_HD, qh, zero)], axis=0)


def _ones_column(n):
    return jnp.where(lax.broadcasted_iota(jnp.int32, (n, DA_V), 1) == 0, 1.0, 0.0).astype(BF16)


def _diff_finish(o, lam, g, tq, lambda_init):
    d = o[:tq] - lam * o[tq:]
    return _rms(d, g) * (1.0 - lambda_init)


def _attn_self_kernel(tab_ref, lam_ref, qt_ref, k_ref, vt_ref, bkt_ref, g_ref, *rest, tq, lambda_init, n_cast):
    w_refs, o_ref, wb_refs = rest[:n_cast], rest[n_cast], rest[n_cast + 1:2 * n_cast + 1]
    bias_sc, q2t_sc, m_sc, l_sc, acc_sc, s_sc = rest[2 * n_cast + 1:]
    for w_ref, wb_ref in zip(w_refs, wb_refs):
        wb_ref[...] = w_ref[...].astype(BF16)

    b = pl.program_id(0)
    i = pl.program_id(1)

    @pl.when((b == 0) & (i == 0))
    def _build_bias():
        bkt = bkt_ref[...]
        kpos = lax.broadcasted_iota(jnp.int32, bkt.shape, 0)
        qpos = lax.broadcasted_iota(jnp.int32, bkt.shape, 1)
        vis = jnp.right_shift(kpos - tq, CHUNK_SHIFT) <= jnp.right_shift(qpos, CHUNK_SHIFT)
        for h in range(DA_HEADS):
            bias_sc[h] = _bias_from_buckets(bkt, tab_ref, h, vis, FAR_BUCKET)

    feat = lax.broadcasted_iota(jnp.int32, (2 * DA_HD, tq), 0)
    for h in range(DA_HEADS):
        qt = qt_ref[h * 2 * DA_HD:(h + 1) * 2 * DA_HD, :]
        zero = jnp.zeros_like(qt)
        q2t_sc[h] = jnp.concatenate([jnp.where(feat < DA_HD, qt, zero), jnp.where(feat >= DA_HD, qt, zero)], axis=1)
    m_sc[...] = jnp.full(m_sc.shape, MASK_VALUE, F32)
    l_sc[...] = jnp.zeros(l_sc.shape, F32)
    acc_sc[...] = jnp.zeros(acc_sc.shape, F32)

    def logits(h, start, nk):
        kj = k_ref[pl.ds(start, nk), h * DA_V:(h + 1) * DA_V]
        return _dot(kj, q2t_sc[h])

    def update(h, start, nk, s, bias):
        vtj = vt_ref[h * DA_V:(h + 1) * DA_V, pl.ds(start, nk)]
        if bias is not None:
            s = s + jnp.concatenate([bias, bias], axis=1)
        m_prev = m_sc[h]
        m_new = jnp.maximum(m_prev, jnp.max(s, axis=0, keepdims=True))
        alpha = jnp.exp2(m_prev - m_new)
        p = jnp.exp2(s - m_new[0:1])
        l_sc[h] = alpha * l_sc[h] + jnp.sum(p, axis=0, keepdims=True)
        acc_sc[h] = alpha[0:1] * acc_sc[h] + _dot(vtj, p.astype(BF16))
        m_sc[h] = m_new

    def key_tile(start, nk, bias_of, after=None):
        nslot = ATTN_LOOKAHEAD + 1
        for h in range(ATTN_LOOKAHEAD):
            s_sc[h % nslot, :nk] = logits(h, start, nk)
        for h in range(DA_HEADS):
            if h + ATTN_LOOKAHEAD < DA_HEADS:
                s_sc[(h + ATTN_LOOKAHEAD) % nslot, :nk] = logits(h + ATTN_LOOKAHEAD, start, nk)
            update(h, start, nk, s_sc[h % nslot, :nk], bias_of(h))
            if after is not None:
                after(h)

    n_far = jnp.maximum(i - 1, 0)

    def far_pair(j, carry):
        key_tile(pl.multiple_of(j * 2 * tq, 2 * tq), 2 * tq, lambda h: None)
        return carry

    lax.fori_loop(0, n_far // 2, far_pair, 0)

    @pl.when(n_far % 2 == 1)
    def _odd_far_tile():
        key_tile(pl.multiple_of((n_far - 1) * tq, tq), tq, lambda h: None)

    lam = lam_ref[0]
    g = g_ref[...]

    def finish(h):
        o = acc_sc[h] / l_sc[h][0:1]
        d = o[:, :tq] - lam * o[:, tq:]
        y = d * lax.rsqrt(jnp.mean(d * d, axis=0, keepdims=True) + EPS) * g * (1.0 - lambda_init)
        o_ref[:, h * DA_V:(h + 1) * DA_V] = y.T.astype(BF16)

    @pl.when(i >= 1)
    def _prev_and_diagonal():
        key_tile(pl.multiple_of((i - 1) * tq, tq), 2 * tq, lambda h: bias_sc[h], after=finish)

    @pl.when(i == 0)
    def _diagonal_only():
        key_tile(0, tq, lambda h: bias_sc[h, tq:, :], after=finish)


def _attn_self(qtb, kb, vtb, rel_table, lam, subln_g, batch, seq, lambda_init, tq, cast_weights):
    assert seq % tq == 0 and tq % 128 == 0 and tq >= MAX_DISTANCE and tq % CHUNK == 0
    nq = seq // tq
    rel = jnp.arange(2 * tq, dtype=jnp.int32)[:, None] - (tq + jnp.arange(tq, dtype=jnp.int32))[None, :]
    bkt = _rel_bucket(rel)
    smem = pl.BlockSpec(memory_space=pltpu.SMEM)

    cast_specs = _cast_block_specs(cast_weights, batch * nq, lambda b, i: b * nq + i)
    n_cast = len(cast_weights)
    outs = pl.pallas_call(
        functools.partial(_attn_self_kernel, tq=tq, lambda_init=lambda_init, n_cast=n_cast),
        grid=(batch, nq),
        in_specs=[_resident((DA_HEADS, 128)), smem,
                  pl.BlockSpec((DA_QK_W, tq), lambda b, i: (b, i)),
                  pl.BlockSpec((seq, DA_QK_W), lambda b, i: (b, 0)),
                  pl.BlockSpec((DA_V_W, seq), lambda b, i: (b, 0)),
                  _resident((2 * tq, tq)), _resident((DA_V, 1))] + cast_specs,
        out_specs=[pl.BlockSpec((tq, DA_V_W), lambda b, i: (b * nq + i, 0))] + cast_specs,
        out_shape=[jax.ShapeDtypeStruct((batch * seq, DA_V_W), BF16)]
                  + [jax.ShapeDtypeStruct(w.shape, BF16) for w in cast_weights],
        scratch_shapes=[pltpu.VMEM((DA_HEADS, 2 * tq, tq), F32),
                        pltpu.VMEM((DA_HEADS, 2 * DA_HD, 2 * tq), BF16),
                        pltpu.VMEM((DA_HEADS, 8, 2 * tq), F32),
                        pltpu.VMEM((DA_HEADS, 8, 2 * tq), F32),
                        pltpu.VMEM((DA_HEADS, DA_V, 2 * tq), F32),
                        pltpu.VMEM((ATTN_LOOKAHEAD + 1, 2 * tq, 2 * tq), F32)],
        compiler_params=_cparams("arbitrary", "arbitrary"),
        name="attn_self",
    )(_table_lanes(rel_table), lam, qtb, kb, vtb, bkt, subln_g.reshape(DA_V, 1), *cast_weights)
    return outs[0], outs[1:]


def _attn_cached_kernel(tab_ref, lam_ref, q_ref, ckt_ref, cv_ref, nk_ref, nv_ref, bktc_ref, bktn_ref, g_ref, o_ref,
                        biasc_sc, biasn_sc, *, sq, past, lambda_init):
    @pl.when(pl.program_id(0) == 0)
    def _build_bias():
        for bkt_ref, sc, k0 in ((bktc_ref, biasc_sc, 0), (bktn_ref, biasn_sc, past)):
            bkt = bkt_ref[...]
            qpos = past + lax.broadcasted_iota(jnp.int32, bkt.shape, 0)
            kpos = k0 + lax.broadcasted_iota(jnp.int32, bkt.shape, 1)
            vis = jnp.right_shift(kpos, CHUNK_SHIFT) <= jnp.right_shift(qpos, CHUNK_SHIFT)
            for h in range(DA_HEADS):
                sc[h] = _bias_from_buckets(bkt, tab_ref, h, vis, None)[:, :sc.shape[2]]

    q = q_ref[...]
    lam = lam_ref[0]
    g = g_ref[...]
    for h in range(DA_HEADS):
        cols = slice(h * DA_V, (h + 1) * DA_V)
        q2 = _split_maps(q[:, cols])
        s_c = _dot(q2, ckt_ref[cols, :].astype(BF16))
        s_c = (s_c.reshape(2, sq, past) + biasc_sc[h][None]).reshape(2 * sq, past)
        s_n = lax.dot_general(q2, nk_ref[:, cols].astype(BF16), NT_DIMS, preferred_element_type=F32)
        s_n = (s_n.reshape(2, sq, sq) + biasn_sc[h][None]).reshape(2 * sq, sq)
        m = jnp.maximum(jnp.max(s_c, axis=1, keepdims=True), jnp.max(s_n, axis=1, keepdims=True))
        p_c = jnp.exp2(s_c - m)
        p_n = jnp.exp2(s_n - m)
        l = jnp.sum(p_c, axis=1, keepdims=True) + jnp.sum(p_n, axis=1, keepdims=True)
        v_c = cv_ref[pl.ds(h, past, stride=DA_HEADS), :]
        o = (_dot(p_c.astype(BF16), v_c.astype(BF16))
             + _dot(p_n.astype(BF16), nv_ref[:, cols].astype(BF16))) / l
        o_ref[:, cols] = _diff_finish(o, lam, g, sq, lambda_init).astype(BF16)


def _attn_cached(q, cache_kt2d, cache_v2d, nk, nv, rel_table, lam, subln_g, batch, sq, past, lambda_init):
    qpos = past + jnp.arange(sq, dtype=jnp.int32)
    bkt_c = _rel_bucket(jnp.arange(past, dtype=jnp.int32)[None, :] - qpos[:, None])
    assert sq <= 128 and past % 128 == 0
    bkt_n = jnp.pad(_rel_bucket(qpos[None, :] - qpos[:, None]), ((0, 0), (0, 128 - sq)))
    smem = pl.BlockSpec(memory_space=pltpu.SMEM)
    new_spec = pl.BlockSpec((sq, DA_QK_W), lambda b: (b, 0))
    return pl.pallas_call(
        functools.partial(_attn_cached_kernel, sq=sq, past=past, lambda_init=lambda_init),
        grid=(batch,),
        in_specs=[_resident((DA_HEADS, 128)), smem, new_spec,
                  pl.BlockSpec((DA_QK_W, past), lambda b: (b, 0)),
                  pl.BlockSpec((past * DA_HEADS, DA_V), lambda b: (b, 0)),
                  new_spec, new_spec,
                  _resident((sq, past)), _resident((sq, 128)), _resident((1, DA_V))],
        out_specs=new_spec,
        out_shape=jax.ShapeDtypeStruct((batch * sq, DA_V_W), BF16),
        scratch_shapes=[pltpu.VMEM((DA_HEADS, sq, past), F32), pltpu.VMEM((DA_HEADS, sq, sq), F32)],
        compiler_params=_cparams("arbitrary"),
        name="attn_cached",
    )(_table_lanes(rel_table), lam, q, cache_kt2d, cache_v2d, nk, nv, bkt_c, bkt_n, subln_g)


def _mix_ffn_kernel(x_ref, a_ref, b_ref, qc_ref, mk_ref, mv_ref, gmix_ref, wg_ref, bg_ref, wpa_ref, wpb_ref, wpc_ref,
                    wo_ref, gffn_ref, wfi_ref, wfo_ref, gfin_ref, y_ref, *, seg):
    tm = x_ref.shape[0]
    nseg = tm // seg
    n_mem = mk_ref.shape[0] // nseg

    def xattn_head(hh):
        cols = slice(hh * XA_HD, (hh + 1) * XA_HD)
        outs = []
        for s in range(nseg):
            mem = slice(s * n_mem, (s + 1) * n_mem)
            logit = lax.dot_general(qc_ref[s * seg:(s + 1) * seg, cols], mk_ref[mem, cols].astype(BF16), NT_DIMS,
                                    preferred_element_type=F32)
            p = jnp.exp(logit - jnp.max(logit, axis=1, keepdims=True))
            l = jnp.sum(p, axis=1, keepdims=True)
            outs.append((_dot(p.astype(BF16), mv_ref[mem, cols].astype(BF16)) / l).astype(BF16))
        return outs[0] if nseg == 1 else jnp.concatenate(outs, axis=0)

    x = x_ref[...]
    h = _rms(x, gmix_ref[...]).astype(BF16)

    def branch(n, br, wp_ref):
        cols = slice(n * D_MODEL, (n + 1) * D_MODEL)
        gate = jax.nn.sigmoid(_dot(h, wg_ref[:, cols]) + bg_ref[:, cols])
        return gate * _dot(br, wp_ref[...])

    merged = branch(0, a_ref[...], wpa_ref)
    c_heads = [xattn_head(hh) for hh in range(XA_HEADS // 2)]
    merged = merged + branch(1, b_ref[...], wpb_ref)
    c_heads += [xattn_head(hh) for hh in range(XA_HEADS // 2, XA_HEADS)]
    merged = merged + branch(2, jnp.concatenate(c_heads, axis=1), wpc_ref)
    x1 = x + _dot(merged.astype(BF16), wo_ref[...])
    h2 = _rms(x1, gffn_ref[...]).astype(BF16)
    ff = None
    for c0, c1 in zip(FFN_SPLITS[:-1], FFN_SPLITS[1:]):
        gcols = slice(c0, c1)
        ucols = slice(D_FF + c0, D_FF + c1)
        act = (jax.nn.silu(_dot(h2, wfi_ref[:, gcols])) * _dot(h2, wfi_ref[:, ucols])).astype(BF16)
        term = _dot(act, wfo_ref[gcols, :])
        ff = term if ff is None else ff + term
    y_ref[...] = _rms(x1 + ff, gfin_ref[...])


def _mix_ffn(x2d, a_out, b_out, qc, mk2d, mv2d, p, tm, seq):
    t = x2d.shape[0]
    seg = min(tm, seq)
    nseg = tm // seg
    nt = seq // seg
    n_mem = mk2d.shape[0] // (t // seq)
    assert tm % seg == 0 and seq % seg == 0
    row = pl.BlockSpec((tm, D_MODEL), lambda i: (i, 0))
    mem = pl.BlockSpec((nseg * n_mem, XA_W), lambda i: (i // nt, 0))
    vec = _resident((1, D_MODEL))
    sq_w = _resident((D_MODEL, D_MODEL))
    return pl.pallas_call(
        functools.partial(_mix_ffn_kernel, seg=seg),
        grid=(t // tm,),
        in_specs=[row, row, row, row, mem, mem, vec, _resident((D_MODEL, N_BRANCH * D_MODEL)),
                  _resident((1, N_BRANCH * D_MODEL)), sq_w, sq_w, sq_w, sq_w, vec,
                  _resident((D_MODEL, 2 * D_FF)), _resident((D_FF, D_MODEL)), vec],
        out_specs=row,
        out_shape=jax.ShapeDtypeStruct((t, D_MODEL), F32),
        compiler_params=_cparams("parallel"),
        name="mix_ffn",
    )(x2d, a_out, b_out, qc, mk2d, mv2d, p["norm_mix"], p["w_gate"], p["b_gate"], p["w_proj_a"], p["w_proj_b"],
      p["w_proj_c"], p["w_out"], p["norm_ffn"], p["w_ffn_in"], p["w_ffn_out"], p["norm_final"])


def _layer(x, mk2d, mv2d, cache_kt2d, cache_v2d, conv_state, h0, rel_table, lam, p, lambda_init, tm, tq, tm_out):
    batch, seq, _ = x.shape
    x2d = x.reshape(batch * seq, D_MODEL)
    no_history = cache_kt2d is None
    q, k, v, kb, vb, qc, b_out, new_conv, h_last = _in_proj(x2d, conv_state, h0, p, tm, seq, no_history)
    if no_history:
        a_out, out_w = _attn_self(q, kb, vb, rel_table, lam, p["subln_g"], batch, seq, lambda_init, tq,
                                  [p[n] for n in OUT_WEIGHTS])
        p = {**p, **dict(zip(OUT_WEIGHTS, out_w))}
    else:
        past = cache_kt2d.shape[1]
        a_out = _attn_cached(q, cache_kt2d, cache_v2d, k, v, rel_table, lam, p["subln_g"], batch, seq, past,
                             lambda_init)
    y = _mix_ffn(x2d, a_out, b_out, qc, mk2d, mv2d, p, tm_out, seq)
    return y.reshape(batch, seq, D_MODEL), k, v, new_conv, h_last.reshape(batch, LRU_W), p


def kernel(x_prompt, x_sample, mem_prompt, cache_k, cache_v, state_conv, state_lru, cache_mem_k, cache_mem_v,
           rel_table, norm_mix, w_in, lambda_q1, lambda_k1, lambda_q2, lambda_k2, subln_g, conv_w, conv_b,
           w_rg_a, b_rg_a, w_rg_x, b_rg_x, rg_lambda, norm_mem, w_mem_kv, w_proj_a, w_proj_b, w_proj_c,
           w_gate, b_gate, w_out, norm_ffn, w_ffn_in, w_ffn_out, norm_final):
    depth = w_in.shape[0]
    assert depth == 1, "the final norm is fused into the layer's last kernel"
    l = 0
    lambda_init = 0.8 - 0.6 * math.exp(-0.3 * l)
    bp, sp_, _ = x_prompt.shape
    bs, ss, _ = x_sample.shape
    past = cache_k.shape[2]
    n_mem = mem_prompt.shape[1]
    row = lambda a: a.reshape(1, -1).astype(F32)
    mk, mv, mkb, mvb = _mem_kv(mem_prompt.reshape(bp * n_mem, D_MODEL), row(norm_mem[l]),
                               w_mem_kv[l].astype(BF16), 256)
    p = dict(norm_mix=row(norm_mix[l]), w_in=w_in[l].astype(BF16), subln_g=row(subln_g[l]),
             conv_w=conv_w[l], conv_b=row(conv_b[l]),
             w_rg=jnp.concatenate([w_rg_a[l], w_rg_x[l]], axis=-1).astype(BF16),
             b_rg_a=row(b_rg_a[l]), b_rg_x=row(b_rg_x[l]), rg_lambda=row(rg_lambda[l]),
             w_proj_a=w_proj_a[l], w_proj_b=w_proj_b[l], w_proj_c=w_proj_c[l], w_gate=w_gate[l],
             b_gate=row(b_gate[l]), w_out=w_out[l], norm_ffn=row(norm_ffn[l]), w_ffn_in=w_ffn_in[l],
             w_ffn_out=w_ffn_out[l], norm_final=row(norm_final))
    lam = (jnp.exp(jnp.sum(lambda_q1[l] * lambda_k1[l]).astype(F32))
           - jnp.exp(jnp.sum(lambda_q2[l] * lambda_k2[l]).astype(F32)) + lambda_init).reshape(1)

    zeros_conv = jnp.zeros((bp, CONV_W - 1, LRU_W), F32)
    zeros_h = jnp.zeros((bp, 1, LRU_W), F32)
    yp, kp, vp, cp, hp, p = _layer(x_prompt, mkb, mvb, None, None, zeros_conv, zeros_h, rel_table, lam, p,
                                   lambda_init, tm=512, tq=256, tm_out=512)
    cache_kt = jnp.transpose(cache_k[l], (0, 2, 3, 4, 1)).reshape(bs * DA_QK_W, past)
    ys, ks, vs, cs, hs, _ = _layer(x_sample, cache_mem_k[l].reshape(bs * n_mem, XA_W),
                                   cache_mem_v[l].reshape(bs * n_mem, XA_W),
                                   cache_kt, cache_v[l].reshape(bs * past * DA_HEADS, DA_V),
                                   state_conv[l], state_lru[l].reshape(bs, 1, LRU_W), rel_table, lam, p,
                                   lambda_init, tm=256, tq=ss, tm_out=128)
    kp = jnp.transpose(kp.reshape(bp, DA_HEADS, 2, DA_HD, sp_), (0, 4, 1, 2, 3))
    return (yp, ys,
            kp[None], vp.reshape(1, bp, sp_, DA_HEADS, DA_V),
            cp[None], hp[None],
            mk.reshape(1, bp, n_mem, XA_HEADS, XA_HD), mv.reshape(1, bp, n_mem, XA_HEADS, XA_HD),
            ks.reshape(1, bs, ss, DA_HEADS, 2, DA_HD), vs.reshape(1, bs, ss, DA_HEADS, DA_V),
            cs[None], hs[None])
```

```python
import functools
import math

import jax
import jax.numpy as jnp
from jax import lax
from jax.experimental import pallas as pl
from jax.experimental.pallas import tpu as pltpu

F32 = jnp.float32
BF16 = jnp.bfloat16

D_MODEL = 1024
CHUNK = 64
CHUNK_SHIFT = 6
DA_HEADS = 8
DA_HD = 64
DA_V = 2 * DA_HD
LRU_W = D_MODEL
LRU_BLOCKS = 8
LRU_BW = LRU_W // LRU_BLOCKS
CONV_W = 4
LRU_C = 8.0
XA_HEADS = 4
XA_HD = 256
N_BUCKETS = 32
MAX_DISTANCE = 128
D_FF = ((8 * D_MODEL + 3 * 256 - 1) // (3 * 256)) * 256
N_BRANCH = 3
EPS = 1e-6
DA_QK_W = DA_HEADS * 2 * DA_HD
DA_V_W = DA_HEADS * DA_V
XA_W = XA_HEADS * XA_HD
IN_OFFS = (0, DA_QK_W, 2 * DA_QK_W, 2 * DA_QK_W + DA_V_W, 2 * DA_QK_W + DA_V_W + LRU_W,
           2 * DA_QK_W + DA_V_W + 2 * LRU_W)
IN_W = 2 * DA_QK_W + DA_V_W + 2 * LRU_W + XA_W
LOG2E = math.log2(math.e)
DA_Q_SCALE = DA_HD ** -0.5 * LOG2E
XA_SCALE = XA_HD ** -0.5
MASK_VALUE = -1e30
FAR_BUCKET = N_BUCKETS // 2 - 1
V7X_MXU_DIM = 256
FFN_SPLITS = (0, -(-(D_FF // V7X_MXU_DIM) // 2) * V7X_MXU_DIM, D_FF)
PROJ_CHUNK = V7X_MXU_DIM
OUT_WEIGHTS = ("w_gate", "w_proj_a", "w_proj_b", "w_proj_c", "w_out", "w_ffn_in", "w_ffn_out")
SCAN_GROUP = 8
BF16_SUBLANES = 16
ATTN_LOOKAHEAD = 2

V7X_VMEM_BYTES = 64 * 1024 * 1024
VMEM_LIMIT = V7X_VMEM_BYTES - 8 * 1024 * 1024

NT_DIMS = (((1,), (1,)), ((), ()))


def _cparams(*sem):
    return pltpu.CompilerParams(dimension_semantics=sem, vmem_limit_bytes=VMEM_LIMIT)


def _resident(shape):
    nd = len(shape)
    return pl.BlockSpec(shape, lambda *_: (0,) * nd, pipeline_mode=pl.Buffered(1))


def _rms(x, g):
    return x * lax.rsqrt(jnp.mean(x * x, axis=-1, keepdims=True) + EPS) * g


def _dot(a, b):
    return jnp.dot(a, b, preferred_element_type=F32)


def _in_proj_kernel(x_ref, g_ref, w_ref, cs_ref, h0_ref, cw_ref, cb_ref, wrg_ref, ba_ref, bx_ref, lam_ref,
                    q_ref, k_ref, v_ref, kb_ref, vb_ref, qc_ref, bo_ref, nc_ref, hl_ref,
                    xpad_sc, a_sc, u_sc, hs_sc, gg_sc, hc_sc, p_sc, q_sc, hin_sc, *, nt, seg, feature_major):
    tm = x_ref.shape[0]
    nseg = tm // seg
    two_level = nseg == 1
    groups = tm // SCAN_GROUP
    halo = CONV_W - 1
    base = 8 - halo

    @pl.when(pl.program_id(0) % nt == 0)
    def _load_state():
        for s in range(nseg):
            xpad_sc[s, base:8, :] = cs_ref[s]
            for n in range(LRU_BLOCKS):
                hc_sc[s, n] = h0_ref[s][:, n * LRU_BW:(n + 1) * LRU_BW]

    h = _rms(x_ref[...], g_ref[...]).astype(BF16)

    def proj(n, c0, width):
        return _dot(h, w_ref[:, IN_OFFS[n] + c0:IN_OFFS[n] + c0 + width])

    xb = proj(3, 0, LRU_W)
    for s in range(nseg):
        xpad_sc[s, 8:8 + seg, :] = xb[s * seg:(s + 1) * seg]

    def proj_chunk(kind, c0):
        cols = slice(c0, c0 + PROJ_CHUNK)
        if kind == "g":
            gg_sc[:, cols] = jax.nn.gelu(proj(4, c0, PROJ_CHUNK))
        elif kind == "q":
            qs = proj(0, c0, PROJ_CHUNK) * DA_Q_SCALE
            if feature_major:
                q_ref[cols, :] = qs.T.astype(BF16)
            else:
                q_ref[:, cols] = qs.astype(BF16)
        elif kind == "k":
            kc = proj(1, c0, PROJ_CHUNK)
            kb_ref[:, cols] = kc.astype(BF16)
            if feature_major:
                k_ref[cols, :] = kc.T
            else:
                for j in range(PROJ_CHUNK // DA_HD):
                    m = (c0 + j * DA_HD) // DA_HD
                    k_ref[:, m // 2, m % 2, :] = kc[:, j * DA_HD:(j + 1) * DA_HD]
        elif kind == "v":
            vc = proj(2, c0, PROJ_CHUNK)
            v_ref[:, cols] = vc
            if feature_major:
                vb_ref[cols, :] = vc.T.astype(BF16)
            else:
                vb_ref[:, cols] = vc.astype(BF16)
        else:
            qc_ref[:, cols] = (proj(5, c0, PROJ_CHUNK) * XA_SCALE).astype(BF16)

    chunks = [(kind, c0) for kind, width in (("g", LRU_W), ("q", DA_QK_W), ("k", DA_QK_W), ("v", DA_V_W), ("c", XA_W))
              for c0 in range(0, width, PROJ_CHUNK)]

    cw = cw_ref[...]
    sp = jax.nn.softplus(-lam_ref[...])
    for n in range(LRU_BLOCKS):
        for kind, c0 in chunks[n * len(chunks) // LRU_BLOCKS:(n + 1) * len(chunks) // LRU_BLOCKS]:
            proj_chunk(kind, c0)
        cols = slice(n * LRU_BW, (n + 1) * LRU_BW)
        xc_segs = []
        for s in range(nseg):
            xc = xpad_sc[s, base:base + seg, cols] * cw[0:1, cols]
            for j in range(1, CONV_W):
                xc = xc + xpad_sc[s, base + j:base + j + seg, cols] * cw[j:j + 1, cols]
            xc_segs.append(cb_ref[:, cols] + xc)
        xc = xc_segs[0] if nseg == 1 else jnp.concatenate(xc_segs, axis=0)
        xcb = xc.astype(BF16)
        pre = _dot(xcb, wrg_ref[n])
        tanh_r = jnp.tanh(0.5 * (pre[:, :LRU_BW] + ba_ref[:, cols]))
        tanh_i = jnp.tanh(0.5 * (pre[:, LRU_BW:] + bx_ref[:, cols]))
        half_c_sp = (-0.5 * LRU_C) * sp[:, cols]
        log_a = half_c_sp * tanh_r + half_c_sp
        a = jnp.exp(log_a)
        a_sc[n] = a
        one_minus_a2 = -jnp.tanh(log_a) * (a * a + 1.0)
        mult = jnp.where(one_minus_a2 > 0.0, one_minus_a2 * lax.rsqrt(one_minus_a2), 0.0)
        u_sc[n] = mult * ((0.5 * tanh_i + 0.5) * xc)
        if two_level:
            for r in range(SCAN_GROUP):
                a_r = a_sc[n, pl.ds(r, groups, stride=SCAN_GROUP), :]
                u_r = u_sc[n, pl.ds(r, groups, stride=SCAN_GROUP), :]
                p_r, q_r = (a_r, u_r) if r == 0 else (a_r * p_r, a_r * q_r + u_r)
                p_sc[n, r] = p_r
                q_sc[n, r] = q_r
    for s in range(nseg):
        tail = xpad_sc[s, base + seg:8 + seg, :]
        nc_ref[s] = tail
        xpad_sc[s, base:8, :] = tail

    if two_level:
        def group_step(gi, h_in):
            hin_sc[:, pl.ds(gi, 1), :] = h_in
            last = SCAN_GROUP - 1
            return p_sc[:, last, pl.ds(gi, 1), :] * h_in + q_sc[:, last, pl.ds(gi, 1), :]

        hlast = lax.fori_loop(0, groups, group_step, hc_sc[0], unroll=8)
        hc_sc[0] = hlast
        hl_ref[0] = jnp.concatenate([hlast[n] for n in range(LRU_BLOCKS)], axis=1)
        for n in range(LRU_BLOCKS):
            h_in = hin_sc[n]
            for r in range(SCAN_GROUP):
                hs_sc[n, pl.ds(r, groups, stride=SCAN_GROUP), :] = p_sc[n, r] * h_in + q_sc[n, r]
    else:
        for s in range(nseg):
            def step(t, hprev, row0=s * seg):
                hnew = a_sc[:, pl.ds(row0 + t, 1), :] * hprev + u_sc[:, pl.ds(row0 + t, 1), :]
                hs_sc[:, pl.ds(row0 + t, 1), :] = hnew
                return hnew

            hlast = lax.fori_loop(0, seg, step, hc_sc[s], unroll=8)
            hc_sc[s] = hlast
            hl_ref[s] = jnp.concatenate([hlast[n] for n in range(LRU_BLOCKS)], axis=1)
    for n in range(LRU_BLOCKS):
        cols = slice(n * LRU_BW, (n + 1) * LRU_BW)
        bo_ref[:, cols] = (hs_sc[n] * gg_sc[:, cols]).astype(BF16)


def _in_proj(x2d, conv_state, h0, p, tm, seq, feature_major):
    t = x2d.shape[0]
    seg = min(tm, seq)
    nseg = tm // seg
    nt = seq // seg
    assert tm % seg == 0 and seq % seg == 0 and seg % 8 == 0 and seg >= CONV_W - 1
    batch_block = lambda i: (i // nt, 0, 0)
    row = lambda width: pl.BlockSpec((tm, width), lambda i: (i, 0))
    sds = lambda width, dt: jax.ShapeDtypeStruct((t, width), dt)
    if feature_major:
        assert nseg == 1
        fm_spec = lambda width: pl.BlockSpec((width, tm), lambda i: (i // nt, i % nt))
        fm_sds = lambda width, dt: jax.ShapeDtypeStruct((t // seq * width, seq), dt)
        k_spec, k_sds = fm_spec(DA_QK_W), fm_sds(DA_QK_W, F32)
    else:
        fm_spec, fm_sds = row, sds
        k_spec = pl.BlockSpec((tm, DA_HEADS, 2, DA_HD), lambda i: (i, 0, 0, 0))
        k_sds = jax.ShapeDtypeStruct((t, DA_HEADS, 2, DA_HD), F32)
    vec = _resident((1, LRU_W))
    gate_w = _resident((LRU_BLOCKS, LRU_BW, 2 * LRU_BW))
    slab = pltpu.VMEM((LRU_BLOCKS, tm, LRU_BW), F32)
    conv_spec = pl.BlockSpec((nseg, CONV_W - 1, LRU_W), batch_block)
    state_spec = pl.BlockSpec((nseg, 1, LRU_W), batch_block)
    return pl.pallas_call(
        functools.partial(_in_proj_kernel, nt=nt, seg=seg, feature_major=feature_major),
        grid=(t // tm,),
        in_specs=[row(D_MODEL), _resident((1, D_MODEL)), _resident((D_MODEL, IN_W)), conv_spec, state_spec,
                  _resident((CONV_W, LRU_W)), vec, gate_w, vec, vec, vec],
        out_specs=[fm_spec(DA_QK_W), k_spec, row(DA_V_W), row(DA_QK_W), fm_spec(DA_V_W), row(XA_W),
                   row(LRU_W), conv_spec, state_spec],
        out_shape=[fm_sds(DA_QK_W, BF16), k_sds, sds(DA_V_W, F32), sds(DA_QK_W, BF16),
                   fm_sds(DA_V_W, BF16), sds(XA_W, BF16), sds(LRU_W, BF16),
                   jax.ShapeDtypeStruct(conv_state.shape, F32), jax.ShapeDtypeStruct(h0.shape, F32)],
        scratch_shapes=[pltpu.VMEM((nseg, 8 + seg, LRU_W), F32), slab, slab, slab, pltpu.VMEM((tm, LRU_W), F32),
                        pltpu.VMEM((nseg, LRU_BLOCKS, 1, LRU_BW), F32),
                        pltpu.VMEM((LRU_BLOCKS, SCAN_GROUP, tm // SCAN_GROUP, LRU_BW), F32),
                        pltpu.VMEM((LRU_BLOCKS, SCAN_GROUP, tm // SCAN_GROUP, LRU_BW), F32),
                        pltpu.VMEM((LRU_BLOCKS, tm // SCAN_GROUP, LRU_BW), F32)],
        compiler_params=_cparams("arbitrary"),
        name="in_proj",
    )(x2d, p["norm_mix"], p["w_in"], conv_state, h0, p["conv_w"], p["conv_b"], p["w_rg"], p["b_rg_a"],
      p["b_rg_x"], p["rg_lambda"])


def _cast_block_specs(weights, steps, step_index):
    specs = []
    for w in weights:
        rows, cols = w.shape
        nblk = max(d for d in range(1, steps + 1)
                   if steps % d == 0 and rows % d == 0 and (rows // d) % BF16_SUBLANES == 0)
        specs.append(pl.BlockSpec((rows // nblk, cols), lambda *g, per=steps // nblk: (step_index(*g) // per, 0)))
    return specs


def _mem_kv_kernel(m_ref, g_ref, w_ref, mk_ref, mv_ref, mkb_ref, mvb_ref):
    h = _rms(m_ref[...], g_ref[...]).astype(BF16)
    mk = _dot(h, w_ref[:, :XA_W])
    mv = _dot(h, w_ref[:, XA_W:])
    for hh in range(XA_HEADS):
        mk_ref[:, hh, :] = mk[:, hh * XA_HD:(hh + 1) * XA_HD]
        mv_ref[:, hh, :] = mv[:, hh * XA_HD:(hh + 1) * XA_HD]
    mkb_ref[...] = mk.astype(BF16)
    mvb_ref[...] = mv.astype(BF16)


def _mem_kv(mem2d, g, w_bf16, tm):
    t = mem2d.shape[0]
    row = pl.BlockSpec((tm, XA_W), lambda i: (i, 0))
    heads = pl.BlockSpec((tm, XA_HEADS, XA_HD), lambda i: (i, 0, 0))
    return pl.pallas_call(
        _mem_kv_kernel,
        grid=(t // tm,),
        in_specs=[pl.BlockSpec((tm, D_MODEL), lambda i: (i, 0)), _resident((1, D_MODEL)),
                  _resident((D_MODEL, 2 * XA_W))],
        out_specs=[heads, heads, row, row],
        out_shape=[jax.ShapeDtypeStruct((t, XA_HEADS, XA_HD), F32), jax.ShapeDtypeStruct((t, XA_HEADS, XA_HD), F32),
                   jax.ShapeDtypeStruct((t, XA_W), BF16), jax.ShapeDtypeStruct((t, XA_W), BF16)],
        compiler_params=_cparams("parallel"),
        name="mem_kv",
    )(mem2d, g, w_bf16)


def _rel_bucket(rel):
    half = N_BUCKETS // 2
    max_exact = half // 2
    n = jnp.abs(rel)
    nf = jnp.maximum(n, 1).astype(F32)
    large = max_exact + (jnp.log(nf / max_exact) / math.log(MAX_DISTANCE / max_exact)
                         * (half - max_exact)).astype(jnp.int32)
    large = jnp.minimum(large, half - 1)
    return jnp.where(rel > 0, half, 0) + jnp.where(n < max_exact, n, large)


def _table_lanes(rel_table):
    return jnp.pad(rel_table.T, ((0, 0), (0, 128 - N_BUCKETS)))


def _bias_from_buckets(bkt, tabt_ref, head, vis, shift_bucket):
    rows, width = bkt.shape
    table = jnp.broadcast_to(tabt_ref[head:head + 1, :], (rows, 128))
    bias = jnp.concatenate([jnp.take_along_axis(table, bkt[:, c:c + 128], axis=1) for c in range(0, width, 128)],
                           axis=1)
    if shift_bucket is not None:
        bias = bias - tabt_ref[head:head + 1, shift_bucket:shift_bucket + 1]
    return jnp.where(vis, bias * LOG2E, MASK_VALUE)


def _split_maps(qh):
    lane = lax.broadcasted_iota(jnp.int32, qh.shape, 1)
    zero = jnp.zeros_like(qh)
    return jnp.concatenate([jnp.where(lane < DA_HD, qh, zero), jnp.where(lane >= DA_HD, qh, zero)], axis=0)


def _ones_column(n):
    return jnp.where(lax.broadcasted_iota(jnp.int32, (n, DA_V), 1) == 0, 1.0, 0.0).astype(BF16)


def _diff_finish(o, lam, g, tq, lambda_init):
    d = o[:tq] - lam * o[tq:]
    return _rms(d, g) * (1.0 - lambda_init)


def _attn_self_kernel(tab_ref, lam_ref, qt_ref, k_ref, vt_ref, bkt_ref, g_ref, *rest, tq, lambda_init, n_cast):
    w_refs, o_ref, wb_refs = rest[:n_cast], rest[n_cast], rest[n_cast + 1:2 * n_cast + 1]
    bias_sc, q2t_sc, m_sc, l_sc, acc_sc, s_sc = rest[2 * n_cast + 1:]
    for w_ref, wb_ref in zip(w_refs, wb_refs):
        wb_ref[...] = w_ref[...].astype(BF16)

    b = pl.program_id(0)
    i = pl.program_id(1)

    @pl.when((b == 0) & (i == 0))
    def _build_bias():
        bkt = bkt_ref[...]
        kpos = lax.broadcasted_iota(jnp.int32, bkt.shape, 0)
        qpos = lax.broadcasted_iota(jnp.int32, bkt.shape, 1)
        vis = jnp.right_shift(kpos - tq, CHUNK_SHIFT) <= jnp.right_shift(qpos, CHUNK_SHIFT)
        for h in range(DA_HEADS):
            bias_sc[h] = _bias_from_buckets(bkt, tab_ref, h, vis, FAR_BUCKET)

    feat = lax.broadcasted_iota(jnp.int32, (2 * DA_HD, tq), 0)
    for h in range(DA_HEADS):
        qt = qt_ref[h * 2 * DA_HD:(h + 1) * 2 * DA_HD, :]
        zero = jnp.zeros_like(qt)
        q2t_sc[h] = jnp.concatenate([jnp.where(feat < DA_HD, qt, zero), jnp.where(feat >= DA_HD, qt, zero)], axis=1)
    m_sc[...] = jnp.full(m_sc.shape, MASK_VALUE, F32)
    l_sc[...] = jnp.zeros(l_sc.shape, F32)
    acc_sc[...] = jnp.zeros(acc_sc.shape, F32)

    def logits(h, start, nk):
        kj = k_ref[pl.ds(start, nk), h * DA_V:(h + 1) * DA_V]
        return _dot(kj, q2t_sc[h])

    def update(h, start, nk, s, bias):
        vtj = vt_ref[h * DA_V:(h + 1) * DA_V, pl.ds(start, nk)]
        if bias is not None:
            s = s + jnp.concatenate([bias, bias], axis=1)
        m_prev = m_sc[h]
        m_new = jnp.maximum(m_prev, jnp.max(s, axis=0, keepdims=True))
        alpha = jnp.exp2(m_prev - m_new)
        p = jnp.exp2(s - m_new[0:1])
        l_sc[h] = alpha * l_sc[h] + jnp.sum(p, axis=0, keepdims=True)
        acc_sc[h] = alpha[0:1] * acc_sc[h] + _dot(vtj, p.astype(BF16))
        m_sc[h] = m_new

    def key_tile(start, nk, bias_of, after=None):
        nslot = ATTN_LOOKAHEAD + 1
        for h in range(ATTN_LOOKAHEAD):
            s_sc[h % nslot, :nk] = logits(h, start, nk)
        for h in range(DA_HEADS):
            if h + ATTN_LOOKAHEAD < DA_HEADS:
                s_sc[(h + ATTN_LOOKAHEAD) % nslot, :nk] = logits(h + ATTN_LOOKAHEAD, start, nk)
            update(h, start, nk, s_sc[h % nslot, :nk], bias_of(h))
            if after is not None:
                after(h)

    n_far = jnp.maximum(i - 1, 0)

    def far_pair(j, carry):
        key_tile(pl.multiple_of(j * 2 * tq, 2 * tq), 2 * tq, lambda h: None)
        return carry

    lax.fori_loop(0, n_far // 2, far_pair, 0)

    @pl.when(n_far % 2 == 1)
    def _odd_far_tile():
        key_tile(pl.multiple_of((n_far - 1) * tq, tq), tq, lambda h: None)

    lam = lam_ref[0]
    g = g_ref[...]

    def finish(h):
        o = acc_sc[h] / l_sc[h][0:1]
        d = o[:, :tq] - lam * o[:, tq:]
        y = d * lax.rsqrt(jnp.mean(d * d, axis=0, keepdims=True) + EPS) * g * (1.0 - lambda_init)
        o_ref[:, h * DA_V:(h + 1) * DA_V] = y.T.astype(BF16)

    @pl.when(i >= 1)
    def _prev_and_diagonal():
        key_tile(pl.multiple_of((i - 1) * tq, tq), 2 * tq, lambda h: bias_sc[h], after=finish)

    @pl.when(i == 0)
    def _diagonal_only():
        key_tile(0, tq, lambda h: bias_sc[h, tq:, :], after=finish)


def _attn_self(qtb, kb, vtb, rel_table, lam, subln_g, batch, seq, lambda_init, tq, cast_weights):
    assert seq % tq == 0 and tq % 128 == 0 and tq >= MAX_DISTANCE and tq % CHUNK == 0
    nq = seq // tq
    rel = jnp.arange(2 * tq, dtype=jnp.int32)[:, None] - (tq + jnp.arange(tq, dtype=jnp.int32))[None, :]
    bkt = _rel_bucket(rel)
    smem = pl.BlockSpec(memory_space=pltpu.SMEM)

    cast_specs = _cast_block_specs(cast_weights, batch * nq, lambda b, i: b * nq + i)
    n_cast = len(cast_weights)
    outs = pl.pallas_call(
        functools.partial(_attn_self_kernel, tq=tq, lambda_init=lambda_init, n_cast=n_cast),
        grid=(batch, nq),
        in_specs=[_resident((DA_HEADS, 128)), smem,
                  pl.BlockSpec((DA_QK_W, tq), lambda b, i: (b, i)),
                  pl.BlockSpec((seq, DA_QK_W), lambda b, i: (b, 0)),
                  pl.BlockSpec((DA_V_W, seq), lambda b, i: (b, 0)),
                  _resident((2 * tq, tq)), _resident((DA_V, 1))] + cast_specs,
        out_specs=[pl.BlockSpec((tq, DA_V_W), lambda b, i: (b * nq + i, 0))] + cast_specs,
        out_shape=[jax.ShapeDtypeStruct((batch * seq, DA_V_W), BF16)]
                  + [jax.ShapeDtypeStruct(w.shape, BF16) for w in cast_weights],
        scratch_shapes=[pltpu.VMEM((DA_HEADS, 2 * tq, tq), F32),
                        pltpu.VMEM((DA_HEADS, 2 * DA_HD, 2 * tq), BF16),
                        pltpu.VMEM((DA_HEADS, 8, 2 * tq), F32),
                        pltpu.VMEM((DA_HEADS, 8, 2 * tq), F32),
                        pltpu.VMEM((DA_HEADS, DA_V, 2 * tq), F32),
                        pltpu.VMEM((ATTN_LOOKAHEAD + 1, 2 * tq, 2 * tq), F32)],
        compiler_params=_cparams("arbitrary", "arbitrary"),
        name="attn_self",
    )(_table_lanes(rel_table), lam, qtb, kb, vtb, bkt, subln_g.reshape(DA_V, 1), *cast_weights)
    return outs[0], outs[1:]


def _attn_cached_kernel(tab_ref, lam_ref, q_ref, ckt_ref, cv_ref, nk_ref, nv_ref, bktc_ref, bktn_ref, g_ref, o_ref,
                        biasc_sc, biasn_sc, *, sq, past, lambda_init):
    @pl.when(pl.program_id(0) == 0)
    def _build_bias():
        for bkt_ref, sc, k0 in ((bktc_ref, biasc_sc, 0), (bktn_ref, biasn_sc, past)):
            bkt = bkt_ref[...]
            qpos = past + lax.broadcasted_iota(jnp.int32, bkt.shape, 0)
            kpos = k0 + lax.broadcasted_iota(jnp.int32, bkt.shape, 1)
            vis = jnp.right_shift(kpos, CHUNK_SHIFT) <= jnp.right_shift(qpos, CHUNK_SHIFT)
            for h in range(DA_HEADS):
                sc[h] = _bias_from_buckets(bkt, tab_ref, h, vis, None)[:, :sc.shape[2]]

    q = q_ref[...]
    lam = lam_ref[0]
    g = g_ref[...]
    for h in range(DA_HEADS):
        cols = slice(h * DA_V, (h + 1) * DA_V)
        q2 = _split_maps(q[:, cols])
        s_c = _dot(q2, ckt_ref[cols, :].astype(BF16))
        s_c = (s_c.reshape(2, sq, past) + biasc_sc[h][None]).reshape(2 * sq, past)
        s_n = lax.dot_general(q2, nk_ref[:, cols].astype(BF16), NT_DIMS, preferred_element_type=F32)
        s_n = (s_n.reshape(2, sq, sq) + biasn_sc[h][None]).reshape(2 * sq, sq)
        m = jnp.maximum(jnp.max(s_c, axis=1, keepdims=True), jnp.max(s_n, axis=1, keepdims=True))
        p_c = jnp.exp2(s_c - m)
        p_n = jnp.exp2(s_n - m)
        l = jnp.sum(p_c, axis=1, keepdims=True) + jnp.sum(p_n, axis=1, keepdims=True)
        v_c = cv_ref[pl.ds(h, past, stride=DA_HEADS), :]
        o = (_dot(p_c.astype(BF16), v_c.astype(BF16))
             + _dot(p_n.astype(BF16), nv_ref[:, cols].astype(BF16))) / l
        o_ref[:, cols] = _diff_finish(o, lam, g, sq, lambda_init).astype(BF16)


def _attn_cached(q, cache_kt2d, cache_v2d, nk, nv, rel_table, lam, subln_g, batch, sq, past, lambda_init):
    qpos = past + jnp.arange(sq, dtype=jnp.int32)
    bkt_c = _rel_bucket(jnp.arange(past, dtype=jnp.int32)[None, :] - qpos[:, None])
    assert sq <= 128 and past % 128 == 0
    bkt_n = jnp.pad(_rel_bucket(qpos[None, :] - qpos[:, None]), ((0, 0), (0, 128 - sq)))
    smem = pl.BlockSpec(memory_space=pltpu.SMEM)
    new_spec = pl.BlockSpec((sq, DA_QK_W), lambda b: (b, 0))
    return pl.pallas_call(
        functools.partial(_attn_cached_kernel, sq=sq, past=past, lambda_init=lambda_init),
        grid=(batch,),
        in_specs=[_resident((DA_HEADS, 128)), smem, new_spec,
                  pl.BlockSpec((DA_QK_W, past), lambda b: (b, 0)),
                  pl.BlockSpec((past * DA_HEADS, DA_V), lambda b: (b, 0)),
                  new_spec, new_spec,
                  _resident((sq, past)), _resident((sq, 128)), _resident((1, DA_V))],
        out_specs=new_spec,
        out_shape=jax.ShapeDtypeStruct((batch * sq, DA_V_W), BF16),
        scratch_shapes=[pltpu.VMEM((DA_HEADS, sq, past), F32), pltpu.VMEM((DA_HEADS, sq, sq), F32)],
        compiler_params=_cparams("arbitrary"),
        name="attn_cached",
    )(_table_lanes(rel_table), lam, q, cache_kt2d, cache_v2d, nk, nv, bkt_c, bkt_n, subln_g)


def _mix_ffn_kernel(x_ref, a_ref, b_ref, qc_ref, mk_ref, mv_ref, gmix_ref, wg_ref, bg_ref, wpa_ref, wpb_ref, wpc_ref,
                    wo_ref, gffn_ref, wfi_ref, wfo_ref, gfin_ref, y_ref, *, seg):
    tm = x_ref.shape[0]
    nseg = tm // seg
    n_mem = mk_ref.shape[0] // nseg

    def xattn_head(hh):
        cols = slice(hh * XA_HD, (hh + 1) * XA_HD)
        outs = []
        for s in range(nseg):
            mem = slice(s * n_mem, (s + 1) * n_mem)
            logit = lax.dot_general(qc_ref[s * seg:(s + 1) * seg, cols], mk_ref[mem, cols].astype(BF16), NT_DIMS,
                                    preferred_element_type=F32)
            p = jnp.exp(logit - jnp.max(logit, axis=1, keepdims=True))
            l = jnp.sum(p, axis=1, keepdims=True)
            outs.append((_dot(p.astype(BF16), mv_ref[mem, cols].astype(BF16)) / l).astype(BF16))
        return outs[0] if nseg == 1 else jnp.concatenate(outs, axis=0)

    x = x_ref[...]
    h = _rms(x, gmix_ref[...]).astype(BF16)

    def branch(n, br, wp_ref):
        cols = slice(n * D_MODEL, (n + 1) * D_MODEL)
        gate = jax.nn.sigmoid(_dot(h, wg_ref[:, cols]) + bg_ref[:, cols])
        return gate * _dot(br, wp_ref[...])

    merged = branch(0, a_ref[...], wpa_ref)
    c_heads = [xattn_head(hh) for hh in range(XA_HEADS // 2)]
    merged = merged + branch(1, b_ref[...], wpb_ref)
    c_heads += [xattn_head(hh) for hh in range(XA_HEADS // 2, XA_HEADS)]
    merged = merged + branch(2, jnp.concatenate(c_heads, axis=1), wpc_ref)
    x1 = x + _dot(merged.astype(BF16), wo_ref[...])
    h2 = _rms(x1, gffn_ref[...]).astype(BF16)
    ff = None
    for c0, c1 in zip(FFN_SPLITS[:-1], FFN_SPLITS[1:]):
        gcols = slice(c0, c1)
        ucols = slice(D_FF + c0, D_FF + c1)
        act = (jax.nn.silu(_dot(h2, wfi_ref[:, gcols])) * _dot(h2, wfi_ref[:, ucols])).astype(BF16)
        term = _dot(act, wfo_ref[gcols, :])
        ff = term if ff is None else ff + term
    y_ref[...] = _rms(x1 + ff, gfin_ref[...])


def _mix_ffn(x2d, a_out, b_out, qc, mk2d, mv2d, p, tm, seq):
    t = x2d.shape[0]
    seg = min(tm, seq)
    nseg = tm // seg
    nt = seq // seg
    n_mem = mk2d.shape[0] // (t // seq)
    assert tm % seg == 0 and seq % seg == 0
    row = pl.BlockSpec((tm, D_MODEL), lambda i: (i, 0))
    mem = pl.BlockSpec((nseg * n_mem, XA_W), lambda i: (i // nt, 0))
    vec = _resident((1, D_MODEL))
    sq_w = _resident((D_MODEL, D_MODEL))
    return pl.pallas_call(
        functools.partial(_mix_ffn_kernel, seg=seg),
        grid=(t // tm,),
        in_specs=[row, row, row, row, mem, mem, vec, _resident((D_MODEL, N_BRANCH * D_MODEL)),
                  _resident((1, N_BRANCH * D_MODEL)), sq_w, sq_w, sq_w, sq_w, vec,
                  _resident((D_MODEL, 2 * D_FF)), _resident((D_FF, D_MODEL)), vec],
        out_specs=row,
        out_shape=jax.ShapeDtypeStruct((t, D_MODEL), F32),
        compiler_params=_cparams("parallel"),
        name="mix_ffn",
    )(x2d, a_out, b_out, qc, mk2d, mv2d, p["norm_mix"], p["w_gate"], p["b_gate"], p["w_proj_a"], p["w_proj_b"],
      p["w_proj_c"], p["w_out"], p["norm_ffn"], p["w_ffn_in"], p["w_ffn_out"], p["norm_final"])


def _layer(x, mk2d, mv2d, cache_kt2d, cache_v2d, conv_state, h0, rel_table, lam, p, lambda_init, tm, tq, tm_out):
    batch, seq, _ = x.shape
    x2d = x.reshape(batch * seq, D_MODEL)
    no_history = cache_kt2d is None
    q, k, v, kb, vb, qc, b_out, new_conv, h_last = _in_proj(x2d, conv_state, h0, p, tm, seq, no_history)
    if no_history:
        a_out, out_w = _attn_self(q, kb, vb, rel_table, lam, p["subln_g"], batch, seq, lambda_init, tq,
                                  [p[n] for n in OUT_WEIGHTS])
        p = {**p, **dict(zip(OUT_WEIGHTS, out_w))}
    else:
        past = cache_kt2d.shape[1]
        a_out = _attn_cached(q, cache_kt2d, cache_v2d, kb, vb, rel_table, lam, p["subln_g"], batch, seq, past,
                             lambda_init)
    y = _mix_ffn(x2d, a_out, b_out, qc, mk2d, mv2d, p, tm_out, seq)
    return y.reshape(batch, seq, D_MODEL), k, v, new_conv, h_last.reshape(batch, LRU_W), p


def kernel(x_prompt, x_sample, mem_prompt, cache_k, cache_v, state_conv, state_lru, cache_mem_k, cache_mem_v,
           rel_table, norm_mix, w_in, lambda_q1, lambda_k1, lambda_q2, lambda_k2, subln_g, conv_w, conv_b,
           w_rg_a, b_rg_a, w_rg_x, b_rg_x, rg_lambda, norm_mem, w_mem_kv, w_proj_a, w_proj_b, w_proj_c,
           w_gate, b_gate, w_out, norm_ffn, w_ffn_in, w_ffn_out, norm_final):
    depth = w_in.shape[0]
    assert depth == 1, "the final norm is fused into the layer's last kernel"
    l = 0
    lambda_init = 0.8 - 0.6 * math.exp(-0.3 * l)
    bp, sp_, _ = x_prompt.shape
    bs, ss, _ = x_sample.shape
    past = cache_k.shape[2]
    n_mem = mem_prompt.shape[1]
    row = lambda a: a.reshape(1, -1).astype(F32)
    mk, mv, mkb, mvb = _mem_kv(mem_prompt.reshape(bp * n_mem, D_MODEL), row(norm_mem[l]),
                               w_mem_kv[l].astype(BF16), 256)
    p = dict(norm_mix=row(norm_mix[l]), w_in=w_in[l].astype(BF16), subln_g=row(subln_g[l]),
             conv_w=conv_w[l], conv_b=row(conv_b[l]),
             w_rg=jnp.concatenate([w_rg_a[l], w_rg_x[l]], axis=-1).astype(BF16),
             b_rg_a=row(b_rg_a[l]), b_rg_x=row(b_rg_x[l]), rg_lambda=row(rg_lambda[l]),
             w_proj_a=w_proj_a[l], w_proj_b=w_proj_b[l], w_proj_c=w_proj_c[l], w_gate=w_gate[l],
             b_gate=row(b_gate[l]), w_out=w_out[l], norm_ffn=row(norm_ffn[l]), w_ffn_in=w_ffn_in[l],
             w_ffn_out=w_ffn_out[l], norm_final=row(norm_final))
    lam = (jnp.exp(jnp.sum(lambda_q1[l] * lambda_k1[l]).astype(F32))
           - jnp.exp(jnp.sum(lambda_q2[l] * lambda_k2[l]).astype(F32)) + lambda_init).reshape(1)

    zeros_conv = jnp.zeros((bp, CONV_W - 1, LRU_W), F32)
    zeros_h = jnp.zeros((bp, 1, LRU_W), F32)
    yp, kp, vp, cp, hp, p = _layer(x_prompt, mkb, mvb, None, None, zeros_conv, zeros_h, rel_table, lam, p,
                                   lambda_init, tm=512, tq=256, tm_out=512)
    cache_kt = jnp.transpose(cache_k[l], (0, 2, 3, 4, 1)).reshape(bs * DA_QK_W, past)
    ys, ks, vs, cs, hs, _ = _layer(x_sample, cache_mem_k[l].reshape(bs * n_mem, XA_W),
                                   cache_mem_v[l].reshape(bs * n_mem, XA_W),
                                   cache_kt, cache_v[l].reshape(bs * past * DA_HEADS, DA_V),
                                   state_conv[l], state_lru[l].reshape(bs, 1, LRU_W), rel_table, lam, p,
                                   lambda_init, tm=256, tq=ss, tm_out=128)
    kp = jnp.transpose(kp.reshape(bp, DA_HEADS, 2, DA_HD, sp_), (0, 4, 1, 2, 3))
    return (yp, ys,
            kp[None], vp.reshape(1, bp, sp_, DA_HEADS, DA_V),
            cp[None], hp[None],
            mk.reshape(1, bp, n_mem, XA_HEADS, XA_HD), mv.reshape(1, bp, n_mem, XA_HEADS, XA_HD),
            ks.reshape(1, bs, ss, DA_HEADS, 2, DA_HD), vs.reshape(1, bs, ss, DA_HEADS, DA_V),
            cs[None], hs[None])
```

```python
import functools
import math

import jax
import jax.numpy as jnp
from jax import lax
from jax.experimental import pallas as pl
from jax.experimental.pallas import tpu as pltpu

F32 = jnp.float32
BF16 = jnp.bfloat16

D_MODEL = 1024
CHUNK = 64
CHUNK_SHIFT = 6
DA_HEADS = 8
DA_HD = 64
DA_V = 2 * DA_HD
LRU_W = D_MODEL
LRU_BLOCKS = 8
LRU_BW = LRU_W // LRU_BLOCKS
CONV_W = 4
LRU_C = 8.0
XA_HEADS = 4
XA_HD = 256
N_BUCKETS = 32
MAX_DISTANCE = 128
D_FF = ((8 * D_MODEL + 3 * 256 - 1) // (3 * 256)) * 256
N_BRANCH = 3
EPS = 1e-6
DA_QK_W = DA_HEADS * 2 * DA_HD
DA_V_W = DA_HEADS * DA_V
XA_W = XA_HEADS * XA_HD
IN_OFFS = (0, DA_QK_W, 2 * DA_QK_W, 2 * DA_QK_W + DA_V_W, 2 * DA_QK_W + DA_V_W + LRU_W,
           2 * DA_QK_W + DA_V_W + 2 * LRU_W)
IN_W = 2 * DA_QK_W + DA_V_W + 2 * LRU_W + XA_W
LOG2E = math.log2(math.e)
DA_Q_SCALE = DA_HD ** -0.5 * LOG2E
XA_SCALE = XA_HD ** -0.5
MASK_VALUE = -1e30
FAR_BUCKET = N_BUCKETS // 2 - 1
V7X_MXU_DIM = 256
FFN_SPLITS = (0, -(-(D_FF // V7X_MXU_DIM) // 2) * V7X_MXU_DIM, D_FF)
PROJ_CHUNK = V7X_MXU_DIM
OUT_WEIGHTS = ("w_gate", "w_proj_a", "w_proj_b", "w_proj_c", "w_out", "w_ffn_in", "w_ffn_out")
SCAN_GROUP = 8
BF16_SUBLANES = 16
ATTN_LOOKAHEAD = 2

V7X_VMEM_BYTES = 64 * 1024 * 1024
VMEM_LIMIT = V7X_VMEM_BYTES - 4 * 1024 * 1024

NT_DIMS = (((1,), (1,)), ((), ()))


def _cparams(*sem):
    return pltpu.CompilerParams(dimension_semantics=sem, vmem_limit_bytes=VMEM_LIMIT)


def _resident(shape):
    nd = len(shape)
    return pl.BlockSpec(shape, lambda *_: (0,) * nd, pipeline_mode=pl.Buffered(1))


def _rms(x, g):
    return x * lax.rsqrt(jnp.mean(x * x, axis=-1, keepdims=True) + EPS) * g


def _dot(a, b):
    return jnp.dot(a, b, preferred_element_type=F32)


def _in_proj_kernel(x_ref, g_ref, w_ref, cs_ref, h0_ref, cw_ref, cb_ref, wrg_ref, ba_ref, bx_ref, lam_ref,
                    q_ref, k_ref, v_ref, kb_ref, vb_ref, qc_ref, bo_ref, nc_ref, hl_ref,
                    xpad_sc, a_sc, u_sc, hs_sc, gg_sc, hc_sc, p_sc, q_sc, hin_sc, *, nt, seg, feature_major):
    tm = x_ref.shape[0]
    nseg = tm // seg
    two_level = nseg == 1
    groups = tm // SCAN_GROUP
    halo = CONV_W - 1
    base = 8 - halo

    @pl.when(pl.program_id(0) % nt == 0)
    def _load_state():
        for s in range(nseg):
            xpad_sc[s, base:8, :] = cs_ref[s]
            for n in range(LRU_BLOCKS):
                hc_sc[s, n] = h0_ref[s][:, n * LRU_BW:(n + 1) * LRU_BW]

    h = _rms(x_ref[...], g_ref[...]).astype(BF16)

    def proj(n, c0, width):
        return _dot(h, w_ref[:, IN_OFFS[n] + c0:IN_OFFS[n] + c0 + width])

    xb = proj(3, 0, LRU_W)
    for s in range(nseg):
        xpad_sc[s, 8:8 + seg, :] = xb[s * seg:(s + 1) * seg]

    def proj_chunk(kind, c0):
        cols = slice(c0, c0 + PROJ_CHUNK)
        if kind == "g":
            gg_sc[:, cols] = jax.nn.gelu(proj(4, c0, PROJ_CHUNK))
        elif kind == "q":
            qs = proj(0, c0, PROJ_CHUNK) * DA_Q_SCALE
            if feature_major:
                q_ref[cols, :] = qs.T.astype(BF16)
            else:
                q_ref[:, cols] = qs.astype(BF16)
        elif kind == "k":
            kc = proj(1, c0, PROJ_CHUNK)
            kb_ref[:, cols] = kc.astype(BF16)
            if feature_major:
                k_ref[cols, :] = kc.T
            else:
                for j in range(PROJ_CHUNK // DA_HD):
                    m = (c0 + j * DA_HD) // DA_HD
                    k_ref[:, m // 2, m % 2, :] = kc[:, j * DA_HD:(j + 1) * DA_HD]
        elif kind == "v":
            vc = proj(2, c0, PROJ_CHUNK)
            v_ref[:, cols] = vc
            if feature_major:
                vb_ref[cols, :] = vc.T.astype(BF16)
            else:
                vb_ref[:, cols] = vc.astype(BF16)
        else:
            qc_ref[:, cols] = (proj(5, c0, PROJ_CHUNK) * XA_SCALE).astype(BF16)

    chunks = [(kind, c0) for kind, width in (("g", LRU_W), ("q", DA_QK_W), ("k", DA_QK_W), ("v", DA_V_W), ("c", XA_W))
              for c0 in range(0, width, PROJ_CHUNK)]

    cw = cw_ref[...]
    sp = jax.nn.softplus(-lam_ref[...])
    for n in range(LRU_BLOCKS):
        for kind, c0 in chunks[n * len(chunks) // LRU_BLOCKS:(n + 1) * len(chunks) // LRU_BLOCKS]:
            proj_chunk(kind, c0)
        cols = slice(n * LRU_BW, (n + 1) * LRU_BW)
        xc_segs = []
        for s in range(nseg):
            xc = xpad_sc[s, base:base + seg, cols] * cw[0:1, cols]
            for j in range(1, CONV_W):
                xc = xc + xpad_sc[s, base + j:base + j + seg, cols] * cw[j:j + 1, cols]
            xc_segs.append(cb_ref[:, cols] + xc)
        xc = xc_segs[0] if nseg == 1 else jnp.concatenate(xc_segs, axis=0)
        xcb = xc.astype(BF16)
        pre = _dot(xcb, wrg_ref[n])
        tanh_r = jnp.tanh(0.5 * (pre[:, :LRU_BW] + ba_ref[:, cols]))
        tanh_i = jnp.tanh(0.5 * (pre[:, LRU_BW:] + bx_ref[:, cols]))
        half_c_sp = (-0.5 * LRU_C) * sp[:, cols]
        log_a = half_c_sp * tanh_r + half_c_sp
        a = jnp.exp(log_a)
        a_sc[n] = a
        one_minus_a2 = -jnp.tanh(log_a) * (a * a + 1.0)
        mult = jnp.where(one_minus_a2 > 0.0, one_minus_a2 * lax.rsqrt(one_minus_a2), 0.0)
        u_sc[n] = mult * ((0.5 * tanh_i + 0.5) * xc)
        if two_level:
            for r in range(SCAN_GROUP):
                a_r = a_sc[n, pl.ds(r, groups, stride=SCAN_GROUP), :]
                u_r = u_sc[n, pl.ds(r, groups, stride=SCAN_GROUP), :]
                p_r, q_r = (a_r, u_r) if r == 0 else (a_r * p_r, a_r * q_r + u_r)
                p_sc[n, r] = p_r
                q_sc[n, r] = q_r
    for s in range(nseg):
        tail = xpad_sc[s, base + seg:8 + seg, :]
        nc_ref[s] = tail
        xpad_sc[s, base:8, :] = tail

    if two_level:
        def group_step(gi, h_in):
            hin_sc[:, pl.ds(gi, 1), :] = h_in
            last = SCAN_GROUP - 1
            return p_sc[:, last, pl.ds(gi, 1), :] * h_in + q_sc[:, last, pl.ds(gi, 1), :]

        hlast = lax.fori_loop(0, groups, group_step, hc_sc[0], unroll=8)
        hc_sc[0] = hlast
        hl_ref[0] = jnp.concatenate([hlast[n] for n in range(LRU_BLOCKS)], axis=1)
        for n in range(LRU_BLOCKS):
            h_in = hin_sc[n]
            for r in range(SCAN_GROUP):
                hs_sc[n, pl.ds(r, groups, stride=SCAN_GROUP), :] = p_sc[n, r] * h_in + q_sc[n, r]
    else:
        for s in range(nseg):
            def step(t, hprev, row0=s * seg):
                hnew = a_sc[:, pl.ds(row0 + t, 1), :] * hprev + u_sc[:, pl.ds(row0 + t, 1), :]
                hs_sc[:, pl.ds(row0 + t, 1), :] = hnew
                return hnew

            hlast = lax.fori_loop(0, seg, step, hc_sc[s], unroll=8)
            hc_sc[s] = hlast
            hl_ref[s] = jnp.concatenate([hlast[n] for n in range(LRU_BLOCKS)], axis=1)
    for n in range(LRU_BLOCKS):
        cols = slice(n * LRU_BW, (n + 1) * LRU_BW)
        bo_ref[:, cols] = (hs_sc[n] * gg_sc[:, cols]).astype(BF16)


def _in_proj(x2d, conv_state, h0, p, tm, seq, feature_major):
    t = x2d.shape[0]
    seg = min(tm, seq)
    nseg = tm // seg
    nt = seq // seg
    assert tm % seg == 0 and seq % seg == 0 and seg % 8 == 0 and seg >= CONV_W - 1
    batch_block = lambda i: (i // nt, 0, 0)
    row = lambda width: pl.BlockSpec((tm, width), lambda i: (i, 0))
    sds = lambda width, dt: jax.ShapeDtypeStruct((t, width), dt)
    if feature_major:
        assert nseg == 1
        fm_spec = lambda width: pl.BlockSpec((width, tm), lambda i: (i // nt, i % nt))
        fm_sds = lambda width, dt: jax.ShapeDtypeStruct((t // seq * width, seq), dt)
        k_spec, k_sds = fm_spec(DA_QK_W), fm_sds(DA_QK_W, F32)
    else:
        fm_spec, fm_sds = row, sds
        k_spec = pl.BlockSpec((tm, DA_HEADS, 2, DA_HD), lambda i: (i, 0, 0, 0))
        k_sds = jax.ShapeDtypeStruct((t, DA_HEADS, 2, DA_HD), F32)
    vec = _resident((1, LRU_W))
    gate_w = _resident((LRU_BLOCKS, LRU_BW, 2 * LRU_BW))
    slab = pltpu.VMEM((LRU_BLOCKS, tm, LRU_BW), F32)
    conv_spec = pl.BlockSpec((nseg, CONV_W - 1, LRU_W), batch_block)
    state_spec = pl.BlockSpec((nseg, 1, LRU_W), batch_block)
    return pl.pallas_call(
        functools.partial(_in_proj_kernel, nt=nt, seg=seg, feature_major=feature_major),
        grid=(t // tm,),
        in_specs=[row(D_MODEL), _resident((1, D_MODEL)), _resident((D_MODEL, IN_W)), conv_spec, state_spec,
                  _resident((CONV_W, LRU_W)), vec, gate_w, vec, vec, vec],
        out_specs=[fm_spec(DA_QK_W), k_spec, row(DA_V_W), row(DA_QK_W), fm_spec(DA_V_W), row(XA_W),
                   row(LRU_W), conv_spec, state_spec],
        out_shape=[fm_sds(DA_QK_W, BF16), k_sds, sds(DA_V_W, F32), sds(DA_QK_W, BF16),
                   fm_sds(DA_V_W, BF16), sds(XA_W, BF16), sds(LRU_W, BF16),
                   jax.ShapeDtypeStruct(conv_state.shape, F32), jax.ShapeDtypeStruct(h0.shape, F32)],
        scratch_shapes=[pltpu.VMEM((nseg, 8 + seg, LRU_W), F32), slab, slab, slab, pltpu.VMEM((tm, LRU_W), F32),
                        pltpu.VMEM((nseg, LRU_BLOCKS, 1, LRU_BW), F32),
                        pltpu.VMEM((LRU_BLOCKS, SCAN_GROUP, tm // SCAN_GROUP, LRU_BW), F32),
                        pltpu.VMEM((LRU_BLOCKS, SCAN_GROUP, tm // SCAN_GROUP, LRU_BW), F32),
                        pltpu.VMEM((LRU_BLOCKS, tm // SCAN_GROUP, LRU_BW), F32)],
        compiler_params=_cparams("arbitrary"),
        name="in_proj",
    )(x2d, p["norm_mix"], p["w_in"], conv_state, h0, p["conv_w"], p["conv_b"], p["w_rg"], p["b_rg_a"],
      p["b_rg_x"], p["rg_lambda"])


def _cast_block_specs(weights, steps, step_index):
    specs = []
    for w in weights:
        rows, cols = w.shape
        nblk = max(d for d in range(1, steps + 1)
                   if steps % d == 0 and rows % d == 0 and (rows // d) % BF16_SUBLANES == 0)
        specs.append(pl.BlockSpec((rows // nblk, cols), lambda *g, per=steps // nblk: (step_index(*g) // per, 0)))
    return specs


def _mem_kv_kernel(m_ref, g_ref, w_ref, mk_ref, mv_ref, mkb_ref, mvb_ref):
    h = _rms(m_ref[...], g_ref[...]).astype(BF16)
    mk = _dot(h, w_ref[:, :XA_W])
    mv = _dot(h, w_ref[:, XA_W:])
    for hh in range(XA_HEADS):
        mk_ref[:, hh, :] = mk[:, hh * XA_HD:(hh + 1) * XA_HD]
        mv_ref[:, hh, :] = mv[:, hh * XA_HD:(hh + 1) * XA_HD]
    mkb_ref[...] = mk.astype(BF16)
    mvb_ref[...] = mv.astype(BF16)


def _mem_kv(mem2d, g, w_bf16, tm):
    t = mem2d.shape[0]
    row = pl.BlockSpec((tm, XA_W), lambda i: (i, 0))
    heads = pl.BlockSpec((tm, XA_HEADS, XA_HD), lambda i: (i, 0, 0))
    return pl.pallas_call(
        _mem_kv_kernel,
        grid=(t // tm,),
        in_specs=[pl.BlockSpec((tm, D_MODEL), lambda i: (i, 0)), _resident((1, D_MODEL)),
                  _resident((D_MODEL, 2 * XA_W))],
        out_specs=[heads, heads, row, row],
        out_shape=[jax.ShapeDtypeStruct((t, XA_HEADS, XA_HD), F32), jax.ShapeDtypeStruct((t, XA_HEADS, XA_HD), F32),
                   jax.ShapeDtypeStruct((t, XA_W), BF16), jax.ShapeDtypeStruct((t, XA_W), BF16)],
        compiler_params=_cparams("parallel"),
        name="mem_kv",
    )(mem2d, g, w_bf16)


def _rel_bucket(rel):
    half = N_BUCKETS // 2
    max_exact = half // 2
    n = jnp.abs(rel)
    nf = jnp.maximum(n, 1).astype(F32)
    large = max_exact + (jnp.log(nf / max_exact) / math.log(MAX_DISTANCE / max_exact)
                         * (half - max_exact)).astype(jnp.int32)
    large = jnp.minimum(large, half - 1)
    return jnp.where(rel > 0, half, 0) + jnp.where(n < max_exact, n, large)


def _table_lanes(rel_table):
    return jnp.pad(rel_table.T, ((0, 0), (0, 128 - N_BUCKETS)))


def _bias_from_buckets(bkt, tabt_ref, head, vis, shift_bucket):
    rows, width = bkt.shape
    table = jnp.broadcast_to(tabt_ref[head:head + 1, :], (rows, 128))
    bias = jnp.concatenate([jnp.take_along_axis(table, bkt[:, c:c + 128], axis=1) for c in range(0, width, 128)],
                           axis=1)
    if shift_bucket is not None:
        bias = bias - tabt_ref[head:head + 1, shift_bucket:shift_bucket + 1]
    return jnp.where(vis, bias * LOG2E, MASK_VALUE)


def _split_maps(qh):
    lane = lax.broadcasted_iota(jnp.int32, qh.shape, 1)
    zero = jnp.zeros_like(qh)
    return jnp.concatenate([jnp.where(lane < DA_HD, qh, zero), jnp.where(lane >= DA_HD, qh, zero)], axis=0)


def _ones_column(n):
    return jnp.where(lax.broadcasted_iota(jnp.int32, (n, DA_V), 1) == 0, 1.0, 0.0).astype(BF16)


def _diff_finish(o, lam, g, tq, lambda_init):
    d = o[:tq] - lam * o[tq:]
    return _rms(d, g) * (1.0 - lambda_init)


def _attn_self_kernel(tab_ref, lam_ref, qt_ref, k_ref, vt_ref, bkt_ref, g_ref, *rest, tq, lambda_init, n_cast):
    w_refs, o_ref, wb_refs = rest[:n_cast], rest[n_cast], rest[n_cast + 1:2 * n_cast + 1]
    bias_sc, q2t_sc, m_sc, l_sc, acc_sc, s_sc = rest[2 * n_cast + 1:]
    for w_ref, wb_ref in zip(w_refs, wb_refs):
        wb_ref[...] = w_ref[...].astype(BF16)

    b = pl.program_id(0)
    i = pl.program_id(1)

    @pl.when((b == 0) & (i == 0))
    def _build_bias():
        bkt = bkt_ref[...]
        kpos = lax.broadcasted_iota(jnp.int32, bkt.shape, 0)
        qpos = lax.broadcasted_iota(jnp.int32, bkt.shape, 1)
        vis = jnp.right_shift(kpos - tq, CHUNK_SHIFT) <= jnp.right_shift(qpos, CHUNK_SHIFT)
        for h in range(DA_HEADS):
            bias_sc[h] = _bias_from_buckets(bkt, tab_ref, h, vis, FAR_BUCKET)

    feat = lax.broadcasted_iota(jnp.int32, (2 * DA_HD, tq), 0)
    for h in range(DA_HEADS):
        qt = qt_ref[h * 2 * DA_HD:(h + 1) * 2 * DA_HD, :]
        zero = jnp.zeros_like(qt)
        q2t_sc[h] = jnp.concatenate([jnp.where(feat < DA_HD, qt, zero), jnp.where(feat >= DA_HD, qt, zero)], axis=1)
    m_sc[...] = jnp.full(m_sc.shape, MASK_VALUE, F32)
    l_sc[...] = jnp.zeros(l_sc.shape, F32)
    acc_sc[...] = jnp.zeros(acc_sc.shape, F32)

    def logits(h, start, nk):
        kj = k_ref[pl.ds(start, nk), h * DA_V:(h + 1) * DA_V]
        return _dot(kj, q2t_sc[h])

    def update(h, start, nk, s, bias):
        vtj = vt_ref[h * DA_V:(h + 1) * DA_V, pl.ds(start, nk)]
        if bias is not None:
            s = s + jnp.concatenate([bias, bias], axis=1)
        m_prev = m_sc[h]
        m_new = jnp.maximum(m_prev, jnp.max(s, axis=0, keepdims=True))
        alpha = jnp.exp2(m_prev - m_new)
        p = jnp.exp2(s - m_new[0:1])
        l_sc[h] = alpha * l_sc[h] + jnp.sum(p, axis=0, keepdims=True)
        acc_sc[h] = alpha[0:1] * acc_sc[h] + _dot(vtj, p.astype(BF16))
        m_sc[h] = m_new

    def key_tile(start, nk, bias_of, after=None):
        nslot = ATTN_LOOKAHEAD + 1
        for h in range(ATTN_LOOKAHEAD):
            s_sc[h % nslot, :nk] = logits(h, start, nk)
        for h in range(DA_HEADS):
            if h + ATTN_LOOKAHEAD < DA_HEADS:
                s_sc[(h + ATTN_LOOKAHEAD) % nslot, :nk] = logits(h + ATTN_LOOKAHEAD, start, nk)
            update(h, start, nk, s_sc[h % nslot, :nk], bias_of(h))
            if after is not None:
                after(h)

    n_far = jnp.maximum(i - 1, 0)

    def far_pair(j, carry):
        key_tile(pl.multiple_of(j * 2 * tq, 2 * tq), 2 * tq, lambda h: None)
        return carry

    lax.fori_loop(0, n_far // 2, far_pair, 0)

    @pl.when(n_far % 2 == 1)
    def _odd_far_tile():
        key_tile(pl.multiple_of((n_far - 1) * tq, tq), tq, lambda h: None)

    lam = lam_ref[0]
    g = g_ref[...]

    def finish(h):
        o = acc_sc[h] / l_sc[h][0:1]
        d = o[:, :tq] - lam * o[:, tq:]
        y = d * lax.rsqrt(jnp.mean(d * d, axis=0, keepdims=True) + EPS) * g * (1.0 - lambda_init)
        o_ref[:, h * DA_V:(h + 1) * DA_V] = y.T.astype(BF16)

    @pl.when(i >= 1)
    def _prev_and_diagonal():
        key_tile(pl.multiple_of((i - 1) * tq, tq), 2 * tq, lambda h: bias_sc[h], after=finish)

    @pl.when(i == 0)
    def _diagonal_only():
        key_tile(0, tq, lambda h: bias_sc[h, tq:, :], after=finish)


def _attn_self(qtb, kb, vtb, rel_table, lam, subln_g, batch, seq, lambda_init, tq, cast_weights):
    assert seq % tq == 0 and tq % 128 == 0 and tq >= MAX_DISTANCE and tq % CHUNK == 0
    nq = seq // tq
    rel = jnp.arange(2 * tq, dtype=jnp.int32)[:, None] - (tq + jnp.arange(tq, dtype=jnp.int32))[None, :]
    bkt = _rel_bucket(rel)
    smem = pl.BlockSpec(memory_space=pltpu.SMEM)

    cast_specs = _cast_block_specs(cast_weights, batch * nq, lambda b, i: b * nq + i)
    n_cast = len(cast_weights)
    outs = pl.pallas_call(
        functools.partial(_attn_self_kernel, tq=tq, lambda_init=lambda_init, n_cast=n_cast),
        grid=(batch, nq),
        in_specs=[_resident((DA_HEADS, 128)), smem,
                  pl.BlockSpec((DA_QK_W, tq), lambda b, i: (b, i)),
                  pl.BlockSpec((seq, DA_QK_W), lambda b, i: (b, 0)),
                  pl.BlockSpec((DA_V_W, seq), lambda b, i: (b, 0)),
                  _resident((2 * tq, tq)), _resident((DA_V, 1))] + cast_specs,
        out_specs=[pl.BlockSpec((tq, DA_V_W), lambda b, i: (b * nq + i, 0))] + cast_specs,
        out_shape=[jax.ShapeDtypeStruct((batch * seq, DA_V_W), BF16)]
                  + [jax.ShapeDtypeStruct(w.shape, BF16) for w in cast_weights],
        scratch_shapes=[pltpu.VMEM((DA_HEADS, 2 * tq, tq), F32),
                        pltpu.VMEM((DA_HEADS, 2 * DA_HD, 2 * tq), BF16),
                        pltpu.VMEM((DA_HEADS, 8, 2 * tq), F32),
                        pltpu.VMEM((DA_HEADS, 8, 2 * tq), F32),
                        pltpu.VMEM((DA_HEADS, DA_V, 2 * tq), F32),
                        pltpu.VMEM((ATTN_LOOKAHEAD + 1, 2 * tq, 2 * tq), F32)],
        compiler_params=_cparams("arbitrary", "arbitrary"),
        name="attn_self",
    )(_table_lanes(rel_table), lam, qtb, kb, vtb, bkt, subln_g.reshape(DA_V, 1), *cast_weights)
    return outs[0], outs[1:]


def _attn_cached_kernel(tab_ref, lam_ref, q_ref, ckt_ref, cv_ref, nk_ref, nv_ref, bktc_ref, bktn_ref, g_ref, o_ref,
                        biasc_sc, biasn_sc, *, sq, past, lambda_init):
    @pl.when(pl.program_id(0) == 0)
    def _build_bias():
        for bkt_ref, sc, k0 in ((bktc_ref, biasc_sc, 0), (bktn_ref, biasn_sc, past)):
            bkt = bkt_ref[...]
            qpos = past + lax.broadcasted_iota(jnp.int32, bkt.shape, 0)
            kpos = k0 + lax.broadcasted_iota(jnp.int32, bkt.shape, 1)
            vis = jnp.right_shift(kpos, CHUNK_SHIFT) <= jnp.right_shift(qpos, CHUNK_SHIFT)
            for h in range(DA_HEADS):
                sc[h] = _bias_from_buckets(bkt, tab_ref, h, vis, None)[:, :sc.shape[2]]

    q = q_ref[...]
    lam = lam_ref[0]
    g = g_ref[...]
    for h in range(DA_HEADS):
        cols = slice(h * DA_V, (h + 1) * DA_V)
        q2 = _split_maps(q[:, cols])
        s_c = _dot(q2, ckt_ref[cols, :].astype(BF16))
        s_c = (s_c.reshape(2, sq, past) + biasc_sc[h][None]).reshape(2 * sq, past)
        s_n = lax.dot_general(q2, nk_ref[:, cols].astype(BF16), NT_DIMS, preferred_element_type=F32)
        s_n = (s_n.reshape(2, sq, sq) + biasn_sc[h][None]).reshape(2 * sq, sq)
        m = jnp.maximum(jnp.max(s_c, axis=1, keepdims=True), jnp.max(s_n, axis=1, keepdims=True))
        p_c = jnp.exp2(s_c - m)
        p_n = jnp.exp2(s_n - m)
        l = jnp.sum(p_c, axis=1, keepdims=True) + jnp.sum(p_n, axis=1, keepdims=True)
        v_c = cv_ref[pl.ds(h, past, stride=DA_HEADS), :]
        o = (_dot(p_c.astype(BF16), v_c.astype(BF16))
             + _dot(p_n.astype(BF16), nv_ref[:, cols].astype(BF16))) / l
        o_ref[:, cols] = _diff_finish(o, lam, g, sq, lambda_init).astype(BF16)


def _attn_cached(q, cache_kt2d, cache_v2d, nk, nv, rel_table, lam, subln_g, batch, sq, past, lambda_init):
    qpos = past + jnp.arange(sq, dtype=jnp.int32)
    bkt_c = _rel_bucket(jnp.arange(past, dtype=jnp.int32)[None, :] - qpos[:, None])
    assert sq <= 128 and past % 128 == 0
    bkt_n = jnp.pad(_rel_bucket(qpos[None, :] - qpos[:, None]), ((0, 0), (0, 128 - sq)))
    smem = pl.BlockSpec(memory_space=pltpu.SMEM)
    new_spec = pl.BlockSpec((sq, DA_QK_W), lambda b: (b, 0))
    return pl.pallas_call(
        functools.partial(_attn_cached_kernel, sq=sq, past=past, lambda_init=lambda_init),
        grid=(batch,),
        in_specs=[_resident((DA_HEADS, 128)), smem, new_spec,
                  pl.BlockSpec((DA_QK_W, past), lambda b: (b, 0)),
                  pl.BlockSpec((past * DA_HEADS, DA_V), lambda b: (b, 0)),
                  new_spec, new_spec,
                  _resident((sq, past)), _resident((sq, 128)), _resident((1, DA_V))],
        out_specs=new_spec,
        out_shape=jax.ShapeDtypeStruct((batch * sq, DA_V_W), BF16),
        scratch_shapes=[pltpu.VMEM((DA_HEADS, sq, past), F32), pltpu.VMEM((DA_HEADS, sq, sq), F32)],
        compiler_params=_cparams("arbitrary"),
        name="attn_cached",
    )(_table_lanes(rel_table), lam, q, cache_kt2d, cache_v2d, nk, nv, bkt_c, bkt_n, subln_g)


def _mix_ffn_kernel(x_ref, a_ref, b_ref, qc_ref, mk_ref, mv_ref, gmix_ref, wg_ref, bg_ref, wpa_ref, wpb_ref, wpc_ref,
                    wo_ref, gffn_ref, wfi_ref, wfo_ref, gfin_ref, y_ref, *, seg):
    tm = x_ref.shape[0]
    nseg = tm // seg
    n_mem = mk_ref.shape[0] // nseg

    def xattn_head(hh):
        cols = slice(hh * XA_HD, (hh + 1) * XA_HD)
        outs = []
        for s in range(nseg):
            mem = slice(s * n_mem, (s + 1) * n_mem)
            logit = lax.dot_general(qc_ref[s * seg:(s + 1) * seg, cols], mk_ref[mem, cols].astype(BF16), NT_DIMS,
                                    preferred_element_type=F32)
            p = jnp.exp(logit - jnp.max(logit, axis=1, keepdims=True))
            l = jnp.sum(p, axis=1, keepdims=True)
            outs.append((_dot(p.astype(BF16), mv_ref[mem, cols].astype(BF16)) / l).astype(BF16))
        return outs[0] if nseg == 1 else jnp.concatenate(outs, axis=0)

    x = x_ref[...]
    h = _rms(x, gmix_ref[...]).astype(BF16)

    def branch(n, br, wp_ref):
        cols = slice(n * D_MODEL, (n + 1) * D_MODEL)
        gate = jax.nn.sigmoid(_dot(h, wg_ref[:, cols]) + bg_ref[:, cols])
        return gate * _dot(br, wp_ref[...])

    merged = branch(0, a_ref[...], wpa_ref)
    c_heads = [xattn_head(hh) for hh in range(XA_HEADS // 2)]
    merged = merged + branch(1, b_ref[...], wpb_ref)
    c_heads += [xattn_head(hh) for hh in range(XA_HEADS // 2, XA_HEADS)]
    merged = merged + branch(2, jnp.concatenate(c_heads, axis=1), wpc_ref)
    x1 = x + _dot(merged.astype(BF16), wo_ref[...])
    h2 = _rms(x1, gffn_ref[...]).astype(BF16)
    ff = None
    for c0, c1 in zip(FFN_SPLITS[:-1], FFN_SPLITS[1:]):
        gcols = slice(c0, c1)
        ucols = slice(D_FF + c0, D_FF + c1)
        act = (jax.nn.silu(_dot(h2, wfi_ref[:, gcols])) * _dot(h2, wfi_ref[:, ucols])).astype(BF16)
        term = _dot(act, wfo_ref[gcols, :])
        ff = term if ff is None else ff + term
    y_ref[...] = _rms(x1 + ff, gfin_ref[...])


def _mix_ffn(x2d, a_out, b_out, qc, mk2d, mv2d, p, tm, seq):
    t = x2d.shape[0]
    seg = min(tm, seq)
    nseg = tm // seg
    nt = seq // seg
    n_mem = mk2d.shape[0] // (t // seq)
    assert tm % seg == 0 and seq % seg == 0
    row = pl.BlockSpec((tm, D_MODEL), lambda i: (i, 0))
    mem = pl.BlockSpec((nseg * n_mem, XA_W), lambda i: (i // nt, 0))
    vec = _resident((1, D_MODEL))
    sq_w = _resident((D_MODEL, D_MODEL))
    return pl.pallas_call(
        functools.partial(_mix_ffn_kernel, seg=seg),
        grid=(t // tm,),
        in_specs=[row, row, row, row, mem, mem, vec, _resident((D_MODEL, N_BRANCH * D_MODEL)),
                  _resident((1, N_BRANCH * D_MODEL)), sq_w, sq_w, sq_w, sq_w, vec,
                  _resident((D_MODEL, 2 * D_FF)), _resident((D_FF, D_MODEL)), vec],
        out_specs=row,
        out_shape=jax.ShapeDtypeStruct((t, D_MODEL), F32),
        compiler_params=_cparams("parallel"),
        name="mix_ffn",
    )(x2d, a_out, b_out, qc, mk2d, mv2d, p["norm_mix"], p["w_gate"], p["b_gate"], p["w_proj_a"], p["w_proj_b"],
      p["w_proj_c"], p["w_out"], p["norm_ffn"], p["w_ffn_in"], p["w_ffn_out"], p["norm_final"])


def _layer(x, mk2d, mv2d, cache_kt2d, cache_v2d, conv_state, h0, rel_table, lam, p, lambda_init, tm, tq, tm_out):
    batch, seq, _ = x.shape
    x2d = x.reshape(batch * seq, D_MODEL)
    no_history = cache_kt2d is None
    q, k, v, kb, vb, qc, b_out, new_conv, h_last = _in_proj(x2d, conv_state, h0, p, tm, seq, no_history)
    if no_history:
        a_out, out_w = _attn_self(q, kb, vb, rel_table, lam, p["subln_g"], batch, seq, lambda_init, tq,
                                  [p[n] for n in OUT_WEIGHTS])
        p = {**p, **dict(zip(OUT_WEIGHTS, out_w))}
    else:
        past = cache_kt2d.shape[1]
        a_out = _attn_cached(q, cache_kt2d, cache_v2d, kb, vb, rel_table, lam, p["subln_g"], batch, seq, past,
                             lambda_init)
    y = _mix_ffn(x2d, a_out, b_out, qc, mk2d, mv2d, p, tm_out, seq)
    return y.reshape(batch, seq, D_MODEL), k, v, new_conv, h_last.reshape(batch, LRU_W), p


def kernel(x_prompt, x_sample, mem_prompt, cache_k, cache_v, state_conv, state_lru, cache_mem_k, cache_mem_v,
           rel_table, norm_mix, w_in, lambda_q1, lambda_k1, lambda_q2, lambda_k2, subln_g, conv_w, conv_b,
           w_rg_a, b_rg_a, w_rg_x, b_rg_x, rg_lambda, norm_mem, w_mem_kv, w_proj_a, w_proj_b, w_proj_c,
           w_gate, b_gate, w_out, norm_ffn, w_ffn_in, w_ffn_out, norm_final):
    depth = w_in.shape[0]
    assert depth == 1, "the final norm is fused into the layer's last kernel"
    l = 0
    lambda_init = 0.8 - 0.6 * math.exp(-0.3 * l)
    bp, sp_, _ = x_prompt.shape
    bs, ss, _ = x_sample.shape
    past = cache_k.shape[2]
    n_mem = mem_prompt.shape[1]
    row = lambda a: a.reshape(1, -1).astype(F32)
    mk, mv, mkb, mvb = _mem_kv(mem_prompt.reshape(bp * n_mem, D_MODEL), row(norm_mem[l]),
                               w_mem_kv[l].astype(BF16), 256)
    p = dict(norm_mix=row(norm_mix[l]), w_in=w_in[l].astype(BF16), subln_g=row(subln_g[l]),
             conv_w=conv_w[l], conv_b=row(conv_b[l]),
             w_rg=jnp.concatenate([w_rg_a[l], w_rg_x[l]], axis=-1).astype(BF16),
             b_rg_a=row(b_rg_a[l]), b_rg_x=row(b_rg_x[l]), rg_lambda=row(rg_lambda[l]),
             w_proj_a=w_proj_a[l], w_proj_b=w_proj_b[l], w_proj_c=w_proj_c[l], w_gate=w_gate[l],
             b_gate=row(b_gate[l]), w_out=w_out[l], norm_ffn=row(norm_ffn[l]), w_ffn_in=w_ffn_in[l],
             w_ffn_out=w_ffn_out[l], norm_final=row(norm_final))
    lam = (jnp.exp(jnp.sum(lambda_q1[l] * lambda_k1[l]).astype(F32))
           - jnp.exp(jnp.sum(lambda_q2[l] * lambda_k2[l]).astype(F32)) + lambda_init).reshape(1)

    zeros_conv = jnp.zeros((bp, CONV_W - 1, LRU_W), F32)
    zeros_h = jnp.zeros((bp, 1, LRU_W), F32)
    yp, kp, vp, cp, hp, p = _layer(x_prompt, mkb, mvb, None, None, zeros_conv, zeros_h, rel_table, lam, p,
                                   lambda_init, tm=512, tq=256, tm_out=512)
    cache_kt = jnp.transpose(cache_k[l], (0, 2, 3, 4, 1)).reshape(bs * DA_QK_W, past)
    ys, ks, vs, cs, hs, _ = _layer(x_sample, cache_mem_k[l].reshape(bs * n_mem, XA_W).astype(BF16),
                                   cache_mem_v[l].reshape(bs * n_mem, XA_W).astype(BF16),
                                   cache_kt, cache_v[l].reshape(bs * past * DA_HEADS, DA_V),
                                   state_conv[l], state_lru[l].reshape(bs, 1, LRU_W), rel_table, lam, p,
                                   lambda_init, tm=256, tq=ss, tm_out=256)
    kp = jnp.transpose(kp.reshape(bp, DA_HEADS, 2, DA_HD, sp_), (0, 4, 1, 2, 3))
    return (yp, ys,
            kp[None], vp.reshape(1, bp, sp_, DA_HEADS, DA_V),
            cp[None], hp[None],
            mk.reshape(1, bp, n_mem, XA_HEADS, XA_HD), mv.reshape(1, bp, n_mem, XA_HEADS, XA_HD),
            ks.reshape(1, bs, ss, DA_HEADS, 2, DA_HD), vs.reshape(1, bs, ss, DA_HEADS, DA_V),
            cs[None], hs[None])
```
